```python
import math
import jax, jax.numpy as jnp
from jax import lax
import numpy as np

D_MODEL = 1024
BATCH = 2
SEQ = 8192
DEPTH = 1

ATT_HEADS = 8
ATT_KV_HEADS = 2
HEAD_DIM = 64
WINDOW = 128
BLOCK = 128
ROPE_THETA = 10000.0
CONV_CH = 512
CONV_WIDTH = 31
MEM_LEN = 256
MEM_HEADS = 4
MEM_HEAD_DIM = 128
N_BRANCHES = 3
N_GROUPS = 4
EXPERTS_PER_GROUP = 4
N_EXPERTS = N_GROUPS * EXPERTS_PER_GROUP
TOP_K_IN_GROUP = 2
EXPERT_FF = 512
EPS = 1e-6
LN_EPS = 1e-5
NEG_INF = -1e30

Q_WIDTH = ATT_HEADS * HEAD_DIM
KV_WIDTH = ATT_KV_HEADS * HEAD_DIM
GLU_WIDTH = 2 * CONV_CH
XQ_WIDTH = MEM_HEADS * MEM_HEAD_DIM
GATE_WIDTH = N_BRANCHES * D_MODEL
IN_WIDTH = Q_WIDTH + 2 * KV_WIDTH + GLU_WIDTH + XQ_WIDTH + GATE_WIDTH

kernel_name = "hybrid_swa_conformer_memxattn_hiermoe"


def rms_norm(x, g):
    xf = x.astype(jnp.float32)
    y = xf * lax.rsqrt(jnp.mean(xf * xf, axis=-1, keepdims=True) + EPS)
    return (y * g.astype(jnp.float32)).astype(x.dtype)


def layer_norm(x, g, b):
    xf = x.astype(jnp.float32)
    mu = jnp.mean(xf, axis=-1, keepdims=True)
    var = jnp.mean(jnp.square(xf - mu), axis=-1, keepdims=True)
    y = (xf - mu) * lax.rsqrt(var + LN_EPS)
    return (y * g.astype(jnp.float32) + b.astype(jnp.float32)).astype(x.dtype)


def rope_tables(positions):
    inv_freq = 1.0 / (ROPE_THETA ** (jnp.arange(0, HEAD_DIM, 2, dtype=jnp.float32) / HEAD_DIM))
    ang = positions.astype(jnp.float32)[..., None] * inv_freq
    return jnp.cos(ang), jnp.sin(ang)


def apply_rope(t, cos, sin):
    tf = t.astype(jnp.float32)
    t1, t2 = jnp.split(tf, 2, axis=-1)
    c = cos[:, :, None, :]
    s = sin[:, :, None, :]
    out = jnp.concatenate([t1 * c - t2 * s, t2 * c + t1 * s], axis=-1)
    return out.astype(t.dtype)


def split_columns(proj, widths):
    points = []
    acc = 0
    for w in widths[:-1]:
        acc += w
        points.append(acc)
    return jnp.split(proj, points, axis=-1)


def sliding_window_attention(q, k, v, sinks):
    B, S, Hq, Dh = q.shape
    G = Hq // ATT_KV_HEADS
    nb = S // BLOCK
    qb = q.reshape(B, nb, BLOCK, ATT_KV_HEADS, G, Dh)

    def band(t):
        tb = t.reshape(B, nb, BLOCK, ATT_KV_HEADS, Dh)
        prev = jnp.pad(tb, ((0, 0), (1, 0), (0, 0), (0, 0), (0, 0)))[:, :nb]
        return jnp.concatenate([prev, tb], axis=2)

    kb, vb = band(k), band(v)
    s = jnp.einsum('bnqhgd,bnkhd->bnhgqk', qb, kb).astype(jnp.float32) * (Dh ** -0.5)
    q_pos = jnp.arange(nb)[:, None] * BLOCK + jnp.arange(BLOCK)[None, :]
    k_pos = (jnp.arange(nb)[:, None] - 1) * BLOCK + jnp.arange(2 * BLOCK)[None, :]
    delta = q_pos[:, :, None] - k_pos[:, None, :]
    valid = (delta >= 0) & (delta < WINDOW) & (k_pos[:, None, :] >= 0)
    s = jnp.where(valid[None, :, None, None], s, NEG_INF)
    sink = jnp.broadcast_to(sinks.astype(jnp.float32).reshape(1, 1, ATT_KV_HEADS, G, 1, 1),
                            s.shape[:-1] + (1,))
    p = jax.nn.softmax(jnp.concatenate([s, sink], axis=-1), axis=-1)[..., :-1]
    o = jnp.einsum('bnhgqk,bnkhd->bnqhgd', p.astype(v.dtype), vb)
    return o.reshape(B, S, Hq * Dh)


def causal_depthwise_conv(u, w, b):
    y = lax.conv_general_dilated(
        u, w[:, None, :].astype(u.dtype), window_strides=(1,),
        padding=[(CONV_WIDTH - 1, 0)], dimension_numbers=('NWC', 'WIO', 'NWC'),
        feature_group_count=CONV_CH)
    return y + b.astype(u.dtype)


def memory_cross_attention(xq, mem_n, w_kv, g_xq, g_xk):
    B, S, _ = xq.shape
    M = mem_n.shape[1]
    mk, mv = jnp.split(mem_n @ w_kv, 2, axis=-1)
    mk = rms_norm(mk.reshape(B, M, MEM_HEADS, MEM_HEAD_DIM), g_xk)
    mv = mv.reshape(B, M, MEM_HEADS, MEM_HEAD_DIM)
    qh = rms_norm(xq.reshape(B, S, MEM_HEADS, MEM_HEAD_DIM), g_xq)
    s = jnp.einsum('bshd,bmhd->bhsm', qh, mk).astype(jnp.float32) * (MEM_HEAD_DIM ** -0.5)
    p = jax.nn.softmax(s, axis=-1)
    o = jnp.einsum('bhsm,bmhd->bshd', p.astype(mv.dtype), mv)
    return o.reshape(B, S, XQ_WIDTH)


def hierarchical_moe(h, w_group, b_group, w_router, b_router, w_gate, w_up, w_down):
    B, S, D = h.shape
    t = h.reshape(-1, D)
    T = t.shape[0]
    gl = (t @ w_group).astype(jnp.float32) + b_group.astype(jnp.float32)
    gp = jax.nn.softmax(gl, axis=-1)
    g_idx = jnp.argmax(gl, axis=-1)
    p_g = jnp.take_along_axis(gp, g_idx[:, None], axis=1)
    el = ((t @ w_router).astype(jnp.float32) + b_router.astype(jnp.float32)).reshape(
        T, N_GROUPS, EXPERTS_PER_GROUP)
    el = jnp.take_along_axis(el, g_idx[:, None, None], axis=1)[:, 0]
    top_v, top_i = lax.top_k(el, TOP_K_IN_GROUP)
    p_e = jax.nn.softmax(top_v, axis=-1)
    ids = g_idx[:, None] * EXPERTS_PER_GROUP + top_i
    combine = jnp.sum(jax.nn.one_hot(ids, N_EXPERTS, dtype=jnp.float32) * (p_g * p_e)[..., None],
                      axis=1).astype(h.dtype)
    y = jnp.zeros_like(t)
    for e in range(N_EXPERTS):
        hid = jax.nn.silu(t @ w_gate[e]) * (t @ w_up[e])
        y = y + combine[:, e:e + 1] * (hid @ w_down[e])
    return y.reshape(B, S, D)


def setup_inputs(seed: int = 0) -> dict:
    key = jax.random.key(seed)
    ks = iter(jax.random.split(key, 40))
    f32 = jnp.float32
    L = DEPTH

    def nrm(shape, scale):
        return jax.random.normal(next(ks), shape, f32) * scale

    def gain(shape):
        return 1.0 + 0.05 * jax.random.normal(next(ks), shape, f32)

    x = jax.random.normal(next(ks), (BATCH, SEQ, D_MODEL), f32)
    mem = jax.random.normal(next(ks), (BATCH, MEM_LEN, D_MODEL), f32)
    positions = jnp.broadcast_to(jnp.arange(SEQ, dtype=jnp.int32), (BATCH, SEQ))
    return {
        "x": x,
        "mem": mem,
        "positions": positions,
        "g_norm1": gain((L, D_MODEL)),
        "w_in": nrm((L, D_MODEL, IN_WIDTH), D_MODEL ** -0.5),
        "g_q": gain((L, HEAD_DIM)),
        "g_k": gain((L, HEAD_DIM)),
        "sinks": nrm((L, ATT_HEADS), 0.5),
        "w_o_attn": nrm((L, Q_WIDTH, D_MODEL), Q_WIDTH ** -0.5),
        "w_conv_dw": nrm((L, CONV_WIDTH, CONV_CH), CONV_WIDTH ** -0.5),
        "b_conv_dw": nrm((L, CONV_CH), 0.02),
        "g_conv_ln": gain((L, CONV_CH)),
        "b_conv_ln": nrm((L, CONV_CH), 0.02),
        "w_conv_out": nrm((L, CONV_CH, D_MODEL), CONV_CH ** -0.5),
        "g_mem": gain((L, D_MODEL)),
        "w_kv_mem": nrm((L, D_MODEL, 2 * XQ_WIDTH), D_MODEL ** -0.5),
        "g_xq": gain((L, MEM_HEAD_DIM)),
        "g_xk": gain((L, MEM_HEAD_DIM)),
        "w_o_mem": nrm((L, XQ_WIDTH, D_MODEL), XQ_WIDTH ** -0.5),
        "w_out": nrm((L, D_MODEL, D_MODEL), D_MODEL ** -0.5),
        "g_norm2": gain((L, D_MODEL)),
        "w_group": nrm((L, D_MODEL, N_GROUPS), D_MODEL ** -0.5),
        "b_group": nrm((L, N_GROUPS), 0.01),
        "w_router": nrm((L, D_MODEL, N_EXPERTS), D_MODEL ** -0.5),
        "b_router": nrm((L, N_EXPERTS), 0.01),
        "w_gate": nrm((L, N_EXPERTS, D_MODEL, EXPERT_FF), D_MODEL ** -0.5),
        "w_up": nrm((L, N_EXPERTS, D_MODEL, EXPERT_FF), D_MODEL ** -0.5),
        "w_down": nrm((L, N_EXPERTS, EXPERT_FF, D_MODEL), EXPERT_FF ** -0.5),
    }


def reference(x, mem, positions, g_norm1, w_in, g_q, g_k, sinks, w_o_attn, w_conv_dw, b_conv_dw,
              g_conv_ln, b_conv_ln, w_conv_out, g_mem, w_kv_mem, g_xq, g_xk, w_o_mem, w_out,
              g_norm2, w_group, b_group, w_router, b_router, w_gate, w_up, w_down):
    B, S, D = x.shape
    cos, sin = rope_tables(positions)
    for l in range(DEPTH):
        h = rms_norm(x, g_norm1[l])
        q, k, v, glu_in, xq, gate_logits = split_columns(
            h @ w_in[l], (Q_WIDTH, KV_WIDTH, KV_WIDTH, GLU_WIDTH, XQ_WIDTH, GATE_WIDTH))

        q = apply_rope(rms_norm(q.reshape(B, S, ATT_HEADS, HEAD_DIM), g_q[l]), cos, sin)
        k = apply_rope(rms_norm(k.reshape(B, S, ATT_KV_HEADS, HEAD_DIM), g_k[l]), cos, sin)
        v = v.reshape(B, S, ATT_KV_HEADS, HEAD_DIM)
        y_attn = sliding_window_attention(q, k, v, sinks[l]) @ w_o_attn[l]

        ga, gb = jnp.split(glu_in, 2, axis=-1)
        u = ga * jax.nn.sigmoid(gb)
        u = causal_depthwise_conv(u, w_conv_dw[l], b_conv_dw[l])
        u = jax.nn.silu(layer_norm(u, g_conv_ln[l], b_conv_ln[l]))
        y_conv = u @ w_conv_out[l]

        mem_n = rms_norm(mem, g_mem[l])
        y_mem = memory_cross_attention(xq, mem_n, w_kv_mem[l], g_xq[l], g_xk[l]) @ w_o_mem[l]

        gates = jax.nn.sigmoid(gate_logits.astype(jnp.float32)).astype(x.dtype).reshape(
            B, S, N_BRANCHES, D)
        merged = gates[:, :, 0] * y_attn + gates[:, :, 1] * y_conv + gates[:, :, 2] * y_mem
        x = x + merged @ w_out[l]

        h2 = rms_norm(x, g_norm2[l])
        x = x + hierarchical_moe(h2, w_group[l], b_group[l], w_router[l], b_router[l],
                                 w_gate[l], w_up[l], w_down[l])
    return x
```

```python
import functools

import jax
import jax.numpy as jnp
from jax import lax
from jax.experimental import pallas as pl
from jax.experimental.pallas import tpu as pltpu

F32 = jnp.float32
BF16 = jnp.bfloat16

D_MODEL = 1024
ATT_HEADS = 8
ATT_KV_HEADS = 2
HEAD_DIM = 64
WINDOW = 128
ROPE_THETA = 10000.0
CONV_CH = 512
CONV_WIDTH = 31
MEM_HEADS = 4
MEM_HEAD_DIM = 128
N_BRANCHES = 3
N_GROUPS = 4
EXPERTS_PER_GROUP = 4
N_EXPERTS = N_GROUPS * EXPERTS_PER_GROUP
EXPERT_FF = 512
EPS = 1e-6
LN_EPS = 1e-5
NEG_INF = -1e30

Q_WIDTH = ATT_HEADS * HEAD_DIM
KV_WIDTH = ATT_KV_HEADS * HEAD_DIM
GLU_WIDTH = 2 * CONV_CH
XQ_WIDTH = MEM_HEADS * MEM_HEAD_DIM
GATE_WIDTH = N_BRANCHES * D_MODEL

LANES = 128
TQ = 256
CONV_HALO = 32
TM = 512
RT_ROWS = 32
VMEM_LIMIT = 56 * 1024 * 1024


def _dot(a, b):
    return jnp.dot(a, b, preferred_element_type=F32)


def _dot_nt(a, b):
    return lax.dot_general(a, b, (((1,), (1,)), ((), ())), preferred_element_type=F32)


def _split_bf16(t):
    hi = t.astype(BF16)
    lo = (t - hi.astype(F32)).astype(BF16)
    return hi, lo


def _block_sum(t, bd):
    hi, lo = _split_bf16(t)
    return _dot(hi, bd) + _dot(lo, bd)


def _rms(t):
    return t * lax.rsqrt(jnp.mean(t * t, axis=-1, keepdims=True) + EPS)


def _memkv_kernel(mem_ref, gmem_ref, wkv_ref, gxk_ref, bd_ref, mk_ref, mv_ref):
    mn = (_rms(mem_ref[0]) * gmem_ref[...]).astype(BF16)
    kv = _dot(mn, wkv_ref[...])
    mk = kv[:, :XQ_WIDTH]
    ss = _block_sum(mk * mk, bd_ref[...])
    mk = mk * lax.rsqrt(ss * (1.0 / MEM_HEAD_DIM) + EPS) * gxk_ref[...]
    mk_ref[0] = mk.astype(BF16)
    mv_ref[0] = kv[:, XQ_WIDTH:].astype(BF16)


def _mixer_kernel(x_ref, cos_ref, sin_ref, mk_ref, mv_ref, g1_ref, wqkv_ref, wglu_ref, wxq_ref,
                  wgt_ref, gq_ref, gk_ref, sinks_ref, bd64_ref, bd128_ref, woa_ref, wdw_ref,
                  bdw_ref, gln_ref, bln_ref, wco_ref, gxq_ref, wom_ref, wout_ref, g2_ref,
                  wrt_ref, brt_ref,
                  x1_ref, rt_ref,
                  k0_ref, k1_ref, v0_ref, v1_ref, ubuf_ref, o_ref):
    i = pl.program_id(1)
    kbands = (k0_ref, k1_ref)
    vbands = (v0_ref, v1_ref)

    @pl.when(i == 0)
    def _():
        for r in kbands + vbands:
            r[0:WINDOW, :] = jnp.zeros((WINDOW, LANES), BF16)
        ubuf_ref[0:CONV_HALO, :] = jnp.zeros((CONV_HALO, CONV_CH), F32)

    x = x_ref[0]
    h = (_rms(x) * g1_ref[...]).astype(BF16)

    qkv = _dot(h, wqkv_ref[...])
    q = qkv[:, :Q_WIDTH]
    k = qkv[:, Q_WIDTH:Q_WIDTH + KV_WIDTH]
    v = qkv[:, Q_WIDTH + KV_WIDTH:]
    bd64 = bd64_ref[...]
    q = q * lax.rsqrt(_block_sum(q * q, bd64) * (1.0 / HEAD_DIM) + EPS) * gq_ref[...]
    k = k * lax.rsqrt(_block_sum(k * k, bd64[:LANES, :LANES]) * (1.0 / HEAD_DIM) + EPS) * gk_ref[...]

    cosd = cos_ref[0]
    sind = sin_ref[0]
    lane = lax.broadcasted_iota(jnp.int32, (TQ, LANES), 1)
    first_half = (lane % HEAD_DIM) < (HEAD_DIM // 2)
    low_head = lane < HEAD_DIM

    def rope(t):
        rot = jnp.where(first_half, pltpu.roll(t, LANES - HEAD_DIM // 2, 1),
                        pltpu.roll(t, HEAD_DIM // 2, 1))
        return t * cosd + rot * sind

    def dup_halves(t):
        swapped = pltpu.roll(t, HEAD_DIM, 1)
        return jnp.where(low_head, t, swapped), jnp.where(low_head, swapped, t)

    kd = dup_halves(rope(k))
    vd = dup_halves(v)
    for kvh in range(ATT_KV_HEADS):
        kbands[kvh][WINDOW:WINDOW + TQ, :] = kd[kvh].astype(BF16)
        vbands[kvh][WINDOW:WINDOW + TQ, :] = vd[kvh].astype(BF16)

    q_heads = []
    for c in range(Q_WIDTH // LANES):
        qc = rope(q[:, c * LANES:(c + 1) * LANES])
        q_heads.append(jnp.where(low_head, qc, 0.0).astype(BF16))
        q_heads.append(jnp.where(low_head, 0.0, qc).astype(BF16))

    qi = lax.broadcasted_iota(jnp.int32, (WINDOW, 2 * WINDOW), 0)
    kj = lax.broadcasted_iota(jnp.int32, (WINDOW, 2 * WINDOW), 1)
    band_mask = (kj > qi) & (kj <= qi + WINDOW)
    first_mask = band_mask & (kj >= jnp.where(i > 0, 0, WINDOW))
    low_head_w = lax.broadcasted_iota(jnp.int32, (WINDOW, LANES), 1) < HEAD_DIM
    group = ATT_HEADS // ATT_KV_HEADS

    for n in range(TQ // WINDOW):
        rows = slice(n * WINDOW, (n + 1) * WINDOW)
        mask = first_mask if n == 0 else band_mask
        for kvh in range(ATT_KV_HEADS):
            heads = range(kvh * group, (kvh + 1) * group)
            q_st = jnp.concatenate([q_heads[hd][rows] for hd in heads], axis=0)
            kband = kbands[kvh][n * WINDOW:n * WINDOW + 2 * WINDOW, :]
            vband = vbands[kvh][n * WINDOW:n * WINDOW + 2 * WINDOW, :]
            s_all = _dot_nt(q_st, kband)
            es, dens = [], []
            for j, hd in enumerate(heads):
                s = jnp.where(mask, s_all[j * WINDOW:(j + 1) * WINDOW], NEG_INF)
                sink = sinks_ref[hd]
                m = jnp.maximum(jnp.max(s, axis=-1, keepdims=True), sink)
                e = jnp.exp(s - m)
                dens.append(jnp.sum(e, axis=-1, keepdims=True) + jnp.exp(sink - m))
                es.append(e.astype(BF16))
            o_all = _dot(jnp.concatenate(es, axis=0), vband)
            for cc in range(group // 2):
                oa = o_all[(2 * cc) * WINDOW:(2 * cc + 1) * WINDOW] / dens[2 * cc]
                ob = o_all[(2 * cc + 1) * WINDOW:(2 * cc + 2) * WINDOW] / dens[2 * cc + 1]
                col = kvh * (group // 2) + cc
                o_ref[rows, col * LANES:(col + 1) * LANES] = jnp.where(low_head_w, oa, ob).astype(BF16)

    for r in kbands + vbands:
        r[0:WINDOW, :] = r[TQ:TQ + WINDOW, :]

    y_attn = _dot(o_ref[...], woa_ref[...])
    merged = jax.nn.sigmoid(_dot(h, wgt_ref[:, 0:D_MODEL])) * y_attn

    glu = _dot(h, wglu_ref[...])
    u = glu[:, :CONV_CH] * jax.nn.sigmoid(glu[:, CONV_CH:])
    ubuf_ref[CONV_HALO:CONV_HALO + TQ, :] = u
    base = CONV_HALO - (CONV_WIDTH - 1)
    span = 8 * ((CONV_WIDTH - 1) // 8)
    acc = jnp.broadcast_to(bdw_ref[...], (TQ, CONV_CH))
    for r in range(8):
        ur = ubuf_ref[base + r:base + r + TQ + span, :]
        for a in range(span // 8 + 1):
            j = 8 * a + r
            if j < CONV_WIDTH:
                acc = acc + wdw_ref[j:j + 1, :] * ur[8 * a:8 * a + TQ]
    ubuf_ref[0:CONV_HALO, :] = ubuf_ref[TQ:TQ + CONV_HALO, :]
    mu = jnp.mean(acc, axis=-1, keepdims=True)
    cen = acc - mu
    var = jnp.mean(cen * cen, axis=-1, keepdims=True)
    yln = cen * lax.rsqrt(var + LN_EPS) * gln_ref[...] + bln_ref[...]
    uo = (yln * jax.nn.sigmoid(yln)).astype(BF16)
    y_conv = _dot(uo, wco_ref[...])
    merged = merged + jax.nn.sigmoid(_dot(h, wgt_ref[:, D_MODEL:2 * D_MODEL])) * y_conv

    xq = _dot(h, wxq_ref[...])
    xq = xq * lax.rsqrt(_block_sum(xq * xq, bd128_ref[...]) * (1.0 / MEM_HEAD_DIM) + EPS) * gxq_ref[...]
    xq = xq.astype(BF16)
    for hd in range(MEM_HEADS):
        cols = slice(hd * MEM_HEAD_DIM, (hd + 1) * MEM_HEAD_DIM)
        s = _dot_nt(xq[:, cols], mk_ref[0, :, cols])
        m = jnp.max(s, axis=-1, keepdims=True)
        e = jnp.exp(s - m)
        den = jnp.sum(e, axis=-1, keepdims=True)
        o_ref[:, cols] = (_dot(e.astype(BF16), mv_ref[0, :, cols]) / den).astype(BF16)
    y_mem = _dot(o_ref[...], wom_ref[...])
    merged = merged + jax.nn.sigmoid(_dot(h, wgt_ref[:, 2 * D_MODEL:3 * D_MODEL])) * y_mem

    x1 = x + _dot(merged.astype(BF16), wout_ref[...])
    x1_ref[0] = x1

    h2 = _rms(x1) * g2_ref[...]
    h_hi, h_lo = _split_bf16(h2)
    w_hi, w_lo = _split_bf16(wrt_ref[...])
    lt = _dot_nt(w_hi, h_hi) + _dot_nt(w_hi, h_lo) + _dot_nt(w_lo, h_hi) + brt_ref[...]
    gl = [lt[g:g + 1] for g in range(N_GROUPS)]
    gmax = functools.reduce(jnp.maximum, gl)
    g_idx = jnp.full((1, TQ), N_GROUPS - 1, jnp.int32)
    for g in reversed(range(N_GROUPS - 1)):
        g_idx = jnp.where(gl[g] == gmax, g, g_idx)
    p_g = 1.0 / functools.reduce(jnp.add, [jnp.exp(r - gmax) for r in gl])
    el = []
    for kk in range(EXPERTS_PER_GROUP):
        row = lt[8 + kk:9 + kk]
        for g in range(1, N_GROUPS):
            off = 8 + g * EXPERTS_PER_GROUP + kk
            row = jnp.where(g_idx == g, lt[off:off + 1], row)
        el.append(row)

    def first_argmax(rows):
        top = functools.reduce(jnp.maximum, rows)
        idx = jnp.full((1, TQ), len(rows) - 1, jnp.int32)
        for kk in reversed(range(len(rows) - 1)):
            idx = jnp.where(rows[kk] == top, kk, idx)
        return top, idx

    v1, i1 = first_argmax(el)
    v2, i2 = first_argmax([jnp.where(i1 == kk, -jnp.inf, el[kk]) for kk in range(EXPERTS_PER_GROUP)])
    t = jnp.exp(v2 - v1)
    p1 = 1.0 / (1.0 + t)
    p2 = t * p1
    zero = jnp.zeros((1, TQ), F32)
    rt_ref[0] = jnp.concatenate(
        [g_idx.astype(F32), i1.astype(F32), i2.astype(F32), p_g * p1, p_g * p2, zero, zero, zero], axis=0)


def _moe_kernel(x1_ref, comb_ref, g2_ref, wg_ref, wu_ref, wd_ref, out_ref, h2_ref):
    e = pl.program_id(1)

    @pl.when(e == 0)
    def _():
        x1 = x1_ref[...]
        h2_ref[...] = (_rms(x1) * g2_ref[...]).astype(BF16)
        out_ref[...] = x1

    lane = lax.broadcasted_iota(jnp.int32, (TM, N_EXPERTS), 1)
    c = jnp.sum(jnp.where(lane == e, comb_ref[...], 0.0), axis=-1, keepdims=True)
    h2 = h2_ref[...]
    hid = jax.nn.silu(_dot(h2, wg_ref[0])) * _dot(h2, wu_ref[0]) * c
    out_ref[...] += _dot(hid.astype(BF16), wd_ref[0])


def _const_spec(shape):
    nd = len(shape)
    return pl.BlockSpec(shape, lambda *_: (0,) * nd, pipeline_mode=pl.Buffered(1))


def _block_diag_ones(n, blk):
    r = jnp.arange(n) // blk
    return (r[:, None] == r[None, :]).astype(BF16)


def kernel(x, mem, positions, g_norm1, w_in, g_q, g_k, sinks, w_o_attn, w_conv_dw, b_conv_dw, g_conv_ln, b_conv_ln, w_conv_out, g_mem, w_kv_mem, g_xq, g_xk, w_o_mem, w_out, g_norm2, w_group, b_group, w_router, b_router, w_gate, w_up, w_down):
    B, S, D = x.shape
    M = mem.shape[1]
    assert D == D_MODEL and S % TQ == 0 and (B * S) % TM == 0 and w_in.shape[0] == 1
    NT = S // TQ
    T = B * S
    l = 0

    bd64 = _block_diag_ones(Q_WIDTH, HEAD_DIM)
    bd128 = _block_diag_ones(XQ_WIDTH, MEM_HEAD_DIM)
    row = lambda v: v.reshape(1, -1).astype(F32)

    mk, mv = pl.pallas_call(
        _memkv_kernel,
        grid=(B,),
        in_specs=[pl.BlockSpec((1, M, D), lambda b: (b, 0, 0)),
                  _const_spec((1, D)), _const_spec((D, 2 * XQ_WIDTH)),
                  _const_spec((1, XQ_WIDTH)), _const_spec((XQ_WIDTH, XQ_WIDTH))],
        out_specs=[pl.BlockSpec((1, M, XQ_WIDTH), lambda b: (b, 0, 0))] * 2,
        out_shape=[jax.ShapeDtypeStruct((B, M, XQ_WIDTH), BF16)] * 2,
        compiler_params=pltpu.CompilerParams(dimension_semantics=("arbitrary",)),
        name="memkv",
    )(mem, row(g_mem[l]), w_kv_mem[l].astype(BF16), row(jnp.tile(g_xk[l], MEM_HEADS)), bd128)

    inv_freq = 1.0 / (ROPE_THETA ** (jnp.arange(0, HEAD_DIM, 2, dtype=F32) / HEAD_DIM))
    ang = positions.astype(F32)[..., None] * inv_freq
    cos, sin = jnp.cos(ang), jnp.sin(ang)
    cosd = jnp.tile(cos, (1, 1, LANES // (HEAD_DIM // 2)))
    sind = jnp.tile(jnp.concatenate([-sin, sin], axis=-1), (1, 1, LANES // HEAD_DIM))

    w = w_in[l]
    c0 = Q_WIDTH + 2 * KV_WIDTH
    c1 = c0 + GLU_WIDTH
    c2 = c1 + XQ_WIDTH
    w_qkv = w[:, :c0].astype(BF16)
    w_glu = w[:, c0:c1].astype(BF16)
    w_xq = w[:, c1:c2].astype(BF16)
    w_gt = w[:, c2:].astype(BF16)
    wdw = jnp.zeros((CONV_HALO, CONV_CH), F32).at[:CONV_WIDTH].set(w_conv_dw[l])
    w_rt = jnp.zeros((RT_ROWS, D), F32).at[0:N_GROUPS].set(w_group[l].T).at[8:8 + N_EXPERTS].set(w_router[l].T)
    b_rt = jnp.zeros((RT_ROWS, 1), F32).at[0:N_GROUPS, 0].set(b_group[l]).at[8:8 + N_EXPERTS, 0].set(b_router[l])

    tile3 = lambda last: pl.BlockSpec((1, TQ, last), lambda b, i: (b, i, 0))
    per_batch = pl.BlockSpec((1, M, XQ_WIDTH), lambda b, i: (b, 0, 0))
    in_specs = [
        tile3(D), tile3(LANES), tile3(LANES), per_batch, per_batch,
        _const_spec((1, D)),
        _const_spec((D, c0)), _const_spec((D, GLU_WIDTH)), _const_spec((D, XQ_WIDTH)), _const_spec((D, GATE_WIDTH)),
        _const_spec((1, Q_WIDTH)), _const_spec((1, LANES)),
        pl.BlockSpec(memory_space=pltpu.SMEM),
        _const_spec((Q_WIDTH, Q_WIDTH)), _const_spec((XQ_WIDTH, XQ_WIDTH)),
        _const_spec((Q_WIDTH, D)),
        _const_spec((CONV_HALO, CONV_CH)), _const_spec((1, CONV_CH)), _const_spec((1, CONV_CH)), _const_spec((1, CONV_CH)),
        _const_spec((CONV_CH, D)),
        _const_spec((1, XQ_WIDTH)),
        _const_spec((XQ_WIDTH, D)),
        _const_spec((D, D)),
        _const_spec((1, D)),
        _const_spec((RT_ROWS, D)), _const_spec((RT_ROWS, 1)),
    ]
    x1, rt = pl.pallas_call(
        _mixer_kernel,
        grid=(B, NT),
        in_specs=in_specs,
        out_specs=[tile3(D), pl.BlockSpec((1, 8, TQ), lambda b, i: (b * NT + i, 0, 0))],
        out_shape=[jax.ShapeDtypeStruct((B, S, D), F32), jax.ShapeDtypeStruct((B * NT, 8, TQ), F32)],
        scratch_shapes=[pltpu.VMEM((WINDOW + TQ, LANES), BF16)] * 4
        + [pltpu.VMEM((CONV_HALO + TQ, CONV_CH), F32), pltpu.VMEM((TQ, Q_WIDTH), BF16)],
        compiler_params=pltpu.CompilerParams(dimension_semantics=("arbitrary", "arbitrary"),
                                             vmem_limit_bytes=VMEM_LIMIT),
        name="mixer",
    )(x, cosd, sind, mk, mv, row(g_norm1[l]), w_qkv, w_glu, w_xq, w_gt,
      row(jnp.tile(g_q[l], ATT_HEADS) * (HEAD_DIM ** -0.5)), row(jnp.tile(g_k[l], ATT_KV_HEADS)),
      sinks[l].astype(F32), bd64, bd128, w_o_attn[l].astype(BF16),
      wdw, row(b_conv_dw[l]), row(g_conv_ln[l]), row(b_conv_ln[l]), w_conv_out[l].astype(BF16),
      row(jnp.tile(g_xq[l], MEM_HEADS) * (MEM_HEAD_DIM ** -0.5)), w_o_mem[l].astype(BF16),
      w_out[l].astype(BF16), row(g_norm2[l]), w_rt, b_rt)

    rt = jnp.transpose(rt, (0, 2, 1)).reshape(T, 8)
    g_idx = rt[:, 0].astype(jnp.int32)
    ids = g_idx[:, None] * EXPERTS_PER_GROUP + rt[:, 1:3].astype(jnp.int32)
    comb = jnp.sum(jax.nn.one_hot(ids, N_EXPERTS, dtype=F32) * rt[:, 3:5, None], axis=1)

    out = pl.pallas_call(
        _moe_kernel,
        grid=(T // TM, N_EXPERTS),
        in_specs=[pl.BlockSpec((TM, D), lambda t, e: (t, 0)),
                  pl.BlockSpec((TM, N_EXPERTS), lambda t, e: (t, 0)),
                  pl.BlockSpec((1, D), lambda t, e: (0, 0)),
                  pl.BlockSpec((1, D, EXPERT_FF), lambda t, e: (e, 0, 0)),
                  pl.BlockSpec((1, D, EXPERT_FF), lambda t, e: (e, 0, 0)),
                  pl.BlockSpec((1, EXPERT_FF, D), lambda t, e: (e, 0, 0))],
        out_specs=pl.BlockSpec((TM, D), lambda t, e: (t, 0)),
        out_shape=jax.ShapeDtypeStruct((T, D), F32),
        scratch_shapes=[pltpu.VMEM((TM, D), BF16)],
        compiler_params=pltpu.CompilerParams(dimension_semantics=("arbitrary", "arbitrary"),
                                             vmem_limit_bytes=VMEM_LIMIT),
        name="moe",
    )(x1.reshape(T, D), comb, row(g_norm2[l]), w_gate[l].astype(BF16), w_up[l].astype(BF16),
      w_down[l].astype(BF16))
    return out.reshape(B, S, D)
```

```python
import functools

import jax
import jax.numpy as jnp
from jax import lax
from jax.experimental import pallas as pl
from jax.experimental.pallas import tpu as pltpu

F32 = jnp.float32
BF16 = jnp.bfloat16

D_MODEL = 1024
ATT_HEADS = 8
ATT_KV_HEADS = 2
HEAD_DIM = 64
WINDOW = 128
ROPE_THETA = 10000.0
CONV_CH = 512
CONV_WIDTH = 31
MEM_HEADS = 4
MEM_HEAD_DIM = 128
N_BRANCHES = 3
N_GROUPS = 4
EXPERTS_PER_GROUP = 4
N_EXPERTS = N_GROUPS * EXPERTS_PER_GROUP
EXPERT_FF = 512
EPS = 1e-6
LN_EPS = 1e-5
NEG_INF = -1e30

Q_WIDTH = ATT_HEADS * HEAD_DIM
KV_WIDTH = ATT_KV_HEADS * HEAD_DIM
GLU_WIDTH = 2 * CONV_CH
XQ_WIDTH = MEM_HEADS * MEM_HEAD_DIM
GATE_WIDTH = N_BRANCHES * D_MODEL

LANES = 128
TQ = 256
CONV_HALO = 32
TM = 256
RT_ROWS = 32
VMEM_LIMIT = 56 * 1024 * 1024


def _dot(a, b):
    return jnp.dot(a, b, preferred_element_type=F32)


def _dot_nt(a, b):
    return lax.dot_general(a, b, (((1,), (1,)), ((), ())), preferred_element_type=F32)


def _split_bf16(t):
    hi = t.astype(BF16)
    lo = (t - hi.astype(F32)).astype(BF16)
    return hi, lo


def _block_sum(t, bd):
    hi, lo = _split_bf16(t)
    return _dot(hi, bd) + _dot(lo, bd)


def _rms(t):
    return t * lax.rsqrt(jnp.mean(t * t, axis=-1, keepdims=True) + EPS)


def _memkv_kernel(mem_ref, gmem_ref, wkv_ref, gxk_ref, bd_ref, mk_ref, mv_ref):
    mn = (_rms(mem_ref[0]) * gmem_ref[...]).astype(BF16)
    kv = _dot(mn, wkv_ref[...])
    mk = kv[:, :XQ_WIDTH]
    ss = _block_sum(mk * mk, bd_ref[...])
    mk = mk * lax.rsqrt(ss * (1.0 / MEM_HEAD_DIM) + EPS) * gxk_ref[...]
    mk_ref[0] = mk.astype(BF16)
    mv_ref[0] = kv[:, XQ_WIDTH:].astype(BF16)


def _mixer_kernel(x_ref, cos_ref, sin_ref, mk_ref, mv_ref, g1_ref, wqkv_ref, wglu_ref, wxq_ref,
                  wgt_ref, gq_ref, gk_ref, sinks_ref, bd64_ref, bd128_ref, woa_ref, wdw_ref,
                  bdw_ref, gln_ref, bln_ref, wco_ref, gxq_ref, wom_ref, wout_ref, g2_ref,
                  wrt_ref, brt_ref,
                  x1_ref, rt_ref,
                  k0_ref, k1_ref, v0_ref, v1_ref, ubuf_ref, o_ref):
    i = pl.program_id(1)
    kbands = (k0_ref, k1_ref)
    vbands = (v0_ref, v1_ref)

    @pl.when(i == 0)
    def _():
        for r in kbands + vbands:
            r[0:WINDOW, :] = jnp.zeros((WINDOW, LANES), BF16)
        ubuf_ref[0:CONV_HALO, :] = jnp.zeros((CONV_HALO, CONV_CH), F32)

    x = x_ref[0]
    h = (_rms(x) * g1_ref[...]).astype(BF16)

    qkv = _dot(h, wqkv_ref[...])
    q = qkv[:, :Q_WIDTH]
    k = qkv[:, Q_WIDTH:Q_WIDTH + KV_WIDTH]
    v = qkv[:, Q_WIDTH + KV_WIDTH:]
    bd64 = bd64_ref[...]
    q = q * lax.rsqrt(_block_sum(q * q, bd64) * (1.0 / HEAD_DIM) + EPS) * gq_ref[...]
    k = k * lax.rsqrt(_block_sum(k * k, bd64[:LANES, :LANES]) * (1.0 / HEAD_DIM) + EPS) * gk_ref[...]

    cosd = cos_ref[0]
    sind = sin_ref[0]
    lane = lax.broadcasted_iota(jnp.int32, (TQ, LANES), 1)
    first_half = (lane % HEAD_DIM) < (HEAD_DIM // 2)
    low_head = lane < HEAD_DIM

    def rope(t):
        rot = jnp.where(first_half, pltpu.roll(t, LANES - HEAD_DIM // 2, 1),
                        pltpu.roll(t, HEAD_DIM // 2, 1))
        return t * cosd + rot * sind

    def dup_halves(t):
        swapped = pltpu.roll(t, HEAD_DIM, 1)
        return jnp.where(low_head, t, swapped), jnp.where(low_head, swapped, t)

    kd = dup_halves(rope(k))
    vd = dup_halves(v)
    for kvh in range(ATT_KV_HEADS):
        kbands[kvh][WINDOW:WINDOW + TQ, :] = kd[kvh].astype(BF16)
        vbands[kvh][WINDOW:WINDOW + TQ, :] = vd[kvh].astype(BF16)

    q_heads = []
    for c in range(Q_WIDTH // LANES):
        qc = rope(q[:, c * LANES:(c + 1) * LANES])
        q_heads.append(jnp.where(low_head, qc, 0.0).astype(BF16))
        q_heads.append(jnp.where(low_head, 0.0, qc).astype(BF16))

    qi = lax.broadcasted_iota(jnp.int32, (WINDOW, 2 * WINDOW), 0)
    kj = lax.broadcasted_iota(jnp.int32, (WINDOW, 2 * WINDOW), 1)
    band_mask = (kj > qi) & (kj <= qi + WINDOW)
    first_mask = band_mask & (kj >= jnp.where(i > 0, 0, WINDOW))
    low_head_w = lax.broadcasted_iota(jnp.int32, (WINDOW, LANES), 1) < HEAD_DIM
    group = ATT_HEADS // ATT_KV_HEADS

    for n in range(TQ // WINDOW):
        rows = slice(n * WINDOW, (n + 1) * WINDOW)
        mask = first_mask if n == 0 else band_mask
        for kvh in range(ATT_KV_HEADS):
            heads = range(kvh * group, (kvh + 1) * group)
            q_st = jnp.concatenate([q_heads[hd][rows] for hd in heads], axis=0)
            kband = kbands[kvh][n * WINDOW:n * WINDOW + 2 * WINDOW, :]
            vband = vbands[kvh][n * WINDOW:n * WINDOW + 2 * WINDOW, :]
            s_all = _dot_nt(q_st, kband)
            es, dens = [], []
            for j, hd in enumerate(heads):
                s = jnp.where(mask, s_all[j * WINDOW:(j + 1) * WINDOW], NEG_INF)
                sink = sinks_ref[hd]
                m = jnp.maximum(jnp.max(s, axis=-1, keepdims=True), sink)
                e = jnp.exp(s - m)
                dens.append(jnp.sum(e, axis=-1, keepdims=True) + jnp.exp(sink - m))
                es.append(e.astype(BF16))
            o_all = _dot(jnp.concatenate(es, axis=0), vband)
            for cc in range(group // 2):
                oa = o_all[(2 * cc) * WINDOW:(2 * cc + 1) * WINDOW] / dens[2 * cc]
                ob = o_all[(2 * cc + 1) * WINDOW:(2 * cc + 2) * WINDOW] / dens[2 * cc + 1]
                col = kvh * (group // 2) + cc
                o_ref[rows, col * LANES:(col + 1) * LANES] = jnp.where(low_head_w, oa, ob).astype(BF16)

    for r in kbands + vbands:
        r[0:WINDOW, :] = r[TQ:TQ + WINDOW, :]

    y_attn = _dot(o_ref[...], woa_ref[...])
    merged = jax.nn.sigmoid(_dot(h, wgt_ref[:, 0:D_MODEL])) * y_attn

    glu = _dot(h, wglu_ref[...])
    u = glu[:, :CONV_CH] * jax.nn.sigmoid(glu[:, CONV_CH:])
    ubuf_ref[CONV_HALO:CONV_HALO + TQ, :] = u
    base = CONV_HALO - (CONV_WIDTH - 1)
    span = 8 * ((CONV_WIDTH - 1) // 8)
    acc = jnp.broadcast_to(bdw_ref[...], (TQ, CONV_CH))
    for r in range(8):
        ur = ubuf_ref[base + r:base + r + TQ + span, :]
        for a in range(span // 8 + 1):
            j = 8 * a + r
            if j < CONV_WIDTH:
                acc = acc + wdw_ref[j:j + 1, :] * ur[8 * a:8 * a + TQ]
    ubuf_ref[0:CONV_HALO, :] = ubuf_ref[TQ:TQ + CONV_HALO, :]
    mu = jnp.mean(acc, axis=-1, keepdims=True)
    cen = acc - mu
    var = jnp.mean(cen * cen, axis=-1, keepdims=True)
    yln = cen * lax.rsqrt(var + LN_EPS) * gln_ref[...] + bln_ref[...]
    uo = (yln * jax.nn.sigmoid(yln)).astype(BF16)
    y_conv = _dot(uo, wco_ref[...])
    merged = merged + jax.nn.sigmoid(_dot(h, wgt_ref[:, D_MODEL:2 * D_MODEL])) * y_conv

    xq = _dot(h, wxq_ref[...])
    xq = xq * lax.rsqrt(_block_sum(xq * xq, bd128_ref[...]) * (1.0 / MEM_HEAD_DIM) + EPS) * gxq_ref[...]
    xq = xq.astype(BF16)
    for hd in range(MEM_HEADS):
        cols = slice(hd * MEM_HEAD_DIM, (hd + 1) * MEM_HEAD_DIM)
        s = _dot_nt(xq[:, cols], mk_ref[0, :, cols])
        m = jnp.max(s, axis=-1, keepdims=True)
        e = jnp.exp(s - m)
        den = jnp.sum(e, axis=-1, keepdims=True)
        o_ref[:, cols] = (_dot(e.astype(BF16), mv_ref[0, :, cols]) / den).astype(BF16)
    y_mem = _dot(o_ref[...], wom_ref[...])
    merged = merged + jax.nn.sigmoid(_dot(h, wgt_ref[:, 2 * D_MODEL:3 * D_MODEL])) * y_mem

    x1 = x + _dot(merged.astype(BF16), wout_ref[...])
    x1_ref[0] = x1

    h2 = _rms(x1) * g2_ref[...]
    h_hi, h_lo = _split_bf16(h2)
    w_hi, w_lo = _split_bf16(wrt_ref[...])
    lt = _dot_nt(w_hi, h_hi) + _dot_nt(w_hi, h_lo) + _dot_nt(w_lo, h_hi) + brt_ref[...]
    gl = [lt[g:g + 1] for g in range(N_GROUPS)]
    gmax = functools.reduce(jnp.maximum, gl)
    g_idx = jnp.full((1, TQ), N_GROUPS - 1, jnp.int32)
    for g in reversed(range(N_GROUPS - 1)):
        g_idx = jnp.where(gl[g] == gmax, g, g_idx)
    p_g = 1.0 / functools.reduce(jnp.add, [jnp.exp(r - gmax) for r in gl])
    el = []
    for kk in range(EXPERTS_PER_GROUP):
        row = lt[8 + kk:9 + kk]
        for g in range(1, N_GROUPS):
            off = 8 + g * EXPERTS_PER_GROUP + kk
            row = jnp.where(g_idx == g, lt[off:off + 1], row)
        el.append(row)

    def first_argmax(rows):
        top = functools.reduce(jnp.maximum, rows)
        idx = jnp.full((1, TQ), len(rows) - 1, jnp.int32)
        for kk in reversed(range(len(rows) - 1)):
            idx = jnp.where(rows[kk] == top, kk, idx)
        return top, idx

    v1, i1 = first_argmax(el)
    v2, i2 = first_argmax([jnp.where(i1 == kk, -jnp.inf, el[kk]) for kk in range(EXPERTS_PER_GROUP)])
    t = jnp.exp(v2 - v1)
    p1 = 1.0 / (1.0 + t)
    p2 = t * p1
    zero = jnp.zeros((1, TQ), F32)
    rt_ref[0] = jnp.concatenate(
        [g_idx.astype(F32), i1.astype(F32), i2.astype(F32), p_g * p1, p_g * p2, zero, zero, zero], axis=0)


FLAG_FIRST, FLAG_LAST, FLAG_VALID = 1, 2, 4


def _moe_kernel(perm_ref, tile_ref, elo_ref, ehi_ref, cls_ref, flag_ref,
                wts_ref, g2_ref, wg_lo, wu_lo, wd_lo, wg_hi, wu_hi, wd_hi, x1_hbm,
                out_hbm,
                xbuf, obuf, h2buf, gsem, ssem):
    k = pl.program_id(0)
    nk = pl.num_programs(0)
    t = tile_ref[k]
    slot = t % 2
    flags = flag_ref[k]
    n_tiles = x1_hbm.shape[0] // TM

    def gather_copy(r, tile, s):
        tok = perm_ref[tile * TM + r]
        return pltpu.make_async_copy(x1_hbm.at[pl.ds(tok, 1), :], xbuf.at[s, pl.ds(r, 1), :], gsem.at[s])

    def scatter_copy(r, tile, s):
        tok = perm_ref[tile * TM + r]
        return pltpu.make_async_copy(obuf.at[s, pl.ds(r, 1), :], out_hbm.at[pl.ds(tok, 1), :], ssem.at[s])

    def for_rows(fn):
        def body(r, carry):
            fn(r)
            return carry
        lax.fori_loop(0, TM, body, 0, unroll=8)

    @pl.when(k == 0)
    def _():
        for_rows(lambda r: gather_copy(r, t, slot).start())

    @pl.when((flags & FLAG_FIRST) != 0)
    def _():
        @pl.when(t + 1 < n_tiles)
        def _():
            for_rows(lambda r: gather_copy(r, t + 1, 1 - slot).start())

        for_rows(lambda r: gather_copy(r, t, slot).wait())

        @pl.when(t >= 2)
        def _():
            for_rows(lambda r: scatter_copy(r, t - 2, slot).wait())

        xr = xbuf[slot]
        h2buf[...] = (_rms(xr) * g2_ref[...]).astype(BF16)
        obuf[slot] = xr

    @pl.when((flags & FLAG_VALID) != 0)
    def _():
        w = wts_ref[...]
        mine = w[:, 2:3] == cls_ref[k].astype(F32)
        h2 = h2buf[...]
        y = None
        for col, wg, wu, wd in ((0, wg_lo, wu_lo, wd_lo), (1, wg_hi, wu_hi, wd_hi)):
            c = jnp.where(mine, w[:, col:col + 1], 0.0)
            hid = jax.nn.silu(_dot(h2, wg[0])) * _dot(h2, wu[0]) * c
            part = _dot(hid.astype(BF16), wd[0])
            y = part if y is None else y + part
        obuf[slot] += y

    @pl.when((flags & FLAG_LAST) != 0)
    def _():
        for_rows(lambda r: scatter_copy(r, t, slot).start())

    @pl.when(k == nk - 1)
    def _():
        for tile in (n_tiles - 2, n_tiles - 1):
            for_rows(lambda r, tile=tile: scatter_copy(r, tile, tile % 2).wait())


_CLASS_PAIRS = [(a, b) for a in range(EXPERTS_PER_GROUP) for b in range(a + 1, EXPERTS_PER_GROUP)]
N_CLASSES = N_GROUPS * len(_CLASS_PAIRS)


def _routing_plan(rt, n_tiles):
    T = rt.shape[1]
    g = rt[0].astype(jnp.int32)
    i1 = rt[1].astype(jnp.int32)
    i2 = rt[2].astype(jnp.int32)
    lo = jnp.minimum(i1, i2)
    hi = jnp.maximum(i1, i2)
    w_lo = jnp.where(i1 < i2, rt[3], rt[4])
    w_hi = jnp.where(i1 < i2, rt[4], rt[3])
    n_pairs = len(_CLASS_PAIRS)
    cls = g * n_pairs + (lo * (2 * EXPERTS_PER_GROUP - 1 - lo)) // 2 + hi - lo - 1
    perm = jnp.argsort(cls, stable=True).astype(jnp.int32)
    wts = jnp.stack([w_lo, w_hi, cls.astype(F32)], axis=-1)[perm]
    wts = jnp.pad(wts, ((0, 0), (0, 8 - wts.shape[1])))

    counts = jnp.sum(cls[:, None] == jnp.arange(N_CLASSES)[None, :], axis=0).astype(jnp.int32)
    off = jnp.concatenate([jnp.zeros((1,), jnp.int32), jnp.cumsum(counts).astype(jnp.int32)])
    first_tile = off[:-1] // TM
    last_tile = (off[1:] - 1) // TM
    n_items = jnp.where(counts > 0, last_tile - first_tile + 1, 0)
    istart = jnp.concatenate([jnp.zeros((1,), jnp.int32), jnp.cumsum(n_items).astype(jnp.int32)])
    total = istart[-1]
    ni = n_tiles + N_CLASSES - 1
    k = jnp.arange(ni, dtype=jnp.int32)
    valid = k < total
    kc = jnp.minimum(k, total - 1)
    c_of_k = jnp.clip(jnp.searchsorted(istart, kc, side="right").astype(jnp.int32) - 1, 0, N_CLASSES - 1)
    tile_k = first_tile[c_of_k] + kc - istart[c_of_k]
    prev_tile = jnp.concatenate([jnp.full((1,), -1, jnp.int32), tile_k[:-1]])
    next_tile = jnp.concatenate([tile_k[1:], jnp.full((1,), -1, jnp.int32)])
    first = valid & (tile_k != prev_tile)
    last = valid & ((tile_k != next_tile) | (k == total - 1))
    flags = (first * FLAG_FIRST + last * FLAG_LAST + valid * FLAG_VALID).astype(jnp.int32)
    pair_lo = jnp.array([p[0] for p in _CLASS_PAIRS], jnp.int32)
    pair_hi = jnp.array([p[1] for p in _CLASS_PAIRS], jnp.int32)
    base = (c_of_k // n_pairs) * EXPERTS_PER_GROUP
    e_lo = base + pair_lo[c_of_k % n_pairs]
    e_hi = base + pair_hi[c_of_k % n_pairs]
    cls_k = jnp.where(valid, c_of_k, -1)
    return perm, tile_k, e_lo, e_hi, cls_k, flags, wts


def _const_spec(shape):
    nd = len(shape)
    return pl.BlockSpec(shape, lambda *_: (0,) * nd, pipeline_mode=pl.Buffered(1))


def _block_diag_ones(n, blk):
    r = jnp.arange(n) // blk
    return (r[:, None] == r[None, :]).astype(BF16)


def kernel(x, mem, positions, g_norm1, w_in, g_q, g_k, sinks, w_o_attn, w_conv_dw, b_conv_dw, g_conv_ln, b_conv_ln, w_conv_out, g_mem, w_kv_mem, g_xq, g_xk, w_o_mem, w_out, g_norm2, w_group, b_group, w_router, b_router, w_gate, w_up, w_down):
    B, S, D = x.shape
    M = mem.shape[1]
    assert D == D_MODEL and S % TQ == 0 and (B * S) % TM == 0 and w_in.shape[0] == 1
    NT = S // TQ
    T = B * S
    l = 0

    bd64 = _block_diag_ones(Q_WIDTH, HEAD_DIM)
    bd128 = _block_diag_ones(XQ_WIDTH, MEM_HEAD_DIM)
    row = lambda v: v.reshape(1, -1).astype(F32)

    mk, mv = pl.pallas_call(
        _memkv_kernel,
        grid=(B,),
        in_specs=[pl.BlockSpec((1, M, D), lambda b: (b, 0, 0)),
                  _const_spec((1, D)), _const_spec((D, 2 * XQ_WIDTH)),
                  _const_spec((1, XQ_WIDTH)), _const_spec((XQ_WIDTH, XQ_WIDTH))],
        out_specs=[pl.BlockSpec((1, M, XQ_WIDTH), lambda b: (b, 0, 0))] * 2,
        out_shape=[jax.ShapeDtypeStruct((B, M, XQ_WIDTH), BF16)] * 2,
        compiler_params=pltpu.CompilerParams(dimension_semantics=("arbitrary",)),
        name="memkv",
    )(mem, row(g_mem[l]), w_kv_mem[l].astype(BF16), row(jnp.tile(g_xk[l], MEM_HEADS)), bd128)

    inv_freq = 1.0 / (ROPE_THETA ** (jnp.arange(0, HEAD_DIM, 2, dtype=F32) / HEAD_DIM))
    ang = positions.astype(F32)[..., None] * inv_freq
    cos, sin = jnp.cos(ang), jnp.sin(ang)
    cosd = jnp.tile(cos, (1, 1, LANES // (HEAD_DIM // 2)))
    sind = jnp.tile(jnp.concatenate([-sin, sin], axis=-1), (1, 1, LANES // HEAD_DIM))

    w = w_in[l]
    c0 = Q_WIDTH + 2 * KV_WIDTH
    c1 = c0 + GLU_WIDTH
    c2 = c1 + XQ_WIDTH
    w_qkv = w[:, :c0].astype(BF16)
    w_glu = w[:, c0:c1].astype(BF16)
    w_xq = w[:, c1:c2].astype(BF16)
    w_gt = w[:, c2:].astype(BF16)
    wdw = jnp.zeros((CONV_HALO, CONV_CH), F32).at[:CONV_WIDTH].set(w_conv_dw[l])
    w_rt = jnp.zeros((RT_ROWS, D), F32).at[0:N_GROUPS].set(w_group[l].T).at[8:8 + N_EXPERTS].set(w_router[l].T)
    b_rt = jnp.zeros((RT_ROWS, 1), F32).at[0:N_GROUPS, 0].set(b_group[l]).at[8:8 + N_EXPERTS, 0].set(b_router[l])

    tile3 = lambda last: pl.BlockSpec((1, TQ, last), lambda b, i: (b, i, 0))
    per_batch = pl.BlockSpec((1, M, XQ_WIDTH), lambda b, i: (b, 0, 0))
    in_specs = [
        tile3(D), tile3(LANES), tile3(LANES), per_batch, per_batch,
        _const_spec((1, D)),
        _const_spec((D, c0)), _const_spec((D, GLU_WIDTH)), _const_spec((D, XQ_WIDTH)), _const_spec((D, GATE_WIDTH)),
        _const_spec((1, Q_WIDTH)), _const_spec((1, LANES)),
        pl.BlockSpec(memory_space=pltpu.SMEM),
        _const_spec((Q_WIDTH, Q_WIDTH)), _const_spec((XQ_WIDTH, XQ_WIDTH)),
        _const_spec((Q_WIDTH, D)),
        _const_spec((CONV_HALO, CONV_CH)), _const_spec((1, CONV_CH)), _const_spec((1, CONV_CH)), _const_spec((1, CONV_CH)),
        _const_spec((CONV_CH, D)),
        _const_spec((1, XQ_WIDTH)),
        _const_spec((XQ_WIDTH, D)),
        _const_spec((D, D)),
        _const_spec((1, D)),
        _const_spec((RT_ROWS, D)), _const_spec((RT_ROWS, 1)),
    ]
    x1, rt = pl.pallas_call(
        _mixer_kernel,
        grid=(B, NT),
        in_specs=in_specs,
        out_specs=[tile3(D), pl.BlockSpec((1, 8, TQ), lambda b, i: (b * NT + i, 0, 0))],
        out_shape=[jax.ShapeDtypeStruct((B, S, D), F32), jax.ShapeDtypeStruct((B * NT, 8, TQ), F32)],
        scratch_shapes=[pltpu.VMEM((WINDOW + TQ, LANES), BF16)] * 4
        + [pltpu.VMEM((CONV_HALO + TQ, CONV_CH), F32), pltpu.VMEM((TQ, Q_WIDTH), BF16)],
        compiler_params=pltpu.CompilerParams(dimension_semantics=("arbitrary", "arbitrary"),
                                             vmem_limit_bytes=VMEM_LIMIT),
        name="mixer",
    )(x, cosd, sind, mk, mv, row(g_norm1[l]), w_qkv, w_glu, w_xq, w_gt,
      row(jnp.tile(g_q[l], ATT_HEADS) * (HEAD_DIM ** -0.5)), row(jnp.tile(g_k[l], ATT_KV_HEADS)),
      sinks[l].astype(F32), bd64, bd128, w_o_attn[l].astype(BF16),
      wdw, row(b_conv_dw[l]), row(g_conv_ln[l]), row(b_conv_ln[l]), w_conv_out[l].astype(BF16),
      row(jnp.tile(g_xq[l], MEM_HEADS) * (MEM_HEAD_DIM ** -0.5)), w_o_mem[l].astype(BF16),
      w_out[l].astype(BF16), row(g_norm2[l]), w_rt, b_rt)

    n_tiles = T // TM
    plan = _routing_plan(jnp.transpose(rt, (1, 0, 2)).reshape(8, T)[:5], n_tiles)
    perm, tile_k, e_lo, e_hi, cls_k, flags, wts = plan
    ni = tile_k.shape[0]
    wg, wu, wd = w_gate[l].astype(BF16), w_up[l].astype(BF16), w_down[l].astype(BF16)
    lo_map = lambda k, perm, tile, elo, ehi, cls, flg: (elo[k], 0, 0)
    hi_map = lambda k, perm, tile, elo, ehi, cls, flg: (ehi[k], 0, 0)
    up_block = (1, D, EXPERT_FF)
    down_block = (1, EXPERT_FF, D)
    out = pl.pallas_call(
        _moe_kernel,
        grid_spec=pltpu.PrefetchScalarGridSpec(
            num_scalar_prefetch=6,
            grid=(ni,),
            in_specs=[pl.BlockSpec((TM, 8), lambda k, perm, tile, *_: (tile[k], 0)),
                      pl.BlockSpec((1, D), lambda k, *_: (0, 0)),
                      pl.BlockSpec(up_block, lo_map), pl.BlockSpec(up_block, lo_map),
                      pl.BlockSpec(down_block, lo_map),
                      pl.BlockSpec(up_block, hi_map), pl.BlockSpec(up_block, hi_map),
                      pl.BlockSpec(down_block, hi_map),
                      pl.BlockSpec(memory_space=pl.ANY)],
            out_specs=pl.BlockSpec(memory_space=pl.ANY),
            scratch_shapes=[pltpu.VMEM((2, TM, D), F32), pltpu.VMEM((2, TM, D), F32),
                            pltpu.VMEM((TM, D), BF16),
                            pltpu.SemaphoreType.DMA((2,)), pltpu.SemaphoreType.DMA((2,))]),
        out_shape=jax.ShapeDtypeStruct((T, D), F32),
        compiler_params=pltpu.CompilerParams(dimension_semantics=("arbitrary",),
                                             vmem_limit_bytes=VMEM_LIMIT),
        name="moe",
    )(perm, tile_k, e_lo, e_hi, cls_k, flags, wts, row(g_norm2[l]), wg, wu, wd, wg, wu, wd,
      x1.reshape(T, D))
    return out.reshape(B, S, D)
```

```python
import functools

import jax
import jax.numpy as jnp
from jax import lax
from jax.experimental import pallas as pl
from jax.experimental.pallas import tpu as pltpu

F32 = jnp.float32
BF16 = jnp.bfloat16

D_MODEL = 1024
ATT_HEADS = 8
ATT_KV_HEADS = 2
HEAD_DIM = 64
WINDOW = 128
ROPE_THETA = 10000.0
CONV_CH = 512
CONV_WIDTH = 31
MEM_HEADS = 4
MEM_HEAD_DIM = 128
N_BRANCHES = 3
N_GROUPS = 4
EXPERTS_PER_GROUP = 4
N_EXPERTS = N_GROUPS * EXPERTS_PER_GROUP
EXPERT_FF = 512
EPS = 1e-6
LN_EPS = 1e-5
NEG_INF = -1e30

Q_WIDTH = ATT_HEADS * HEAD_DIM
KV_WIDTH = ATT_KV_HEADS * HEAD_DIM
GLU_WIDTH = 2 * CONV_CH
XQ_WIDTH = MEM_HEADS * MEM_HEAD_DIM
GATE_WIDTH = N_BRANCHES * D_MODEL

LANES = 128
TQ = 256
CONV_HALO = 32
TM = 256
RT_ROWS = 32
VMEM_LIMIT = 56 * 1024 * 1024


def _dot(a, b):
    return jnp.dot(a, b, preferred_element_type=F32)


def _dot_nt(a, b):
    return lax.dot_general(a, b, (((1,), (1,)), ((), ())), preferred_element_type=F32)


def _split_bf16(t):
    hi = t.astype(BF16)
    lo = (t - hi.astype(F32)).astype(BF16)
    return hi, lo


def _block_sum(t, bd):
    hi, lo = _split_bf16(t)
    return _dot(hi, bd) + _dot(lo, bd)


def _rms(t):
    return t * lax.rsqrt(jnp.mean(t * t, axis=-1, keepdims=True) + EPS)


def _memkv_kernel(mem_ref, gmem_ref, wkv_ref, gxk_ref, bd_ref, mk_ref, mv_ref):
    mn = (_rms(mem_ref[0]) * gmem_ref[...]).astype(BF16)
    kv = _dot(mn, wkv_ref[...])
    mk = kv[:, :XQ_WIDTH]
    ss = _block_sum(mk * mk, bd_ref[...])
    mk = mk * lax.rsqrt(ss * (1.0 / MEM_HEAD_DIM) + EPS) * gxk_ref[...]
    mk_ref[0] = mk.astype(BF16)
    mv_ref[0] = kv[:, XQ_WIDTH:].astype(BF16)


def _mixer_kernel(x_ref, cos_ref, sin_ref, mk_ref, mv_ref, g1_ref, wqkv_ref, wglu_ref, wxq_ref,
                  wgt_ref, gq_ref, gk_ref, sinks_ref, bd64_ref, bd128_ref, woa_ref, wdw_ref,
                  bdw_ref, gln_ref, bln_ref, wco_ref, gxq_ref, wom_ref, wout_ref, g2_ref,
                  wrt_ref, brt_ref,
                  x1_ref, rt_ref,
                  k0_ref, k1_ref, v0_ref, v1_ref, ubuf_ref, ybuf_ref, o_ref):
    i = pl.program_id(1)
    kbands = (k0_ref, k1_ref)
    vbands = (v0_ref, v1_ref)

    @pl.when(i == 0)
    def _():
        for r in kbands + vbands:
            r[0:WINDOW, :] = jnp.zeros((WINDOW, LANES), BF16)
        ubuf_ref[:, 0:CONV_HALO, :] = jnp.zeros((CONV_CH // LANES, CONV_HALO, LANES), F32)

    x = x_ref[0]
    h = (_rms(x) * g1_ref[...]).astype(BF16)

    qkv = _dot(h, wqkv_ref[...])
    q = qkv[:, :Q_WIDTH]
    k = qkv[:, Q_WIDTH:Q_WIDTH + KV_WIDTH]
    v = qkv[:, Q_WIDTH + KV_WIDTH:]
    bd64 = bd64_ref[...]
    q = q * lax.rsqrt(_block_sum(q * q, bd64) * (1.0 / HEAD_DIM) + EPS) * gq_ref[...]
    k = k * lax.rsqrt(_block_sum(k * k, bd64[:LANES, :LANES]) * (1.0 / HEAD_DIM) + EPS) * gk_ref[...]

    cosd = cos_ref[0]
    sind = sin_ref[0]
    lane = lax.broadcasted_iota(jnp.int32, (TQ, LANES), 1)
    first_half = (lane % HEAD_DIM) < (HEAD_DIM // 2)
    low_head = lane < HEAD_DIM

    def rope(t):
        rot = jnp.where(first_half, pltpu.roll(t, LANES - HEAD_DIM // 2, 1),
                        pltpu.roll(t, HEAD_DIM // 2, 1))
        return t * cosd + rot * sind

    def dup_halves(t):
        swapped = pltpu.roll(t, HEAD_DIM, 1)
        return jnp.where(low_head, t, swapped), jnp.where(low_head, swapped, t)

    kd = dup_halves(rope(k))
    vd = dup_halves(v)
    for kvh in range(ATT_KV_HEADS):
        kbands[kvh][WINDOW:WINDOW + TQ, :] = kd[kvh].astype(BF16)
        vbands[kvh][WINDOW:WINDOW + TQ, :] = vd[kvh].astype(BF16)

    q_heads = []
    for c in range(Q_WIDTH // LANES):
        qc = rope(q[:, c * LANES:(c + 1) * LANES])
        q_heads.append(jnp.where(low_head, qc, 0.0).astype(BF16))
        q_heads.append(jnp.where(low_head, 0.0, qc).astype(BF16))

    qi = lax.broadcasted_iota(jnp.int32, (WINDOW, 2 * WINDOW), 0)
    kj = lax.broadcasted_iota(jnp.int32, (WINDOW, 2 * WINDOW), 1)
    band_mask = (kj > qi) & (kj <= qi + WINDOW)
    first_mask = band_mask & (kj >= jnp.where(i > 0, 0, WINDOW))
    low_head_w = lax.broadcasted_iota(jnp.int32, (WINDOW, LANES), 1) < HEAD_DIM
    group = ATT_HEADS // ATT_KV_HEADS

    for n in range(TQ // WINDOW):
        rows = slice(n * WINDOW, (n + 1) * WINDOW)
        mask = first_mask if n == 0 else band_mask
        for kvh in range(ATT_KV_HEADS):
            heads = range(kvh * group, (kvh + 1) * group)
            q_st = jnp.concatenate([q_heads[hd][rows] for hd in heads], axis=0)
            kband = kbands[kvh][n * WINDOW:n * WINDOW + 2 * WINDOW, :]
            vband = vbands[kvh][n * WINDOW:n * WINDOW + 2 * WINDOW, :]
            s_all = _dot_nt(q_st, kband)
            es, dens = [], []
            for j, hd in enumerate(heads):
                s = jnp.where(mask, s_all[j * WINDOW:(j + 1) * WINDOW], NEG_INF)
                sink = sinks_ref[hd]
                m = jnp.maximum(jnp.max(s, axis=-1, keepdims=True), sink)
                e = jnp.exp(s - m)
                dens.append(jnp.sum(e, axis=-1, keepdims=True) + jnp.exp(sink - m))
                es.append(e.astype(BF16))
            o_all = _dot(jnp.concatenate(es, axis=0), vband)
            for cc in range(group // 2):
                oa = o_all[(2 * cc) * WINDOW:(2 * cc + 1) * WINDOW] / dens[2 * cc]
                ob = o_all[(2 * cc + 1) * WINDOW:(2 * cc + 2) * WINDOW] / dens[2 * cc + 1]
                col = kvh * (group // 2) + cc
                o_ref[rows, col * LANES:(col + 1) * LANES] = jnp.where(low_head_w, oa, ob).astype(BF16)

    for r in kbands + vbands:
        r[0:WINDOW, :] = r[TQ:TQ + WINDOW, :]

    y_attn = _dot(o_ref[...], woa_ref[...])
    merged = jax.nn.sigmoid(_dot(h, wgt_ref[:, 0:D_MODEL])) * y_attn

    glu = _dot(h, wglu_ref[...])
    u = glu[:, :CONV_CH] * jax.nn.sigmoid(glu[:, CONV_CH:])
    n_slabs = CONV_CH // LANES
    half = TQ // 2
    for c in range(n_slabs):
        ubuf_ref[c, CONV_HALO:CONV_HALO + TQ, :] = u[:, c * LANES:(c + 1) * LANES]
    base = CONV_HALO - (CONV_WIDTH - 1)
    acc = [[None] * n_slabs for _ in range(2)]
    for c in range(n_slabs):
        cols = slice(c * LANES, (c + 1) * LANES)
        for par in range(2):
            a = jnp.broadcast_to(bdw_ref[:, cols], (half, LANES))
            for j in range(CONV_WIDTH):
                a = a + wdw_ref[j:j + 1, cols] * ubuf_ref[c, pl.ds(base + j + par, half, stride=2), :]
            acc[par][c] = a
        ubuf_ref[c, 0:CONV_HALO, :] = ubuf_ref[c, TQ:TQ + CONV_HALO, :]
    for par in range(2):
        row_sum = functools.reduce(jnp.add, [jnp.sum(a, axis=-1, keepdims=True) for a in acc[par]])
        mu = row_sum * (1.0 / CONV_CH)
        cen = [a - mu for a in acc[par]]
        sq_sum = functools.reduce(jnp.add, [jnp.sum(t * t, axis=-1, keepdims=True) for t in cen])
        rstd = lax.rsqrt(sq_sum * (1.0 / CONV_CH) + LN_EPS)
        for c in range(n_slabs):
            cols = slice(c * LANES, (c + 1) * LANES)
            yln = cen[c] * rstd * gln_ref[:, cols] + bln_ref[:, cols]
            ybuf_ref[c, pl.ds(par, half, stride=2), :] = yln * jax.nn.sigmoid(yln)
    uo = jnp.concatenate([ybuf_ref[c] for c in range(n_slabs)], axis=-1).astype(BF16)
    y_conv = _dot(uo, wco_ref[...])
    merged = merged + jax.nn.sigmoid(_dot(h, wgt_ref[:, D_MODEL:2 * D_MODEL])) * y_conv

    xq = _dot(h, wxq_ref[...])
    xq = xq * lax.rsqrt(_block_sum(xq * xq, bd128_ref[...]) * (1.0 / MEM_HEAD_DIM) + EPS) * gxq_ref[...]
    xq = xq.astype(BF16)
    for hd in range(MEM_HEADS):
        cols = slice(hd * MEM_HEAD_DIM, (hd + 1) * MEM_HEAD_DIM)
        s = _dot_nt(xq[:, cols], mk_ref[0, :, cols])
        m = jnp.max(s, axis=-1, keepdims=True)
        e = jnp.exp(s - m)
        den = jnp.sum(e, axis=-1, keepdims=True)
        o_ref[:, cols] = (_dot(e.astype(BF16), mv_ref[0, :, cols]) / den).astype(BF16)
    y_mem = _dot(o_ref[...], wom_ref[...])
    merged = merged + jax.nn.sigmoid(_dot(h, wgt_ref[:, 2 * D_MODEL:3 * D_MODEL])) * y_mem

    x1 = x + _dot(merged.astype(BF16), wout_ref[...])
    x1_ref[0] = x1

    h2 = _rms(x1) * g2_ref[...]
    h_hi, h_lo = _split_bf16(h2)
    w_hi, w_lo = _split_bf16(wrt_ref[...])
    lt = _dot_nt(w_hi, h_hi) + _dot_nt(w_hi, h_lo) + _dot_nt(w_lo, h_hi) + brt_ref[...]
    gl = [lt[g:g + 1] for g in range(N_GROUPS)]
    gmax = functools.reduce(jnp.maximum, gl)
    g_idx = jnp.full((1, TQ), N_GROUPS - 1, jnp.int32)
    for g in reversed(range(N_GROUPS - 1)):
        g_idx = jnp.where(gl[g] == gmax, g, g_idx)
    p_g = 1.0 / functools.reduce(jnp.add, [jnp.exp(r - gmax) for r in gl])
    el = []
    for kk in range(EXPERTS_PER_GROUP):
        row = lt[8 + kk:9 + kk]
        for g in range(1, N_GROUPS):
            off = 8 + g * EXPERTS_PER_GROUP + kk
            row = jnp.where(g_idx == g, lt[off:off + 1], row)
        el.append(row)

    def first_argmax(rows):
        top = functools.reduce(jnp.maximum, rows)
        idx = jnp.full((1, TQ), len(rows) - 1, jnp.int32)
        for kk in reversed(range(len(rows) - 1)):
            idx = jnp.where(rows[kk] == top, kk, idx)
        return top, idx

    v1, i1 = first_argmax(el)
    v2, i2 = first_argmax([jnp.where(i1 == kk, -jnp.inf, el[kk]) for kk in range(EXPERTS_PER_GROUP)])
    t = jnp.exp(v2 - v1)
    p1 = 1.0 / (1.0 + t)
    p2 = t * p1
    zero = jnp.zeros((1, TQ), F32)
    rt_ref[0] = jnp.concatenate(
        [g_idx.astype(F32), i1.astype(F32), i2.astype(F32), p_g * p1, p_g * p2, zero, zero, zero], axis=0)


FLAG_FIRST, FLAG_LAST, FLAG_VALID = 1, 2, 4


def _moe_kernel(perm_ref, tile_ref, elo_ref, ehi_ref, cls_ref, flag_ref,
                wts_ref, g2_ref, wg_lo, wu_lo, wd_lo, wg_hi, wu_hi, wd_hi, x1_hbm,
                out_hbm,
                xbuf, obuf, h2buf, gsem, ssem):
    k = pl.program_id(0)
    nk = pl.num_programs(0)
    t = tile_ref[k]
    slot = t % 2
    flags = flag_ref[k]
    n_tiles = x1_hbm.shape[0] // TM

    def gather_copy(r, tile, s):
        tok = perm_ref[tile * TM + r]
        return pltpu.make_async_copy(x1_hbm.at[pl.ds(tok, 1), :], xbuf.at[s, pl.ds(r, 1), :], gsem.at[s])

    def scatter_copy(r, tile, s):
        tok = perm_ref[tile * TM + r]
        return pltpu.make_async_copy(obuf.at[s, pl.ds(r, 1), :], out_hbm.at[pl.ds(tok, 1), :], ssem.at[s])

    def for_rows(fn):
        def body(r, carry):
            fn(r)
            return carry
        lax.fori_loop(0, TM, body, 0, unroll=8)

    @pl.when(k == 0)
    def _():
        for_rows(lambda r: gather_copy(r, t, slot).start())

    @pl.when((flags & FLAG_FIRST) != 0)
    def _():
        @pl.when(t + 1 < n_tiles)
        def _():
            for_rows(lambda r: gather_copy(r, t + 1, 1 - slot).start())

        for_rows(lambda r: gather_copy(r, t, slot).wait())

        @pl.when(t >= 2)
        def _():
            for_rows(lambda r: scatter_copy(r, t - 2, slot).wait())

        xr = xbuf[slot]
        h2buf[...] = (_rms(xr) * g2_ref[...]).astype(BF16)
        obuf[slot] = xr

    @pl.when((flags & FLAG_VALID) != 0)
    def _():
        w = wts_ref[...]
        mine = w[:, 2:3] == cls_ref[k].astype(F32)
        h2 = h2buf[...]
        y = None
        for col, wg, wu, wd in ((0, wg_lo, wu_lo, wd_lo), (1, wg_hi, wu_hi, wd_hi)):
            c = jnp.where(mine, w[:, col:col + 1], 0.0)
            hid = jax.nn.silu(_dot(h2, wg[0])) * _dot(h2, wu[0]) * c
            part = _dot(hid.astype(BF16), wd[0])
            y = part if y is None else y + part
        obuf[slot] += y

    @pl.when((flags & FLAG_LAST) != 0)
    def _():
        for_rows(lambda r: scatter_copy(r, t, slot).start())

    @pl.when(k == nk - 1)
    def _():
        for tile in (n_tiles - 2, n_tiles - 1):
            for_rows(lambda r, tile=tile: scatter_copy(r, tile, tile % 2).wait())


_CLASS_PAIRS = [(a, b) for a in range(EXPERTS_PER_GROUP) for b in range(a + 1, EXPERTS_PER_GROUP)]
N_CLASSES = N_GROUPS * len(_CLASS_PAIRS)


def _routing_plan(rt, n_tiles):
    T = rt.shape[1]
    g = rt[0].astype(jnp.int32)
    i1 = rt[1].astype(jnp.int32)
    i2 = rt[2].astype(jnp.int32)
    lo = jnp.minimum(i1, i2)
    hi = jnp.maximum(i1, i2)
    w_lo = jnp.where(i1 < i2, rt[3], rt[4])
    w_hi = jnp.where(i1 < i2, rt[4], rt[3])
    n_pairs = len(_CLASS_PAIRS)
    cls = g * n_pairs + (lo * (2 * EXPERTS_PER_GROUP - 1 - lo)) // 2 + hi - lo - 1
    onehot = (cls[:, None] == jnp.arange(N_CLASSES, dtype=jnp.int32)[None, :]).astype(jnp.int32)
    running = jnp.cumsum(onehot, axis=0)
    counts = running[-1]
    off = jnp.concatenate([jnp.zeros((1,), jnp.int32), jnp.cumsum(counts).astype(jnp.int32)])
    dest = jnp.sum(onehot * (off[None, :-1] + running - 1), axis=1)
    perm = jnp.zeros((T,), jnp.int32).at[dest].set(jnp.arange(T, dtype=jnp.int32))
    wts = jnp.zeros((T, 8), F32).at[dest, 0:3].set(jnp.stack([w_lo, w_hi, cls.astype(F32)], axis=-1))
    first_tile = off[:-1] // TM
    last_tile = (off[1:] - 1) // TM
    n_items = jnp.where(counts > 0, last_tile - first_tile + 1, 0)
    istart = jnp.concatenate([jnp.zeros((1,), jnp.int32), jnp.cumsum(n_items).astype(jnp.int32)])
    total = istart[-1]
    ni = n_tiles + N_CLASSES - 1
    k = jnp.arange(ni, dtype=jnp.int32)
    valid = k < total
    kc = jnp.minimum(k, total - 1)
    c_of_k = jnp.minimum(jnp.sum(istart[None, 1:] <= kc[:, None], axis=1).astype(jnp.int32), N_CLASSES - 1)
    tile_k = first_tile[c_of_k] + kc - istart[c_of_k]
    prev_tile = jnp.concatenate([jnp.full((1,), -1, jnp.int32), tile_k[:-1]])
    next_tile = jnp.concatenate([tile_k[1:], jnp.full((1,), -1, jnp.int32)])
    first = valid & (tile_k != prev_tile)
    last = valid & ((tile_k != next_tile) | (k == total - 1))
    flags = (first * FLAG_FIRST + last * FLAG_LAST + valid * FLAG_VALID).astype(jnp.int32)
    pair_lo = jnp.array([p[0] for p in _CLASS_PAIRS], jnp.int32)
    pair_hi = jnp.array([p[1] for p in _CLASS_PAIRS], jnp.int32)
    base = (c_of_k // n_pairs) * EXPERTS_PER_GROUP
    e_lo = base + pair_lo[c_of_k % n_pairs]
    e_hi = base + pair_hi[c_of_k % n_pairs]
    cls_k = jnp.where(valid, c_of_k, -1)
    return perm, tile_k, e_lo, e_hi, cls_k, flags, wts


def _const_spec(shape):
    nd = len(shape)
    return pl.BlockSpec(shape, lambda *_: (0,) * nd, pipeline_mode=pl.Buffered(1))


def _block_diag_ones(n, blk):
    r = jnp.arange(n) // blk
    return (r[:, None] == r[None, :]).astype(BF16)


def kernel(x, mem, positions, g_norm1, w_in, g_q, g_k, sinks, w_o_attn, w_conv_dw, b_conv_dw, g_conv_ln, b_conv_ln, w_conv_out, g_mem, w_kv_mem, g_xq, g_xk, w_o_mem, w_out, g_norm2, w_group, b_group, w_router, b_router, w_gate, w_up, w_down):
    B, S, D = x.shape
    M = mem.shape[1]
    assert D == D_MODEL and S % TQ == 0 and (B * S) % TM == 0 and w_in.shape[0] == 1
    NT = S // TQ
    T = B * S
    l = 0

    bd64 = _block_diag_ones(Q_WIDTH, HEAD_DIM)
    bd128 = _block_diag_ones(XQ_WIDTH, MEM_HEAD_DIM)
    row = lambda v: v.reshape(1, -1).astype(F32)

    mk, mv = pl.pallas_call(
        _memkv_kernel,
        grid=(B,),
        in_specs=[pl.BlockSpec((1, M, D), lambda b: (b, 0, 0)),
                  _const_spec((1, D)), _const_spec((D, 2 * XQ_WIDTH)),
                  _const_spec((1, XQ_WIDTH)), _const_spec((XQ_WIDTH, XQ_WIDTH))],
        out_specs=[pl.BlockSpec((1, M, XQ_WIDTH), lambda b: (b, 0, 0))] * 2,
        out_shape=[jax.ShapeDtypeStruct((B, M, XQ_WIDTH), BF16)] * 2,
        compiler_params=pltpu.CompilerParams(dimension_semantics=("arbitrary",)),
        name="memkv",
    )(mem, row(g_mem[l]), w_kv_mem[l].astype(BF16), row(jnp.tile(g_xk[l], MEM_HEADS)), bd128)

    inv_freq = 1.0 / (ROPE_THETA ** (jnp.arange(0, HEAD_DIM, 2, dtype=F32) / HEAD_DIM))
    ang = positions.astype(F32)[..., None] * inv_freq
    cos, sin = jnp.cos(ang), jnp.sin(ang)
    cosd = jnp.tile(cos, (1, 1, LANES // (HEAD_DIM // 2)))
    sind = jnp.tile(jnp.concatenate([-sin, sin], axis=-1), (1, 1, LANES // HEAD_DIM))

    w = w_in[l]
    c0 = Q_WIDTH + 2 * KV_WIDTH
    c1 = c0 + GLU_WIDTH
    c2 = c1 + XQ_WIDTH
    w_qkv = w[:, :c0].astype(BF16)
    w_glu = w[:, c0:c1].astype(BF16)
    w_xq = w[:, c1:c2].astype(BF16)
    w_gt = w[:, c2:].astype(BF16)
    wdw = jnp.zeros((CONV_HALO, CONV_CH), F32).at[:CONV_WIDTH].set(w_conv_dw[l])
    w_rt = jnp.zeros((RT_ROWS, D), F32).at[0:N_GROUPS].set(w_group[l].T).at[8:8 + N_EXPERTS].set(w_router[l].T)
    b_rt = jnp.zeros((RT_ROWS, 1), F32).at[0:N_GROUPS, 0].set(b_group[l]).at[8:8 + N_EXPERTS, 0].set(b_router[l])

    tile3 = lambda last: pl.BlockSpec((1, TQ, last), lambda b, i: (b, i, 0))
    per_batch = pl.BlockSpec((1, M, XQ_WIDTH), lambda b, i: (b, 0, 0))
    in_specs = [
        tile3(D), tile3(LANES), tile3(LANES), per_batch, per_batch,
        _const_spec((1, D)),
        _const_spec((D, c0)), _const_spec((D, GLU_WIDTH)), _const_spec((D, XQ_WIDTH)), _const_spec((D, GATE_WIDTH)),
        _const_spec((1, Q_WIDTH)), _const_spec((1, LANES)),
        pl.BlockSpec(memory_space=pltpu.SMEM),
        _const_spec((Q_WIDTH, Q_WIDTH)), _const_spec((XQ_WIDTH, XQ_WIDTH)),
        _const_spec((Q_WIDTH, D)),
        _const_spec((CONV_HALO, CONV_CH)), _const_spec((1, CONV_CH)), _const_spec((1, CONV_CH)), _const_spec((1, CONV_CH)),
        _const_spec((CONV_CH, D)),
        _const_spec((1, XQ_WIDTH)),
        _const_spec((XQ_WIDTH, D)),
        _const_spec((D, D)),
        _const_spec((1, D)),
        _const_spec((RT_ROWS, D)), _const_spec((RT_ROWS, 1)),
    ]
    x1, rt = pl.pallas_call(
        _mixer_kernel,
        grid=(B, NT),
        in_specs=in_specs,
        out_specs=[tile3(D), pl.BlockSpec((1, 8, TQ), lambda b, i: (b * NT + i, 0, 0))],
        out_shape=[jax.ShapeDtypeStruct((B, S, D), F32), jax.ShapeDtypeStruct((B * NT, 8, TQ), F32)],
        scratch_shapes=[pltpu.VMEM((WINDOW + TQ, LANES), BF16)] * 4
        + [pltpu.VMEM((CONV_CH // LANES, CONV_HALO + TQ, LANES), F32),
           pltpu.VMEM((CONV_CH // LANES, TQ, LANES), F32), pltpu.VMEM((TQ, Q_WIDTH), BF16)],
        compiler_params=pltpu.CompilerParams(dimension_semantics=("arbitrary", "arbitrary"),
                                             vmem_limit_bytes=VMEM_LIMIT),
        name="mixer",
    )(x, cosd, sind, mk, mv, row(g_norm1[l]), w_qkv, w_glu, w_xq, w_gt,
      row(jnp.tile(g_q[l], ATT_HEADS) * (HEAD_DIM ** -0.5)), row(jnp.tile(g_k[l], ATT_KV_HEADS)),
      sinks[l].astype(F32), bd64, bd128, w_o_attn[l].astype(BF16),
      wdw, row(b_conv_dw[l]), row(g_conv_ln[l]), row(b_conv_ln[l]), w_conv_out[l].astype(BF16),
      row(jnp.tile(g_xq[l], MEM_HEADS) * (MEM_HEAD_DIM ** -0.5)), w_o_mem[l].astype(BF16),
      w_out[l].astype(BF16), row(g_norm2[l]), w_rt, b_rt)

    n_tiles = T // TM
    plan = _routing_plan(jnp.transpose(rt, (1, 0, 2)).reshape(8, T)[:5], n_tiles)
    perm, tile_k, e_lo, e_hi, cls_k, flags, wts = plan
    ni = tile_k.shape[0]
    wg, wu, wd = w_gate[l].astype(BF16), w_up[l].astype(BF16), w_down[l].astype(BF16)
    lo_map = lambda k, perm, tile, elo, ehi, cls, flg: (elo[k], 0, 0)
    hi_map = lambda k, perm, tile, elo, ehi, cls, flg: (ehi[k], 0, 0)
    up_block = (1, D, EXPERT_FF)
    down_block = (1, EXPERT_FF, D)
    out = pl.pallas_call(
        _moe_kernel,
        grid_spec=pltpu.PrefetchScalarGridSpec(
            num_scalar_prefetch=6,
            grid=(ni,),
            in_specs=[pl.BlockSpec((TM, 8), lambda k, perm, tile, *_: (tile[k], 0)),
                      pl.BlockSpec((1, D), lambda k, *_: (0, 0)),
                      pl.BlockSpec(up_block, lo_map), pl.BlockSpec(up_block, lo_map),
                      pl.BlockSpec(down_block, lo_map),
                      pl.BlockSpec(up_block, hi_map), pl.BlockSpec(up_block, hi_map),
                      pl.BlockSpec(down_block, hi_map),
                      pl.BlockSpec(memory_space=pl.ANY)],
            out_specs=pl.BlockSpec(memory_space=pl.ANY),
            scratch_shapes=[pltpu.VMEM((2, TM, D), F32), pltpu.VMEM((2, TM, D), F32),
                            pltpu.VMEM((TM, D), BF16),
                            pltpu.SemaphoreType.DMA((2,)), pltpu.SemaphoreType.DMA((2,))]),
        out_shape=jax.ShapeDtypeStruct((T, D), F32),
        compiler_params=pltpu.CompilerParams(dimension_semantics=("arbitrary",),
                                             vmem_limit_bytes=VMEM_LIMIT),
        name="moe",
    )(perm, tile_k, e_lo, e_hi, cls_k, flags, wts, row(g_norm2[l]), wg, wu, wd, wg, wu, wd,
      x1.reshape(T, D))
    return out.reshape(B, S, D)
```

```python
import functools

import jax
import jax.numpy as jnp
from jax import lax
from jax.experimental import pallas as pl
from jax.experimental.pallas import tpu as pltpu

F32 = jnp.float32
BF16 = jnp.bfloat16

D_MODEL = 1024
ATT_HEADS = 8
ATT_KV_HEADS = 2
HEAD_DIM = 64
WINDOW = 128
ROPE_THETA = 10000.0
CONV_CH = 512
CONV_WIDTH = 31
MEM_HEADS = 4
MEM_HEAD_DIM = 128
N_BRANCHES = 3
N_GROUPS = 4
EXPERTS_PER_GROUP = 4
N_EXPERTS = N_GROUPS * EXPERTS_PER_GROUP
EXPERT_FF = 512
EPS = 1e-6
LN_EPS = 1e-5
NEG_INF = -1e30

Q_WIDTH = ATT_HEADS * HEAD_DIM
KV_WIDTH = ATT_KV_HEADS * HEAD_DIM
GLU_WIDTH = 2 * CONV_CH
XQ_WIDTH = MEM_HEADS * MEM_HEAD_DIM
GATE_WIDTH = N_BRANCHES * D_MODEL

LANES = 128
TQ = 256
CONV_HALO = 32
TM = 256
RT_ROWS = 32
N_PAIRS = EXPERTS_PER_GROUP * (EXPERTS_PER_GROUP - 1) // 2
N_CLASSES = N_GROUPS * N_PAIRS
CLS_ROWS = 32
XW = D_MODEL + LANES
VMEM_LIMIT = 56 * 1024 * 1024


def _dot(a, b):
    return jnp.dot(a, b, preferred_element_type=F32)


def _dot_nt(a, b):
    return lax.dot_general(a, b, (((1,), (1,)), ((), ())), preferred_element_type=F32)


def _split_bf16(t):
    hi = t.astype(BF16)
    lo = (t - hi.astype(F32)).astype(BF16)
    return hi, lo


def _block_sum(t, bd):
    hi, lo = _split_bf16(t)
    return _dot(hi, bd) + _dot(lo, bd)


def _rms(t):
    return t * lax.rsqrt(jnp.mean(t * t, axis=-1, keepdims=True) + EPS)


def _memkv_kernel(mem_ref, gmem_ref, wkv_ref, gxk_ref, bd_ref, mk_ref, mv_ref):
    mn = (_rms(mem_ref[0]) * gmem_ref[...]).astype(BF16)
    kv = _dot(mn, wkv_ref[...])
    mk = kv[:, :XQ_WIDTH]
    ss = _block_sum(mk * mk, bd_ref[...])
    mk = mk * lax.rsqrt(ss * (1.0 / MEM_HEAD_DIM) + EPS) * gxk_ref[...]
    mk_ref[0] = mk.astype(BF16)
    mv_ref[0] = kv[:, XQ_WIDTH:].astype(BF16)


def _mixer_kernel(x_ref, cos_ref, sin_ref, mk_ref, mv_ref, g1_ref, wqkv_ref, wglu_ref, wxq_ref,
                  wgt_ref, gq_ref, gk_ref, sinks_ref, bd64_ref, bd128_ref, woa_ref, wdw_ref,
                  bdw_ref, gln_ref, bln_ref, wco_ref, gxq_ref, wom_ref, wout_ref, g2_ref,
                  wrt_ref, brt_ref, tri_ref,
                  x1_ref, rt_ref, cnt_ref,
                  k0_ref, k1_ref, v0_ref, v1_ref, ubuf_ref, ybuf_ref, o_ref):
    i = pl.program_id(1)
    kbands = (k0_ref, k1_ref)
    vbands = (v0_ref, v1_ref)

    @pl.when(i == 0)
    def _():
        for r in kbands + vbands:
            r[0:WINDOW, :] = jnp.zeros((WINDOW, LANES), BF16)
        ubuf_ref[:, 0:CONV_HALO, :] = jnp.zeros((CONV_CH // LANES, CONV_HALO, LANES), F32)

    x = x_ref[0]
    h = (_rms(x) * g1_ref[...]).astype(BF16)

    qkv = _dot(h, wqkv_ref[...])
    q = qkv[:, :Q_WIDTH]
    k = qkv[:, Q_WIDTH:Q_WIDTH + KV_WIDTH]
    v = qkv[:, Q_WIDTH + KV_WIDTH:]
    bd64 = bd64_ref[...]
    q = q * lax.rsqrt(_block_sum(q * q, bd64) * (1.0 / HEAD_DIM) + EPS) * gq_ref[...]
    k = k * lax.rsqrt(_block_sum(k * k, bd64[:LANES, :LANES]) * (1.0 / HEAD_DIM) + EPS) * gk_ref[...]

    cosd = cos_ref[0]
    sind = sin_ref[0]
    lane = lax.broadcasted_iota(jnp.int32, (TQ, LANES), 1)
    first_half = (lane % HEAD_DIM) < (HEAD_DIM // 2)
    low_head = lane < HEAD_DIM

    def rope(t):
        rot = jnp.where(first_half, pltpu.roll(t, LANES - HEAD_DIM // 2, 1),
                        pltpu.roll(t, HEAD_DIM // 2, 1))
        return t * cosd + rot * sind

    def dup_halves(t):
        swapped = pltpu.roll(t, HEAD_DIM, 1)
        return jnp.where(low_head, t, swapped), jnp.where(low_head, swapped, t)

    kd = dup_halves(rope(k))
    vd = dup_halves(v)
    for kvh in range(ATT_KV_HEADS):
        kbands[kvh][WINDOW:WINDOW + TQ, :] = kd[kvh].astype(BF16)
        vbands[kvh][WINDOW:WINDOW + TQ, :] = vd[kvh].astype(BF16)

    q_heads = []
    for c in range(Q_WIDTH // LANES):
        qc = rope(q[:, c * LANES:(c + 1) * LANES])
        q_heads.append(jnp.where(low_head, qc, 0.0).astype(BF16))
        q_heads.append(jnp.where(low_head, 0.0, qc).astype(BF16))

    qi = lax.broadcasted_iota(jnp.int32, (WINDOW, 2 * WINDOW), 0)
    kj = lax.broadcasted_iota(jnp.int32, (WINDOW, 2 * WINDOW), 1)
    band_mask = (kj > qi) & (kj <= qi + WINDOW)
    first_mask = band_mask & (kj >= jnp.where(i > 0, 0, WINDOW))
    low_head_w = lax.broadcasted_iota(jnp.int32, (WINDOW, LANES), 1) < HEAD_DIM
    group = ATT_HEADS // ATT_KV_HEADS

    for n in range(TQ // WINDOW):
        rows = slice(n * WINDOW, (n + 1) * WINDOW)
        mask = first_mask if n == 0 else band_mask
        for kvh in range(ATT_KV_HEADS):
            heads = range(kvh * group, (kvh + 1) * group)
            q_st = jnp.concatenate([q_heads[hd][rows] for hd in heads], axis=0)
            kband = kbands[kvh][n * WINDOW:n * WINDOW + 2 * WINDOW, :]
            vband = vbands[kvh][n * WINDOW:n * WINDOW + 2 * WINDOW, :]
            s_all = _dot_nt(q_st, kband)
            es, dens = [], []
            for j, hd in enumerate(heads):
                s = jnp.where(mask, s_all[j * WINDOW:(j + 1) * WINDOW], NEG_INF)
                sink = sinks_ref[hd]
                m = jnp.maximum(jnp.max(s, axis=-1, keepdims=True), sink)
                e = jnp.exp(s - m)
                dens.append(jnp.sum(e, axis=-1, keepdims=True) + jnp.exp(sink - m))
                es.append(e.astype(BF16))
            o_all = _dot(jnp.concatenate(es, axis=0), vband)
            for cc in range(group // 2):
                oa = o_all[(2 * cc) * WINDOW:(2 * cc + 1) * WINDOW] / dens[2 * cc]
                ob = o_all[(2 * cc + 1) * WINDOW:(2 * cc + 2) * WINDOW] / dens[2 * cc + 1]
                col = kvh * (group // 2) + cc
                o_ref[rows, col * LANES:(col + 1) * LANES] = jnp.where(low_head_w, oa, ob).astype(BF16)

    for r in kbands + vbands:
        r[0:WINDOW, :] = r[TQ:TQ + WINDOW, :]

    y_attn = _dot(o_ref[...], woa_ref[...])
    merged = jax.nn.sigmoid(_dot(h, wgt_ref[:, 0:D_MODEL])) * y_attn

    glu = _dot(h, wglu_ref[...])
    u = glu[:, :CONV_CH] * jax.nn.sigmoid(glu[:, CONV_CH:])
    n_slabs = CONV_CH // LANES
    half = TQ // 2
    for c in range(n_slabs):
        ubuf_ref[c, CONV_HALO:CONV_HALO + TQ, :] = u[:, c * LANES:(c + 1) * LANES]
    base = CONV_HALO - (CONV_WIDTH - 1)
    acc = [[None] * n_slabs for _ in range(2)]
    for c in range(n_slabs):
        cols = slice(c * LANES, (c + 1) * LANES)
        for par in range(2):
            a = jnp.broadcast_to(bdw_ref[:, cols], (half, LANES))
            for j in range(CONV_WIDTH):
                a = a + wdw_ref[j:j + 1, cols] * ubuf_ref[c, pl.ds(base + j + par, half, stride=2), :]
            acc[par][c] = a
        ubuf_ref[c, 0:CONV_HALO, :] = ubuf_ref[c, TQ:TQ + CONV_HALO, :]
    for par in range(2):
        row_sum = functools.reduce(jnp.add, [jnp.sum(a, axis=-1, keepdims=True) for a in acc[par]])
        mu = row_sum * (1.0 / CONV_CH)
        cen = [a - mu for a in acc[par]]
        sq_sum = functools.reduce(jnp.add, [jnp.sum(t * t, axis=-1, keepdims=True) for t in cen])
        rstd = lax.rsqrt(sq_sum * (1.0 / CONV_CH) + LN_EPS)
        for c in range(n_slabs):
            cols = slice(c * LANES, (c + 1) * LANES)
            yln = cen[c] * rstd * gln_ref[:, cols] + bln_ref[:, cols]
            ybuf_ref[c, pl.ds(par, half, stride=2), :] = yln * jax.nn.sigmoid(yln)
    uo = jnp.concatenate([ybuf_ref[c] for c in range(n_slabs)], axis=-1).astype(BF16)
    y_conv = _dot(uo, wco_ref[...])
    merged = merged + jax.nn.sigmoid(_dot(h, wgt_ref[:, D_MODEL:2 * D_MODEL])) * y_conv

    xq = _dot(h, wxq_ref[...])
    xq = xq * lax.rsqrt(_block_sum(xq * xq, bd128_ref[...]) * (1.0 / MEM_HEAD_DIM) + EPS) * gxq_ref[...]
    xq = xq.astype(BF16)
    for hd in range(MEM_HEADS):
        cols = slice(hd * MEM_HEAD_DIM, (hd + 1) * MEM_HEAD_DIM)
        s = _dot_nt(xq[:, cols], mk_ref[0, :, cols])
        m = jnp.max(s, axis=-1, keepdims=True)
        e = jnp.exp(s - m)
        den = jnp.sum(e, axis=-1, keepdims=True)
        o_ref[:, cols] = (_dot(e.astype(BF16), mv_ref[0, :, cols]) / den).astype(BF16)
    y_mem = _dot(o_ref[...], wom_ref[...])
    merged = merged + jax.nn.sigmoid(_dot(h, wgt_ref[:, 2 * D_MODEL:3 * D_MODEL])) * y_mem

    x1 = x + _dot(merged.astype(BF16), wout_ref[...])
    x1_ref[0, :, 0:D_MODEL] = x1

    h2 = _rms(x1) * g2_ref[...]
    h_hi, h_lo = _split_bf16(h2)
    w_hi, w_lo = _split_bf16(wrt_ref[...])
    lt = _dot_nt(w_hi, h_hi) + _dot_nt(w_hi, h_lo) + _dot_nt(w_lo, h_hi) + brt_ref[...]
    gl = [lt[g:g + 1] for g in range(N_GROUPS)]
    gmax = functools.reduce(jnp.maximum, gl)
    g_idx = jnp.full((1, TQ), N_GROUPS - 1, jnp.int32)
    for g in reversed(range(N_GROUPS - 1)):
        g_idx = jnp.where(gl[g] == gmax, g, g_idx)
    p_g = 1.0 / functools.reduce(jnp.add, [jnp.exp(r - gmax) for r in gl])
    el = []
    for kk in range(EXPERTS_PER_GROUP):
        row = lt[8 + kk:9 + kk]
        for g in range(1, N_GROUPS):
            off = 8 + g * EXPERTS_PER_GROUP + kk
            row = jnp.where(g_idx == g, lt[off:off + 1], row)
        el.append(row)

    def first_argmax(rows):
        top = functools.reduce(jnp.maximum, rows)
        idx = jnp.full((1, TQ), len(rows) - 1, jnp.int32)
        for kk in reversed(range(len(rows) - 1)):
            idx = jnp.where(rows[kk] == top, kk, idx)
        return top, idx

    v1, i1 = first_argmax(el)
    v2, i2 = first_argmax([jnp.where(i1 == kk, -jnp.inf, el[kk]) for kk in range(EXPERTS_PER_GROUP)])
    t = jnp.exp(v2 - v1)
    p1 = 1.0 / (1.0 + t)
    p2 = t * p1

    lo = jnp.minimum(i1, i2)
    hi = jnp.maximum(i1, i2)
    w_lo = jnp.where(i1 < i2, p_g * p1, p_g * p2)
    w_hi = jnp.where(i1 < i2, p_g * p2, p_g * p1)
    cls = g_idx * N_PAIRS + ((lo * (2 * EXPERTS_PER_GROUP - 1 - lo)) >> 1) + hi - lo - 1
    cls_f = cls.astype(F32)
    onehot = jnp.broadcast_to(cls, (CLS_ROWS, TQ)) == lax.broadcasted_iota(jnp.int32, (CLS_ROWS, TQ), 0)
    onehot_bf = onehot.astype(F32).astype(BF16)
    prefix = _dot(onehot_bf, tri_ref[...])
    rank = jnp.sum(jnp.where(onehot, prefix, 0.0), axis=0, keepdims=True) - 1.0
    counts = _dot_nt(jnp.ones((8, TQ), BF16), onehot_bf)
    zero = jnp.zeros((1, TQ), F32)
    rt_ref[0] = jnp.concatenate([cls_f, rank] + [zero] * 6, axis=0)
    cnt_ref[0] = jnp.concatenate([counts, jnp.zeros((8, LANES - CLS_ROWS), F32)], axis=1)
    cols = jnp.concatenate([w_lo, w_hi, cls_f, jnp.zeros((LANES - 3, TQ), F32)], axis=0)
    x1_ref[0, :, D_MODEL:] = cols.T


FLAG_FIRST, FLAG_LAST, FLAG_VALID = 1, 2, 4


def _moe_kernel(dest_ref, tile_ref, elo_ref, ehi_ref, cls_ref, flag_ref,
                g2_ref, wg_lo, wu_lo, wd_lo, wg_hi, wu_hi, wd_hi, x1_hbm,
                out_hbm,
                perm_ref, xbuf, obuf, h2buf, gsem, ssem):
    k = pl.program_id(0)
    nk = pl.num_programs(0)
    t = tile_ref[k]
    slot = t % 2
    flags = flag_ref[k]
    n_tok = x1_hbm.shape[0]
    n_tiles = n_tok // TM

    def gather_copy(r, tile, s):
        tok = perm_ref[tile * TM + r]
        return pltpu.make_async_copy(x1_hbm.at[pl.ds(tok, 1), :], xbuf.at[s, pl.ds(r, 1), :], gsem.at[s])

    def scatter_copy(r, tile, s):
        tok = perm_ref[tile * TM + r]
        return pltpu.make_async_copy(obuf.at[s, pl.ds(r, 1), :], out_hbm.at[pl.ds(tok, 1), :], ssem.at[s])

    def for_rows(fn):
        def body(r, carry):
            fn(r)
            return carry
        lax.fori_loop(0, TM, body, 0, unroll=8)

    @pl.when(k == 0)
    def _():
        def invert(tok, carry):
            perm_ref[dest_ref[tok]] = tok
            return carry
        lax.fori_loop(0, n_tok, invert, 0, unroll=8)
        for_rows(lambda r: gather_copy(r, t, slot).start())

    @pl.when((flags & FLAG_FIRST) != 0)
    def _():
        @pl.when(t + 1 < n_tiles)
        def _():
            for_rows(lambda r: gather_copy(r, t + 1, 1 - slot).start())

        for_rows(lambda r: gather_copy(r, t, slot).wait())

        @pl.when(t >= 2)
        def _():
            for_rows(lambda r: scatter_copy(r, t - 2, slot).wait())

        xr = xbuf[slot, :, 0:D_MODEL]
        h2buf[...] = (_rms(xr) * g2_ref[...]).astype(BF16)
        obuf[slot] = xr

    @pl.when((flags & FLAG_VALID) != 0)
    def _():
        w = xbuf[slot, :, D_MODEL:XW]
        mine = w[:, 2:3] == cls_ref[k].astype(F32)
        h2 = h2buf[...]
        y = None
        for col, wg, wu, wd in ((0, wg_lo, wu_lo, wd_lo), (1, wg_hi, wu_hi, wd_hi)):
            c = jnp.where(mine, w[:, col:col + 1], 0.0)
            hid = jax.nn.silu(_dot(h2, wg[0])) * _dot(h2, wu[0]) * c
            part = _dot(hid.astype(BF16), wd[0])
            y = part if y is None else y + part
        obuf[slot] += y

    @pl.when((flags & FLAG_LAST) != 0)
    def _():
        for_rows(lambda r: scatter_copy(r, t, slot).start())

    @pl.when(k == nk - 1)
    def _():
        for tile in (n_tiles - 2, n_tiles - 1):
            for_rows(lambda r, tile=tile: scatter_copy(r, tile, tile % 2).wait())


_CLASS_PAIRS = [(a, b) for a in range(EXPERTS_PER_GROUP) for b in range(a + 1, EXPERTS_PER_GROUP)]


def _routing_plan(cls, rank, tile_counts, n_tiles):
    counts = jnp.sum(tile_counts, axis=0)
    off = jnp.concatenate([jnp.zeros((1,), jnp.int32), jnp.cumsum(counts).astype(jnp.int32)])
    tile_base = jnp.cumsum(tile_counts, axis=0) - tile_counts + off[None, :-1]
    onehot = cls[:, :, None] == jnp.arange(N_CLASSES, dtype=jnp.int32)[None, None, :]
    dest = (jnp.sum(jnp.where(onehot, tile_base[:, None, :], 0), axis=-1) + rank).reshape(-1)
    first_tile = off[:-1] // TM
    last_tile = (off[1:] - 1) // TM
    n_items = jnp.where(counts > 0, last_tile - first_tile + 1, 0)
    istart = jnp.concatenate([jnp.zeros((1,), jnp.int32), jnp.cumsum(n_items).astype(jnp.int32)])
    total = istart[-1]
    ni = n_tiles + N_CLASSES - 1
    k = jnp.arange(ni, dtype=jnp.int32)
    valid = k < total
    kc = jnp.minimum(k, total - 1)
    c_of_k = jnp.minimum(jnp.sum(istart[None, 1:] <= kc[:, None], axis=1).astype(jnp.int32), N_CLASSES - 1)
    tile_k = first_tile[c_of_k] + kc - istart[c_of_k]
    prev_tile = jnp.concatenate([jnp.full((1,), -1, jnp.int32), tile_k[:-1]])
    next_tile = jnp.concatenate([tile_k[1:], jnp.full((1,), -1, jnp.int32)])
    first = valid & (tile_k != prev_tile)
    last = valid & ((tile_k != next_tile) | (k == total - 1))
    flags = (first * FLAG_FIRST + last * FLAG_LAST + valid * FLAG_VALID).astype(jnp.int32)
    pair_lo = jnp.array([p[0] for p in _CLASS_PAIRS], jnp.int32)
    pair_hi = jnp.array([p[1] for p in _CLASS_PAIRS], jnp.int32)
    base = (c_of_k // N_PAIRS) * EXPERTS_PER_GROUP
    e_lo = base + pair_lo[c_of_k % N_PAIRS]
    e_hi = base + pair_hi[c_of_k % N_PAIRS]
    cls_k = jnp.where(valid, c_of_k, -1)
    return dest.astype(jnp.int32), tile_k, e_lo, e_hi, cls_k, flags


def _const_spec(shape):
    nd = len(shape)
    return pl.BlockSpec(shape, lambda *_: (0,) * nd, pipeline_mode=pl.Buffered(1))


def _block_diag_ones(n, blk):
    r = jnp.arange(n) // blk
    return (r[:, None] == r[None, :]).astype(BF16)


def kernel(x, mem, positions, g_norm1, w_in, g_q, g_k, sinks, w_o_attn, w_conv_dw, b_conv_dw, g_conv_ln, b_conv_ln, w_conv_out, g_mem, w_kv_mem, g_xq, g_xk, w_o_mem, w_out, g_norm2, w_group, b_group, w_router, b_router, w_gate, w_up, w_down):
    B, S, D = x.shape
    M = mem.shape[1]
    assert D == D_MODEL and S % TQ == 0 and (B * S) % TM == 0 and w_in.shape[0] == 1
    NT = S // TQ
    T = B * S
    l = 0

    bd64 = _block_diag_ones(Q_WIDTH, HEAD_DIM)
    bd128 = _block_diag_ones(XQ_WIDTH, MEM_HEAD_DIM)
    row = lambda v: v.reshape(1, -1).astype(F32)

    mk, mv = pl.pallas_call(
        _memkv_kernel,
        grid=(B,),
        in_specs=[pl.BlockSpec((1, M, D), lambda b: (b, 0, 0)),
                  _const_spec((1, D)), _const_spec((D, 2 * XQ_WIDTH)),
                  _const_spec((1, XQ_WIDTH)), _const_spec((XQ_WIDTH, XQ_WIDTH))],
        out_specs=[pl.BlockSpec((1, M, XQ_WIDTH), lambda b: (b, 0, 0))] * 2,
        out_shape=[jax.ShapeDtypeStruct((B, M, XQ_WIDTH), BF16)] * 2,
        compiler_params=pltpu.CompilerParams(dimension_semantics=("arbitrary",)),
        name="memkv",
    )(mem, row(g_mem[l]), w_kv_mem[l].astype(BF16), row(jnp.tile(g_xk[l], MEM_HEADS)), bd128)

    inv_freq = 1.0 / (ROPE_THETA ** (jnp.arange(0, HEAD_DIM, 2, dtype=F32) / HEAD_DIM))
    ang = positions.astype(F32)[..., None] * inv_freq
    cos, sin = jnp.cos(ang), jnp.sin(ang)
    cosd = jnp.tile(cos, (1, 1, LANES // (HEAD_DIM // 2)))
    sind = jnp.tile(jnp.concatenate([-sin, sin], axis=-1), (1, 1, LANES // HEAD_DIM))

    w = w_in[l]
    c0 = Q_WIDTH + 2 * KV_WIDTH
    c1 = c0 + GLU_WIDTH
    c2 = c1 + XQ_WIDTH
    w_qkv = w[:, :c0].astype(BF16)
    w_glu = w[:, c0:c1].astype(BF16)
    w_xq = w[:, c1:c2].astype(BF16)
    w_gt = w[:, c2:].astype(BF16)
    wdw = jnp.zeros((CONV_HALO, CONV_CH), F32).at[:CONV_WIDTH].set(w_conv_dw[l])
    w_rt = jnp.zeros((RT_ROWS, D), F32).at[0:N_GROUPS].set(w_group[l].T).at[8:8 + N_EXPERTS].set(w_router[l].T)
    b_rt = jnp.zeros((RT_ROWS, 1), F32).at[0:N_GROUPS, 0].set(b_group[l]).at[8:8 + N_EXPERTS, 0].set(b_router[l])

    tile3 = lambda last: pl.BlockSpec((1, TQ, last), lambda b, i: (b, i, 0))
    per_batch = pl.BlockSpec((1, M, XQ_WIDTH), lambda b, i: (b, 0, 0))
    in_specs = [
        tile3(D), tile3(LANES), tile3(LANES), per_batch, per_batch,
        _const_spec((1, D)),
        _const_spec((D, c0)), _const_spec((D, GLU_WIDTH)), _const_spec((D, XQ_WIDTH)), _const_spec((D, GATE_WIDTH)),
        _const_spec((1, Q_WIDTH)), _const_spec((1, LANES)),
        pl.BlockSpec(memory_space=pltpu.SMEM),
        _const_spec((Q_WIDTH, Q_WIDTH)), _const_spec((XQ_WIDTH, XQ_WIDTH)),
        _const_spec((Q_WIDTH, D)),
        _const_spec((CONV_HALO, CONV_CH)), _const_spec((1, CONV_CH)), _const_spec((1, CONV_CH)), _const_spec((1, CONV_CH)),
        _const_spec((CONV_CH, D)),
        _const_spec((1, XQ_WIDTH)),
        _const_spec((XQ_WIDTH, D)),
        _const_spec((D, D)),
        _const_spec((1, D)),
        _const_spec((RT_ROWS, D)), _const_spec((RT_ROWS, 1)), _const_spec((TQ, TQ)),
    ]
    tri = (jnp.arange(TQ)[:, None] <= jnp.arange(TQ)[None, :]).astype(BF16)
    per_tile = lambda rows, last: pl.BlockSpec((1, rows, last), lambda b, i: (b * NT + i, 0, 0))
    x1, rt, cnt = pl.pallas_call(
        _mixer_kernel,
        grid=(B, NT),
        in_specs=in_specs,
        out_specs=[tile3(XW), per_tile(8, TQ), per_tile(8, LANES)],
        out_shape=[jax.ShapeDtypeStruct((B, S, XW), F32), jax.ShapeDtypeStruct((B * NT, 8, TQ), F32),
                   jax.ShapeDtypeStruct((B * NT, 8, LANES), F32)],
        scratch_shapes=[pltpu.VMEM((WINDOW + TQ, LANES), BF16)] * 4
        + [pltpu.VMEM((CONV_CH // LANES, CONV_HALO + TQ, LANES), F32),
           pltpu.VMEM((CONV_CH // LANES, TQ, LANES), F32), pltpu.VMEM((TQ, Q_WIDTH), BF16)],
        compiler_params=pltpu.CompilerParams(dimension_semantics=("arbitrary", "arbitrary"),
                                             vmem_limit_bytes=VMEM_LIMIT),
        name="mixer",
    )(x, cosd, sind, mk, mv, row(g_norm1[l]), w_qkv, w_glu, w_xq, w_gt,
      row(jnp.tile(g_q[l], ATT_HEADS) * (HEAD_DIM ** -0.5)), row(jnp.tile(g_k[l], ATT_KV_HEADS)),
      sinks[l].astype(F32), bd64, bd128, w_o_attn[l].astype(BF16),
      wdw, row(b_conv_dw[l]), row(g_conv_ln[l]), row(b_conv_ln[l]), w_conv_out[l].astype(BF16),
      row(jnp.tile(g_xq[l], MEM_HEADS) * (MEM_HEAD_DIM ** -0.5)), w_o_mem[l].astype(BF16),
      w_out[l].astype(BF16), row(g_norm2[l]), w_rt, b_rt, tri)

    n_tiles = T // TM
    plan = _routing_plan(rt[:, 0, :].astype(jnp.int32), rt[:, 1, :].astype(jnp.int32),
                         cnt[:, 0, :N_CLASSES].astype(jnp.int32), n_tiles)
    dest, tile_k, e_lo, e_hi, cls_k, flags = plan
    ni = tile_k.shape[0]
    wg, wu, wd = w_gate[l].astype(BF16), w_up[l].astype(BF16), w_down[l].astype(BF16)
    lo_map = lambda k, dest, tile, elo, ehi, cls, flg: (elo[k], 0, 0)
    hi_map = lambda k, dest, tile, elo, ehi, cls, flg: (ehi[k], 0, 0)
    up_block = (1, D, EXPERT_FF)
    down_block = (1, EXPERT_FF, D)
    out = pl.pallas_call(
        _moe_kernel,
        grid_spec=pltpu.PrefetchScalarGridSpec(
            num_scalar_prefetch=6,
            grid=(ni,),
            in_specs=[pl.BlockSpec((1, D), lambda k, *_: (0, 0)),
                      pl.BlockSpec(up_block, lo_map), pl.BlockSpec(up_block, lo_map),
                      pl.BlockSpec(down_block, lo_map),
                      pl.BlockSpec(up_block, hi_map), pl.BlockSpec(up_block, hi_map),
                      pl.BlockSpec(down_block, hi_map),
                      pl.BlockSpec(memory_space=pl.ANY)],
            out_specs=pl.BlockSpec(memory_space=pl.ANY),
            scratch_shapes=[pltpu.SMEM((T,), jnp.int32),
                            pltpu.VMEM((2, TM, XW), F32), pltpu.VMEM((2, TM, D), F32),
                            pltpu.VMEM((TM, D), BF16),
                            pltpu.SemaphoreType.DMA((2,)), pltpu.SemaphoreType.DMA((2,))]),
        out_shape=jax.ShapeDtypeStruct((T, D), F32),
        compiler_params=pltpu.CompilerParams(dimension_semantics=("arbitrary",),
                                             vmem_limit_bytes=VMEM_LIMIT),
        name="moe",
    )(dest, tile_k, e_lo, e_hi, cls_k, flags, row(g_norm2[l]), wg, wu, wd, wg, wu, wd,
      x1.reshape(T, XW))
    return out.reshape(B, S, D)
```

```python
import functools

import jax
import jax.numpy as jnp
from jax import lax
from jax.experimental import pallas as pl
from jax.experimental.pallas import tpu as pltpu

F32 = jnp.float32
BF16 = jnp.bfloat16

D_MODEL = 1024
ATT_HEADS = 8
ATT_KV_HEADS = 2
HEAD_DIM = 64
WINDOW = 128
ROPE_THETA = 10000.0
CONV_CH = 512
CONV_WIDTH = 31
MEM_HEADS = 4
MEM_HEAD_DIM = 128
N_BRANCHES = 3
N_GROUPS = 4
EXPERTS_PER_GROUP = 4
N_EXPERTS = N_GROUPS * EXPERTS_PER_GROUP
EXPERT_FF = 512
EPS = 1e-6
LN_EPS = 1e-5
NEG_INF = -1e30

Q_WIDTH = ATT_HEADS * HEAD_DIM
KV_WIDTH = ATT_KV_HEADS * HEAD_DIM
GLU_WIDTH = 2 * CONV_CH
XQ_WIDTH = MEM_HEADS * MEM_HEAD_DIM
GATE_WIDTH = N_BRANCHES * D_MODEL

LANES = 128
TQ = 256
CONV_HALO = 32
TM = 256
RT_ROWS = 32
N_PAIRS = EXPERTS_PER_GROUP * (EXPERTS_PER_GROUP - 1) // 2
N_CLASSES = N_GROUPS * N_PAIRS
CLS_ROWS = 32
XW = D_MODEL + LANES
VMEM_LIMIT = 56 * 1024 * 1024


def _dot(a, b):
    return jnp.dot(a, b, preferred_element_type=F32)


def _dot_nt(a, b):
    return lax.dot_general(a, b, (((1,), (1,)), ((), ())), preferred_element_type=F32)


def _split_bf16(t):
    hi = t.astype(BF16)
    lo = (t - hi.astype(F32)).astype(BF16)
    return hi, lo


def _block_sum(t, bd):
    return _dot(t.astype(BF16), bd)


def _rms(t):
    return t * lax.rsqrt(jnp.mean(t * t, axis=-1, keepdims=True) + EPS)


def _memkv_kernel(mem_ref, gmem_ref, wkv_ref, gxk_ref, bd_ref, mk_ref, mv_ref):
    mn = (_rms(mem_ref[0]) * gmem_ref[...]).astype(BF16)
    kv = _dot(mn, wkv_ref[...])
    mk = kv[:, :XQ_WIDTH]
    ss = _block_sum(mk * mk, bd_ref[...])
    mk = mk * lax.rsqrt(ss * (1.0 / MEM_HEAD_DIM) + EPS) * gxk_ref[...]
    mk_ref[0] = mk.astype(BF16)
    mv_ref[0] = kv[:, XQ_WIDTH:].astype(BF16)


def _mixer_kernel(x_ref, cos_ref, sin_ref, mk_ref, mv_ref, g1_ref, wqkv_ref, wglu_ref, wxq_ref,
                  wgt_ref, gq_ref, gk_ref, sinks_ref, bd64_ref, bd128_ref, woa_ref, wdw_ref,
                  bdw_ref, gln_ref, bln_ref, wco_ref, gxq_ref, wom_ref, wout_ref, g2_ref,
                  wrt_ref, brt_ref, tri_ref,
                  x1_ref, rt_ref, cnt_ref,
                  k0_ref, k1_ref, v0_ref, v1_ref, ubuf_ref, ybuf_ref, o_ref):
    i = pl.program_id(1)
    kbands = (k0_ref, k1_ref)
    vbands = (v0_ref, v1_ref)

    @pl.when(i == 0)
    def _():
        for r in kbands + vbands:
            r[0:WINDOW, :] = jnp.zeros((WINDOW, LANES), BF16)
        ubuf_ref[:, 0:CONV_HALO, :] = jnp.zeros((CONV_CH // LANES, CONV_HALO, LANES), F32)

    x = x_ref[0]
    h = (_rms(x) * g1_ref[...]).astype(BF16)

    qkv = _dot(h, wqkv_ref[...])
    q = qkv[:, :Q_WIDTH]
    k = qkv[:, Q_WIDTH:Q_WIDTH + KV_WIDTH]
    v = qkv[:, Q_WIDTH + KV_WIDTH:]
    bd64 = bd64_ref[...]
    q = q * lax.rsqrt(_block_sum(q * q, bd64) * (1.0 / HEAD_DIM) + EPS) * gq_ref[...]
    k = k * lax.rsqrt(_block_sum(k * k, bd64[:LANES, :LANES]) * (1.0 / HEAD_DIM) + EPS) * gk_ref[...]

    cosd = cos_ref[0]
    sind = sin_ref[0]
    lane = lax.broadcasted_iota(jnp.int32, (TQ, LANES), 1)
    first_half = (lane % HEAD_DIM) < (HEAD_DIM // 2)
    low_head = lane < HEAD_DIM

    def rope(t):
        rot = jnp.where(first_half, pltpu.roll(t, LANES - HEAD_DIM // 2, 1),
                        pltpu.roll(t, HEAD_DIM // 2, 1))
        return t * cosd + rot * sind

    def dup_halves(t):
        swapped = pltpu.roll(t, HEAD_DIM, 1)
        return jnp.where(low_head, t, swapped), jnp.where(low_head, swapped, t)

    kd = dup_halves(rope(k))
    vd = dup_halves(v)
    for kvh in range(ATT_KV_HEADS):
        kbands[kvh][WINDOW:WINDOW + TQ, :] = kd[kvh].astype(BF16)
        vbands[kvh][WINDOW:WINDOW + TQ, :] = vd[kvh].astype(BF16)

    q_heads = []
    for c in range(Q_WIDTH // LANES):
        qc = rope(q[:, c * LANES:(c + 1) * LANES])
        q_heads.append(jnp.where(low_head, qc, 0.0).astype(BF16))
        q_heads.append(jnp.where(low_head, 0.0, qc).astype(BF16))

    qi = lax.broadcasted_iota(jnp.int32, (WINDOW, 2 * WINDOW), 0)
    kj = lax.broadcasted_iota(jnp.int32, (WINDOW, 2 * WINDOW), 1)
    band_mask = (kj > qi) & (kj <= qi + WINDOW)
    first_mask = band_mask & (kj >= jnp.where(i > 0, 0, WINDOW))
    low_head_w = lax.broadcasted_iota(jnp.int32, (WINDOW, LANES), 1) < HEAD_DIM
    group = ATT_HEADS // ATT_KV_HEADS

    def attention_block(n, kvh):
        rows = slice(n * WINDOW, (n + 1) * WINDOW)
        mask = first_mask if n == 0 else band_mask
        heads = range(kvh * group, (kvh + 1) * group)
        q_st = jnp.concatenate([q_heads[hd][rows] for hd in heads], axis=0)
        kband = kbands[kvh][n * WINDOW:n * WINDOW + 2 * WINDOW, :]
        vband = vbands[kvh][n * WINDOW:n * WINDOW + 2 * WINDOW, :]
        s_all = _dot_nt(q_st, kband)
        es, dens = [], []
        for j, hd in enumerate(heads):
            s = jnp.where(mask, s_all[j * WINDOW:(j + 1) * WINDOW], NEG_INF)
            sink = sinks_ref[hd]
            m = jnp.maximum(jnp.max(s, axis=-1, keepdims=True), sink)
            e = jnp.exp(s - m)
            dens.append(jnp.sum(e, axis=-1, keepdims=True) + jnp.exp(sink - m))
            es.append(e.astype(BF16))
        o_all = _dot(jnp.concatenate(es, axis=0), vband)
        for cc in range(group // 2):
            oa = o_all[(2 * cc) * WINDOW:(2 * cc + 1) * WINDOW] / dens[2 * cc]
            ob = o_all[(2 * cc + 1) * WINDOW:(2 * cc + 2) * WINDOW] / dens[2 * cc + 1]
            col = kvh * (group // 2) + cc
            o_ref[rows, col * LANES:(col + 1) * LANES] = jnp.where(low_head_w, oa, ob).astype(BF16)

    glu = _dot(h, wglu_ref[...])
    u = glu[:, :CONV_CH] * jax.nn.sigmoid(glu[:, CONV_CH:])
    n_slabs = CONV_CH // LANES
    half = TQ // 2
    for c in range(n_slabs):
        ubuf_ref[c, CONV_HALO:CONV_HALO + TQ, :] = u[:, c * LANES:(c + 1) * LANES]
    base = CONV_HALO - (CONV_WIDTH - 1)
    acc = [[None] * n_slabs for _ in range(2)]

    def conv_slab(c):
        cols = slice(c * LANES, (c + 1) * LANES)
        for par in range(2):
            a = jnp.broadcast_to(bdw_ref[:, cols], (half, LANES))
            for j in range(CONV_WIDTH):
                a = a + wdw_ref[j:j + 1, cols] * ubuf_ref[c, pl.ds(base + j + par, half, stride=2), :]
            acc[par][c] = a
        ubuf_ref[c, 0:CONV_HALO, :] = ubuf_ref[c, TQ:TQ + CONV_HALO, :]

    def gate(b):
        return jax.nn.sigmoid(_dot(h, wgt_ref[:, b * D_MODEL:(b + 1) * D_MODEL]))

    gates = []
    for step in range(n_slabs):
        attention_block(step // ATT_KV_HEADS, step % ATT_KV_HEADS)
        conv_slab(step)
        if step < N_BRANCHES:
            gates.append(gate(step))

    for r in kbands + vbands:
        r[0:WINDOW, :] = r[TQ:TQ + WINDOW, :]

    y_attn = _dot(o_ref[...], woa_ref[...])
    merged = gates[0] * y_attn

    for par in range(2):
        row_sum = functools.reduce(jnp.add, [jnp.sum(a, axis=-1, keepdims=True) for a in acc[par]])
        mu = row_sum * (1.0 / CONV_CH)
        cen = [a - mu for a in acc[par]]
        sq_sum = functools.reduce(jnp.add, [jnp.sum(t * t, axis=-1, keepdims=True) for t in cen])
        rstd = lax.rsqrt(sq_sum * (1.0 / CONV_CH) + LN_EPS)
        for c in range(n_slabs):
            cols = slice(c * LANES, (c + 1) * LANES)
            yln = cen[c] * rstd * gln_ref[:, cols] + bln_ref[:, cols]
            ybuf_ref[c, pl.ds(par, half, stride=2), :] = yln * jax.nn.sigmoid(yln)
    uo = jnp.concatenate([ybuf_ref[c] for c in range(n_slabs)], axis=-1).astype(BF16)
    y_conv = _dot(uo, wco_ref[...])
    merged = merged + gates[1] * y_conv

    xq = _dot(h, wxq_ref[...])
    xq = xq * lax.rsqrt(_block_sum(xq * xq, bd128_ref[...]) * (1.0 / MEM_HEAD_DIM) + EPS) * gxq_ref[...]
    xq = xq.astype(BF16)
    for hd in range(MEM_HEADS):
        cols = slice(hd * MEM_HEAD_DIM, (hd + 1) * MEM_HEAD_DIM)
        s = _dot_nt(xq[:, cols], mk_ref[0, :, cols])
        m = jnp.max(s, axis=-1, keepdims=True)
        e = jnp.exp(s - m)
        den = jnp.sum(e, axis=-1, keepdims=True)
        o_ref[:, cols] = (_dot(e.astype(BF16), mv_ref[0, :, cols]) / den).astype(BF16)
    y_mem = _dot(o_ref[...], wom_ref[...])
    merged = merged + gates[2] * y_mem

    x1 = x + _dot(merged.astype(BF16), wout_ref[...])
    x1_ref[0, :, 0:D_MODEL] = x1

    h2 = _rms(x1) * g2_ref[...]
    h_hi, h_lo = _split_bf16(h2)
    w_hi, w_lo = _split_bf16(wrt_ref[...])
    lt = _dot_nt(w_hi, h_hi) + _dot_nt(w_hi, h_lo) + _dot_nt(w_lo, h_hi) + brt_ref[...]
    gl = [lt[g:g + 1] for g in range(N_GROUPS)]
    gmax = functools.reduce(jnp.maximum, gl)
    g_idx = jnp.full((1, TQ), N_GROUPS - 1, jnp.int32)
    for g in reversed(range(N_GROUPS - 1)):
        g_idx = jnp.where(gl[g] == gmax, g, g_idx)
    p_g = 1.0 / functools.reduce(jnp.add, [jnp.exp(r - gmax) for r in gl])
    el = []
    for kk in range(EXPERTS_PER_GROUP):
        row = lt[8 + kk:9 + kk]
        for g in range(1, N_GROUPS):
            off = 8 + g * EXPERTS_PER_GROUP + kk
            row = jnp.where(g_idx == g, lt[off:off + 1], row)
        el.append(row)

    def first_argmax(rows):
        top = functools.reduce(jnp.maximum, rows)
        idx = jnp.full((1, TQ), len(rows) - 1, jnp.int32)
        for kk in reversed(range(len(rows) - 1)):
            idx = jnp.where(rows[kk] == top, kk, idx)
        return top, idx

    v1, i1 = first_argmax(el)
    v2, i2 = first_argmax([jnp.where(i1 == kk, -jnp.inf, el[kk]) for kk in range(EXPERTS_PER_GROUP)])
    t = jnp.exp(v2 - v1)
    p1 = 1.0 / (1.0 + t)
    p2 = t * p1

    lo = jnp.minimum(i1, i2)
    hi = jnp.maximum(i1, i2)
    w_lo = jnp.where(i1 < i2, p_g * p1, p_g * p2)
    w_hi = jnp.where(i1 < i2, p_g * p2, p_g * p1)
    cls = g_idx * N_PAIRS + ((lo * (2 * EXPERTS_PER_GROUP - 1 - lo)) >> 1) + hi - lo - 1
    cls_f = cls.astype(F32)
    onehot = jnp.broadcast_to(cls, (CLS_ROWS, TQ)) == lax.broadcasted_iota(jnp.int32, (CLS_ROWS, TQ), 0)
    onehot_bf = onehot.astype(F32).astype(BF16)
    prefix = _dot(onehot_bf, tri_ref[...])
    rank = jnp.sum(jnp.where(onehot, prefix, 0.0), axis=0, keepdims=True) - 1.0
    counts = _dot_nt(jnp.ones((8, TQ), BF16), onehot_bf)
    zero = jnp.zeros((1, TQ), F32)
    rt_ref[0] = jnp.concatenate([cls_f, rank] + [zero] * 6, axis=0)
    cnt_ref[0] = jnp.concatenate([counts, jnp.zeros((8, LANES - CLS_ROWS), F32)], axis=1)
    cols = jnp.concatenate([w_lo, w_hi, cls_f, jnp.zeros((LANES - 3, TQ), F32)], axis=0)
    x1_ref[0, :, D_MODEL:] = cols.T


FLAG_FIRST, FLAG_LAST, FLAG_VALID = 1, 2, 4


def _moe_kernel(dest_ref, tile_ref, elo_ref, ehi_ref, cls_ref, flag_ref,
                g2_ref, wg_lo, wu_lo, wd_lo, wg_hi, wu_hi, wd_hi, x1_hbm,
                out_hbm,
                perm_ref, xbuf, obuf, h2buf, gsem, ssem):
    k = pl.program_id(0)
    nk = pl.num_programs(0)
    t = tile_ref[k]
    slot = t % 2
    flags = flag_ref[k]
    n_tok = x1_hbm.shape[0]
    n_tiles = n_tok // TM

    def gather_copy(r, tile, s):
        tok = perm_ref[tile * TM + r]
        return pltpu.make_async_copy(x1_hbm.at[pl.ds(tok, 1), :], xbuf.at[s, pl.ds(r, 1), :], gsem.at[s])

    def scatter_copy(r, tile, s):
        tok = perm_ref[tile * TM + r]
        return pltpu.make_async_copy(obuf.at[s, pl.ds(r, 1), :], out_hbm.at[pl.ds(tok, 1), :], ssem.at[s])

    def for_rows(fn):
        def body(r, carry):
            fn(r)
            return carry
        lax.fori_loop(0, TM, body, 0, unroll=8)

    @pl.when(k == 0)
    def _():
        def invert(tok, carry):
            perm_ref[dest_ref[tok]] = tok
            return carry
        lax.fori_loop(0, n_tok, invert, 0, unroll=8)
        for_rows(lambda r: gather_copy(r, t, slot).start())

    is_first = (flags & FLAG_FIRST) != 0
    is_valid = (flags & FLAG_VALID) != 0
    next_tile = jnp.minimum(t + 1, n_tiles - 1)

    @pl.when(is_first)
    def _():
        for_rows(lambda r: gather_copy(r, t, slot).wait())

        @pl.when(t >= 2)
        def _():
            for_rows(lambda r: scatter_copy(r, t - 2, slot).wait())

        xr = xbuf[slot, :, 0:D_MODEL]
        h2buf[...] = (_rms(xr) * g2_ref[...]).astype(BF16)
        obuf[slot] = xr

    def experts(row_dmas):
        n_stages = 6
        per_stage = -(-len(row_dmas) // n_stages)
        pending = list(row_dmas)

        def issue_some():
            for start in pending[:per_stage]:
                start()
            del pending[:per_stage]

        w = xbuf[slot, :, D_MODEL:XW]
        mine = w[:, 2:3] == cls_ref[k].astype(F32)
        h2 = h2buf[...]
        y = None
        for col, wg, wu, wd in ((0, wg_lo, wu_lo, wd_lo), (1, wg_hi, wu_hi, wd_hi)):
            c = jnp.where(mine, w[:, col:col + 1], 0.0)
            gate_act = _dot(h2, wg[0])
            issue_some()
            up_act = _dot(h2, wu[0])
            issue_some()
            hid = jax.nn.silu(gate_act) * up_act * c
            part = _dot(hid.astype(BF16), wd[0])
            issue_some()
            y = part if y is None else y + part
        obuf[slot] += y

    gathers = [functools.partial(lambda r: gather_copy(r, next_tile, 1 - slot).start(), r) for r in range(TM)]
    scatters = [functools.partial(lambda r: scatter_copy(r, t - 1, 1 - slot).start(), r) for r in range(TM)]
    interleaved = [d for pair in zip(gathers, scatters) for d in pair]

    @pl.when(is_first & (t > 0))
    def _():
        experts(interleaved)

    @pl.when(is_first & (t == 0))
    def _():
        experts(gathers)

    @pl.when(is_valid & jnp.logical_not(is_first))
    def _():
        experts([])

    @pl.when(k == nk - 1)
    def _():
        last = n_tiles - 1
        for_rows(lambda r: scatter_copy(r, last, last % 2).start())
        for tile in (last - 1, last):
            for_rows(lambda r, tile=tile: scatter_copy(r, tile, tile % 2).wait())
        for_rows(lambda r: gather_copy(r, last, 1 - last % 2).wait())


_CLASS_PAIRS = [(a, b) for a in range(EXPERTS_PER_GROUP) for b in range(a + 1, EXPERTS_PER_GROUP)]


def _routing_plan(cls, rank, tile_counts, n_tiles):
    counts = jnp.sum(tile_counts, axis=0)
    off = jnp.concatenate([jnp.zeros((1,), jnp.int32), jnp.cumsum(counts).astype(jnp.int32)])
    tile_base = jnp.cumsum(tile_counts, axis=0) - tile_counts + off[None, :-1]
    onehot = cls[:, :, None] == jnp.arange(N_CLASSES, dtype=jnp.int32)[None, None, :]
    dest = (jnp.sum(jnp.where(onehot, tile_base[:, None, :], 0), axis=-1) + rank).reshape(-1)
    first_tile = off[:-1] // TM
    last_tile = (off[1:] - 1) // TM
    n_items = jnp.where(counts > 0, last_tile - first_tile + 1, 0)
    istart = jnp.concatenate([jnp.zeros((1,), jnp.int32), jnp.cumsum(n_items).astype(jnp.int32)])
    total = istart[-1]
    ni = n_tiles + N_CLASSES - 1
    k = jnp.arange(ni, dtype=jnp.int32)
    valid = k < total
    kc = jnp.minimum(k, total - 1)
    c_of_k = jnp.minimum(jnp.sum(istart[None, 1:] <= kc[:, None], axis=1).astype(jnp.int32), N_CLASSES - 1)
    tile_k = first_tile[c_of_k] + kc - istart[c_of_k]
    prev_tile = jnp.concatenate([jnp.full((1,), -1, jnp.int32), tile_k[:-1]])
    next_tile = jnp.concatenate([tile_k[1:], jnp.full((1,), -1, jnp.int32)])
    first = valid & (tile_k != prev_tile)
    last = valid & ((tile_k != next_tile) | (k == total - 1))
    flags = (first * FLAG_FIRST + last * FLAG_LAST + valid * FLAG_VALID).astype(jnp.int32)
    pair_lo = jnp.array([p[0] for p in _CLASS_PAIRS], jnp.int32)
    pair_hi = jnp.array([p[1] for p in _CLASS_PAIRS], jnp.int32)
    base = (c_of_k // N_PAIRS) * EXPERTS_PER_GROUP
    e_lo = base + pair_lo[c_of_k % N_PAIRS]
    e_hi = base + pair_hi[c_of_k % N_PAIRS]
    cls_k = jnp.where(valid, c_of_k, -1)
    return dest.astype(jnp.int32), tile_k, e_lo, e_hi, cls_k, flags


def _const_spec(shape):
    nd = len(shape)
    return pl.BlockSpec(shape, lambda *_: (0,) * nd, pipeline_mode=pl.Buffered(1))


def _block_diag_ones(n, blk):
    r = jnp.arange(n) // blk
    return (r[:, None] == r[None, :]).astype(BF16)


def kernel(x, mem, positions, g_norm1, w_in, g_q, g_k, sinks, w_o_attn, w_conv_dw, b_conv_dw, g_conv_ln, b_conv_ln, w_conv_out, g_mem, w_kv_mem, g_xq, g_xk, w_o_mem, w_out, g_norm2, w_group, b_group, w_router, b_router, w_gate, w_up, w_down):
    B, S, D = x.shape
    M = mem.shape[1]
    assert D == D_MODEL and S % TQ == 0 and (B * S) % TM == 0 and w_in.shape[0] == 1
    NT = S // TQ
    T = B * S
    l = 0

    bd64 = _block_diag_ones(Q_WIDTH, HEAD_DIM)
    bd128 = _block_diag_ones(XQ_WIDTH, MEM_HEAD_DIM)
    row = lambda v: v.reshape(1, -1).astype(F32)

    mk, mv = pl.pallas_call(
        _memkv_kernel,
        grid=(B,),
        in_specs=[pl.BlockSpec((1, M, D), lambda b: (b, 0, 0)),
                  _const_spec((1, D)), _const_spec((D, 2 * XQ_WIDTH)),
                  _const_spec((1, XQ_WIDTH)), _const_spec((XQ_WIDTH, XQ_WIDTH))],
        out_specs=[pl.BlockSpec((1, M, XQ_WIDTH), lambda b: (b, 0, 0))] * 2,
        out_shape=[jax.ShapeDtypeStruct((B, M, XQ_WIDTH), BF16)] * 2,
        compiler_params=pltpu.CompilerParams(dimension_semantics=("arbitrary",)),
        name="memkv",
    )(mem, row(g_mem[l]), w_kv_mem[l].astype(BF16), row(jnp.tile(g_xk[l], MEM_HEADS)), bd128)

    inv_freq = 1.0 / (ROPE_THETA ** (jnp.arange(0, HEAD_DIM, 2, dtype=F32) / HEAD_DIM))
    ang = positions.astype(F32)[..., None] * inv_freq
    cos, sin = jnp.cos(ang), jnp.sin(ang)
    cosd = jnp.tile(cos, (1, 1, LANES // (HEAD_DIM // 2)))
    sind = jnp.tile(jnp.concatenate([-sin, sin], axis=-1), (1, 1, LANES // HEAD_DIM))

    w = w_in[l]
    c0 = Q_WIDTH + 2 * KV_WIDTH
    c1 = c0 + GLU_WIDTH
    c2 = c1 + XQ_WIDTH
    w_qkv = w[:, :c0].astype(BF16)
    w_glu = w[:, c0:c1].astype(BF16)
    w_xq = w[:, c1:c2].astype(BF16)
    w_gt = w[:, c2:].astype(BF16)
    wdw = jnp.zeros((CONV_HALO, CONV_CH), F32).at[:CONV_WIDTH].set(w_conv_dw[l])
    w_rt = jnp.zeros((RT_ROWS, D), F32).at[0:N_GROUPS].set(w_group[l].T).at[8:8 + N_EXPERTS].set(w_router[l].T)
    b_rt = jnp.zeros((RT_ROWS, 1), F32).at[0:N_GROUPS, 0].set(b_group[l]).at[8:8 + N_EXPERTS, 0].set(b_router[l])

    tile3 = lambda last: pl.BlockSpec((1, TQ, last), lambda b, i: (b, i, 0))
    per_batch = pl.BlockSpec((1, M, XQ_WIDTH), lambda b, i: (b, 0, 0))
    in_specs = [
        tile3(D), tile3(LANES), tile3(LANES), per_batch, per_batch,
        _const_spec((1, D)),
        _const_spec((D, c0)), _const_spec((D, GLU_WIDTH)), _const_spec((D, XQ_WIDTH)), _const_spec((D, GATE_WIDTH)),
        _const_spec((1, Q_WIDTH)), _const_spec((1, LANES)),
        pl.BlockSpec(memory_space=pltpu.SMEM),
        _const_spec((Q_WIDTH, Q_WIDTH)), _const_spec((XQ_WIDTH, XQ_WIDTH)),
        _const_spec((Q_WIDTH, D)),
        _const_spec((CONV_HALO, CONV_CH)), _const_spec((1, CONV_CH)), _const_spec((1, CONV_CH)), _const_spec((1, CONV_CH)),
        _const_spec((CONV_CH, D)),
        _const_spec((1, XQ_WIDTH)),
        _const_spec((XQ_WIDTH, D)),
        _const_spec((D, D)),
        _const_spec((1, D)),
        _const_spec((RT_ROWS, D)), _const_spec((RT_ROWS, 1)), _const_spec((TQ, TQ)),
    ]
    tri = (jnp.arange(TQ)[:, None] <= jnp.arange(TQ)[None, :]).astype(BF16)
    per_tile = lambda rows, last: pl.BlockSpec((1, rows, last), lambda b, i: (b * NT + i, 0, 0))
    x1, rt, cnt = pl.pallas_call(
        _mixer_kernel,
        grid=(B, NT),
        in_specs=in_specs,
        out_specs=[tile3(XW), per_tile(8, TQ), per_tile(8, LANES)],
        out_shape=[jax.ShapeDtypeStruct((B, S, XW), F32), jax.ShapeDtypeStruct((B * NT, 8, TQ), F32),
                   jax.ShapeDtypeStruct((B * NT, 8, LANES), F32)],
        scratch_shapes=[pltpu.VMEM((WINDOW + TQ, LANES), BF16)] * 4
        + [pltpu.VMEM((CONV_CH // LANES, CONV_HALO + TQ, LANES), F32),
           pltpu.VMEM((CONV_CH // LANES, TQ, LANES), F32), pltpu.VMEM((TQ, Q_WIDTH), BF16)],
        compiler_params=pltpu.CompilerParams(dimension_semantics=("arbitrary", "arbitrary"),
                                             vmem_limit_bytes=VMEM_LIMIT),
        name="mixer",
    )(x, cosd, sind, mk, mv, row(g_norm1[l]), w_qkv, w_glu, w_xq, w_gt,
      row(jnp.tile(g_q[l], ATT_HEADS) * (HEAD_DIM ** -0.5)), row(jnp.tile(g_k[l], ATT_KV_HEADS)),
      sinks[l].astype(F32), bd64, bd128, w_o_attn[l].astype(BF16),
      wdw, row(b_conv_dw[l]), row(g_conv_ln[l]), row(b_conv_ln[l]), w_conv_out[l].astype(BF16),
      row(jnp.tile(g_xq[l], MEM_HEADS) * (MEM_HEAD_DIM ** -0.5)), w_o_mem[l].astype(BF16),
      w_out[l].astype(BF16), row(g_norm2[l]), w_rt, b_rt, tri)

    n_tiles = T // TM
    plan = _routing_plan(rt[:, 0, :].astype(jnp.int32), rt[:, 1, :].astype(jnp.int32),
                         cnt[:, 0, :N_CLASSES].astype(jnp.int32), n_tiles)
    dest, tile_k, e_lo, e_hi, cls_k, flags = plan
    ni = tile_k.shape[0]
    wg, wu, wd = w_gate[l].astype(BF16), w_up[l].astype(BF16), w_down[l].astype(BF16)
    lo_map = lambda k, dest, tile, elo, ehi, cls, flg: (elo[k], 0, 0)
    hi_map = lambda k, dest, tile, elo, ehi, cls, flg: (ehi[k], 0, 0)
    up_block = (1, D, EXPERT_FF)
    down_block = (1, EXPERT_FF, D)
    out = pl.pallas_call(
        _moe_kernel,
        grid_spec=pltpu.PrefetchScalarGridSpec(
            num_scalar_prefetch=6,
            grid=(ni,),
            in_specs=[pl.BlockSpec((1, D), lambda k, *_: (0, 0)),
                      pl.BlockSpec(up_block, lo_map), pl.BlockSpec(up_block, lo_map),
                      pl.BlockSpec(down_block, lo_map),
                      pl.BlockSpec(up_block, hi_map), pl.BlockSpec(up_block, hi_map),
                      pl.BlockSpec(down_block, hi_map),
                      pl.BlockSpec(memory_space=pl.ANY)],
            out_specs=pl.BlockSpec(memory_space=pl.ANY),
            scratch_shapes=[pltpu.SMEM((T,), jnp.int32),
                            pltpu.VMEM((2, TM, XW), F32), pltpu.VMEM((2, TM, D), F32),
                            pltpu.VMEM((TM, D), BF16),
                            pltpu.SemaphoreType.DMA((2,)), pltpu.SemaphoreType.DMA((2,))]),
        out_shape=jax.ShapeDtypeStruct((T, D), F32),
        compiler_params=pltpu.CompilerParams(dimension_semantics=("arbitrary",),
                                             vmem_limit_bytes=VMEM_LIMIT),
        name="moe",
    )(dest, tile_k, e_lo, e_hi, cls_k, flags, row(g_norm2[l]), wg, wu, wd, wg, wu, wd,
      x1.reshape(T, XW))
    return out.reshape(B, S, D)
```

```python
import functools

import jax
import jax.numpy as jnp
from jax import lax
from jax.experimental import pallas as pl
from jax.experimental.pallas import tpu as pltpu

F32 = jnp.float32
BF16 = jnp.bfloat16

D_MODEL = 1024
ATT_HEADS = 8
ATT_KV_HEADS = 2
HEAD_DIM = 64
WINDOW = 128
ROPE_THETA = 10000.0
CONV_CH = 512
CONV_WIDTH = 31
MEM_HEADS = 4
MEM_HEAD_DIM = 128
N_BRANCHES = 3
N_GROUPS = 4
EXPERTS_PER_GROUP = 4
N_EXPERTS = N_GROUPS * EXPERTS_PER_GROUP
EXPERT_FF = 512
EPS = 1e-6
LN_EPS = 1e-5
NEG_INF = -1e30

Q_WIDTH = ATT_HEADS * HEAD_DIM
KV_WIDTH = ATT_KV_HEADS * HEAD_DIM
GLU_WIDTH = 2 * CONV_CH
XQ_WIDTH = MEM_HEADS * MEM_HEAD_DIM
GATE_WIDTH = N_BRANCHES * D_MODEL

LANES = 128
TQ = 256
CONV_HALO = 32
TM = 256
RT_ROWS = 32
N_PAIRS = EXPERTS_PER_GROUP * (EXPERTS_PER_GROUP - 1) // 2
N_CLASSES = N_GROUPS * N_PAIRS
CLS_ROWS = 32
XW = D_MODEL + LANES
VMEM_LIMIT = 56 * 1024 * 1024


def _dot(a, b):
    return jnp.dot(a, b, preferred_element_type=F32)


def _dot_nt(a, b):
    return lax.dot_general(a, b, (((1,), (1,)), ((), ())), preferred_element_type=F32)


def _split_bf16(t):
    hi = t.astype(BF16)
    lo = (t - hi.astype(F32)).astype(BF16)
    return hi, lo


def _block_sum(t, bd):
    return _dot(t.astype(BF16), bd)


def _rms(t):
    return t * lax.rsqrt(jnp.mean(t * t, axis=-1, keepdims=True) + EPS)


def _memkv_kernel(mem_ref, gmem_ref, wkv_ref, gxk_ref, bd_ref, mk_ref, mv_ref):
    mn = (_rms(mem_ref[0]) * gmem_ref[...]).astype(BF16)
    kv = _dot(mn, wkv_ref[...])
    mk = kv[:, :XQ_WIDTH]
    ss = _block_sum(mk * mk, bd_ref[...])
    mk = mk * lax.rsqrt(ss * (1.0 / MEM_HEAD_DIM) + EPS) * gxk_ref[...]
    mk_ref[0] = mk.astype(BF16)
    mv_ref[0] = kv[:, XQ_WIDTH:].astype(BF16)


def _mixer_kernel(x_ref, cos_ref, sin_ref, mk_ref, mv_ref, g1_ref, wqkv_ref, wglu_ref, wxq_ref,
                  wgt_ref, gq_ref, gk_ref, sinks_ref, bd64_ref, bd128_ref, woa_ref, wdw_ref,
                  bdw_ref, gln_ref, bln_ref, wco_ref, gxq_ref, wom_ref, wout_ref, g2_ref,
                  wrt_ref, brt_ref, tri_ref,
                  x1_ref, rt_ref, cnt_ref,
                  k0_ref, k1_ref, v0_ref, v1_ref, ubuf_ref, ybuf_ref, o_ref):
    i = pl.program_id(1)
    kbands = (k0_ref, k1_ref)
    vbands = (v0_ref, v1_ref)

    @pl.when(i == 0)
    def _():
        for r in kbands + vbands:
            r[0:WINDOW, :] = jnp.zeros((WINDOW, LANES), BF16)
        ubuf_ref[:, 0:CONV_HALO, :] = jnp.zeros((CONV_CH // LANES, CONV_HALO, LANES), F32)

    x = x_ref[0]
    h = (_rms(x) * g1_ref[...]).astype(BF16)

    qkv = _dot(h, wqkv_ref[...])
    q = qkv[:, :Q_WIDTH]
    k = qkv[:, Q_WIDTH:Q_WIDTH + KV_WIDTH]
    v = qkv[:, Q_WIDTH + KV_WIDTH:]
    bd64 = bd64_ref[...]
    q = q * lax.rsqrt(_block_sum(q * q, bd64) * (1.0 / HEAD_DIM) + EPS) * gq_ref[...]
    k = k * lax.rsqrt(_block_sum(k * k, bd64[:LANES, :LANES]) * (1.0 / HEAD_DIM) + EPS) * gk_ref[...]

    cosd = cos_ref[0]
    sind = sin_ref[0]
    lane = lax.broadcasted_iota(jnp.int32, (TQ, LANES), 1)
    first_half = (lane % HEAD_DIM) < (HEAD_DIM // 2)
    low_head = lane < HEAD_DIM

    def rope(t):
        rot = jnp.where(first_half, pltpu.roll(t, LANES - HEAD_DIM // 2, 1),
                        pltpu.roll(t, HEAD_DIM // 2, 1))
        return t * cosd + rot * sind

    def dup_halves(t):
        swapped = pltpu.roll(t, HEAD_DIM, 1)
        return jnp.where(low_head, t, swapped), jnp.where(low_head, swapped, t)

    kd = dup_halves(rope(k))
    vd = dup_halves(v)
    for kvh in range(ATT_KV_HEADS):
        kbands[kvh][WINDOW:WINDOW + TQ, :] = kd[kvh].astype(BF16)
        vbands[kvh][WINDOW:WINDOW + TQ, :] = vd[kvh].astype(BF16)

    q_heads = []
    for c in range(Q_WIDTH // LANES):
        qc = rope(q[:, c * LANES:(c + 1) * LANES])
        q_heads.append(jnp.where(low_head, qc, 0.0).astype(BF16))
        q_heads.append(jnp.where(low_head, 0.0, qc).astype(BF16))

    qi = lax.broadcasted_iota(jnp.int32, (WINDOW, 2 * WINDOW), 0)
    kj = lax.broadcasted_iota(jnp.int32, (WINDOW, 2 * WINDOW), 1)
    band_mask = (kj > qi) & (kj <= qi + WINDOW)
    first_mask = band_mask & (kj >= jnp.where(i > 0, 0, WINDOW))
    low_head_w = lax.broadcasted_iota(jnp.int32, (WINDOW, LANES), 1) < HEAD_DIM
    group = ATT_HEADS // ATT_KV_HEADS

    def attention_block(n, kvh):
        rows = slice(n * WINDOW, (n + 1) * WINDOW)
        mask = first_mask if n == 0 else band_mask
        heads = range(kvh * group, (kvh + 1) * group)
        q_st = jnp.concatenate([q_heads[hd][rows] for hd in heads], axis=0)
        kband = kbands[kvh][n * WINDOW:n * WINDOW + 2 * WINDOW, :]
        vband = vbands[kvh][n * WINDOW:n * WINDOW + 2 * WINDOW, :]
        s_all = _dot_nt(q_st, kband)
        es, dens = [], []
        for j, hd in enumerate(heads):
            s = jnp.where(mask, s_all[j * WINDOW:(j + 1) * WINDOW], NEG_INF)
            sink = sinks_ref[hd]
            m = jnp.maximum(jnp.max(s, axis=-1, keepdims=True), sink)
            e = jnp.exp(s - m)
            dens.append(jnp.sum(e, axis=-1, keepdims=True) + jnp.exp(sink - m))
            es.append(e.astype(BF16))
        o_all = _dot(jnp.concatenate(es, axis=0), vband)
        for cc in range(group // 2):
            oa = o_all[(2 * cc) * WINDOW:(2 * cc + 1) * WINDOW] / dens[2 * cc]
            ob = o_all[(2 * cc + 1) * WINDOW:(2 * cc + 2) * WINDOW] / dens[2 * cc + 1]
            col = kvh * (group // 2) + cc
            o_ref[rows, col * LANES:(col + 1) * LANES] = jnp.where(low_head_w, oa, ob).astype(BF16)

    glu = _dot(h, wglu_ref[...])
    u = glu[:, :CONV_CH] * jax.nn.sigmoid(glu[:, CONV_CH:])
    n_slabs = CONV_CH // LANES
    half = TQ // 2
    for c in range(n_slabs):
        ubuf_ref[c, CONV_HALO:CONV_HALO + TQ, :] = u[:, c * LANES:(c + 1) * LANES]
    base = CONV_HALO - (CONV_WIDTH - 1)
    acc = [[None] * n_slabs for _ in range(2)]

    def conv_slab(c):
        cols = slice(c * LANES, (c + 1) * LANES)
        for par in range(2):
            a = jnp.broadcast_to(bdw_ref[:, cols], (half, LANES))
            for j in range(CONV_WIDTH):
                a = a + wdw_ref[j:j + 1, cols] * ubuf_ref[c, pl.ds(base + j + par, half, stride=2), :]
            acc[par][c] = a
        ubuf_ref[c, 0:CONV_HALO, :] = ubuf_ref[c, TQ:TQ + CONV_HALO, :]

    def gate(b):
        return jax.nn.sigmoid(_dot(h, wgt_ref[:, b * D_MODEL:(b + 1) * D_MODEL]))

    gates = []
    for step in range(n_slabs):
        attention_block(step // ATT_KV_HEADS, step % ATT_KV_HEADS)
        conv_slab(step)
        if step < N_BRANCHES:
            gates.append(gate(step))

    for r in kbands + vbands:
        r[0:WINDOW, :] = r[TQ:TQ + WINDOW, :]

    y_attn = _dot(o_ref[...], woa_ref[...])
    merged = gates[0] * y_attn

    for par in range(2):
        row_sum = functools.reduce(jnp.add, [jnp.sum(a, axis=-1, keepdims=True) for a in acc[par]])
        mu = row_sum * (1.0 / CONV_CH)
        cen = [a - mu for a in acc[par]]
        sq_sum = functools.reduce(jnp.add, [jnp.sum(t * t, axis=-1, keepdims=True) for t in cen])
        rstd = lax.rsqrt(sq_sum * (1.0 / CONV_CH) + LN_EPS)
        for c in range(n_slabs):
            cols = slice(c * LANES, (c + 1) * LANES)
            yln = cen[c] * rstd * gln_ref[:, cols] + bln_ref[:, cols]
            ybuf_ref[c, pl.ds(par, half, stride=2), :] = yln * jax.nn.sigmoid(yln)
    uo = jnp.concatenate([ybuf_ref[c] for c in range(n_slabs)], axis=-1).astype(BF16)
    y_conv = _dot(uo, wco_ref[...])
    merged = merged + gates[1] * y_conv

    xq = _dot(h, wxq_ref[...])
    xq = xq * lax.rsqrt(_block_sum(xq * xq, bd128_ref[...]) * (1.0 / MEM_HEAD_DIM) + EPS) * gxq_ref[...]
    xq = xq.astype(BF16)
    for hd in range(MEM_HEADS):
        cols = slice(hd * MEM_HEAD_DIM, (hd + 1) * MEM_HEAD_DIM)
        s = _dot_nt(xq[:, cols], mk_ref[0, :, cols])
        m = jnp.max(s, axis=-1, keepdims=True)
        e = jnp.exp(s - m)
        den = jnp.sum(e, axis=-1, keepdims=True)
        o_ref[:, cols] = (_dot(e.astype(BF16), mv_ref[0, :, cols]) / den).astype(BF16)
    y_mem = _dot(o_ref[...], wom_ref[...])
    merged = merged + gates[2] * y_mem

    x1 = x + _dot(merged.astype(BF16), wout_ref[...])
    x1_ref[0, :, 0:D_MODEL] = x1

    h2 = _rms(x1) * g2_ref[...]
    h_hi, h_lo = _split_bf16(h2)
    w_hi, w_lo = _split_bf16(wrt_ref[...])
    lt = _dot_nt(w_hi, h_hi) + _dot_nt(w_hi, h_lo) + _dot_nt(w_lo, h_hi) + brt_ref[...]
    gl = [lt[g:g + 1] for g in range(N_GROUPS)]
    gmax = functools.reduce(jnp.maximum, gl)
    g_idx = jnp.full((1, TQ), N_GROUPS - 1, jnp.int32)
    for g in reversed(range(N_GROUPS - 1)):
        g_idx = jnp.where(gl[g] == gmax, g, g_idx)
    p_g = 1.0 / functools.reduce(jnp.add, [jnp.exp(r - gmax) for r in gl])
    el = []
    for kk in range(EXPERTS_PER_GROUP):
        row = lt[8 + kk:9 + kk]
        for g in range(1, N_GROUPS):
            off = 8 + g * EXPERTS_PER_GROUP + kk
            row = jnp.where(g_idx == g, lt[off:off + 1], row)
        el.append(row)

    def first_argmax(rows):
        top = functools.reduce(jnp.maximum, rows)
        idx = jnp.full((1, TQ), len(rows) - 1, jnp.int32)
        for kk in reversed(range(len(rows) - 1)):
            idx = jnp.where(rows[kk] == top, kk, idx)
        return top, idx

    v1, i1 = first_argmax(el)
    v2, i2 = first_argmax([jnp.where(i1 == kk, -jnp.inf, el[kk]) for kk in range(EXPERTS_PER_GROUP)])
    t = jnp.exp(v2 - v1)
    p1 = 1.0 / (1.0 + t)
    p2 = t * p1

    lo = jnp.minimum(i1, i2)
    hi = jnp.maximum(i1, i2)
    w_lo = jnp.where(i1 < i2, p_g * p1, p_g * p2)
    w_hi = jnp.where(i1 < i2, p_g * p2, p_g * p1)
    cls = g_idx * N_PAIRS + ((lo * (2 * EXPERTS_PER_GROUP - 1 - lo)) >> 1) + hi - lo - 1
    cls_f = cls.astype(F32)
    onehot = jnp.broadcast_to(cls, (CLS_ROWS, TQ)) == lax.broadcasted_iota(jnp.int32, (CLS_ROWS, TQ), 0)
    onehot_bf = onehot.astype(F32).astype(BF16)
    prefix = _dot(onehot_bf, tri_ref[...])
    rank = jnp.sum(jnp.where(onehot, prefix, 0.0), axis=0, keepdims=True) - 1.0
    counts = _dot_nt(jnp.ones((8, TQ), BF16), onehot_bf)
    zero = jnp.zeros((1, TQ), F32)
    rt_ref[0] = jnp.concatenate([cls_f, rank] + [zero] * 6, axis=0)
    cnt_ref[0] = jnp.concatenate([counts, jnp.zeros((8, LANES - CLS_ROWS), F32)], axis=1)
    cols = jnp.concatenate([w_lo, w_hi, cls_f, jnp.zeros((LANES - 3, TQ), F32)], axis=0)
    x1_ref[0, :, D_MODEL:] = cols.T


FLAG_FIRST, FLAG_LAST, FLAG_VALID = 1, 2, 4
GATHER_SLOTS = 3


def _moe_kernel(dest_ref, tile_ref, elo_ref, ehi_ref, cls_ref, flag_ref,
                g2_ref, wg_lo, wu_lo, wd_lo, wg_hi, wu_hi, wd_hi, x1_hbm,
                out_hbm,
                perm_ref, xbuf, obuf, h2buf, wbuf_up, wbuf_down, gsem, ssem):
    k = pl.program_id(0)
    nk = pl.num_programs(0)
    t = tile_ref[k]
    xslot = t % GATHER_SLOTS
    oslot = t % 2
    flags = flag_ref[k]
    n_tok = x1_hbm.shape[0]
    n_tiles = n_tok // TM

    def gather_copy(r, tile, s):
        tok = perm_ref[tile * TM + r]
        return pltpu.make_async_copy(x1_hbm.at[pl.ds(tok, 1), :], xbuf.at[s, pl.ds(r, 1), :], gsem.at[s])

    def scatter_copy(r, tile, s):
        tok = perm_ref[tile * TM + r]
        return pltpu.make_async_copy(obuf.at[s, pl.ds(r, 1), :], out_hbm.at[pl.ds(tok, 1), :], ssem.at[s])

    k_prev = jnp.maximum(k - 1, 0)
    for side, (e_ref, wg, wu, wd) in enumerate(((elo_ref, wg_lo, wu_lo, wd_lo), (ehi_ref, wg_hi, wu_hi, wd_hi))):
        @pl.when((k == 0) | (e_ref[k] != e_ref[k_prev]))
        def _(side=side, wg=wg, wu=wu, wd=wd):
            wbuf_up[2 * side] = wg[0].astype(BF16)
            wbuf_up[2 * side + 1] = wu[0].astype(BF16)
            wbuf_down[side] = wd[0].astype(BF16)

    def for_rows(fn):
        def body(r, carry):
            fn(r)
            return carry
        lax.fori_loop(0, TM, body, 0, unroll=8)

    @pl.when(k == 0)
    def _():
        def invert(tok, carry):
            perm_ref[dest_ref[tok]] = tok
            return carry
        lax.fori_loop(0, n_tok, invert, 0, unroll=8)
        for tile in range(GATHER_SLOTS - 1):
            for_rows(lambda r, tile=tile: gather_copy(r, tile, tile).start())

    is_first = (flags & FLAG_FIRST) != 0
    is_valid = (flags & FLAG_VALID) != 0
    ahead = GATHER_SLOTS - 1
    prefetch_tile = jnp.minimum(t + ahead, n_tiles - 1)
    prefetch_slot = (t + ahead) % GATHER_SLOTS

    @pl.when(is_first)
    def _():
        for_rows(lambda r: gather_copy(r, t, xslot).wait())

        @pl.when(t >= 2)
        def _():
            for_rows(lambda r: scatter_copy(r, t - 2, oslot).wait())

        xr = xbuf[xslot, :, 0:D_MODEL]
        h2buf[...] = (_rms(xr) * g2_ref[...]).astype(BF16)
        obuf[oslot] = xr

    def experts():
        w = xbuf[xslot, :, D_MODEL:XW]
        mine = w[:, 2:3] == cls_ref[k].astype(F32)
        h2 = h2buf[...]
        y = None
        for side in range(2):
            c = jnp.where(mine, w[:, side:side + 1], 0.0)
            hid = jax.nn.silu(_dot(h2, wbuf_up[2 * side])) * _dot(h2, wbuf_up[2 * side + 1]) * c
            part = _dot(hid.astype(BF16), wbuf_down[side])
            y = part if y is None else y + part
        obuf[oslot] += y

    @pl.when(is_first & (t > 0))
    def _():
        for r in range(TM):
            gather_copy(r, prefetch_tile, prefetch_slot).start()
            scatter_copy(r, t - 1, 1 - oslot).start()
        experts()

    @pl.when(is_first & (t == 0))
    def _():
        for r in range(TM):
            gather_copy(r, prefetch_tile, prefetch_slot).start()
        experts()

    @pl.when(is_valid & jnp.logical_not(is_first))
    def _():
        experts()

    @pl.when(k == nk - 1)
    def _():
        last = n_tiles - 1
        for_rows(lambda r: scatter_copy(r, last, last % 2).start())
        for tile in (last - 1, last):
            for_rows(lambda r, tile=tile: scatter_copy(r, tile, tile % 2).wait())
        for extra in range(1, GATHER_SLOTS):
            for_rows(lambda r, extra=extra: gather_copy(r, last, (last + extra) % GATHER_SLOTS).wait())


_CLASS_PAIRS = [(a, b) for a in range(EXPERTS_PER_GROUP) for b in range(a + 1, EXPERTS_PER_GROUP)]


def _routing_plan(cls, rank, tile_counts, n_tiles):
    counts = jnp.sum(tile_counts, axis=0)
    off = jnp.concatenate([jnp.zeros((1,), jnp.int32), jnp.cumsum(counts).astype(jnp.int32)])
    tile_base = jnp.cumsum(tile_counts, axis=0) - tile_counts + off[None, :-1]
    onehot = cls[:, :, None] == jnp.arange(N_CLASSES, dtype=jnp.int32)[None, None, :]
    dest = (jnp.sum(jnp.where(onehot, tile_base[:, None, :], 0), axis=-1) + rank).reshape(-1)
    first_tile = off[:-1] // TM
    last_tile = (off[1:] - 1) // TM
    n_items = jnp.where(counts > 0, last_tile - first_tile + 1, 0)
    istart = jnp.concatenate([jnp.zeros((1,), jnp.int32), jnp.cumsum(n_items).astype(jnp.int32)])
    total = istart[-1]
    ni = n_tiles + N_CLASSES - 1
    k = jnp.arange(ni, dtype=jnp.int32)
    valid = k < total
    kc = jnp.minimum(k, total - 1)
    c_of_k = jnp.minimum(jnp.sum(istart[None, 1:] <= kc[:, None], axis=1).astype(jnp.int32), N_CLASSES - 1)
    tile_k = first_tile[c_of_k] + kc - istart[c_of_k]
    prev_tile = jnp.concatenate([jnp.full((1,), -1, jnp.int32), tile_k[:-1]])
    next_tile = jnp.concatenate([tile_k[1:], jnp.full((1,), -1, jnp.int32)])
    first = valid & (tile_k != prev_tile)
    last = valid & ((tile_k != next_tile) | (k == total - 1))
    flags = (first * FLAG_FIRST + last * FLAG_LAST + valid * FLAG_VALID).astype(jnp.int32)
    pair_lo = jnp.array([p[0] for p in _CLASS_PAIRS], jnp.int32)
    pair_hi = jnp.array([p[1] for p in _CLASS_PAIRS], jnp.int32)
    base = (c_of_k // N_PAIRS) * EXPERTS_PER_GROUP
    e_lo = base + pair_lo[c_of_k % N_PAIRS]
    e_hi = base + pair_hi[c_of_k % N_PAIRS]
    cls_k = jnp.where(valid, c_of_k, -1)
    return dest.astype(jnp.int32), tile_k, e_lo, e_hi, cls_k, flags


def _const_spec(shape):
    nd = len(shape)
    return pl.BlockSpec(shape, lambda *_: (0,) * nd, pipeline_mode=pl.Buffered(1))


def _block_diag_ones(n, blk):
    r = jnp.arange(n) // blk
    return (r[:, None] == r[None, :]).astype(BF16)


def kernel(x, mem, positions, g_norm1, w_in, g_q, g_k, sinks, w_o_attn, w_conv_dw, b_conv_dw, g_conv_ln, b_conv_ln, w_conv_out, g_mem, w_kv_mem, g_xq, g_xk, w_o_mem, w_out, g_norm2, w_group, b_group, w_router, b_router, w_gate, w_up, w_down):
    B, S, D = x.shape
    M = mem.shape[1]
    assert D == D_MODEL and S % TQ == 0 and (B * S) % TM == 0 and w_in.shape[0] == 1
    NT = S // TQ
    T = B * S
    l = 0

    bd64 = _block_diag_ones(Q_WIDTH, HEAD_DIM)
    bd128 = _block_diag_ones(XQ_WIDTH, MEM_HEAD_DIM)
    row = lambda v: v.reshape(1, -1).astype(F32)

    mk, mv = pl.pallas_call(
        _memkv_kernel,
        grid=(B,),
        in_specs=[pl.BlockSpec((1, M, D), lambda b: (b, 0, 0)),
                  _const_spec((1, D)), _const_spec((D, 2 * XQ_WIDTH)),
                  _const_spec((1, XQ_WIDTH)), _const_spec((XQ_WIDTH, XQ_WIDTH))],
        out_specs=[pl.BlockSpec((1, M, XQ_WIDTH), lambda b: (b, 0, 0))] * 2,
        out_shape=[jax.ShapeDtypeStruct((B, M, XQ_WIDTH), BF16)] * 2,
        compiler_params=pltpu.CompilerParams(dimension_semantics=("arbitrary",)),
        name="memkv",
    )(mem, row(g_mem[l]), w_kv_mem[l].astype(BF16), row(jnp.tile(g_xk[l], MEM_HEADS)), bd128)

    inv_freq = 1.0 / (ROPE_THETA ** (jnp.arange(0, HEAD_DIM, 2, dtype=F32) / HEAD_DIM))
    ang = positions.astype(F32)[..., None] * inv_freq
    cos, sin = jnp.cos(ang), jnp.sin(ang)
    cosd = jnp.tile(cos, (1, 1, LANES // (HEAD_DIM // 2)))
    sind = jnp.tile(jnp.concatenate([-sin, sin], axis=-1), (1, 1, LANES // HEAD_DIM))

    w = w_in[l]
    c0 = Q_WIDTH + 2 * KV_WIDTH
    c1 = c0 + GLU_WIDTH
    c2 = c1 + XQ_WIDTH
    w_qkv = w[:, :c0].astype(BF16)
    w_glu = w[:, c0:c1].astype(BF16)
    w_xq = w[:, c1:c2].astype(BF16)
    w_gt = w[:, c2:].astype(BF16)
    wdw = jnp.zeros((CONV_HALO, CONV_CH), F32).at[:CONV_WIDTH].set(w_conv_dw[l])
    w_rt = jnp.zeros((RT_ROWS, D), F32).at[0:N_GROUPS].set(w_group[l].T).at[8:8 + N_EXPERTS].set(w_router[l].T)
    b_rt = jnp.zeros((RT_ROWS, 1), F32).at[0:N_GROUPS, 0].set(b_group[l]).at[8:8 + N_EXPERTS, 0].set(b_router[l])

    tile3 = lambda last: pl.BlockSpec((1, TQ, last), lambda b, i: (b, i, 0))
    per_batch = pl.BlockSpec((1, M, XQ_WIDTH), lambda b, i: (b, 0, 0))
    in_specs = [
        tile3(D), tile3(LANES), tile3(LANES), per_batch, per_batch,
        _const_spec((1, D)),
        _const_spec((D, c0)), _const_spec((D, GLU_WIDTH)), _const_spec((D, XQ_WIDTH)), _const_spec((D, GATE_WIDTH)),
        _const_spec((1, Q_WIDTH)), _const_spec((1, LANES)),
        pl.BlockSpec(memory_space=pltpu.SMEM),
        _const_spec((Q_WIDTH, Q_WIDTH)), _const_spec((XQ_WIDTH, XQ_WIDTH)),
        _const_spec((Q_WIDTH, D)),
        _const_spec((CONV_HALO, CONV_CH)), _const_spec((1, CONV_CH)), _const_spec((1, CONV_CH)), _const_spec((1, CONV_CH)),
        _const_spec((CONV_CH, D)),
        _const_spec((1, XQ_WIDTH)),
        _const_spec((XQ_WIDTH, D)),
        _const_spec((D, D)),
        _const_spec((1, D)),
        _const_spec((RT_ROWS, D)), _const_spec((RT_ROWS, 1)), _const_spec((TQ, TQ)),
    ]
    tri = (jnp.arange(TQ)[:, None] <= jnp.arange(TQ)[None, :]).astype(BF16)
    per_tile = lambda rows, last: pl.BlockSpec((1, rows, last), lambda b, i: (b * NT + i, 0, 0))
    x1, rt, cnt = pl.pallas_call(
        _mixer_kernel,
        grid=(B, NT),
        in_specs=in_specs,
        out_specs=[tile3(XW), per_tile(8, TQ), per_tile(8, LANES)],
        out_shape=[jax.ShapeDtypeStruct((B, S, XW), F32), jax.ShapeDtypeStruct((B * NT, 8, TQ), F32),
                   jax.ShapeDtypeStruct((B * NT, 8, LANES), F32)],
        scratch_shapes=[pltpu.VMEM((WINDOW + TQ, LANES), BF16)] * 4
        + [pltpu.VMEM((CONV_CH // LANES, CONV_HALO + TQ, LANES), F32),
           pltpu.VMEM((CONV_CH // LANES, TQ, LANES), F32), pltpu.VMEM((TQ, Q_WIDTH), BF16)],
        compiler_params=pltpu.CompilerParams(dimension_semantics=("arbitrary", "arbitrary"),
                                             vmem_limit_bytes=VMEM_LIMIT),
        name="mixer",
    )(x, cosd, sind, mk, mv, row(g_norm1[l]), w_qkv, w_glu, w_xq, w_gt,
      row(jnp.tile(g_q[l], ATT_HEADS) * (HEAD_DIM ** -0.5)), row(jnp.tile(g_k[l], ATT_KV_HEADS)),
      sinks[l].astype(F32), bd64, bd128, w_o_attn[l].astype(BF16),
      wdw, row(b_conv_dw[l]), row(g_conv_ln[l]), row(b_conv_ln[l]), w_conv_out[l].astype(BF16),
      row(jnp.tile(g_xq[l], MEM_HEADS) * (MEM_HEAD_DIM ** -0.5)), w_o_mem[l].astype(BF16),
      w_out[l].astype(BF16), row(g_norm2[l]), w_rt, b_rt, tri)

    n_tiles = T // TM
    plan = _routing_plan(rt[:, 0, :].astype(jnp.int32), rt[:, 1, :].astype(jnp.int32),
                         cnt[:, 0, :N_CLASSES].astype(jnp.int32), n_tiles)
    dest, tile_k, e_lo, e_hi, cls_k, flags = plan
    ni = tile_k.shape[0]
    wg, wu, wd = w_gate[l], w_up[l], w_down[l]
    lo_map = lambda k, dest, tile, elo, ehi, cls, flg: (elo[k], 0, 0)
    hi_map = lambda k, dest, tile, elo, ehi, cls, flg: (ehi[k], 0, 0)
    up_block = (1, D, EXPERT_FF)
    down_block = (1, EXPERT_FF, D)
    out = pl.pallas_call(
        _moe_kernel,
        grid_spec=pltpu.PrefetchScalarGridSpec(
            num_scalar_prefetch=6,
            grid=(ni,),
            in_specs=[pl.BlockSpec((1, D), lambda k, *_: (0, 0)),
                      pl.BlockSpec(up_block, lo_map), pl.BlockSpec(up_block, lo_map),
                      pl.BlockSpec(down_block, lo_map),
                      pl.BlockSpec(up_block, hi_map), pl.BlockSpec(up_block, hi_map),
                      pl.BlockSpec(down_block, hi_map),
                      pl.BlockSpec(memory_space=pl.ANY)],
            out_specs=pl.BlockSpec(memory_space=pl.ANY),
            scratch_shapes=[pltpu.SMEM((T,), jnp.int32),
                            pltpu.VMEM((GATHER_SLOTS, TM, XW), F32), pltpu.VMEM((2, TM, D), F32),
                            pltpu.VMEM((TM, D), BF16),
                            pltpu.VMEM((4, D, EXPERT_FF), BF16), pltpu.VMEM((2, EXPERT_FF, D), BF16),
                            pltpu.SemaphoreType.DMA((GATHER_SLOTS,)), pltpu.SemaphoreType.DMA((2,))]),
        out_shape=jax.ShapeDtypeStruct((T, D), F32),
        compiler_params=pltpu.CompilerParams(dimension_semantics=("arbitrary",),
                                             vmem_limit_bytes=VMEM_LIMIT),
        name="moe",
    )(dest, tile_k, e_lo, e_hi, cls_k, flags, row(g_norm2[l]), wg, wu, wd, wg, wu, wd,
      x1.reshape(T, XW))
    return out.reshape(B, S, D)
```

```python
import functools

import jax
import jax.numpy as jnp
from jax import lax
from jax.experimental import pallas as pl
from jax.experimental.pallas import tpu as pltpu

F32 = jnp.float32
BF16 = jnp.bfloat16

D_MODEL = 1024
ATT_HEADS = 8
ATT_KV_HEADS = 2
HEAD_DIM = 64
WINDOW = 128
ROPE_THETA = 10000.0
CONV_CH = 512
CONV_WIDTH = 31
MEM_HEADS = 4
MEM_HEAD_DIM = 128
N_BRANCHES = 3
N_GROUPS = 4
EXPERTS_PER_GROUP = 4
N_EXPERTS = N_GROUPS * EXPERTS_PER_GROUP
EXPERT_FF = 512
EPS = 1e-6
LN_EPS = 1e-5
NEG_INF = -1e30

Q_WIDTH = ATT_HEADS * HEAD_DIM
KV_WIDTH = ATT_KV_HEADS * HEAD_DIM
GLU_WIDTH = 2 * CONV_CH
XQ_WIDTH = MEM_HEADS * MEM_HEAD_DIM
GATE_WIDTH = N_BRANCHES * D_MODEL

LANES = 128
TQ = 256
CONV_HALO = 32
CONV_ROWS = 128
TM = 256
RT_ROWS = 32
N_PAIRS = EXPERTS_PER_GROUP * (EXPERTS_PER_GROUP - 1) // 2
N_CLASSES = N_GROUPS * N_PAIRS
CLS_ROWS = 32
XW = D_MODEL + LANES
VMEM_LIMIT = 56 * 1024 * 1024


def _dot(a, b):
    return jnp.dot(a, b, preferred_element_type=F32)


def _dot_nt(a, b):
    return lax.dot_general(a, b, (((1,), (1,)), ((), ())), preferred_element_type=F32)


def _split_bf16(t):
    hi = t.astype(BF16)
    lo = (t - hi.astype(F32)).astype(BF16)
    return hi, lo


def _block_sum(t, bd):
    return _dot(t.astype(BF16), bd)


def _rms(t):
    return t * lax.rsqrt(jnp.mean(t * t, axis=-1, keepdims=True) + EPS)


def _memkv_kernel(mem_ref, gmem_ref, wkv_ref, gxk_ref, bd_ref, mk_ref, mv_ref):
    mn = (_rms(mem_ref[0]) * gmem_ref[...]).astype(BF16)
    kv = _dot(mn, wkv_ref[...])
    mk = kv[:, :XQ_WIDTH]
    ss = _block_sum(mk * mk, bd_ref[...])
    mk = mk * lax.rsqrt(ss * (1.0 / MEM_HEAD_DIM) + EPS) * gxk_ref[...]
    mk_ref[0] = mk.astype(BF16)
    mv_ref[0] = kv[:, XQ_WIDTH:].astype(BF16)


def _mixer_kernel(x_ref, rope_ref, mk_ref, mv_ref, g1_ref, wqkv_ref, wglu_ref, wxq_ref,
                  wgt_ref, gq_ref, gk_ref, sinks_ref, bd64_ref, bd128_ref, woa_ref, wdw_ref,
                  bdw_ref, gln_ref, bln_ref, wco_ref, gxq_ref, wom_ref, wout_ref, g2_ref,
                  wrt_ref, brt_ref, tri_ref,
                  x1_ref, rt_ref, cnt_ref,
                  k0_ref, k1_ref, v0_ref, v1_ref, ubuf_ref, ybuf_ref, o_ref, om_ref, x1prev_ref,
                  *, tiles_per_seq):
    s = pl.program_id(0)
    n_tiles = pl.num_programs(0) - 1
    i = jnp.minimum(s, n_tiles - 1) % tiles_per_seq
    kbands = (k0_ref, k1_ref)
    vbands = (v0_ref, v1_ref)

    @pl.when(s == 0)
    def _():
        x1prev_ref[...] = jnp.zeros((TQ, D_MODEL), F32)

    @pl.when(i == 0)
    def _():
        for r in kbands + vbands:
            r[0:WINDOW, :] = jnp.zeros((WINDOW, LANES), BF16)
        ubuf_ref[:, 0:CONV_HALO, :] = jnp.zeros((CONV_CH // LANES, CONV_HALO, LANES), F32)

    x1_prev = x1prev_ref[...]
    x1_ref[0, :, 0:D_MODEL] = x1_prev

    x = x_ref[0]
    h = (_rms(x) * g1_ref[...]).astype(BF16)

    n_slabs = CONV_CH // LANES
    half = TQ // 2
    group = ATT_HEADS // ATT_KV_HEADS

    glu = _dot(h, wglu_ref[...])
    route_logits = _router_logits(x1_prev, g2_ref, wrt_ref, brt_ref)
    u = glu[:, :CONV_CH] * jax.nn.sigmoid(glu[:, CONV_CH:])
    for c in range(n_slabs):
        ubuf_ref[c, CONV_HALO:CONV_HALO + TQ, :] = u[:, c * LANES:(c + 1) * LANES]
    base = CONV_HALO - (CONV_WIDTH - 1)
    acc = [[None] * n_slabs for _ in range(2)]

    def conv_slab(c):
        cols = slice(c * LANES, (c + 1) * LANES)
        for par in range(2):
            parts = []
            for q0 in range(0, half, CONV_ROWS):
                a = jnp.broadcast_to(bdw_ref[:, cols], (CONV_ROWS, LANES))
                for j in range(CONV_WIDTH):
                    a = a + wdw_ref[j:j + 1, cols] * ubuf_ref[
                        c, pl.ds(base + j + par + 2 * q0, CONV_ROWS, stride=2), :]
                parts.append(a)
            acc[par][c] = jnp.concatenate(parts, axis=0)
        ubuf_ref[c, 0:CONV_HALO, :] = ubuf_ref[c, TQ:TQ + CONV_HALO, :]

    def gate(b):
        return jax.nn.sigmoid(_dot(h, wgt_ref[:, b * D_MODEL:(b + 1) * D_MODEL]))

    def conv_finish():
        for par in range(2):
            row_sum = functools.reduce(jnp.add, [jnp.sum(a, axis=-1, keepdims=True) for a in acc[par]])
            mu = row_sum * (1.0 / CONV_CH)
            cen = [a - mu for a in acc[par]]
            sq_sum = functools.reduce(jnp.add, [jnp.sum(t * t, axis=-1, keepdims=True) for t in cen])
            rstd = lax.rsqrt(sq_sum * (1.0 / CONV_CH) + LN_EPS)
            for c in range(n_slabs):
                cols = slice(c * LANES, (c + 1) * LANES)
                yln = cen[c] * rstd * gln_ref[:, cols] + bln_ref[:, cols]
                ybuf_ref[c, pl.ds(par, half, stride=2), :] = yln * jax.nn.sigmoid(yln)
        uo = jnp.concatenate([ybuf_ref[c] for c in range(n_slabs)], axis=-1).astype(BF16)
        return _dot(uo, wco_ref[...])

    qkv = _dot(h, wqkv_ref[...])
    q = qkv[:, :Q_WIDTH]
    k = qkv[:, Q_WIDTH:Q_WIDTH + KV_WIDTH]
    v = qkv[:, Q_WIDTH + KV_WIDTH:]
    bd64 = bd64_ref[...]
    q = q * lax.rsqrt(_block_sum(q * q, bd64) * (1.0 / HEAD_DIM) + EPS) * gq_ref[...]
    k = k * lax.rsqrt(_block_sum(k * k, bd64[:LANES, :LANES]) * (1.0 / HEAD_DIM) + EPS) * gk_ref[...]

    lane = lax.broadcasted_iota(jnp.int32, (TQ, LANES), 1)
    first_half = (lane % HEAD_DIM) < (HEAD_DIM // 2)
    low_head = lane < HEAD_DIM
    packed = rope_ref[0]
    up32 = pltpu.roll(packed, HEAD_DIM // 2, 1)
    up96 = pltpu.roll(packed, LANES - HEAD_DIM // 2, 1)
    quarter = lane // (HEAD_DIM // 2)
    cosd = jnp.where(quarter == 1, up32, jnp.where(quarter == 2, up96, packed))
    sind = jnp.where(low_head, up96, up32)

    def rope(t):
        rot = jnp.where(first_half, pltpu.roll(t, LANES - HEAD_DIM // 2, 1),
                        pltpu.roll(t, HEAD_DIM // 2, 1))
        return t * cosd + rot * sind

    def dup_halves(t):
        swapped = pltpu.roll(t, HEAD_DIM, 1)
        return jnp.where(low_head, t, swapped), jnp.where(low_head, swapped, t)

    kd = dup_halves(rope(k))
    vd = dup_halves(v)
    for kvh in range(ATT_KV_HEADS):
        kbands[kvh][WINDOW:WINDOW + TQ, :] = kd[kvh].astype(BF16)
        vbands[kvh][WINDOW:WINDOW + TQ, :] = vd[kvh].astype(BF16)

    q_heads = []
    for c in range(Q_WIDTH // LANES):
        qc = rope(q[:, c * LANES:(c + 1) * LANES])
        q_heads.append(jnp.where(low_head, qc, 0.0).astype(BF16))
        q_heads.append(jnp.where(low_head, 0.0, qc).astype(BF16))

    qi = lax.broadcasted_iota(jnp.int32, (WINDOW, 2 * WINDOW), 0)
    kj = lax.broadcasted_iota(jnp.int32, (WINDOW, 2 * WINDOW), 1)
    band_mask = (kj > qi) & (kj <= qi + WINDOW)
    first_mask = band_mask & (kj >= jnp.where(i > 0, 0, WINDOW))
    low_head_w = lax.broadcasted_iota(jnp.int32, (WINDOW, LANES), 1) < HEAD_DIM

    def attention_block(n, kvh):
        rows = slice(n * WINDOW, (n + 1) * WINDOW)
        mask = first_mask if n == 0 else band_mask
        heads = range(kvh * group, (kvh + 1) * group)
        q_st = jnp.concatenate([q_heads[hd][rows] for hd in heads], axis=0)
        kband = kbands[kvh][n * WINDOW:n * WINDOW + 2 * WINDOW, :]
        vband = vbands[kvh][n * WINDOW:n * WINDOW + 2 * WINDOW, :]
        s_all = _dot_nt(q_st, kband)
        es, dens = [], []
        for j, hd in enumerate(heads):
            s = jnp.where(mask, s_all[j * WINDOW:(j + 1) * WINDOW], NEG_INF)
            sink = sinks_ref[hd]
            m = jnp.maximum(jnp.max(s, axis=-1, keepdims=True), sink)
            e = jnp.exp(s - m)
            dens.append(jnp.sum(e, axis=-1, keepdims=True) + jnp.exp(sink - m))
            es.append(e.astype(BF16))
        o_all = _dot(jnp.concatenate(es, axis=0), vband)
        for cc in range(group // 2):
            oa = o_all[(2 * cc) * WINDOW:(2 * cc + 1) * WINDOW] / dens[2 * cc]
            ob = o_all[(2 * cc + 1) * WINDOW:(2 * cc + 2) * WINDOW] / dens[2 * cc + 1]
            col = kvh * (group // 2) + cc
            o_ref[rows, col * LANES:(col + 1) * LANES] = jnp.where(low_head_w, oa, ob).astype(BF16)

    route = _router_select(route_logits)
    gates = []
    for step in range(n_slabs):
        attention_block(step // ATT_KV_HEADS, step % ATT_KV_HEADS)
        conv_slab(step)
        if step < N_BRANCHES:
            gates.append(gate(step))
        if step == 0:
            _router_emit(*route, tri_ref, x1_ref, rt_ref, cnt_ref)

    for r in kbands + vbands:
        r[0:WINDOW, :] = r[TQ:TQ + WINDOW, :]

    merged = gates[0] * _dot(o_ref[...], woa_ref[...])
    merged = merged + gates[1] * conv_finish()

    xq = _dot(h, wxq_ref[...])
    xq = xq * lax.rsqrt(_block_sum(xq * xq, bd128_ref[...]) * (1.0 / MEM_HEAD_DIM) + EPS) * gxq_ref[...]
    xq = xq.astype(BF16)

    def mem_head(hd):
        cols = slice(hd * MEM_HEAD_DIM, (hd + 1) * MEM_HEAD_DIM)
        s = _dot_nt(xq[:, cols], mk_ref[0, :, cols])
        m = jnp.max(s, axis=-1, keepdims=True)
        e = jnp.exp(s - m)
        den = jnp.sum(e, axis=-1, keepdims=True)
        om_ref[:, cols] = (_dot(e.astype(BF16), mv_ref[0, :, cols]) / den).astype(BF16)

    for hd in range(MEM_HEADS):
        mem_head(hd)
    merged = merged + gates[2] * _dot(om_ref[...], wom_ref[...])

    x1prev_ref[...] = x + _dot(merged.astype(BF16), wout_ref[...])


def _router_logits(x1, g2_ref, wrt_ref, brt_ref):
    h2 = _rms(x1) * g2_ref[...]
    h_hi, h_lo = _split_bf16(h2)
    w_hi, w_lo = _split_bf16(wrt_ref[...])
    return _dot_nt(w_hi, h_hi) + _dot_nt(w_hi, h_lo) + _dot_nt(w_lo, h_hi) + brt_ref[...]


def _router_select(lt):
    gl = [lt[g:g + 1] for g in range(N_GROUPS)]
    gmax = functools.reduce(jnp.maximum, gl)
    g_idx = jnp.full((1, TQ), N_GROUPS - 1, jnp.int32)
    for g in reversed(range(N_GROUPS - 1)):
        g_idx = jnp.where(gl[g] == gmax, g, g_idx)
    p_g = 1.0 / functools.reduce(jnp.add, [jnp.exp(r - gmax) for r in gl])
    el = []
    for kk in range(EXPERTS_PER_GROUP):
        row = lt[8 + kk:9 + kk]
        for g in range(1, N_GROUPS):
            off = 8 + g * EXPERTS_PER_GROUP + kk
            row = jnp.where(g_idx == g, lt[off:off + 1], row)
        el.append(row)

    def first_argmax(rows):
        top = functools.reduce(jnp.maximum, rows)
        idx = jnp.full((1, TQ), len(rows) - 1, jnp.int32)
        for kk in reversed(range(len(rows) - 1)):
            idx = jnp.where(rows[kk] == top, kk, idx)
        return top, idx

    v1, i1 = first_argmax(el)
    v2, i2 = first_argmax([jnp.where(i1 == kk, -jnp.inf, el[kk]) for kk in range(EXPERTS_PER_GROUP)])
    t = jnp.exp(v2 - v1)
    p1 = 1.0 / (1.0 + t)
    p2 = t * p1

    lo = jnp.minimum(i1, i2)
    hi = jnp.maximum(i1, i2)
    w_lo = jnp.where(i1 < i2, p_g * p1, p_g * p2)
    w_hi = jnp.where(i1 < i2, p_g * p2, p_g * p1)
    cls = g_idx * N_PAIRS + ((lo * (2 * EXPERTS_PER_GROUP - 1 - lo)) >> 1) + hi - lo - 1
    return cls, w_lo, w_hi


def _router_emit(cls, w_lo, w_hi, tri_ref, x1_ref, rt_ref, cnt_ref):
    cls_f = cls.astype(F32)
    onehot = jnp.broadcast_to(cls, (CLS_ROWS, TQ)) == lax.broadcasted_iota(jnp.int32, (CLS_ROWS, TQ), 0)
    onehot_bf = onehot.astype(F32).astype(BF16)
    prefix = _dot(onehot_bf, tri_ref[...])
    rank = jnp.sum(jnp.where(onehot, prefix, 0.0), axis=0, keepdims=True) - 1.0
    counts = _dot_nt(jnp.ones((8, TQ), BF16), onehot_bf)
    zero = jnp.zeros((1, TQ), F32)
    rt_ref[0] = jnp.concatenate([cls_f, rank] + [zero] * 6, axis=0)
    cnt_ref[0] = jnp.concatenate([counts, jnp.zeros((8, LANES - CLS_ROWS), F32)], axis=1)
    cols = jnp.concatenate([w_lo, w_hi, cls_f, jnp.zeros((LANES - 3, TQ), F32)], axis=0)
    x1_ref[0, :, D_MODEL:] = cols.T


FLAG_FIRST, FLAG_LAST, FLAG_VALID = 1, 2, 4
GATHER_SLOTS = 3


def _moe_kernel(dest_ref, tile_ref, elo_ref, ehi_ref, cls_ref, flag_ref,
                g2_ref, wg_lo, wu_lo, wd_lo, wg_hi, wu_hi, wd_hi, x1_hbm,
                out_hbm,
                perm_ref, xbuf, obuf, h2buf, wbuf_up, wbuf_down, gsem, ssem):
    k = pl.program_id(0)
    nk = pl.num_programs(0)
    t = tile_ref[k]
    xslot = t % GATHER_SLOTS
    oslot = t % 2
    flags = flag_ref[k]
    n_tok = x1_hbm.shape[0]
    n_tiles = n_tok // TM

    def gather_copy(r, tile, s):
        tok = perm_ref[tile * TM + r]
        return pltpu.make_async_copy(x1_hbm.at[pl.ds(tok, 1), :], xbuf.at[s, pl.ds(r, 1), :], gsem.at[s])

    def scatter_copy(r, tile, s):
        tok = perm_ref[tile * TM + r]
        return pltpu.make_async_copy(obuf.at[s, pl.ds(r, 1), :], out_hbm.at[pl.ds(tok, 1), :], ssem.at[s])

    k_prev = jnp.maximum(k - 1, 0)
    for side, (e_ref, wg, wu, wd) in enumerate(((elo_ref, wg_lo, wu_lo, wd_lo), (ehi_ref, wg_hi, wu_hi, wd_hi))):
        @pl.when((k == 0) | (e_ref[k] != e_ref[k_prev]))
        def _(side=side, wg=wg, wu=wu, wd=wd):
            wbuf_up[2 * side] = wg[0].astype(BF16)
            wbuf_up[2 * side + 1] = wu[0].astype(BF16)
            wbuf_down[side] = wd[0].astype(BF16)

    def for_rows(fn):
        def body(r, carry):
            fn(r)
            return carry
        lax.fori_loop(0, TM, body, 0, unroll=8)

    @pl.when(k == 0)
    def _():
        def invert(tok, carry):
            perm_ref[dest_ref[tok]] = tok
            return carry
        lax.fori_loop(0, n_tok, invert, 0, unroll=8)
        for tile in range(GATHER_SLOTS - 1):
            for_rows(lambda r, tile=tile: gather_copy(r, tile, tile).start())

    is_first = (flags & FLAG_FIRST) != 0
    is_valid = (flags & FLAG_VALID) != 0
    ahead = GATHER_SLOTS - 1
    prefetch_tile = jnp.minimum(t + ahead, n_tiles - 1)
    prefetch_slot = (t + ahead) % GATHER_SLOTS

    @pl.when(is_first)
    def _():
        for_rows(lambda r: gather_copy(r, t, xslot).wait())

        @pl.when(t >= 2)
        def _():
            for_rows(lambda r: scatter_copy(r, t - 2, oslot).wait())

        xr = xbuf[xslot, :, 0:D_MODEL]
        h2buf[...] = (_rms(xr) * g2_ref[...]).astype(BF16)
        obuf[oslot] = xr

    def experts():
        w = xbuf[xslot, :, D_MODEL:XW]
        mine = w[:, 2:3] == cls_ref[k].astype(F32)
        h2 = h2buf[...]
        y = None
        for side in range(2):
            c = jnp.where(mine, w[:, side:side + 1], 0.0)
            hid = jax.nn.silu(_dot(h2, wbuf_up[2 * side])) * _dot(h2, wbuf_up[2 * side + 1]) * c
            part = _dot(hid.astype(BF16), wbuf_down[side])
            y = part if y is None else y + part
        obuf[oslot] += y

    @pl.when(is_first & (t > 0))
    def _():
        for r in range(TM):
            gather_copy(r, prefetch_tile, prefetch_slot).start()
            scatter_copy(r, t - 1, 1 - oslot).start()
        experts()

    @pl.when(is_first & (t == 0))
    def _():
        for r in range(TM):
            gather_copy(r, prefetch_tile, prefetch_slot).start()
        experts()

    @pl.when(is_valid & jnp.logical_not(is_first))
    def _():
        experts()

    @pl.when(k == nk - 1)
    def _():
        last = n_tiles - 1
        for_rows(lambda r: scatter_copy(r, last, last % 2).start())
        for tile in (last - 1, last):
            for_rows(lambda r, tile=tile: scatter_copy(r, tile, tile % 2).wait())
        for extra in range(1, GATHER_SLOTS):
            for_rows(lambda r, extra=extra: gather_copy(r, last, (last + extra) % GATHER_SLOTS).wait())


_CLASS_PAIRS = [(a, b) for a in range(EXPERTS_PER_GROUP) for b in range(a + 1, EXPERTS_PER_GROUP)]


def _routing_plan(cls, rank, tile_counts, n_tiles):
    counts = jnp.sum(tile_counts, axis=0)
    off = jnp.concatenate([jnp.zeros((1,), jnp.int32), jnp.cumsum(counts).astype(jnp.int32)])
    tile_base = jnp.cumsum(tile_counts, axis=0) - tile_counts + off[None, :-1]
    onehot = cls[:, :, None] == jnp.arange(N_CLASSES, dtype=jnp.int32)[None, None, :]
    dest = (jnp.sum(jnp.where(onehot, tile_base[:, None, :], 0), axis=-1) + rank).reshape(-1)
    first_tile = off[:-1] // TM
    last_tile = (off[1:] - 1) // TM
    n_items = jnp.where(counts > 0, last_tile - first_tile + 1, 0)
    istart = jnp.concatenate([jnp.zeros((1,), jnp.int32), jnp.cumsum(n_items).astype(jnp.int32)])
    total = istart[-1]
    ni = n_tiles + N_CLASSES - 1
    k = jnp.arange(ni, dtype=jnp.int32)
    valid = k < total
    kc = jnp.minimum(k, total - 1)
    c_of_k = jnp.minimum(jnp.sum(istart[None, 1:] <= kc[:, None], axis=1).astype(jnp.int32), N_CLASSES - 1)
    tile_k = first_tile[c_of_k] + kc - istart[c_of_k]
    prev_tile = jnp.concatenate([jnp.full((1,), -1, jnp.int32), tile_k[:-1]])
    next_tile = jnp.concatenate([tile_k[1:], jnp.full((1,), -1, jnp.int32)])
    first = valid & (tile_k != prev_tile)
    last = valid & ((tile_k != next_tile) | (k == total - 1))
    flags = (first * FLAG_FIRST + last * FLAG_LAST + valid * FLAG_VALID).astype(jnp.int32)
    pair_lo = jnp.array([p[0] for p in _CLASS_PAIRS], jnp.int32)
    pair_hi = jnp.array([p[1] for p in _CLASS_PAIRS], jnp.int32)
    base = (c_of_k // N_PAIRS) * EXPERTS_PER_GROUP
    e_lo = base + pair_lo[c_of_k % N_PAIRS]
    e_hi = base + pair_hi[c_of_k % N_PAIRS]
    cls_k = jnp.where(valid, c_of_k, -1)
    return dest.astype(jnp.int32), tile_k, e_lo, e_hi, cls_k, flags


def _const_spec(shape):
    nd = len(shape)
    return pl.BlockSpec(shape, lambda *_: (0,) * nd, pipeline_mode=pl.Buffered(1))


def _block_diag_ones(n, blk):
    r = jnp.arange(n) // blk
    return (r[:, None] == r[None, :]).astype(BF16)


def kernel(x, mem, positions, g_norm1, w_in, g_q, g_k, sinks, w_o_attn, w_conv_dw, b_conv_dw, g_conv_ln, b_conv_ln, w_conv_out, g_mem, w_kv_mem, g_xq, g_xk, w_o_mem, w_out, g_norm2, w_group, b_group, w_router, b_router, w_gate, w_up, w_down):
    B, S, D = x.shape
    M = mem.shape[1]
    assert D == D_MODEL and S % TQ == 0 and (B * S) % TM == 0 and w_in.shape[0] == 1
    assert (TQ // WINDOW) * ATT_KV_HEADS == MEM_HEADS and CONV_CH // LANES == N_BRANCHES + 1
    NT = S // TQ
    T = B * S
    l = 0

    bd64 = _block_diag_ones(Q_WIDTH, HEAD_DIM)
    bd128 = _block_diag_ones(XQ_WIDTH, MEM_HEAD_DIM)
    row = lambda v: v.reshape(1, -1).astype(F32)

    mk, mv = pl.pallas_call(
        _memkv_kernel,
        grid=(B,),
        in_specs=[pl.BlockSpec((1, M, D), lambda b: (b, 0, 0)),
                  _const_spec((1, D)), _const_spec((D, 2 * XQ_WIDTH)),
                  _const_spec((1, XQ_WIDTH)), _const_spec((XQ_WIDTH, XQ_WIDTH))],
        out_specs=[pl.BlockSpec((1, M, XQ_WIDTH), lambda b: (b, 0, 0))] * 2,
        out_shape=[jax.ShapeDtypeStruct((B, M, XQ_WIDTH), BF16)] * 2,
        compiler_params=pltpu.CompilerParams(dimension_semantics=("arbitrary",)),
        name="memkv",
    )(mem, row(g_mem[l]), w_kv_mem[l].astype(BF16), row(jnp.tile(g_xk[l], MEM_HEADS)), bd128)

    inv_freq = 1.0 / (ROPE_THETA ** (jnp.arange(0, HEAD_DIM, 2, dtype=F32) / HEAD_DIM))
    ang = positions.astype(F32)[..., None] * inv_freq
    cos, sin = jnp.cos(ang), jnp.sin(ang)
    rope = jnp.concatenate([cos, -sin, sin, cos], axis=-1)

    w = w_in[l]
    c0 = Q_WIDTH + 2 * KV_WIDTH
    c1 = c0 + GLU_WIDTH
    c2 = c1 + XQ_WIDTH
    w_qkv = w[:, :c0].astype(BF16)
    w_glu = w[:, c0:c1].astype(BF16)
    w_xq = w[:, c1:c2].astype(BF16)
    w_gt = w[:, c2:].astype(BF16)
    wdw = jnp.zeros((CONV_HALO, CONV_CH), F32).at[:CONV_WIDTH].set(w_conv_dw[l])
    w_rt = jnp.zeros((RT_ROWS, D), F32).at[0:N_GROUPS].set(w_group[l].T).at[8:8 + N_EXPERTS].set(w_router[l].T)
    b_rt = jnp.zeros((RT_ROWS, 1), F32).at[0:N_GROUPS, 0].set(b_group[l]).at[8:8 + N_EXPERTS, 0].set(b_router[l])

    n_mix = B * NT
    tile_in = lambda s: jnp.minimum(s, n_mix - 1)
    tile_out = lambda s: jnp.maximum(s - 1, 0)
    tile3 = lambda last: pl.BlockSpec((1, TQ, last), lambda s: (tile_in(s) // NT, tile_in(s) % NT, 0))
    per_batch = pl.BlockSpec((1, M, XQ_WIDTH), lambda s: (tile_in(s) // NT, 0, 0))
    in_specs = [
        tile3(D), tile3(LANES), per_batch, per_batch,
        _const_spec((1, D)),
        _const_spec((D, c0)), _const_spec((D, GLU_WIDTH)), _const_spec((D, XQ_WIDTH)), _const_spec((D, GATE_WIDTH)),
        _const_spec((1, Q_WIDTH)), _const_spec((1, LANES)),
        pl.BlockSpec(memory_space=pltpu.SMEM),
        _const_spec((Q_WIDTH, Q_WIDTH)), _const_spec((XQ_WIDTH, XQ_WIDTH)),
        _const_spec((Q_WIDTH, D)),
        _const_spec((CONV_HALO, CONV_CH)), _const_spec((1, CONV_CH)), _const_spec((1, CONV_CH)), _const_spec((1, CONV_CH)),
        _const_spec((CONV_CH, D)),
        _const_spec((1, XQ_WIDTH)),
        _const_spec((XQ_WIDTH, D)),
        _const_spec((D, D)),
        _const_spec((1, D)),
        _const_spec((RT_ROWS, D)), _const_spec((RT_ROWS, 1)), _const_spec((TQ, TQ)),
    ]
    tri = (jnp.arange(TQ)[:, None] <= jnp.arange(TQ)[None, :]).astype(BF16)
    per_tile = lambda rows, last: pl.BlockSpec((1, rows, last), lambda s: (tile_out(s), 0, 0))
    x1_spec = pl.BlockSpec((1, TQ, XW), lambda s: (tile_out(s) // NT, tile_out(s) % NT, 0))
    x1, rt, cnt = pl.pallas_call(
        functools.partial(_mixer_kernel, tiles_per_seq=NT),
        grid=(n_mix + 1,),
        in_specs=in_specs,
        out_specs=[x1_spec, per_tile(8, TQ), per_tile(8, LANES)],
        out_shape=[jax.ShapeDtypeStruct((B, S, XW), F32), jax.ShapeDtypeStruct((B * NT, 8, TQ), F32),
                   jax.ShapeDtypeStruct((B * NT, 8, LANES), F32)],
        scratch_shapes=[pltpu.VMEM((WINDOW + TQ, LANES), BF16)] * 4
        + [pltpu.VMEM((CONV_CH // LANES, CONV_HALO + TQ, LANES), F32),
           pltpu.VMEM((CONV_CH // LANES, TQ, LANES), F32), pltpu.VMEM((TQ, Q_WIDTH), BF16),
           pltpu.VMEM((TQ, XQ_WIDTH), BF16), pltpu.VMEM((TQ, D), F32)],
        compiler_params=pltpu.CompilerParams(dimension_semantics=("arbitrary",),
                                             vmem_limit_bytes=VMEM_LIMIT),
        name="mixer",
    )(x, rope, mk, mv, row(g_norm1[l]), w_qkv, w_glu, w_xq, w_gt,
      row(jnp.tile(g_q[l], ATT_HEADS) * (HEAD_DIM ** -0.5)), row(jnp.tile(g_k[l], ATT_KV_HEADS)),
      sinks[l].astype(F32), bd64, bd128, w_o_attn[l].astype(BF16),
      wdw, row(b_conv_dw[l]), row(g_conv_ln[l]), row(b_conv_ln[l]), w_conv_out[l].astype(BF16),
      row(jnp.tile(g_xq[l], MEM_HEADS) * (MEM_HEAD_DIM ** -0.5)), w_o_mem[l].astype(BF16),
      w_out[l].astype(BF16), row(g_norm2[l]), w_rt, b_rt, tri)

    n_tiles = T // TM
    plan = _routing_plan(rt[:, 0, :].astype(jnp.int32), rt[:, 1, :].astype(jnp.int32),
                         cnt[:, 0, :N_CLASSES].astype(jnp.int32), n_tiles)
    dest, tile_k, e_lo, e_hi, cls_k, flags = plan
    ni = tile_k.shape[0]
    wg, wu, wd = w_gate[l], w_up[l], w_down[l]
    lo_map = lambda k, dest, tile, elo, ehi, cls, flg: (elo[k], 0, 0)
    hi_map = lambda k, dest, tile, elo, ehi, cls, flg: (ehi[k], 0, 0)
    up_block = (1, D, EXPERT_FF)
    down_block = (1, EXPERT_FF, D)
    out = pl.pallas_call(
        _moe_kernel,
        grid_spec=pltpu.PrefetchScalarGridSpec(
            num_scalar_prefetch=6,
            grid=(ni,),
            in_specs=[pl.BlockSpec((1, D), lambda k, *_: (0, 0)),
                      pl.BlockSpec(up_block, lo_map), pl.BlockSpec(up_block, lo_map),
                      pl.BlockSpec(down_block, lo_map),
                      pl.BlockSpec(up_block, hi_map), pl.BlockSpec(up_block, hi_map),
                      pl.BlockSpec(down_block, hi_map),
                      pl.BlockSpec(memory_space=pl.ANY)],
            out_specs=pl.BlockSpec(memory_space=pl.ANY),
            scratch_shapes=[pltpu.SMEM((T,), jnp.int32),
                            pltpu.VMEM((GATHER_SLOTS, TM, XW), F32), pltpu.VMEM((2, TM, D), F32),
                            pltpu.VMEM((TM, D), BF16),
                            pltpu.VMEM((4, D, EXPERT_FF), BF16), pltpu.VMEM((2, EXPERT_FF, D), BF16),
                            pltpu.SemaphoreType.DMA((GATHER_SLOTS,)), pltpu.SemaphoreType.DMA((2,))]),
        out_shape=jax.ShapeDtypeStruct((T, D), F32),
        compiler_params=pltpu.CompilerParams(dimension_semantics=("arbitrary",),
                                             vmem_limit_bytes=VMEM_LIMIT),
        name="moe",
    )(dest, tile_k, e_lo, e_hi, cls_k, flags, row(g_norm2[l]), wg, wu, wd, wg, wu, wd,
      x1.reshape(T, XW))
    return out.reshape(B, S, D)
```

```python
import functools

import jax
import jax.numpy as jnp
from jax import lax
from jax.experimental import pallas as pl
from jax.experimental.pallas import tpu as pltpu

F32 = jnp.float32
BF16 = jnp.bfloat16

D_MODEL = 1024
ATT_HEADS = 8
ATT_KV_HEADS = 2
HEAD_DIM = 64
WINDOW = 128
ROPE_THETA = 10000.0
CONV_CH = 512
CONV_WIDTH = 31
MEM_HEADS = 4
MEM_HEAD_DIM = 128
N_BRANCHES = 3
N_GROUPS = 4
EXPERTS_PER_GROUP = 4
N_EXPERTS = N_GROUPS * EXPERTS_PER_GROUP
EXPERT_FF = 512
EPS = 1e-6
LN_EPS = 1e-5
NEG_INF = -1e30

Q_WIDTH = ATT_HEADS * HEAD_DIM
KV_WIDTH = ATT_KV_HEADS * HEAD_DIM
GLU_WIDTH = 2 * CONV_CH
XQ_WIDTH = MEM_HEADS * MEM_HEAD_DIM
GATE_WIDTH = N_BRANCHES * D_MODEL

LANES = 128
TQ = 256
CONV_HALO = 32
CONV_ROWS = 128
TM = 256
RT_ROWS = 32
N_PAIRS = EXPERTS_PER_GROUP * (EXPERTS_PER_GROUP - 1) // 2
N_CLASSES = N_GROUPS * N_PAIRS
CLS_ROWS = 32
XW = D_MODEL + LANES
VMEM_LIMIT = 56 * 1024 * 1024


def _dot(a, b):
    return jnp.dot(a, b, preferred_element_type=F32)


def _dot_nt(a, b):
    return lax.dot_general(a, b, (((1,), (1,)), ((), ())), preferred_element_type=F32)


def _split_bf16(t):
    hi = t.astype(BF16)
    lo = (t - hi.astype(F32)).astype(BF16)
    return hi, lo


def _block_sum(t, bd):
    return _dot(t.astype(BF16), bd)


def _rms(t):
    return t * lax.rsqrt(jnp.mean(t * t, axis=-1, keepdims=True) + EPS)


def _memkv_kernel(mem_ref, gmem_ref, wkv_ref, gxk_ref, bd_ref, mk_ref, mv_ref):
    mn = (_rms(mem_ref[0]) * gmem_ref[...]).astype(BF16)
    kv = _dot(mn, wkv_ref[...])
    mk = kv[:, :XQ_WIDTH]
    ss = _block_sum(mk * mk, bd_ref[...])
    mk = mk * lax.rsqrt(ss * (1.0 / MEM_HEAD_DIM) + EPS) * gxk_ref[...]
    mk_ref[0] = mk.astype(BF16)
    mv_ref[0] = kv[:, XQ_WIDTH:].astype(BF16)


def _mixer_kernel(x_ref, cos_ref, sin_ref, mk_ref, mv_ref, g1_ref, wqkv_ref, wglu_ref, wxq_ref,
                  wgt_ref, gq_ref, gk_ref, sinks_ref, bd64_ref, bd128_ref, woa_ref, wdw_ref,
                  bdw_ref, gln_ref, bln_ref, wco_ref, gxq_ref, wom_ref, wout_ref, g2_ref,
                  wrt_ref, brt_ref, tri_ref,
                  x1_ref, rt_ref, cnt_ref,
                  k0_ref, k1_ref, v0_ref, v1_ref, ubuf_ref, ybuf_ref, o_ref, om_ref, x1prev_ref,
                  cosd_ref, sind_ref, *, tiles_per_seq):
    s = pl.program_id(0)
    n_tiles = pl.num_programs(0) - 1
    i = jnp.minimum(s, n_tiles - 1) % tiles_per_seq
    kbands = (k0_ref, k1_ref)
    vbands = (v0_ref, v1_ref)

    @pl.when(s == 0)
    def _():
        x1prev_ref[...] = jnp.zeros((TQ, D_MODEL), F32)

    @pl.when(i == 0)
    def _():
        for r in kbands + vbands:
            r[0:WINDOW, :] = jnp.zeros((WINDOW, LANES), BF16)
        ubuf_ref[:, 0:CONV_HALO, :] = jnp.zeros((CONV_CH // LANES, CONV_HALO, LANES), F32)

    x1_prev = x1prev_ref[...]
    x1_ref[0, :, 0:D_MODEL] = x1_prev

    x = x_ref[0]
    h = (_rms(x) * g1_ref[...]).astype(BF16)

    n_slabs = CONV_CH // LANES
    half = TQ // 2
    group = ATT_HEADS // ATT_KV_HEADS

    glu = _dot(h, wglu_ref[...])
    route_logits = _router_logits(x1_prev, g2_ref, wrt_ref, brt_ref)
    u = glu[:, :CONV_CH] * jax.nn.sigmoid(glu[:, CONV_CH:])
    for c in range(n_slabs):
        ubuf_ref[c, CONV_HALO:CONV_HALO + TQ, :] = u[:, c * LANES:(c + 1) * LANES]
    base = CONV_HALO - (CONV_WIDTH - 1)
    acc = [[None] * n_slabs for _ in range(2)]

    def conv_slab(c):
        cols = slice(c * LANES, (c + 1) * LANES)
        for par in range(2):
            parts = []
            for q0 in range(0, half, CONV_ROWS):
                a = jnp.broadcast_to(bdw_ref[:, cols], (CONV_ROWS, LANES))
                for j in range(CONV_WIDTH):
                    a = a + wdw_ref[j:j + 1, cols] * ubuf_ref[
                        c, pl.ds(base + j + par + 2 * q0, CONV_ROWS, stride=2), :]
                parts.append(a)
            acc[par][c] = jnp.concatenate(parts, axis=0)
        ubuf_ref[c, 0:CONV_HALO, :] = ubuf_ref[c, TQ:TQ + CONV_HALO, :]

    def gate(b):
        return jax.nn.sigmoid(_dot(h, wgt_ref[:, b * D_MODEL:(b + 1) * D_MODEL]))

    def conv_finish():
        for par in range(2):
            row_sum = functools.reduce(jnp.add, [jnp.sum(a, axis=-1, keepdims=True) for a in acc[par]])
            mu = row_sum * (1.0 / CONV_CH)
            cen = [a - mu for a in acc[par]]
            sq_sum = functools.reduce(jnp.add, [jnp.sum(t * t, axis=-1, keepdims=True) for t in cen])
            rstd = lax.rsqrt(sq_sum * (1.0 / CONV_CH) + LN_EPS)
            for c in range(n_slabs):
                cols = slice(c * LANES, (c + 1) * LANES)
                yln = cen[c] * rstd * gln_ref[:, cols] + bln_ref[:, cols]
                ybuf_ref[c, pl.ds(par, half, stride=2), :] = yln * jax.nn.sigmoid(yln)
        uo = jnp.concatenate([ybuf_ref[c] for c in range(n_slabs)], axis=-1).astype(BF16)
        return _dot(uo, wco_ref[...])

    for c in range(n_slabs):
        conv_slab(c)
    qkv = _dot(h, wqkv_ref[...])
    gates = [gate(0)]
    q = qkv[:, :Q_WIDTH]
    k = qkv[:, Q_WIDTH:Q_WIDTH + KV_WIDTH]
    v = qkv[:, Q_WIDTH + KV_WIDTH:]
    bd64 = bd64_ref[...]
    q = q * lax.rsqrt(_block_sum(q * q, bd64) * (1.0 / HEAD_DIM) + EPS) * gq_ref[...]
    k = k * lax.rsqrt(_block_sum(k * k, bd64[:LANES, :LANES]) * (1.0 / HEAD_DIM) + EPS) * gk_ref[...]
    gates.append(gate(1))

    lane = lax.broadcasted_iota(jnp.int32, (TQ, LANES), 1)
    first_half = (lane % HEAD_DIM) < (HEAD_DIM // 2)
    low_head = lane < HEAD_DIM
    n_freq = HEAD_DIM // 2
    per_row = LANES // n_freq
    quarter = lax.broadcasted_iota(jnp.int32, (TQ // per_row, LANES), 1) // n_freq
    sign = jnp.where(quarter % 2 == 0, -1.0, 1.0)
    for table_ref, out_ref, scale in ((cos_ref, cosd_ref, None), (sin_ref, sind_ref, sign)):
        dense = table_ref[0]
        rolled = [dense] + [pltpu.roll(dense, n_freq * kk, 1) for kk in range(1, per_row)]
        for j in range(per_row):
            spread = rolled[(-j) % per_row]
            for qq in range(1, per_row):
                spread = jnp.where(quarter == qq, rolled[(qq - j) % per_row], spread)
            out_ref[pl.ds(j, TQ // per_row, stride=per_row), :] = spread if scale is None else spread * scale
    cosd = cosd_ref[...]
    sind = sind_ref[...]

    def rope(t):
        rot = jnp.where(first_half, pltpu.roll(t, LANES - HEAD_DIM // 2, 1),
                        pltpu.roll(t, HEAD_DIM // 2, 1))
        return t * cosd + rot * sind

    def dup_halves(t):
        swapped = pltpu.roll(t, HEAD_DIM, 1)
        return jnp.where(low_head, t, swapped), jnp.where(low_head, swapped, t)

    kd = dup_halves(rope(k))
    vd = dup_halves(v)
    for kvh in range(ATT_KV_HEADS):
        kbands[kvh][WINDOW:WINDOW + TQ, :] = kd[kvh].astype(BF16)
        vbands[kvh][WINDOW:WINDOW + TQ, :] = vd[kvh].astype(BF16)

    q_heads = []
    for c in range(Q_WIDTH // LANES):
        qc = rope(q[:, c * LANES:(c + 1) * LANES])
        q_heads.append(jnp.where(low_head, qc, 0.0).astype(BF16))
        q_heads.append(jnp.where(low_head, 0.0, qc).astype(BF16))

    qi = lax.broadcasted_iota(jnp.int32, (WINDOW, 2 * WINDOW), 0)
    kj = lax.broadcasted_iota(jnp.int32, (WINDOW, 2 * WINDOW), 1)
    band_mask = (kj > qi) & (kj <= qi + WINDOW)
    first_mask = band_mask & (kj >= jnp.where(i > 0, 0, WINDOW))
    low_head_w = lax.broadcasted_iota(jnp.int32, (WINDOW, LANES), 1) < HEAD_DIM

    def attention_scores(step):
        n, kvh = divmod(step, ATT_KV_HEADS)
        rows = slice(n * WINDOW, (n + 1) * WINDOW)
        mask = first_mask if n == 0 else band_mask
        heads = range(kvh * group, (kvh + 1) * group)
        q_st = jnp.concatenate([q_heads[hd][rows] for hd in heads], axis=0)
        kband = kbands[kvh][n * WINDOW:n * WINDOW + 2 * WINDOW, :]
        s_all = _dot_nt(q_st, kband)
        es, dens = [], []
        for j, hd in enumerate(heads):
            s = jnp.where(mask, s_all[j * WINDOW:(j + 1) * WINDOW], NEG_INF)
            sink = sinks_ref[hd]
            m = jnp.maximum(jnp.max(s, axis=-1, keepdims=True), sink)
            e = jnp.exp(s - m)
            dens.append(jnp.sum(e, axis=-1, keepdims=True) + jnp.exp(sink - m))
            es.append(e.astype(BF16))
        return jnp.concatenate(es, axis=0), dens

    def attention_output(step, probs):
        n, kvh = divmod(step, ATT_KV_HEADS)
        rows = slice(n * WINDOW, (n + 1) * WINDOW)
        e_all, dens = probs
        vband = vbands[kvh][n * WINDOW:n * WINDOW + 2 * WINDOW, :]
        o_all = _dot(e_all, vband)
        for cc in range(group // 2):
            oa = o_all[(2 * cc) * WINDOW:(2 * cc + 1) * WINDOW] / dens[2 * cc]
            ob = o_all[(2 * cc + 1) * WINDOW:(2 * cc + 2) * WINDOW] / dens[2 * cc + 1]
            col = kvh * (group // 2) + cc
            o_ref[rows, col * LANES:(col + 1) * LANES] = jnp.where(low_head_w, oa, ob).astype(BF16)

    def mem_scores(hd, xq):
        cols = slice(hd * MEM_HEAD_DIM, (hd + 1) * MEM_HEAD_DIM)
        s = _dot_nt(xq[:, cols], mk_ref[0, :, cols])
        e = jnp.exp(s - jnp.max(s, axis=-1, keepdims=True))
        return e.astype(BF16), jnp.sum(e, axis=-1, keepdims=True)

    def mem_output(hd, probs):
        cols = slice(hd * MEM_HEAD_DIM, (hd + 1) * MEM_HEAD_DIM)
        e, den = probs
        om_ref[:, cols] = (_dot(e, mv_ref[0, :, cols]) / den).astype(BF16)

    route = _router_select(route_logits)
    p0 = attention_scores(0)
    gates.append(gate(2))
    p1 = attention_scores(1)
    attention_output(0, p0)
    xq = _dot(h, wxq_ref[...])
    p2 = attention_scores(2)
    attention_output(1, p1)
    _router_emit(*route, tri_ref, x1_ref, rt_ref, cnt_ref)
    xq = xq * lax.rsqrt(_block_sum(xq * xq, bd128_ref[...]) * (1.0 / MEM_HEAD_DIM) + EPS) * gxq_ref[...]
    xq = xq.astype(BF16)
    p3 = attention_scores(3)
    attention_output(2, p2)
    y_conv = conv_finish()
    m0 = mem_scores(0, xq)
    attention_output(3, p3)
    for r in kbands + vbands:
        r[0:WINDOW, :] = r[TQ:TQ + WINDOW, :]
    m1 = mem_scores(1, xq)
    y_attn = _dot(o_ref[...], woa_ref[...])
    mem_output(0, m0)
    m2 = mem_scores(2, xq)
    mem_output(1, m1)
    m3 = mem_scores(3, xq)
    mem_output(2, m2)
    merged = gates[0] * y_attn + gates[1] * y_conv
    mem_output(3, m3)
    merged = merged + gates[2] * _dot(om_ref[...], wom_ref[...])

    x1prev_ref[...] = x + _dot(merged.astype(BF16), wout_ref[...])


def _router_logits(x1, g2_ref, wrt_ref, brt_ref):
    h2 = _rms(x1) * g2_ref[...]
    h_hi, h_lo = _split_bf16(h2)
    w_hi, w_lo = _split_bf16(wrt_ref[...])
    return _dot_nt(w_hi, h_hi) + _dot_nt(w_hi, h_lo) + _dot_nt(w_lo, h_hi) + brt_ref[...]


def _router_select(lt):
    gl = [lt[g:g + 1] for g in range(N_GROUPS)]
    gmax = functools.reduce(jnp.maximum, gl)
    g_idx = jnp.full((1, TQ), N_GROUPS - 1, jnp.int32)
    for g in reversed(range(N_GROUPS - 1)):
        g_idx = jnp.where(gl[g] == gmax, g, g_idx)
    p_g = 1.0 / functools.reduce(jnp.add, [jnp.exp(r - gmax) for r in gl])
    el = []
    for kk in range(EXPERTS_PER_GROUP):
        row = lt[8 + kk:9 + kk]
        for g in range(1, N_GROUPS):
            off = 8 + g * EXPERTS_PER_GROUP + kk
            row = jnp.where(g_idx == g, lt[off:off + 1], row)
        el.append(row)

    def first_argmax(rows):
        top = functools.reduce(jnp.maximum, rows)
        idx = jnp.full((1, TQ), len(rows) - 1, jnp.int32)
        for kk in reversed(range(len(rows) - 1)):
            idx = jnp.where(rows[kk] == top, kk, idx)
        return top, idx

    v1, i1 = first_argmax(el)
    v2, i2 = first_argmax([jnp.where(i1 == kk, -jnp.inf, el[kk]) for kk in range(EXPERTS_PER_GROUP)])
    t = jnp.exp(v2 - v1)
    p1 = 1.0 / (1.0 + t)
    p2 = t * p1

    lo = jnp.minimum(i1, i2)
    hi = jnp.maximum(i1, i2)
    w_lo = jnp.where(i1 < i2, p_g * p1, p_g * p2)
    w_hi = jnp.where(i1 < i2, p_g * p2, p_g * p1)
    cls = g_idx * N_PAIRS + ((lo * (2 * EXPERTS_PER_GROUP - 1 - lo)) >> 1) + hi - lo - 1
    return cls, w_lo, w_hi


def _router_emit(cls, w_lo, w_hi, tri_ref, x1_ref, rt_ref, cnt_ref):
    cls_f = cls.astype(F32)
    onehot = jnp.broadcast_to(cls, (CLS_ROWS, TQ)) == lax.broadcasted_iota(jnp.int32, (CLS_ROWS, TQ), 0)
    onehot_bf = onehot.astype(F32).astype(BF16)
    prefix = _dot(onehot_bf, tri_ref[...])
    rank = jnp.sum(jnp.where(onehot, prefix, 0.0), axis=0, keepdims=True) - 1.0
    counts = _dot_nt(jnp.ones((8, TQ), BF16), onehot_bf)
    zero = jnp.zeros((1, TQ), F32)
    rt_ref[0] = jnp.concatenate([cls_f, rank] + [zero] * 6, axis=0)
    cnt_ref[0] = jnp.concatenate([counts, jnp.zeros((8, LANES - CLS_ROWS), F32)], axis=1)
    cols = jnp.concatenate([w_lo, w_hi, cls_f, jnp.zeros((LANES - 3, TQ), F32)], axis=0)
    x1_ref[0, :, D_MODEL:] = cols.T


FLAG_FIRST, FLAG_LAST, FLAG_VALID = 1, 2, 4
GATHER_SLOTS = 3


def _moe_kernel(dest_ref, tile_ref, elo_ref, ehi_ref, cls_ref, flag_ref,
                g2_ref, wg_lo, wu_lo, wd_lo, wg_hi, wu_hi, wd_hi, x1_hbm,
                out_hbm,
                perm_ref, xbuf, obuf, h2buf, wbuf_up, wbuf_down, gsem, ssem):
    k = pl.program_id(0)
    nk = pl.num_programs(0)
    t = tile_ref[k]
    xslot = t % GATHER_SLOTS
    oslot = t % 2
    flags = flag_ref[k]
    n_tok = x1_hbm.shape[0]
    n_tiles = n_tok // TM

    def gather_copy(r, tile, s):
        tok = perm_ref[tile * TM + r]
        return pltpu.make_async_copy(x1_hbm.at[pl.ds(tok, 1), :], xbuf.at[s, pl.ds(r, 1), :], gsem.at[s])

    def scatter_copy(r, tile, s):
        tok = perm_ref[tile * TM + r]
        return pltpu.make_async_copy(obuf.at[s, pl.ds(r, 1), :], out_hbm.at[pl.ds(tok, 1), :], ssem.at[s])

    k_prev = jnp.maximum(k - 1, 0)
    for side, (e_ref, wg, wu, wd) in enumerate(((elo_ref, wg_lo, wu_lo, wd_lo), (ehi_ref, wg_hi, wu_hi, wd_hi))):
        @pl.when((k == 0) | (e_ref[k] != e_ref[k_prev]))
        def _(side=side, wg=wg, wu=wu, wd=wd):
            wbuf_up[2 * side] = wg[0].astype(BF16)
            wbuf_up[2 * side + 1] = wu[0].astype(BF16)
            wbuf_down[side] = wd[0].astype(BF16)

    def for_rows(fn):
        def body(r, carry):
            fn(r)
            return carry
        lax.fori_loop(0, TM, body, 0, unroll=8)

    @pl.when(k == 0)
    def _():
        def invert(tok, carry):
            perm_ref[dest_ref[tok]] = tok
            return carry
        lax.fori_loop(0, n_tok, invert, 0, unroll=8)
        for tile in range(GATHER_SLOTS - 1):
            for_rows(lambda r, tile=tile: gather_copy(r, tile, tile).start())

    is_first = (flags & FLAG_FIRST) != 0
    is_valid = (flags & FLAG_VALID) != 0
    ahead = GATHER_SLOTS - 1
    prefetch_tile = jnp.minimum(t + ahead, n_tiles - 1)
    prefetch_slot = (t + ahead) % GATHER_SLOTS

    @pl.when(is_first)
    def _():
        for_rows(lambda r: gather_copy(r, t, xslot).wait())

        @pl.when(t >= 2)
        def _():
            for_rows(lambda r: scatter_copy(r, t - 2, oslot).wait())

        xr = xbuf[xslot, :, 0:D_MODEL]
        h2buf[...] = (_rms(xr) * g2_ref[...]).astype(BF16)
        obuf[oslot] = xr

    def experts():
        w = xbuf[xslot, :, D_MODEL:XW]
        mine = w[:, 2:3] == cls_ref[k].astype(F32)
        h2 = h2buf[...]
        y = None
        for side in range(2):
            c = jnp.where(mine, w[:, side:side + 1], 0.0)
            hid = jax.nn.silu(_dot(h2, wbuf_up[2 * side])) * _dot(h2, wbuf_up[2 * side + 1]) * c
            part = _dot(hid.astype(BF16), wbuf_down[side])
            y = part if y is None else y + part
        obuf[oslot] += y

    @pl.when(is_first & (t > 0))
    def _():
        for r in range(TM):
            gather_copy(r, prefetch_tile, prefetch_slot).start()
            scatter_copy(r, t - 1, 1 - oslot).start()
        experts()

    @pl.when(is_first & (t == 0))
    def _():
        for r in range(TM):
            gather_copy(r, prefetch_tile, prefetch_slot).start()
        experts()

    @pl.when(is_valid & jnp.logical_not(is_first))
    def _():
        experts()

    @pl.when(k == nk - 1)
    def _():
        last = n_tiles - 1
        for_rows(lambda r: scatter_copy(r, last, last % 2).start())
        for tile in (last - 1, last):
            for_rows(lambda r, tile=tile: scatter_copy(r, tile, tile % 2).wait())
        for extra in range(1, GATHER_SLOTS):
            for_rows(lambda r, extra=extra: gather_copy(r, last, (last + extra) % GATHER_SLOTS).wait())


_CLASS_PAIRS = [(a, b) for a in range(EXPERTS_PER_GROUP) for b in range(a + 1, EXPERTS_PER_GROUP)]


def _routing_plan(cls, rank, tile_counts, n_tiles):
    counts = jnp.sum(tile_counts, axis=0)
    off = jnp.concatenate([jnp.zeros((1,), jnp.int32), jnp.cumsum(counts).astype(jnp.int32)])
    tile_base = jnp.cumsum(tile_counts, axis=0) - tile_counts + off[None, :-1]
    onehot = cls[:, :, None] == jnp.arange(N_CLASSES, dtype=jnp.int32)[None, None, :]
    dest = (jnp.sum(jnp.where(onehot, tile_base[:, None, :], 0), axis=-1) + rank).reshape(-1)
    first_tile = off[:-1] // TM
    last_tile = (off[1:] - 1) // TM
    n_items = jnp.where(counts > 0, last_tile - first_tile + 1, 0)
    istart = jnp.concatenate([jnp.zeros((1,), jnp.int32), jnp.cumsum(n_items).astype(jnp.int32)])
    total = istart[-1]
    ni = n_tiles + N_CLASSES - 1
    k = jnp.arange(ni, dtype=jnp.int32)
    valid = k < total
    kc = jnp.minimum(k, total - 1)
    c_of_k = jnp.minimum(jnp.sum(istart[None, 1:] <= kc[:, None], axis=1).astype(jnp.int32), N_CLASSES - 1)
    tile_k = first_tile[c_of_k] + kc - istart[c_of_k]
    prev_tile = jnp.concatenate([jnp.full((1,), -1, jnp.int32), tile_k[:-1]])
    next_tile = jnp.concatenate([tile_k[1:], jnp.full((1,), -1, jnp.int32)])
    first = valid & (tile_k != prev_tile)
    last = valid & ((tile_k != next_tile) | (k == total - 1))
    flags = (first * FLAG_FIRST + last * FLAG_LAST + valid * FLAG_VALID).astype(jnp.int32)
    pair_lo = jnp.array([p[0] for p in _CLASS_PAIRS], jnp.int32)
    pair_hi = jnp.array([p[1] for p in _CLASS_PAIRS], jnp.int32)
    base = (c_of_k // N_PAIRS) * EXPERTS_PER_GROUP
    e_lo = base + pair_lo[c_of_k % N_PAIRS]
    e_hi = base + pair_hi[c_of_k % N_PAIRS]
    cls_k = jnp.where(valid, c_of_k, -1)
    return dest.astype(jnp.int32), tile_k, e_lo, e_hi, cls_k, flags


def _const_spec(shape):
    nd = len(shape)
    return pl.BlockSpec(shape, lambda *_: (0,) * nd, pipeline_mode=pl.Buffered(1))


def _block_diag_ones(n, blk):
    r = jnp.arange(n) // blk
    return (r[:, None] == r[None, :]).astype(BF16)


def kernel(x, mem, positions, g_norm1, w_in, g_q, g_k, sinks, w_o_attn, w_conv_dw, b_conv_dw, g_conv_ln, b_conv_ln, w_conv_out, g_mem, w_kv_mem, g_xq, g_xk, w_o_mem, w_out, g_norm2, w_group, b_group, w_router, b_router, w_gate, w_up, w_down):
    B, S, D = x.shape
    M = mem.shape[1]
    assert D == D_MODEL and S % TQ == 0 and (B * S) % TM == 0 and w_in.shape[0] == 1
    assert (TQ // WINDOW) * ATT_KV_HEADS == MEM_HEADS and CONV_CH // LANES == N_BRANCHES + 1
    NT = S // TQ
    T = B * S
    l = 0

    bd64 = _block_diag_ones(Q_WIDTH, HEAD_DIM)
    bd128 = _block_diag_ones(XQ_WIDTH, MEM_HEAD_DIM)
    row = lambda v: v.reshape(1, -1).astype(F32)

    mk, mv = pl.pallas_call(
        _memkv_kernel,
        grid=(B,),
        in_specs=[pl.BlockSpec((1, M, D), lambda b: (b, 0, 0)),
                  _const_spec((1, D)), _const_spec((D, 2 * XQ_WIDTH)),
                  _const_spec((1, XQ_WIDTH)), _const_spec((XQ_WIDTH, XQ_WIDTH))],
        out_specs=[pl.BlockSpec((1, M, XQ_WIDTH), lambda b: (b, 0, 0))] * 2,
        out_shape=[jax.ShapeDtypeStruct((B, M, XQ_WIDTH), BF16)] * 2,
        compiler_params=pltpu.CompilerParams(dimension_semantics=("arbitrary",)),
        name="memkv",
    )(mem, row(g_mem[l]), w_kv_mem[l].astype(BF16), row(jnp.tile(g_xk[l], MEM_HEADS)), bd128)

    inv_freq = 1.0 / (ROPE_THETA ** (jnp.arange(0, HEAD_DIM, 2, dtype=F32) / HEAD_DIM))
    per_row = LANES // (HEAD_DIM // 2)
    ang = positions.astype(F32).reshape(B, S // per_row, per_row, 1) * inv_freq
    ang = ang.reshape(B, S // per_row, LANES)
    cos_tab, sin_tab = jnp.cos(ang), jnp.sin(ang)

    w = w_in[l]
    c0 = Q_WIDTH + 2 * KV_WIDTH
    c1 = c0 + GLU_WIDTH
    c2 = c1 + XQ_WIDTH
    w_qkv = w[:, :c0].astype(BF16)
    w_glu = w[:, c0:c1].astype(BF16)
    w_xq = w[:, c1:c2].astype(BF16)
    w_gt = w[:, c2:].astype(BF16)
    wdw = jnp.zeros((CONV_HALO, CONV_CH), F32).at[:CONV_WIDTH].set(w_conv_dw[l])
    w_rt = jnp.zeros((RT_ROWS, D), F32).at[0:N_GROUPS].set(w_group[l].T).at[8:8 + N_EXPERTS].set(w_router[l].T)
    b_rt = jnp.zeros((RT_ROWS, 1), F32).at[0:N_GROUPS, 0].set(b_group[l]).at[8:8 + N_EXPERTS, 0].set(b_router[l])

    n_mix = B * NT
    tile_in = lambda s: jnp.minimum(s, n_mix - 1)
    tile_out = lambda s: jnp.maximum(s - 1, 0)
    tile3 = lambda last: pl.BlockSpec((1, TQ, last), lambda s: (tile_in(s) // NT, tile_in(s) % NT, 0))
    per_batch = pl.BlockSpec((1, M, XQ_WIDTH), lambda s: (tile_in(s) // NT, 0, 0))
    rope_spec = pl.BlockSpec((1, TQ // per_row, LANES), lambda s: (tile_in(s) // NT, tile_in(s) % NT, 0))
    in_specs = [
        tile3(D), rope_spec, rope_spec, per_batch, per_batch,
        _const_spec((1, D)),
        _const_spec((D, c0)), _const_spec((D, GLU_WIDTH)), _const_spec((D, XQ_WIDTH)), _const_spec((D, GATE_WIDTH)),
        _const_spec((1, Q_WIDTH)), _const_spec((1, LANES)),
        pl.BlockSpec(memory_space=pltpu.SMEM),
        _const_spec((Q_WIDTH, Q_WIDTH)), _const_spec((XQ_WIDTH, XQ_WIDTH)),
        _const_spec((Q_WIDTH, D)),
        _const_spec((CONV_HALO, CONV_CH)), _const_spec((1, CONV_CH)), _const_spec((1, CONV_CH)), _const_spec((1, CONV_CH)),
        _const_spec((CONV_CH, D)),
        _const_spec((1, XQ_WIDTH)),
        _const_spec((XQ_WIDTH, D)),
        _const_spec((D, D)),
        _const_spec((1, D)),
        _const_spec((RT_ROWS, D)), _const_spec((RT_ROWS, 1)), _const_spec((TQ, TQ)),
    ]
    tri = (jnp.arange(TQ)[:, None] <= jnp.arange(TQ)[None, :]).astype(BF16)
    per_tile = lambda rows, last: pl.BlockSpec((1, rows, last), lambda s: (tile_out(s), 0, 0))
    x1_spec = pl.BlockSpec((1, TQ, XW), lambda s: (tile_out(s) // NT, tile_out(s) % NT, 0))
    x1, rt, cnt = pl.pallas_call(
        functools.partial(_mixer_kernel, tiles_per_seq=NT),
        grid=(n_mix + 1,),
        in_specs=in_specs,
        out_specs=[x1_spec, per_tile(8, TQ), per_tile(8, LANES)],
        out_shape=[jax.ShapeDtypeStruct((B, S, XW), F32), jax.ShapeDtypeStruct((B * NT, 8, TQ), F32),
                   jax.ShapeDtypeStruct((B * NT, 8, LANES), F32)],
        scratch_shapes=[pltpu.VMEM((WINDOW + TQ, LANES), BF16)] * 4
        + [pltpu.VMEM((CONV_CH // LANES, CONV_HALO + TQ, LANES), F32),
           pltpu.VMEM((CONV_CH // LANES, TQ, LANES), F32), pltpu.VMEM((TQ, Q_WIDTH), BF16),
           pltpu.VMEM((TQ, XQ_WIDTH), BF16), pltpu.VMEM((TQ, D), F32),
           pltpu.VMEM((TQ, LANES), F32), pltpu.VMEM((TQ, LANES), F32)],
        compiler_params=pltpu.CompilerParams(dimension_semantics=("arbitrary",),
                                             vmem_limit_bytes=VMEM_LIMIT),
        name="mixer",
    )(x, cos_tab, sin_tab, mk, mv, row(g_norm1[l]), w_qkv, w_glu, w_xq, w_gt,
      row(jnp.tile(g_q[l], ATT_HEADS) * (HEAD_DIM ** -0.5)), row(jnp.tile(g_k[l], ATT_KV_HEADS)),
      sinks[l].astype(F32), bd64, bd128, w_o_attn[l].astype(BF16),
      wdw, row(b_conv_dw[l]), row(g_conv_ln[l]), row(b_conv_ln[l]), w_conv_out[l].astype(BF16),
      row(jnp.tile(g_xq[l], MEM_HEADS) * (MEM_HEAD_DIM ** -0.5)), w_o_mem[l].astype(BF16),
      w_out[l].astype(BF16), row(g_norm2[l]), w_rt, b_rt, tri)

    n_tiles = T // TM
    plan = _routing_plan(rt[:, 0, :].astype(jnp.int32), rt[:, 1, :].astype(jnp.int32),
                         cnt[:, 0, :N_CLASSES].astype(jnp.int32), n_tiles)
    dest, tile_k, e_lo, e_hi, cls_k, flags = plan
    ni = tile_k.shape[0]
    wg, wu, wd = w_gate[l], w_up[l], w_down[l]
    lo_map = lambda k, dest, tile, elo, ehi, cls, flg: (elo[k], 0, 0)
    hi_map = lambda k, dest, tile, elo, ehi, cls, flg: (ehi[k], 0, 0)
    up_block = (1, D, EXPERT_FF)
    down_block = (1, EXPERT_FF, D)
    out = pl.pallas_call(
        _moe_kernel,
        grid_spec=pltpu.PrefetchScalarGridSpec(
            num_scalar_prefetch=6,
            grid=(ni,),
            in_specs=[pl.BlockSpec((1, D), lambda k, *_: (0, 0)),
                      pl.BlockSpec(up_block, lo_map), pl.BlockSpec(up_block, lo_map),
                      pl.BlockSpec(down_block, lo_map),
                      pl.BlockSpec(up_block, hi_map), pl.BlockSpec(up_block, hi_map),
                      pl.BlockSpec(down_block, hi_map),
                      pl.BlockSpec(memory_space=pl.ANY)],
            out_specs=pl.BlockSpec(memory_space=pl.ANY),
            scratch_shapes=[pltpu.SMEM((T,), jnp.int32),
                            pltpu.VMEM((GATHER_SLOTS, TM, XW), F32), pltpu.VMEM((2, TM, D), F32),
                            pltpu.VMEM((TM, D), BF16),
                            pltpu.VMEM((4, D, EXPERT_FF), BF16), pltpu.VMEM((2, EXPERT_FF, D), BF16),
                            pltpu.SemaphoreType.DMA((GATHER_SLOTS,)), pltpu.SemaphoreType.DMA((2,))]),
        out_shape=jax.ShapeDtypeStruct((T, D), F32),
        compiler_params=pltpu.CompilerParams(dimension_semantics=("arbitrary",),
                                             vmem_limit_bytes=VMEM_LIMIT),
        name="moe",
    )(dest, tile_k, e_lo, e_hi, cls_k, flags, row(g_norm2[l]), wg, wu, wd, wg, wu, wd,
      x1.reshape(T, XW))
    return out.reshape(B, S, D)
```

```python
import functools

import jax
import jax.numpy as jnp
from jax import lax
from jax.experimental import pallas as pl
from jax.experimental.pallas import tpu as pltpu

F32 = jnp.float32
BF16 = jnp.bfloat16

D_MODEL = 1024
ATT_HEADS = 8
ATT_KV_HEADS = 2
HEAD_DIM = 64
WINDOW = 128
ROPE_THETA = 10000.0
CONV_CH = 512
CONV_WIDTH = 31
MEM_HEADS = 4
MEM_HEAD_DIM = 128
N_BRANCHES = 3
N_GROUPS = 4
EXPERTS_PER_GROUP = 4
N_EXPERTS = N_GROUPS * EXPERTS_PER_GROUP
EXPERT_FF = 512
EPS = 1e-6
LN_EPS = 1e-5
NEG_INF = -1e30

Q_WIDTH = ATT_HEADS * HEAD_DIM
KV_WIDTH = ATT_KV_HEADS * HEAD_DIM
GLU_WIDTH = 2 * CONV_CH
XQ_WIDTH = MEM_HEADS * MEM_HEAD_DIM
GATE_WIDTH = N_BRANCHES * D_MODEL

LANES = 128
TQ = 256
CONV_HALO = 32
CONV_ROWS = 128
TM = 256
RT_ROWS = 32
N_PAIRS = EXPERTS_PER_GROUP * (EXPERTS_PER_GROUP - 1) // 2
N_CLASSES = N_GROUPS * N_PAIRS
CLS_ROWS = 32
XW = D_MODEL + LANES
VMEM_LIMIT = 56 * 1024 * 1024


def _dot(a, b):
    return jnp.dot(a, b, preferred_element_type=F32)


def _dot_nt(a, b):
    return lax.dot_general(a, b, (((1,), (1,)), ((), ())), preferred_element_type=F32)


def _split_bf16(t):
    hi = t.astype(BF16)
    lo = (t - hi.astype(F32)).astype(BF16)
    return hi, lo


def _block_sum(t, bd):
    return _dot(t.astype(BF16), bd)


def _rms(t):
    return t * lax.rsqrt(jnp.mean(t * t, axis=-1, keepdims=True) + EPS)


def _memkv_kernel(mem_ref, gmem_ref, wkv_ref, gxk_ref, bd_ref, mk_ref, mv_ref):
    mn = (_rms(mem_ref[0]) * gmem_ref[...]).astype(BF16)
    kv = _dot(mn, wkv_ref[...])
    mk = kv[:, :XQ_WIDTH]
    ss = _block_sum(mk * mk, bd_ref[...])
    mk = mk * lax.rsqrt(ss * (1.0 / MEM_HEAD_DIM) + EPS) * gxk_ref[...]
    mk_ref[0] = mk.astype(BF16)
    mv_ref[0] = kv[:, XQ_WIDTH:].astype(BF16)


def _mixer_kernel(x_ref, cos_ref, sin_ref, mk_ref, mv_ref, g1_ref, wqkv_ref, wglu_ref, wxq_ref,
                  wgt_ref, gq_ref, gk_ref, sinks_ref, woa_ref, wdw_ref,
                  bdw_ref, gln_ref, bln_ref, wco_ref, gxq_ref, wom_ref, wout_ref, g2_ref,
                  wrt_ref, brt_ref, tri_ref,
                  x1_ref, rt_ref, cnt_ref,
                  k0_ref, k1_ref, v0_ref, v1_ref, ubuf_ref, ybuf_ref, o_ref, om_ref, x1prev_ref,
                  cosd_ref, sind_ref, *, tiles_per_seq):
    s = pl.program_id(0)
    n_tiles = pl.num_programs(0) - 1
    i = jnp.minimum(s, n_tiles - 1) % tiles_per_seq
    kbands = (k0_ref, k1_ref)
    vbands = (v0_ref, v1_ref)

    @pl.when(s == 0)
    def _():
        x1prev_ref[...] = jnp.zeros((TQ, D_MODEL), F32)

    @pl.when(i == 0)
    def _():
        for r in kbands + vbands:
            r[0:WINDOW, :] = jnp.zeros((WINDOW, LANES), BF16)
        ubuf_ref[:, 0:CONV_HALO, :] = jnp.zeros((CONV_CH // LANES, CONV_HALO, LANES), F32)

    x1_prev = x1prev_ref[...]
    x1_ref[0, :, 0:D_MODEL] = x1_prev

    x = x_ref[0]
    h = (_rms(x) * g1_ref[...]).astype(BF16)

    n_slabs = CONV_CH // LANES
    half = TQ // 2
    group = ATT_HEADS // ATT_KV_HEADS

    glu = _dot(h, wglu_ref[...])
    route_logits = _router_logits(x1_prev, g2_ref, wrt_ref, brt_ref)
    u = glu[:, :CONV_CH] * jax.nn.sigmoid(glu[:, CONV_CH:])
    for c in range(n_slabs):
        ubuf_ref[c, CONV_HALO:CONV_HALO + TQ, :] = u[:, c * LANES:(c + 1) * LANES]
    base = CONV_HALO - (CONV_WIDTH - 1)
    acc = [[None] * n_slabs for _ in range(2)]

    def conv_slab(c):
        cols = slice(c * LANES, (c + 1) * LANES)
        for par in range(2):
            parts = []
            for q0 in range(0, half, CONV_ROWS):
                a = jnp.broadcast_to(bdw_ref[:, cols], (CONV_ROWS, LANES))
                for j in range(CONV_WIDTH):
                    a = a + wdw_ref[j:j + 1, cols] * ubuf_ref[
                        c, pl.ds(base + j + par + 2 * q0, CONV_ROWS, stride=2), :]
                parts.append(a)
            acc[par][c] = jnp.concatenate(parts, axis=0)
        ubuf_ref[c, 0:CONV_HALO, :] = ubuf_ref[c, TQ:TQ + CONV_HALO, :]

    def gate(b):
        return jax.nn.sigmoid(_dot(h, wgt_ref[:, b * D_MODEL:(b + 1) * D_MODEL]))

    def conv_finish():
        for par in range(2):
            row_sum = functools.reduce(jnp.add, [jnp.sum(a, axis=-1, keepdims=True) for a in acc[par]])
            mu = row_sum * (1.0 / CONV_CH)
            cen = [a - mu for a in acc[par]]
            sq_sum = functools.reduce(jnp.add, [jnp.sum(t * t, axis=-1, keepdims=True) for t in cen])
            rstd = lax.rsqrt(sq_sum * (1.0 / CONV_CH) + LN_EPS)
            for c in range(n_slabs):
                cols = slice(c * LANES, (c + 1) * LANES)
                yln = cen[c] * rstd * gln_ref[:, cols] + bln_ref[:, cols]
                ybuf_ref[c, pl.ds(par, half, stride=2), :] = yln * jax.nn.sigmoid(yln)
        uo = jnp.concatenate([ybuf_ref[c] for c in range(n_slabs)], axis=-1).astype(BF16)
        return _dot(uo, wco_ref[...])

    for c in range(n_slabs):
        conv_slab(c)
    qkv = _dot(h, wqkv_ref[...])
    gates = [gate(0)]
    q = qkv[:, :Q_WIDTH]
    k = qkv[:, Q_WIDTH:Q_WIDTH + KV_WIDTH]
    v = qkv[:, Q_WIDTH + KV_WIDTH:]
    gates.append(gate(1))

    lane = lax.broadcasted_iota(jnp.int32, (TQ, LANES), 1)
    first_half = (lane % HEAD_DIM) < (HEAD_DIM // 2)
    low_head = lane < HEAD_DIM
    n_freq = HEAD_DIM // 2
    per_row = LANES // n_freq
    quarter = lax.broadcasted_iota(jnp.int32, (TQ // per_row, LANES), 1) // n_freq
    sign = jnp.where(quarter % 2 == 0, -1.0, 1.0)
    for table_ref, out_ref, scale in ((cos_ref, cosd_ref, None), (sin_ref, sind_ref, sign)):
        dense = table_ref[0]
        rolled = [dense] + [pltpu.roll(dense, n_freq * kk, 1) for kk in range(1, per_row)]
        for j in range(per_row):
            spread = rolled[(-j) % per_row]
            for qq in range(1, per_row):
                spread = jnp.where(quarter == qq, rolled[(qq - j) % per_row], spread)
            out_ref[pl.ds(j, TQ // per_row, stride=per_row), :] = spread if scale is None else spread * scale
    cosd = cosd_ref[...]
    sind = sind_ref[...]

    def rope(t):
        rot = jnp.where(first_half, pltpu.roll(t, LANES - HEAD_DIM // 2, 1),
                        pltpu.roll(t, HEAD_DIM // 2, 1))
        return t * cosd + rot * sind

    def dup_halves(t):
        swapped = pltpu.roll(t, HEAD_DIM, 1)
        return jnp.where(low_head, t, swapped), jnp.where(low_head, swapped, t)

    def qk_norm(t, gain):
        sq = t * t
        lo = jnp.sum(jnp.where(low_head, sq, 0.0), axis=-1, keepdims=True)
        hi = jnp.sum(jnp.where(low_head, 0.0, sq), axis=-1, keepdims=True)
        inv = jnp.where(low_head, lax.rsqrt(lo * (1.0 / HEAD_DIM) + EPS), lax.rsqrt(hi * (1.0 / HEAD_DIM) + EPS))
        return t * inv * gain

    kd = dup_halves(rope(qk_norm(k, gk_ref[...])))
    vd = dup_halves(v)
    for kvh in range(ATT_KV_HEADS):
        kbands[kvh][WINDOW:WINDOW + TQ, :] = kd[kvh].astype(BF16)
        vbands[kvh][WINDOW:WINDOW + TQ, :] = vd[kvh].astype(BF16)

    q_heads = []
    for c in range(Q_WIDTH // LANES):
        cols = slice(c * LANES, (c + 1) * LANES)
        qc = rope(qk_norm(q[:, cols], gq_ref[:, cols]))
        q_heads.append(jnp.where(low_head, qc, 0.0).astype(BF16))
        q_heads.append(jnp.where(low_head, 0.0, qc).astype(BF16))

    qi = lax.broadcasted_iota(jnp.int32, (WINDOW, 2 * WINDOW), 0)
    kj = lax.broadcasted_iota(jnp.int32, (WINDOW, 2 * WINDOW), 1)
    band_mask = (kj > qi) & (kj <= qi + WINDOW)
    first_mask = band_mask & (kj >= jnp.where(i > 0, 0, WINDOW))
    low_head_w = lax.broadcasted_iota(jnp.int32, (WINDOW, LANES), 1) < HEAD_DIM

    def attention_scores(step):
        n, kvh = divmod(step, ATT_KV_HEADS)
        rows = slice(n * WINDOW, (n + 1) * WINDOW)
        mask = first_mask if n == 0 else band_mask
        heads = range(kvh * group, (kvh + 1) * group)
        q_st = jnp.concatenate([q_heads[hd][rows] for hd in heads], axis=0)
        kband = kbands[kvh][n * WINDOW:n * WINDOW + 2 * WINDOW, :]
        s_all = _dot_nt(q_st, kband)
        es, dens = [], []
        for j, hd in enumerate(heads):
            s = jnp.where(mask, s_all[j * WINDOW:(j + 1) * WINDOW], NEG_INF)
            sink = sinks_ref[hd]
            m = jnp.maximum(jnp.max(s, axis=-1, keepdims=True), sink)
            e = jnp.exp(s - m)
            dens.append(jnp.sum(e, axis=-1, keepdims=True) + jnp.exp(sink - m))
            es.append(e.astype(BF16))
        return jnp.concatenate(es, axis=0), dens

    def attention_output(step, probs):
        n, kvh = divmod(step, ATT_KV_HEADS)
        rows = slice(n * WINDOW, (n + 1) * WINDOW)
        e_all, dens = probs
        vband = vbands[kvh][n * WINDOW:n * WINDOW + 2 * WINDOW, :]
        o_all = _dot(e_all, vband)
        for cc in range(group // 2):
            oa = o_all[(2 * cc) * WINDOW:(2 * cc + 1) * WINDOW] / dens[2 * cc]
            ob = o_all[(2 * cc + 1) * WINDOW:(2 * cc + 2) * WINDOW] / dens[2 * cc + 1]
            col = kvh * (group // 2) + cc
            o_ref[rows, col * LANES:(col + 1) * LANES] = jnp.where(low_head_w, oa, ob).astype(BF16)

    def mem_scores(hd, xq):
        cols = slice(hd * MEM_HEAD_DIM, (hd + 1) * MEM_HEAD_DIM)
        s = _dot_nt(xq[:, cols], mk_ref[0, :, cols])
        e = jnp.exp(s - jnp.max(s, axis=-1, keepdims=True))
        return e.astype(BF16), jnp.sum(e, axis=-1, keepdims=True)

    def mem_output(hd, probs):
        cols = slice(hd * MEM_HEAD_DIM, (hd + 1) * MEM_HEAD_DIM)
        e, den = probs
        om_ref[:, cols] = (_dot(e, mv_ref[0, :, cols]) / den).astype(BF16)

    route = _router_select(route_logits)
    p0 = attention_scores(0)
    gates.append(gate(2))
    p1 = attention_scores(1)
    attention_output(0, p0)
    xq = _dot(h, wxq_ref[...])
    p2 = attention_scores(2)
    attention_output(1, p1)
    _router_emit(*route, tri_ref, x1_ref, rt_ref, cnt_ref)
    xq = jnp.concatenate(
        [_rms(xq[:, hd * MEM_HEAD_DIM:(hd + 1) * MEM_HEAD_DIM]) for hd in range(MEM_HEADS)], axis=-1)
    xq = (xq * gxq_ref[...]).astype(BF16)
    p3 = attention_scores(3)
    attention_output(2, p2)
    y_conv = conv_finish()
    m0 = mem_scores(0, xq)
    attention_output(3, p3)
    for r in kbands + vbands:
        r[0:WINDOW, :] = r[TQ:TQ + WINDOW, :]
    m1 = mem_scores(1, xq)
    y_attn = _dot(o_ref[...], woa_ref[...])
    mem_output(0, m0)
    m2 = mem_scores(2, xq)
    mem_output(1, m1)
    m3 = mem_scores(3, xq)
    mem_output(2, m2)
    merged = gates[0] * y_attn + gates[1] * y_conv
    mem_output(3, m3)
    merged = merged + gates[2] * _dot(om_ref[...], wom_ref[...])

    x1prev_ref[...] = x + _dot(merged.astype(BF16), wout_ref[...])


def _router_logits(x1, g2_ref, wrt_ref, brt_ref):
    h2 = _rms(x1) * g2_ref[...]
    h_hi, h_lo = _split_bf16(h2)
    w_hi, w_lo = _split_bf16(wrt_ref[...])
    return _dot_nt(w_hi, h_hi) + _dot_nt(w_hi, h_lo) + _dot_nt(w_lo, h_hi) + brt_ref[...]


def _router_select(lt):
    gl = [lt[g:g + 1] for g in range(N_GROUPS)]
    gmax = functools.reduce(jnp.maximum, gl)
    g_idx = jnp.full((1, TQ), N_GROUPS - 1, jnp.int32)
    for g in reversed(range(N_GROUPS - 1)):
        g_idx = jnp.where(gl[g] == gmax, g, g_idx)
    p_g = 1.0 / functools.reduce(jnp.add, [jnp.exp(r - gmax) for r in gl])
    el = []
    for kk in range(EXPERTS_PER_GROUP):
        row = lt[8 + kk:9 + kk]
        for g in range(1, N_GROUPS):
            off = 8 + g * EXPERTS_PER_GROUP + kk
            row = jnp.where(g_idx == g, lt[off:off + 1], row)
        el.append(row)

    def first_argmax(rows):
        top = functools.reduce(jnp.maximum, rows)
        idx = jnp.full((1, TQ), len(rows) - 1, jnp.int32)
        for kk in reversed(range(len(rows) - 1)):
            idx = jnp.where(rows[kk] == top, kk, idx)
        return top, idx

    v1, i1 = first_argmax(el)
    v2, i2 = first_argmax([jnp.where(i1 == kk, -jnp.inf, el[kk]) for kk in range(EXPERTS_PER_GROUP)])
    t = jnp.exp(v2 - v1)
    p1 = 1.0 / (1.0 + t)
    p2 = t * p1

    lo = jnp.minimum(i1, i2)
    hi = jnp.maximum(i1, i2)
    w_lo = jnp.where(i1 < i2, p_g * p1, p_g * p2)
    w_hi = jnp.where(i1 < i2, p_g * p2, p_g * p1)
    cls = g_idx * N_PAIRS + ((lo * (2 * EXPERTS_PER_GROUP - 1 - lo)) >> 1) + hi - lo - 1
    return cls, w_lo, w_hi


def _router_emit(cls, w_lo, w_hi, tri_ref, x1_ref, rt_ref, cnt_ref):
    cls_f = cls.astype(F32)
    onehot = jnp.broadcast_to(cls, (CLS_ROWS, TQ)) == lax.broadcasted_iota(jnp.int32, (CLS_ROWS, TQ), 0)
    onehot_bf = onehot.astype(F32).astype(BF16)
    prefix = _dot(onehot_bf, tri_ref[...])
    rank = jnp.sum(jnp.where(onehot, prefix, 0.0), axis=0, keepdims=True) - 1.0
    counts = _dot_nt(jnp.ones((8, TQ), BF16), onehot_bf)
    zero = jnp.zeros((1, TQ), F32)
    rt_ref[0] = jnp.concatenate([cls_f, rank] + [zero] * 6, axis=0)
    cnt_ref[0] = jnp.concatenate([counts, jnp.zeros((8, LANES - CLS_ROWS), F32)], axis=1)
    cols = jnp.concatenate([w_lo, w_hi, cls_f, jnp.zeros((LANES - 3, TQ), F32)], axis=0)
    x1_ref[0, :, D_MODEL:] = cols.T


FLAG_FIRST, FLAG_LAST, FLAG_VALID = 1, 2, 4
GATHER_SLOTS = 3


def _moe_kernel(dest_ref, tile_ref, elo_ref, ehi_ref, cls_ref, flag_ref,
                g2_ref, wg_lo, wu_lo, wd_lo, wg_hi, wu_hi, wd_hi, x1_hbm,
                out_hbm,
                perm_ref, xbuf, obuf, h2buf, wbuf_up, wbuf_down, gsem, ssem):
    k = pl.program_id(0)
    nk = pl.num_programs(0)
    t = tile_ref[k]
    xslot = t % GATHER_SLOTS
    oslot = t % 2
    flags = flag_ref[k]
    n_tok = x1_hbm.shape[0]
    n_tiles = n_tok // TM

    def gather_copy(r, tile, s):
        tok = perm_ref[tile * TM + r]
        return pltpu.make_async_copy(x1_hbm.at[pl.ds(tok, 1), :], xbuf.at[s, pl.ds(r, 1), :], gsem.at[s])

    def scatter_copy(r, tile, s):
        tok = perm_ref[tile * TM + r]
        return pltpu.make_async_copy(obuf.at[s, pl.ds(r, 1), :], out_hbm.at[pl.ds(tok, 1), :], ssem.at[s])

    k_prev = jnp.maximum(k - 1, 0)
    for side, (e_ref, wg, wu, wd) in enumerate(((elo_ref, wg_lo, wu_lo, wd_lo), (ehi_ref, wg_hi, wu_hi, wd_hi))):
        @pl.when((k == 0) | (e_ref[k] != e_ref[k_prev]))
        def _(side=side, wg=wg, wu=wu, wd=wd):
            wbuf_up[2 * side] = wg[0].astype(BF16)
            wbuf_up[2 * side + 1] = wu[0].astype(BF16)
            wbuf_down[side] = wd[0].astype(BF16)

    def for_rows(fn):
        def body(r, carry):
            fn(r)
            return carry
        lax.fori_loop(0, TM, body, 0, unroll=8)

    @pl.when(k == 0)
    def _():
        def invert(tok, carry):
            perm_ref[dest_ref[tok]] = tok
            return carry
        lax.fori_loop(0, n_tok, invert, 0, unroll=8)
        for tile in range(GATHER_SLOTS - 1):
            for_rows(lambda r, tile=tile: gather_copy(r, tile, tile).start())

    is_first = (flags & FLAG_FIRST) != 0
    is_valid = (flags & FLAG_VALID) != 0
    ahead = GATHER_SLOTS - 1
    prefetch_tile = jnp.minimum(t + ahead, n_tiles - 1)
    prefetch_slot = (t + ahead) % GATHER_SLOTS

    @pl.when(is_first)
    def _():
        for_rows(lambda r: gather_copy(r, t, xslot).wait())

        @pl.when(t >= 2)
        def _():
            for_rows(lambda r: scatter_copy(r, t - 2, oslot).wait())

        xr = xbuf[xslot, :, 0:D_MODEL]
        h2buf[...] = (_rms(xr) * g2_ref[...]).astype(BF16)
        obuf[oslot] = xr

    def experts():
        w = xbuf[xslot, :, D_MODEL:XW]
        mine = w[:, 2:3] == cls_ref[k].astype(F32)
        h2 = h2buf[...]
        y = None
        for side in range(2):
            c = jnp.where(mine, w[:, side:side + 1], 0.0)
            hid = jax.nn.silu(_dot(h2, wbuf_up[2 * side])) * _dot(h2, wbuf_up[2 * side + 1]) * c
            part = _dot(hid.astype(BF16), wbuf_down[side])
            y = part if y is None else y + part
        obuf[oslot] += y

    @pl.when(is_first & (t > 0))
    def _():
        for r in range(TM):
            gather_copy(r, prefetch_tile, prefetch_slot).start()
            scatter_copy(r, t - 1, 1 - oslot).start()
        experts()

    @pl.when(is_first & (t == 0))
    def _():
        for r in range(TM):
            gather_copy(r, prefetch_tile, prefetch_slot).start()
        experts()

    @pl.when(is_valid & jnp.logical_not(is_first))
    def _():
        experts()

    @pl.when(k == nk - 1)
    def _():
        last = n_tiles - 1
        for_rows(lambda r: scatter_copy(r, last, last % 2).start())
        for tile in (last - 1, last):
            for_rows(lambda r, tile=tile: scatter_copy(r, tile, tile % 2).wait())
        for extra in range(1, GATHER_SLOTS):
            for_rows(lambda r, extra=extra: gather_copy(r, last, (last + extra) % GATHER_SLOTS).wait())


_CLASS_PAIRS = [(a, b) for a in range(EXPERTS_PER_GROUP) for b in range(a + 1, EXPERTS_PER_GROUP)]


def _routing_plan(cls, rank, tile_counts, n_tiles):
    counts = jnp.sum(tile_counts, axis=0)
    off = jnp.concatenate([jnp.zeros((1,), jnp.int32), jnp.cumsum(counts).astype(jnp.int32)])
    tile_base = jnp.cumsum(tile_counts, axis=0) - tile_counts + off[None, :-1]
    onehot = cls[:, :, None] == jnp.arange(N_CLASSES, dtype=jnp.int32)[None, None, :]
    dest = (jnp.sum(jnp.where(onehot, tile_base[:, None, :], 0), axis=-1) + rank).reshape(-1)
    first_tile = off[:-1] // TM
    last_tile = (off[1:] - 1) // TM
    n_items = jnp.where(counts > 0, last_tile - first_tile + 1, 0)
    istart = jnp.concatenate([jnp.zeros((1,), jnp.int32), jnp.cumsum(n_items).astype(jnp.int32)])
    total = istart[-1]
    ni = n_tiles + N_CLASSES - 1
    k = jnp.arange(ni, dtype=jnp.int32)
    valid = k < total
    kc = jnp.minimum(k, total - 1)
    c_of_k = jnp.minimum(jnp.sum(istart[None, 1:] <= kc[:, None], axis=1).astype(jnp.int32), N_CLASSES - 1)
    tile_k = first_tile[c_of_k] + kc - istart[c_of_k]
    prev_tile = jnp.concatenate([jnp.full((1,), -1, jnp.int32), tile_k[:-1]])
    next_tile = jnp.concatenate([tile_k[1:], jnp.full((1,), -1, jnp.int32)])
    first = valid & (tile_k != prev_tile)
    last = valid & ((tile_k != next_tile) | (k == total - 1))
    flags = (first * FLAG_FIRST + last * FLAG_LAST + valid * FLAG_VALID).astype(jnp.int32)
    pair_lo = jnp.array([p[0] for p in _CLASS_PAIRS], jnp.int32)
    pair_hi = jnp.array([p[1] for p in _CLASS_PAIRS], jnp.int32)
    base = (c_of_k // N_PAIRS) * EXPERTS_PER_GROUP
    e_lo = base + pair_lo[c_of_k % N_PAIRS]
    e_hi = base + pair_hi[c_of_k % N_PAIRS]
    cls_k = jnp.where(valid, c_of_k, -1)
    return dest.astype(jnp.int32), tile_k, e_lo, e_hi, cls_k, flags


def _const_spec(shape):
    nd = len(shape)
    return pl.BlockSpec(shape, lambda *_: (0,) * nd, pipeline_mode=pl.Buffered(1))


def _block_diag_ones(n, blk):
    r = jnp.arange(n) // blk
    return (r[:, None] == r[None, :]).astype(BF16)


def kernel(x, mem, positions, g_norm1, w_in, g_q, g_k, sinks, w_o_attn, w_conv_dw, b_conv_dw, g_conv_ln, b_conv_ln, w_conv_out, g_mem, w_kv_mem, g_xq, g_xk, w_o_mem, w_out, g_norm2, w_group, b_group, w_router, b_router, w_gate, w_up, w_down):
    B, S, D = x.shape
    M = mem.shape[1]
    assert D == D_MODEL and S % TQ == 0 and (B * S) % TM == 0 and w_in.shape[0] == 1
    assert (TQ // WINDOW) * ATT_KV_HEADS == MEM_HEADS and CONV_CH // LANES == N_BRANCHES + 1
    NT = S // TQ
    T = B * S
    l = 0

    bd128 = _block_diag_ones(XQ_WIDTH, MEM_HEAD_DIM)
    row = lambda v: v.reshape(1, -1).astype(F32)

    mk, mv = pl.pallas_call(
        _memkv_kernel,
        grid=(B,),
        in_specs=[pl.BlockSpec((1, M, D), lambda b: (b, 0, 0)),
                  _const_spec((1, D)), _const_spec((D, 2 * XQ_WIDTH)),
                  _const_spec((1, XQ_WIDTH)), _const_spec((XQ_WIDTH, XQ_WIDTH))],
        out_specs=[pl.BlockSpec((1, M, XQ_WIDTH), lambda b: (b, 0, 0))] * 2,
        out_shape=[jax.ShapeDtypeStruct((B, M, XQ_WIDTH), BF16)] * 2,
        compiler_params=pltpu.CompilerParams(dimension_semantics=("arbitrary",)),
        name="memkv",
    )(mem, row(g_mem[l]), w_kv_mem[l].astype(BF16), row(jnp.tile(g_xk[l], MEM_HEADS)), bd128)

    inv_freq = 1.0 / (ROPE_THETA ** (jnp.arange(0, HEAD_DIM, 2, dtype=F32) / HEAD_DIM))
    per_row = LANES // (HEAD_DIM // 2)
    ang = positions.astype(F32).reshape(B, S // per_row, per_row, 1) * inv_freq
    ang = ang.reshape(B, S // per_row, LANES)
    cos_tab, sin_tab = jnp.cos(ang), jnp.sin(ang)

    w = w_in[l]
    c0 = Q_WIDTH + 2 * KV_WIDTH
    c1 = c0 + GLU_WIDTH
    c2 = c1 + XQ_WIDTH
    w_qkv = w[:, :c0].astype(BF16)
    w_glu = w[:, c0:c1].astype(BF16)
    w_xq = w[:, c1:c2].astype(BF16)
    w_gt = w[:, c2:].astype(BF16)
    wdw = jnp.zeros((CONV_HALO, CONV_CH), F32).at[:CONV_WIDTH].set(w_conv_dw[l])
    w_rt = jnp.zeros((RT_ROWS, D), F32).at[0:N_GROUPS].set(w_group[l].T).at[8:8 + N_EXPERTS].set(w_router[l].T)
    b_rt = jnp.zeros((RT_ROWS, 1), F32).at[0:N_GROUPS, 0].set(b_group[l]).at[8:8 + N_EXPERTS, 0].set(b_router[l])

    n_mix = B * NT
    tile_in = lambda s: jnp.minimum(s, n_mix - 1)
    tile_out = lambda s: jnp.maximum(s - 1, 0)
    tile3 = lambda last: pl.BlockSpec((1, TQ, last), lambda s: (tile_in(s) // NT, tile_in(s) % NT, 0))
    per_batch = pl.BlockSpec((1, M, XQ_WIDTH), lambda s: (tile_in(s) // NT, 0, 0))
    rope_spec = pl.BlockSpec((1, TQ // per_row, LANES), lambda s: (tile_in(s) // NT, tile_in(s) % NT, 0))
    in_specs = [
        tile3(D), rope_spec, rope_spec, per_batch, per_batch,
        _const_spec((1, D)),
        _const_spec((D, c0)), _const_spec((D, GLU_WIDTH)), _const_spec((D, XQ_WIDTH)), _const_spec((D, GATE_WIDTH)),
        _const_spec((1, Q_WIDTH)), _const_spec((1, LANES)),
        pl.BlockSpec(memory_space=pltpu.SMEM),
        _const_spec((Q_WIDTH, D)),
        _const_spec((CONV_HALO, CONV_CH)), _const_spec((1, CONV_CH)), _const_spec((1, CONV_CH)), _const_spec((1, CONV_CH)),
        _const_spec((CONV_CH, D)),
        _const_spec((1, XQ_WIDTH)),
        _const_spec((XQ_WIDTH, D)),
        _const_spec((D, D)),
        _const_spec((1, D)),
        _const_spec((RT_ROWS, D)), _const_spec((RT_ROWS, 1)), _const_spec((TQ, TQ)),
    ]
    tri = (jnp.arange(TQ)[:, None] <= jnp.arange(TQ)[None, :]).astype(BF16)
    per_tile = lambda rows, last: pl.BlockSpec((1, rows, last), lambda s: (tile_out(s), 0, 0))
    x1_spec = pl.BlockSpec((1, TQ, XW), lambda s: (tile_out(s) // NT, tile_out(s) % NT, 0))
    x1, rt, cnt = pl.pallas_call(
        functools.partial(_mixer_kernel, tiles_per_seq=NT),
        grid=(n_mix + 1,),
        in_specs=in_specs,
        out_specs=[x1_spec, per_tile(8, TQ), per_tile(8, LANES)],
        out_shape=[jax.ShapeDtypeStruct((B, S, XW), F32), jax.ShapeDtypeStruct((B * NT, 8, TQ), F32),
                   jax.ShapeDtypeStruct((B * NT, 8, LANES), F32)],
        scratch_shapes=[pltpu.VMEM((WINDOW + TQ, LANES), BF16)] * 4
        + [pltpu.VMEM((CONV_CH // LANES, CONV_HALO + TQ, LANES), F32),
           pltpu.VMEM((CONV_CH // LANES, TQ, LANES), F32), pltpu.VMEM((TQ, Q_WIDTH), BF16),
           pltpu.VMEM((TQ, XQ_WIDTH), BF16), pltpu.VMEM((TQ, D), F32),
           pltpu.VMEM((TQ, LANES), F32), pltpu.VMEM((TQ, LANES), F32)],
        compiler_params=pltpu.CompilerParams(dimension_semantics=("arbitrary",),
                                             vmem_limit_bytes=VMEM_LIMIT),
        name="mixer",
    )(x, cos_tab, sin_tab, mk, mv, row(g_norm1[l]), w_qkv, w_glu, w_xq, w_gt,
      row(jnp.tile(g_q[l], ATT_HEADS) * (HEAD_DIM ** -0.5)), row(jnp.tile(g_k[l], ATT_KV_HEADS)),
      sinks[l].astype(F32), w_o_attn[l].astype(BF16),
      wdw, row(b_conv_dw[l]), row(g_conv_ln[l]), row(b_conv_ln[l]), w_conv_out[l].astype(BF16),
      row(jnp.tile(g_xq[l], MEM_HEADS) * (MEM_HEAD_DIM ** -0.5)), w_o_mem[l].astype(BF16),
      w_out[l].astype(BF16), row(g_norm2[l]), w_rt, b_rt, tri)

    n_tiles = T // TM
    plan = _routing_plan(rt[:, 0, :].astype(jnp.int32), rt[:, 1, :].astype(jnp.int32),
                         cnt[:, 0, :N_CLASSES].astype(jnp.int32), n_tiles)
    dest, tile_k, e_lo, e_hi, cls_k, flags = plan
    ni = tile_k.shape[0]
    wg, wu, wd = w_gate[l], w_up[l], w_down[l]
    lo_map = lambda k, dest, tile, elo, ehi, cls, flg: (elo[k], 0, 0)
    hi_map = lambda k, dest, tile, elo, ehi, cls, flg: (ehi[k], 0, 0)
    up_block = (1, D, EXPERT_FF)
    down_block = (1, EXPERT_FF, D)
    out = pl.pallas_call(
        _moe_kernel,
        grid_spec=pltpu.PrefetchScalarGridSpec(
            num_scalar_prefetch=6,
            grid=(ni,),
            in_specs=[pl.BlockSpec((1, D), lambda k, *_: (0, 0)),
                      pl.BlockSpec(up_block, lo_map), pl.BlockSpec(up_block, lo_map),
                      pl.BlockSpec(down_block, lo_map),
                      pl.BlockSpec(up_block, hi_map), pl.BlockSpec(up_block, hi_map),
                      pl.BlockSpec(down_block, hi_map),
                      pl.BlockSpec(memory_space=pl.ANY)],
            out_specs=pl.BlockSpec(memory_space=pl.ANY),
            scratch_shapes=[pltpu.SMEM((T,), jnp.int32),
                            pltpu.VMEM((GATHER_SLOTS, TM, XW), F32), pltpu.VMEM((2, TM, D), F32),
                            pltpu.VMEM((TM, D), BF16),
                            pltpu.VMEM((4, D, EXPERT_FF), BF16), pltpu.VMEM((2, EXPERT_FF, D), BF16),
                            pltpu.SemaphoreType.DMA((GATHER_SLOTS,)), pltpu.SemaphoreType.DMA((2,))]),
        out_shape=jax.ShapeDtypeStruct((T, D), F32),
        compiler_params=pltpu.CompilerParams(dimension_semantics=("arbitrary",),
                                             vmem_limit_bytes=VMEM_LIMIT),
        name="moe",
    )(dest, tile_k, e_lo, e_hi, cls_k, flags, row(g_norm2[l]), wg, wu, wd, wg, wu, wd,
      x1.reshape(T, XW))
    return out.reshape(B, S, D)
```

```python
import functools

import jax
import jax.numpy as jnp
from jax import lax
from jax.experimental import pallas as pl
from jax.experimental.pallas import tpu as pltpu

F32 = jnp.float32
BF16 = jnp.bfloat16

D_MODEL = 1024
ATT_HEADS = 8
ATT_KV_HEADS = 2
HEAD_DIM = 64
WINDOW = 128
ROPE_THETA = 10000.0
CONV_CH = 512
CONV_WIDTH = 31
MEM_HEADS = 4
MEM_HEAD_DIM = 128
N_BRANCHES = 3
N_GROUPS = 4
EXPERTS_PER_GROUP = 4
N_EXPERTS = N_GROUPS * EXPERTS_PER_GROUP
EXPERT_FF = 512
EPS = 1e-6
LN_EPS = 1e-5
NEG_INF = -1e30

Q_WIDTH = ATT_HEADS * HEAD_DIM
KV_WIDTH = ATT_KV_HEADS * HEAD_DIM
GLU_WIDTH = 2 * CONV_CH
XQ_WIDTH = MEM_HEADS * MEM_HEAD_DIM
GATE_WIDTH = N_BRANCHES * D_MODEL

LANES = 128
TQ = 256
CONV_HALO = 32
CONV_ROWS = 128
TM = 256
RT_ROWS = 32
N_PAIRS = EXPERTS_PER_GROUP * (EXPERTS_PER_GROUP - 1) // 2
N_CLASSES = N_GROUPS * N_PAIRS
CLS_ROWS = 32
XW = D_MODEL + LANES
VMEM_LIMIT = 56 * 1024 * 1024


def _dot(a, b):
    return jnp.dot(a, b, preferred_element_type=F32)


def _dot_nt(a, b):
    return lax.dot_general(a, b, (((1,), (1,)), ((), ())), preferred_element_type=F32)


def _split_bf16(t):
    hi = t.astype(BF16)
    lo = (t - hi.astype(F32)).astype(BF16)
    return hi, lo


def _block_sum(t, bd):
    return _dot(t.astype(BF16), bd)


def _rms(t):
    return t * lax.rsqrt(jnp.mean(t * t, axis=-1, keepdims=True) + EPS)


def _memkv_kernel(mem_ref, gmem_ref, wkv_ref, gxk_ref, bd_ref, mk_ref, mv_ref):
    mn = (_rms(mem_ref[0]) * gmem_ref[...]).astype(BF16)
    kv = _dot(mn, wkv_ref[...])
    mk = kv[:, :XQ_WIDTH]
    ss = _block_sum(mk * mk, bd_ref[...])
    mk = mk * lax.rsqrt(ss * (1.0 / MEM_HEAD_DIM) + EPS) * gxk_ref[...]
    mk_ref[0] = mk.astype(BF16)
    mv_ref[0] = kv[:, XQ_WIDTH:].astype(BF16)


def _mixer_kernel(x_ref, cos_ref, sin_ref, mk_ref, mv_ref, g1_ref, wqkv_ref, wglu_ref, wxq_ref,
                  wgt_ref, gq_ref, gk_ref, sinks_ref, woa_ref, wdw_ref,
                  bdw_ref, gln_ref, bln_ref, wco_ref, gxq_ref, wom_ref, wout_ref, g2_ref,
                  wrt_ref, brt_ref, tri_ref, ewg_ref, ewu_ref, ewd_ref,
                  x1_ref, rt_ref, cnt_ref, ewg_out, ewu_out, ewd_out,
                  k0_ref, k1_ref, v0_ref, v1_ref, ubuf_ref, ybuf_ref, o_ref, om_ref, x1prev_ref,
                  cosd_ref, sind_ref, *, tiles_per_seq):
    s = pl.program_id(0)
    n_tiles = pl.num_programs(0) - 1
    i = jnp.minimum(s, n_tiles - 1) % tiles_per_seq
    kbands = (k0_ref, k1_ref)
    vbands = (v0_ref, v1_ref)

    @pl.when(s == 0)
    def _():
        x1prev_ref[...] = jnp.zeros((TQ, D_MODEL), F32)

    @pl.when(i == 0)
    def _():
        for r in kbands + vbands:
            r[0:WINDOW, :] = jnp.zeros((WINDOW, LANES), BF16)
        ubuf_ref[:, 0:CONV_HALO, :] = jnp.zeros((CONV_CH // LANES, CONV_HALO, LANES), F32)

    x1_prev = x1prev_ref[...]
    x1_ref[0, :, 0:D_MODEL] = x1_prev

    ewg_out[...] = ewg_ref[...].astype(BF16)
    ewu_out[...] = ewu_ref[...].astype(BF16)
    ewd_out[...] = ewd_ref[...].astype(BF16)

    x = x_ref[0]
    h = (_rms(x) * g1_ref[...]).astype(BF16)

    n_slabs = CONV_CH // LANES
    half = TQ // 2
    group = ATT_HEADS // ATT_KV_HEADS

    glu = _dot(h, wglu_ref[...])
    route_logits = _router_logits(x1_prev, g2_ref, wrt_ref, brt_ref)
    u = glu[:, :CONV_CH] * jax.nn.sigmoid(glu[:, CONV_CH:])
    for c in range(n_slabs):
        ubuf_ref[c, CONV_HALO:CONV_HALO + TQ, :] = u[:, c * LANES:(c + 1) * LANES]
    base = CONV_HALO - (CONV_WIDTH - 1)
    acc = [[None] * n_slabs for _ in range(2)]

    def conv_slab(c):
        cols = slice(c * LANES, (c + 1) * LANES)
        for par in range(2):
            parts = []
            for q0 in range(0, half, CONV_ROWS):
                a = jnp.broadcast_to(bdw_ref[:, cols], (CONV_ROWS, LANES))
                for j in range(CONV_WIDTH):
                    a = a + wdw_ref[j:j + 1, cols] * ubuf_ref[
                        c, pl.ds(base + j + par + 2 * q0, CONV_ROWS, stride=2), :]
                parts.append(a)
            acc[par][c] = jnp.concatenate(parts, axis=0)
        ubuf_ref[c, 0:CONV_HALO, :] = ubuf_ref[c, TQ:TQ + CONV_HALO, :]

    def gate(b):
        return jax.nn.sigmoid(_dot(h, wgt_ref[:, b * D_MODEL:(b + 1) * D_MODEL]))

    def conv_finish():
        for par in range(2):
            row_sum = functools.reduce(jnp.add, [jnp.sum(a, axis=-1, keepdims=True) for a in acc[par]])
            mu = row_sum * (1.0 / CONV_CH)
            cen = [a - mu for a in acc[par]]
            sq_sum = functools.reduce(jnp.add, [jnp.sum(t * t, axis=-1, keepdims=True) for t in cen])
            rstd = lax.rsqrt(sq_sum * (1.0 / CONV_CH) + LN_EPS)
            for c in range(n_slabs):
                cols = slice(c * LANES, (c + 1) * LANES)
                yln = cen[c] * rstd * gln_ref[:, cols] + bln_ref[:, cols]
                ybuf_ref[c, pl.ds(par, half, stride=2), :] = yln * jax.nn.sigmoid(yln)
        uo = jnp.concatenate([ybuf_ref[c] for c in range(n_slabs)], axis=-1).astype(BF16)
        return _dot(uo, wco_ref[...])

    for c in range(n_slabs):
        conv_slab(c)
    qkv = _dot(h, wqkv_ref[...])
    gates = [gate(0)]
    q = qkv[:, :Q_WIDTH]
    k = qkv[:, Q_WIDTH:Q_WIDTH + KV_WIDTH]
    v = qkv[:, Q_WIDTH + KV_WIDTH:]
    gates.append(gate(1))

    lane = lax.broadcasted_iota(jnp.int32, (TQ, LANES), 1)
    first_half = (lane % HEAD_DIM) < (HEAD_DIM // 2)
    low_head = lane < HEAD_DIM
    n_freq = HEAD_DIM // 2
    per_row = LANES // n_freq
    quarter = lax.broadcasted_iota(jnp.int32, (TQ // per_row, LANES), 1) // n_freq
    sign = jnp.where(quarter % 2 == 0, -1.0, 1.0)
    for table_ref, out_ref, scale in ((cos_ref, cosd_ref, None), (sin_ref, sind_ref, sign)):
        dense = table_ref[0]
        rolled = [dense] + [pltpu.roll(dense, n_freq * kk, 1) for kk in range(1, per_row)]
        for j in range(per_row):
            spread = rolled[(-j) % per_row]
            for qq in range(1, per_row):
                spread = jnp.where(quarter == qq, rolled[(qq - j) % per_row], spread)
            out_ref[pl.ds(j, TQ // per_row, stride=per_row), :] = spread if scale is None else spread * scale
    cosd = cosd_ref[...]
    sind = sind_ref[...]

    def rope(t):
        rot = jnp.where(first_half, pltpu.roll(t, LANES - HEAD_DIM // 2, 1),
                        pltpu.roll(t, HEAD_DIM // 2, 1))
        return t * cosd + rot * sind

    def dup_halves(t):
        swapped = pltpu.roll(t, HEAD_DIM, 1)
        return jnp.where(low_head, t, swapped), jnp.where(low_head, swapped, t)

    def qk_norm(t, gain):
        sq = t * t
        lo = jnp.sum(jnp.where(low_head, sq, 0.0), axis=-1, keepdims=True)
        hi = jnp.sum(jnp.where(low_head, 0.0, sq), axis=-1, keepdims=True)
        inv = jnp.where(low_head, lax.rsqrt(lo * (1.0 / HEAD_DIM) + EPS), lax.rsqrt(hi * (1.0 / HEAD_DIM) + EPS))
        return t * inv * gain

    kd = dup_halves(rope(qk_norm(k, gk_ref[...])))
    vd = dup_halves(v)
    for kvh in range(ATT_KV_HEADS):
        kbands[kvh][WINDOW:WINDOW + TQ, :] = kd[kvh].astype(BF16)
        vbands[kvh][WINDOW:WINDOW + TQ, :] = vd[kvh].astype(BF16)

    q_heads = []
    for c in range(Q_WIDTH // LANES):
        cols = slice(c * LANES, (c + 1) * LANES)
        qc = rope(qk_norm(q[:, cols], gq_ref[:, cols]))
        q_heads.append(jnp.where(low_head, qc, 0.0).astype(BF16))
        q_heads.append(jnp.where(low_head, 0.0, qc).astype(BF16))

    qi = lax.broadcasted_iota(jnp.int32, (WINDOW, 2 * WINDOW), 0)
    kj = lax.broadcasted_iota(jnp.int32, (WINDOW, 2 * WINDOW), 1)
    band_mask = (kj > qi) & (kj <= qi + WINDOW)
    first_mask = band_mask & (kj >= jnp.where(i > 0, 0, WINDOW))
    low_head_w = lax.broadcasted_iota(jnp.int32, (WINDOW, LANES), 1) < HEAD_DIM

    def attention_scores(step):
        n, kvh = divmod(step, ATT_KV_HEADS)
        rows = slice(n * WINDOW, (n + 1) * WINDOW)
        mask = first_mask if n == 0 else band_mask
        heads = range(kvh * group, (kvh + 1) * group)
        q_st = jnp.concatenate([q_heads[hd][rows] for hd in heads], axis=0)
        kband = kbands[kvh][n * WINDOW:n * WINDOW + 2 * WINDOW, :]
        s_all = _dot_nt(q_st, kband)
        es, dens = [], []
        for j, hd in enumerate(heads):
            s = jnp.where(mask, s_all[j * WINDOW:(j + 1) * WINDOW], NEG_INF)
            sink = sinks_ref[hd]
            m = jnp.maximum(jnp.max(s, axis=-1, keepdims=True), sink)
            e = jnp.exp(s - m)
            dens.append(jnp.sum(e, axis=-1, keepdims=True) + jnp.exp(sink - m))
            es.append(e.astype(BF16))
        return jnp.concatenate(es, axis=0), dens

    def attention_output(step, probs):
        n, kvh = divmod(step, ATT_KV_HEADS)
        rows = slice(n * WINDOW, (n + 1) * WINDOW)
        e_all, dens = probs
        vband = vbands[kvh][n * WINDOW:n * WINDOW + 2 * WINDOW, :]
        o_all = _dot(e_all, vband)
        for cc in range(group // 2):
            oa = o_all[(2 * cc) * WINDOW:(2 * cc + 1) * WINDOW] / dens[2 * cc]
            ob = o_all[(2 * cc + 1) * WINDOW:(2 * cc + 2) * WINDOW] / dens[2 * cc + 1]
            col = kvh * (group // 2) + cc
            o_ref[rows, col * LANES:(col + 1) * LANES] = jnp.where(low_head_w, oa, ob).astype(BF16)

    def mem_scores(hd, xq):
        cols = slice(hd * MEM_HEAD_DIM, (hd + 1) * MEM_HEAD_DIM)
        s = _dot_nt(xq[:, cols], mk_ref[0, :, cols])
        e = jnp.exp(s - jnp.max(s, axis=-1, keepdims=True))
        return e.astype(BF16), jnp.sum(e, axis=-1, keepdims=True)

    def mem_output(hd, probs):
        cols = slice(hd * MEM_HEAD_DIM, (hd + 1) * MEM_HEAD_DIM)
        e, den = probs
        om_ref[:, cols] = (_dot(e, mv_ref[0, :, cols]) / den).astype(BF16)

    route = _router_select(route_logits)
    p0 = attention_scores(0)
    gates.append(gate(2))
    p1 = attention_scores(1)
    attention_output(0, p0)
    xq = _dot(h, wxq_ref[...])
    p2 = attention_scores(2)
    attention_output(1, p1)
    _router_emit(*route, tri_ref, x1_ref, rt_ref, cnt_ref)
    xq = jnp.concatenate(
        [_rms(xq[:, hd * MEM_HEAD_DIM:(hd + 1) * MEM_HEAD_DIM]) for hd in range(MEM_HEADS)], axis=-1)
    xq = (xq * gxq_ref[...]).astype(BF16)
    p3 = attention_scores(3)
    attention_output(2, p2)
    y_conv = conv_finish()
    m0 = mem_scores(0, xq)
    attention_output(3, p3)
    for r in kbands + vbands:
        r[0:WINDOW, :] = r[TQ:TQ + WINDOW, :]
    m1 = mem_scores(1, xq)
    y_attn = _dot(o_ref[...], woa_ref[...])
    mem_output(0, m0)
    m2 = mem_scores(2, xq)
    mem_output(1, m1)
    m3 = mem_scores(3, xq)
    mem_output(2, m2)
    merged = gates[0] * y_attn + gates[1] * y_conv
    mem_output(3, m3)
    merged = merged + gates[2] * _dot(om_ref[...], wom_ref[...])

    x1prev_ref[...] = x + _dot(merged.astype(BF16), wout_ref[...])


def _router_logits(x1, g2_ref, wrt_ref, brt_ref):
    h2 = _rms(x1) * g2_ref[...]
    h_hi, h_lo = _split_bf16(h2)
    w_hi, w_lo = _split_bf16(wrt_ref[...])
    return _dot_nt(w_hi, h_hi) + _dot_nt(w_hi, h_lo) + _dot_nt(w_lo, h_hi) + brt_ref[...]


def _router_select(lt):
    gl = [lt[g:g + 1] for g in range(N_GROUPS)]
    gmax = functools.reduce(jnp.maximum, gl)
    g_idx = jnp.full((1, TQ), N_GROUPS - 1, jnp.int32)
    for g in reversed(range(N_GROUPS - 1)):
        g_idx = jnp.where(gl[g] == gmax, g, g_idx)
    p_g = 1.0 / functools.reduce(jnp.add, [jnp.exp(r - gmax) for r in gl])
    el = []
    for kk in range(EXPERTS_PER_GROUP):
        row = lt[8 + kk:9 + kk]
        for g in range(1, N_GROUPS):
            off = 8 + g * EXPERTS_PER_GROUP + kk
            row = jnp.where(g_idx == g, lt[off:off + 1], row)
        el.append(row)

    def first_argmax(rows):
        top = functools.reduce(jnp.maximum, rows)
        idx = jnp.full((1, TQ), len(rows) - 1, jnp.int32)
        for kk in reversed(range(len(rows) - 1)):
            idx = jnp.where(rows[kk] == top, kk, idx)
        return top, idx

    v1, i1 = first_argmax(el)
    v2, i2 = first_argmax([jnp.where(i1 == kk, -jnp.inf, el[kk]) for kk in range(EXPERTS_PER_GROUP)])
    t = jnp.exp(v2 - v1)
    p1 = 1.0 / (1.0 + t)
    p2 = t * p1

    lo = jnp.minimum(i1, i2)
    hi = jnp.maximum(i1, i2)
    w_lo = jnp.where(i1 < i2, p_g * p1, p_g * p2)
    w_hi = jnp.where(i1 < i2, p_g * p2, p_g * p1)
    cls = g_idx * N_PAIRS + ((lo * (2 * EXPERTS_PER_GROUP - 1 - lo)) >> 1) + hi - lo - 1
    return cls, w_lo, w_hi


def _router_emit(cls, w_lo, w_hi, tri_ref, x1_ref, rt_ref, cnt_ref):
    cls_f = cls.astype(F32)
    onehot = jnp.broadcast_to(cls, (CLS_ROWS, TQ)) == lax.broadcasted_iota(jnp.int32, (CLS_ROWS, TQ), 0)
    onehot_bf = onehot.astype(F32).astype(BF16)
    prefix = _dot(onehot_bf, tri_ref[...])
    rank = jnp.sum(jnp.where(onehot, prefix, 0.0), axis=0, keepdims=True) - 1.0
    counts = _dot_nt(jnp.ones((8, TQ), BF16), onehot_bf)
    zero = jnp.zeros((1, TQ), F32)
    rt_ref[0] = jnp.concatenate([cls_f, rank] + [zero] * 6, axis=0)
    cnt_ref[0] = jnp.concatenate([counts, jnp.zeros((8, LANES - CLS_ROWS), F32)], axis=1)
    cols = jnp.concatenate([w_lo, w_hi, cls_f, jnp.zeros((LANES - 3, TQ), F32)], axis=0)
    x1_ref[0, :, D_MODEL:] = cols.T


FLAG_FIRST, FLAG_LAST, FLAG_VALID = 1, 2, 4
GATHER_SLOTS = 3


def _moe_kernel(dest_ref, tile_ref, elo_ref, ehi_ref, cls_ref, flag_ref,
                g2_ref, wg_lo, wu_lo, wd_lo, wg_hi, wu_hi, wd_hi, x1_hbm,
                out_hbm,
                perm_ref, xbuf, obuf, h2buf, gsem, ssem):
    k = pl.program_id(0)
    nk = pl.num_programs(0)
    t = tile_ref[k]
    xslot = t % GATHER_SLOTS
    oslot = t % 2
    flags = flag_ref[k]
    n_tok = x1_hbm.shape[0]
    n_tiles = n_tok // TM

    def gather_copy(r, tile, s):
        tok = perm_ref[tile * TM + r]
        return pltpu.make_async_copy(x1_hbm.at[pl.ds(tok, 1), :], xbuf.at[s, pl.ds(r, 1), :], gsem.at[s])

    def scatter_copy(r, tile, s):
        tok = perm_ref[tile * TM + r]
        return pltpu.make_async_copy(obuf.at[s, pl.ds(r, 1), :], out_hbm.at[pl.ds(tok, 1), :], ssem.at[s])

    expert_weights = ((wg_lo, wu_lo, wd_lo), (wg_hi, wu_hi, wd_hi))

    def for_rows(fn):
        def body(r, carry):
            fn(r)
            return carry
        lax.fori_loop(0, TM, body, 0, unroll=8)

    @pl.when(k == 0)
    def _():
        def invert(tok, carry):
            perm_ref[dest_ref[tok]] = tok
            return carry
        lax.fori_loop(0, n_tok, invert, 0, unroll=8)
        for tile in range(GATHER_SLOTS - 1):
            for_rows(lambda r, tile=tile: gather_copy(r, tile, tile).start())

    is_first = (flags & FLAG_FIRST) != 0
    is_valid = (flags & FLAG_VALID) != 0
    ahead = GATHER_SLOTS - 1
    prefetch_tile = jnp.minimum(t + ahead, n_tiles - 1)
    prefetch_slot = (t + ahead) % GATHER_SLOTS

    @pl.when(is_first)
    def _():
        for_rows(lambda r: gather_copy(r, t, xslot).wait())

        @pl.when(t >= 2)
        def _():
            for_rows(lambda r: scatter_copy(r, t - 2, oslot).wait())

        xr = xbuf[xslot, :, 0:D_MODEL]
        h2buf[...] = (_rms(xr) * g2_ref[...]).astype(BF16)
        obuf[oslot] = xr

    def experts():
        w = xbuf[xslot, :, D_MODEL:XW]
        mine = w[:, 2:3] == cls_ref[k].astype(F32)
        h2 = h2buf[...]
        y = None
        for side in range(2):
            c = jnp.where(mine, w[:, side:side + 1], 0.0)
            wg, wu, wd = expert_weights[side]
            hid = jax.nn.silu(_dot(h2, wg[0])) * _dot(h2, wu[0]) * c
            part = _dot(hid.astype(BF16), wd[0])
            y = part if y is None else y + part
        obuf[oslot] += y

    @pl.when(is_first & (t > 0))
    def _():
        for r in range(TM):
            gather_copy(r, prefetch_tile, prefetch_slot).start()
            scatter_copy(r, t - 1, 1 - oslot).start()
        experts()

    @pl.when(is_first & (t == 0))
    def _():
        for r in range(TM):
            gather_copy(r, prefetch_tile, prefetch_slot).start()
        experts()

    @pl.when(is_valid & jnp.logical_not(is_first))
    def _():
        experts()

    @pl.when(k == nk - 1)
    def _():
        last = n_tiles - 1
        for_rows(lambda r: scatter_copy(r, last, last % 2).start())
        for tile in (last - 1, last):
            for_rows(lambda r, tile=tile: scatter_copy(r, tile, tile % 2).wait())
        for extra in range(1, GATHER_SLOTS):
            for_rows(lambda r, extra=extra: gather_copy(r, last, (last + extra) % GATHER_SLOTS).wait())


_CLASS_PAIRS = [(a, b) for a in range(EXPERTS_PER_GROUP) for b in range(a + 1, EXPERTS_PER_GROUP)]


def _routing_plan(cls, rank, tile_counts, n_tiles):
    counts = jnp.sum(tile_counts, axis=0)
    off = jnp.concatenate([jnp.zeros((1,), jnp.int32), jnp.cumsum(counts).astype(jnp.int32)])
    tile_base = jnp.cumsum(tile_counts, axis=0) - tile_counts + off[None, :-1]
    onehot = cls[:, :, None] == jnp.arange(N_CLASSES, dtype=jnp.int32)[None, None, :]
    dest = (jnp.sum(jnp.where(onehot, tile_base[:, None, :], 0), axis=-1) + rank).reshape(-1)
    first_tile = off[:-1] // TM
    last_tile = (off[1:] - 1) // TM
    n_items = jnp.where(counts > 0, last_tile - first_tile + 1, 0)
    istart = jnp.concatenate([jnp.zeros((1,), jnp.int32), jnp.cumsum(n_items).astype(jnp.int32)])
    total = istart[-1]
    ni = n_tiles + N_CLASSES - 1
    k = jnp.arange(ni, dtype=jnp.int32)
    valid = k < total
    kc = jnp.minimum(k, total - 1)
    c_of_k = jnp.minimum(jnp.sum(istart[None, 1:] <= kc[:, None], axis=1).astype(jnp.int32), N_CLASSES - 1)
    tile_k = first_tile[c_of_k] + kc - istart[c_of_k]
    prev_tile = jnp.concatenate([jnp.full((1,), -1, jnp.int32), tile_k[:-1]])
    next_tile = jnp.concatenate([tile_k[1:], jnp.full((1,), -1, jnp.int32)])
    first = valid & (tile_k != prev_tile)
    last = valid & ((tile_k != next_tile) | (k == total - 1))
    flags = (first * FLAG_FIRST + last * FLAG_LAST + valid * FLAG_VALID).astype(jnp.int32)
    pair_lo = jnp.array([p[0] for p in _CLASS_PAIRS], jnp.int32)
    pair_hi = jnp.array([p[1] for p in _CLASS_PAIRS], jnp.int32)
    base = (c_of_k // N_PAIRS) * EXPERTS_PER_GROUP
    e_lo = base + pair_lo[c_of_k % N_PAIRS]
    e_hi = base + pair_hi[c_of_k % N_PAIRS]
    cls_k = jnp.where(valid, c_of_k, -1)
    return dest.astype(jnp.int32), tile_k, e_lo, e_hi, cls_k, flags


def _const_spec(shape):
    nd = len(shape)
    return pl.BlockSpec(shape, lambda *_: (0,) * nd, pipeline_mode=pl.Buffered(1))


def _block_diag_ones(n, blk):
    r = jnp.arange(n) // blk
    return (r[:, None] == r[None, :]).astype(BF16)


def kernel(x, mem, positions, g_norm1, w_in, g_q, g_k, sinks, w_o_attn, w_conv_dw, b_conv_dw, g_conv_ln, b_conv_ln, w_conv_out, g_mem, w_kv_mem, g_xq, g_xk, w_o_mem, w_out, g_norm2, w_group, b_group, w_router, b_router, w_gate, w_up, w_down):
    B, S, D = x.shape
    M = mem.shape[1]
    assert D == D_MODEL and S % TQ == 0 and (B * S) % TM == 0 and w_in.shape[0] == 1
    assert (TQ // WINDOW) * ATT_KV_HEADS == MEM_HEADS and CONV_CH // LANES == N_BRANCHES + 1
    NT = S // TQ
    T = B * S
    l = 0

    bd128 = _block_diag_ones(XQ_WIDTH, MEM_HEAD_DIM)
    row = lambda v: v.reshape(1, -1).astype(F32)

    mk, mv = pl.pallas_call(
        _memkv_kernel,
        grid=(B,),
        in_specs=[pl.BlockSpec((1, M, D), lambda b: (b, 0, 0)),
                  _const_spec((1, D)), _const_spec((D, 2 * XQ_WIDTH)),
                  _const_spec((1, XQ_WIDTH)), _const_spec((XQ_WIDTH, XQ_WIDTH))],
        out_specs=[pl.BlockSpec((1, M, XQ_WIDTH), lambda b: (b, 0, 0))] * 2,
        out_shape=[jax.ShapeDtypeStruct((B, M, XQ_WIDTH), BF16)] * 2,
        compiler_params=pltpu.CompilerParams(dimension_semantics=("arbitrary",)),
        name="memkv",
    )(mem, row(g_mem[l]), w_kv_mem[l].astype(BF16), row(jnp.tile(g_xk[l], MEM_HEADS)), bd128)

    inv_freq = 1.0 / (ROPE_THETA ** (jnp.arange(0, HEAD_DIM, 2, dtype=F32) / HEAD_DIM))
    per_row = LANES // (HEAD_DIM // 2)
    ang = positions.astype(F32).reshape(B, S // per_row, per_row, 1) * inv_freq
    ang = ang.reshape(B, S // per_row, LANES)
    cos_tab, sin_tab = jnp.cos(ang), jnp.sin(ang)

    w = w_in[l]
    c0 = Q_WIDTH + 2 * KV_WIDTH
    c1 = c0 + GLU_WIDTH
    c2 = c1 + XQ_WIDTH
    w_qkv = w[:, :c0].astype(BF16)
    w_glu = w[:, c0:c1].astype(BF16)
    w_xq = w[:, c1:c2].astype(BF16)
    w_gt = w[:, c2:].astype(BF16)
    wdw = jnp.zeros((CONV_HALO, CONV_CH), F32).at[:CONV_WIDTH].set(w_conv_dw[l])
    w_rt = jnp.zeros((RT_ROWS, D), F32).at[0:N_GROUPS].set(w_group[l].T).at[8:8 + N_EXPERTS].set(w_router[l].T)
    b_rt = jnp.zeros((RT_ROWS, 1), F32).at[0:N_GROUPS, 0].set(b_group[l]).at[8:8 + N_EXPERTS, 0].set(b_router[l])

    n_mix = B * NT
    tile_in = lambda s: jnp.minimum(s, n_mix - 1)
    tile_out = lambda s: jnp.maximum(s - 1, 0)
    tile3 = lambda last: pl.BlockSpec((1, TQ, last), lambda s: (tile_in(s) // NT, tile_in(s) % NT, 0))
    per_batch = pl.BlockSpec((1, M, XQ_WIDTH), lambda s: (tile_in(s) // NT, 0, 0))
    rope_spec = pl.BlockSpec((1, TQ // per_row, LANES), lambda s: (tile_in(s) // NT, tile_in(s) % NT, 0))
    in_specs = [
        tile3(D), rope_spec, rope_spec, per_batch, per_batch,
        _const_spec((1, D)),
        _const_spec((D, c0)), _const_spec((D, GLU_WIDTH)), _const_spec((D, XQ_WIDTH)), _const_spec((D, GATE_WIDTH)),
        _const_spec((1, Q_WIDTH)), _const_spec((1, LANES)),
        pl.BlockSpec(memory_space=pltpu.SMEM),
        _const_spec((Q_WIDTH, D)),
        _const_spec((CONV_HALO, CONV_CH)), _const_spec((1, CONV_CH)), _const_spec((1, CONV_CH)), _const_spec((1, CONV_CH)),
        _const_spec((CONV_CH, D)),
        _const_spec((1, XQ_WIDTH)),
        _const_spec((XQ_WIDTH, D)),
        _const_spec((D, D)),
        _const_spec((1, D)),
        _const_spec((RT_ROWS, D)), _const_spec((RT_ROWS, 1)), _const_spec((TQ, TQ)),
    ]
    tri = (jnp.arange(TQ)[:, None] <= jnp.arange(TQ)[None, :]).astype(BF16)
    per_tile = lambda rows, last: pl.BlockSpec((1, rows, last), lambda s: (tile_out(s), 0, 0))
    x1_spec = pl.BlockSpec((1, TQ, XW), lambda s: (tile_out(s) // NT, tile_out(s) % NT, 0))
    assert (N_EXPERTS * D) % n_mix == 0 and (N_EXPERTS * EXPERT_FF) % n_mix == 0
    up_rows, down_rows = N_EXPERTS * D // n_mix, N_EXPERTS * EXPERT_FF // n_mix
    up_slice = pl.BlockSpec((up_rows, EXPERT_FF), lambda s: (tile_in(s), 0))
    down_slice = pl.BlockSpec((down_rows, D), lambda s: (tile_in(s), 0))
    in_specs += [up_slice, up_slice, down_slice]
    x1, rt, cnt, wg, wu, wd = pl.pallas_call(
        functools.partial(_mixer_kernel, tiles_per_seq=NT),
        grid=(n_mix + 1,),
        in_specs=in_specs,
        out_specs=[x1_spec, per_tile(8, TQ), per_tile(8, LANES), up_slice, up_slice, down_slice],
        out_shape=[jax.ShapeDtypeStruct((B, S, XW), F32), jax.ShapeDtypeStruct((B * NT, 8, TQ), F32),
                   jax.ShapeDtypeStruct((B * NT, 8, LANES), F32),
                   jax.ShapeDtypeStruct((N_EXPERTS * D, EXPERT_FF), BF16),
                   jax.ShapeDtypeStruct((N_EXPERTS * D, EXPERT_FF), BF16),
                   jax.ShapeDtypeStruct((N_EXPERTS * EXPERT_FF, D), BF16)],
        scratch_shapes=[pltpu.VMEM((WINDOW + TQ, LANES), BF16)] * 4
        + [pltpu.VMEM((CONV_CH // LANES, CONV_HALO + TQ, LANES), F32),
           pltpu.VMEM((CONV_CH // LANES, TQ, LANES), F32), pltpu.VMEM((TQ, Q_WIDTH), BF16),
           pltpu.VMEM((TQ, XQ_WIDTH), BF16), pltpu.VMEM((TQ, D), F32),
           pltpu.VMEM((TQ, LANES), F32), pltpu.VMEM((TQ, LANES), F32)],
        compiler_params=pltpu.CompilerParams(dimension_semantics=("arbitrary",),
                                             vmem_limit_bytes=VMEM_LIMIT),
        name="mixer",
    )(x, cos_tab, sin_tab, mk, mv, row(g_norm1[l]), w_qkv, w_glu, w_xq, w_gt,
      row(jnp.tile(g_q[l], ATT_HEADS) * (HEAD_DIM ** -0.5)), row(jnp.tile(g_k[l], ATT_KV_HEADS)),
      sinks[l].astype(F32), w_o_attn[l].astype(BF16),
      wdw, row(b_conv_dw[l]), row(g_conv_ln[l]), row(b_conv_ln[l]), w_conv_out[l].astype(BF16),
      row(jnp.tile(g_xq[l], MEM_HEADS) * (MEM_HEAD_DIM ** -0.5)), w_o_mem[l].astype(BF16),
      w_out[l].astype(BF16), row(g_norm2[l]), w_rt, b_rt, tri,
      w_gate[l].reshape(N_EXPERTS * D, EXPERT_FF), w_up[l].reshape(N_EXPERTS * D, EXPERT_FF),
      w_down[l].reshape(N_EXPERTS * EXPERT_FF, D))

    n_tiles = T // TM
    plan = _routing_plan(rt[:, 0, :].astype(jnp.int32), rt[:, 1, :].astype(jnp.int32),
                         cnt[:, 0, :N_CLASSES].astype(jnp.int32), n_tiles)
    dest, tile_k, e_lo, e_hi, cls_k, flags = plan
    ni = tile_k.shape[0]
    wg = wg.reshape(N_EXPERTS, D, EXPERT_FF)
    wu = wu.reshape(N_EXPERTS, D, EXPERT_FF)
    wd = wd.reshape(N_EXPERTS, EXPERT_FF, D)
    lo_map = lambda k, dest, tile, elo, ehi, cls, flg: (elo[k], 0, 0)
    hi_map = lambda k, dest, tile, elo, ehi, cls, flg: (ehi[k], 0, 0)
    up_block = (1, D, EXPERT_FF)
    down_block = (1, EXPERT_FF, D)
    out = pl.pallas_call(
        _moe_kernel,
        grid_spec=pltpu.PrefetchScalarGridSpec(
            num_scalar_prefetch=6,
            grid=(ni,),
            in_specs=[pl.BlockSpec((1, D), lambda k, *_: (0, 0)),
                      pl.BlockSpec(up_block, lo_map), pl.BlockSpec(up_block, lo_map),
                      pl.BlockSpec(down_block, lo_map),
                      pl.BlockSpec(up_block, hi_map), pl.BlockSpec(up_block, hi_map),
                      pl.BlockSpec(down_block, hi_map),
                      pl.BlockSpec(memory_space=pl.ANY)],
            out_specs=pl.BlockSpec(memory_space=pl.ANY),
            scratch_shapes=[pltpu.SMEM((T,), jnp.int32),
                            pltpu.VMEM((GATHER_SLOTS, TM, XW), F32), pltpu.VMEM((2, TM, D), F32),
                            pltpu.VMEM((TM, D), BF16),
                            pltpu.SemaphoreType.DMA((GATHER_SLOTS,)), pltpu.SemaphoreType.DMA((2,))]),
        out_shape=jax.ShapeDtypeStruct((T, D), F32),
        compiler_params=pltpu.CompilerParams(dimension_semantics=("arbitrary",),
                                             vmem_limit_bytes=VMEM_LIMIT),
        name="moe",
    )(dest, tile_k, e_lo, e_hi, cls_k, flags, row(g_norm2[l]), wg, wu, wd, wg, wu, wd,
      x1.reshape(T, XW))
    return out.reshape(B, S, D)
```

```python
import functools

import jax
import jax.numpy as jnp
from jax import lax
from jax.experimental import pallas as pl
from jax.experimental.pallas import tpu as pltpu

F32 = jnp.float32
BF16 = jnp.bfloat16

D_MODEL = 1024
ATT_HEADS = 8
ATT_KV_HEADS = 2
HEAD_DIM = 64
WINDOW = 128
ROPE_THETA = 10000.0
CONV_CH = 512
CONV_WIDTH = 31
MEM_HEADS = 4
MEM_HEAD_DIM = 128
N_BRANCHES = 3
N_GROUPS = 4
EXPERTS_PER_GROUP = 4
N_EXPERTS = N_GROUPS * EXPERTS_PER_GROUP
EXPERT_FF = 512
EPS = 1e-6
LN_EPS = 1e-5
NEG_INF = -1e30

Q_WIDTH = ATT_HEADS * HEAD_DIM
KV_WIDTH = ATT_KV_HEADS * HEAD_DIM
GLU_WIDTH = 2 * CONV_CH
XQ_WIDTH = MEM_HEADS * MEM_HEAD_DIM
GATE_WIDTH = N_BRANCHES * D_MODEL

LANES = 128
TQ = 256
CONV_HALO = 32
CONV_ROWS = 128
TM = 256
RT_ROWS = 32
N_PAIRS = EXPERTS_PER_GROUP * (EXPERTS_PER_GROUP - 1) // 2
N_CLASSES = N_GROUPS * N_PAIRS
CLS_ROWS = 32
XW = D_MODEL + LANES
VMEM_LIMIT = 56 * 1024 * 1024


def _dot(a, b):
    return jnp.dot(a, b, preferred_element_type=F32)


def _dot_nt(a, b):
    return lax.dot_general(a, b, (((1,), (1,)), ((), ())), preferred_element_type=F32)


def _split_bf16(t):
    hi = t.astype(BF16)
    lo = (t - hi.astype(F32)).astype(BF16)
    return hi, lo


def _block_sum(t, bd):
    return _dot(t.astype(BF16), bd)


def _rms(t):
    return t * lax.rsqrt(jnp.mean(t * t, axis=-1, keepdims=True) + EPS)


def _memkv_kernel(mem_ref, gmem_ref, wkv_ref, gxk_ref, bd_ref, mk_ref, mv_ref):
    mn = (_rms(mem_ref[0]) * gmem_ref[...]).astype(BF16)
    kv = _dot(mn, wkv_ref[...])
    mk = kv[:, :XQ_WIDTH]
    ss = _block_sum(mk * mk, bd_ref[...])
    mk = mk * lax.rsqrt(ss * (1.0 / MEM_HEAD_DIM) + EPS) * gxk_ref[...]
    mk_ref[0] = mk.astype(BF16)
    mv_ref[0] = kv[:, XQ_WIDTH:].astype(BF16)


def _mixer_kernel(x_ref, cos_ref, sin_ref, mk_ref, mv_ref, g1_ref, wqkv_ref, wglu_ref, wxq_ref,
                  wgt_ref, gq_ref, gk_ref, sinks_ref, woa_ref, wdw_ref,
                  bdw_ref, gln_ref, bln_ref, wco_ref, gxq_ref, wom_ref, wout_ref, g2_ref,
                  wrt_ref, brt_ref, tri_ref, ewg_ref, ewu_ref, ewd_ref,
                  x1_ref, rt_ref, cnt_ref, ewg_out, ewu_out, ewd_out,
                  k0_ref, k1_ref, v0_ref, v1_ref, ubuf_ref, ybuf_ref, o_ref, om_ref, x1prev_ref,
                  cosd_ref, sind_ref, *, tiles_per_seq):
    s = pl.program_id(0)
    n_tiles = pl.num_programs(0) - 1
    i = jnp.minimum(s, n_tiles - 1) % tiles_per_seq
    kbands = (k0_ref, k1_ref)
    vbands = (v0_ref, v1_ref)

    @pl.when(s == 0)
    def _():
        x1prev_ref[...] = jnp.zeros((TQ, D_MODEL), F32)

    @pl.when(i == 0)
    def _():
        for r in kbands + vbands:
            r[0:WINDOW, :] = jnp.zeros((WINDOW, LANES), BF16)
        ubuf_ref[:, 0:CONV_HALO, :] = jnp.zeros((CONV_CH // LANES, CONV_HALO, LANES), F32)

    x1_prev = x1prev_ref[...]
    x1_ref[0, :, 0:D_MODEL] = x1_prev

    ewg_out[...] = ewg_ref[...].astype(BF16)
    ewu_out[...] = ewu_ref[...].astype(BF16)
    ewd_out[...] = ewd_ref[...].astype(BF16)

    x = x_ref[0]
    h = (_rms(x) * g1_ref[...]).astype(BF16)

    n_slabs = CONV_CH // LANES
    half = TQ // 2
    group = ATT_HEADS // ATT_KV_HEADS

    glu = _dot(h, wglu_ref[...])
    route_logits = _router_logits(x1_prev, g2_ref, wrt_ref, brt_ref)
    u = glu[:, :CONV_CH] * jax.nn.sigmoid(glu[:, CONV_CH:])
    for c in range(n_slabs):
        ubuf_ref[c, CONV_HALO:CONV_HALO + TQ, :] = u[:, c * LANES:(c + 1) * LANES]
    base = CONV_HALO - (CONV_WIDTH - 1)
    acc = [[None] * n_slabs for _ in range(2)]

    def conv_slab(c):
        cols = slice(c * LANES, (c + 1) * LANES)
        for par in range(2):
            parts = []
            for q0 in range(0, half, CONV_ROWS):
                a = jnp.broadcast_to(bdw_ref[:, cols], (CONV_ROWS, LANES))
                for j in range(CONV_WIDTH):
                    a = a + wdw_ref[j:j + 1, cols] * ubuf_ref[
                        c, pl.ds(base + j + par + 2 * q0, CONV_ROWS, stride=2), :]
                parts.append(a)
            acc[par][c] = jnp.concatenate(parts, axis=0)
        ubuf_ref[c, 0:CONV_HALO, :] = ubuf_ref[c, TQ:TQ + CONV_HALO, :]

    def gate(b):
        return jax.nn.sigmoid(_dot(h, wgt_ref[:, b * D_MODEL:(b + 1) * D_MODEL]))

    def conv_finish():
        for par in range(2):
            row_sum = functools.reduce(jnp.add, [jnp.sum(a, axis=-1, keepdims=True) for a in acc[par]])
            mu = row_sum * (1.0 / CONV_CH)
            cen = [a - mu for a in acc[par]]
            sq_sum = functools.reduce(jnp.add, [jnp.sum(t * t, axis=-1, keepdims=True) for t in cen])
            rstd = lax.rsqrt(sq_sum * (1.0 / CONV_CH) + LN_EPS)
            for c in range(n_slabs):
                cols = slice(c * LANES, (c + 1) * LANES)
                yln = cen[c] * rstd * gln_ref[:, cols] + bln_ref[:, cols]
                ybuf_ref[c, pl.ds(par, half, stride=2), :] = yln * jax.nn.sigmoid(yln)
        uo = jnp.concatenate([ybuf_ref[c] for c in range(n_slabs)], axis=-1).astype(BF16)
        return _dot(uo, wco_ref[...])

    for c in range(n_slabs):
        conv_slab(c)
    qkv = _dot(h, wqkv_ref[...])
    gates = [gate(0)]
    q = qkv[:, :Q_WIDTH]
    k = qkv[:, Q_WIDTH:Q_WIDTH + KV_WIDTH]
    v = qkv[:, Q_WIDTH + KV_WIDTH:]
    gates.append(gate(1))

    lane = lax.broadcasted_iota(jnp.int32, (TQ, LANES), 1)
    first_half = (lane % HEAD_DIM) < (HEAD_DIM // 2)
    low_head = lane < HEAD_DIM
    n_freq = HEAD_DIM // 2
    per_row = LANES // n_freq
    quarter = lax.broadcasted_iota(jnp.int32, (TQ // per_row, LANES), 1) // n_freq
    sign = jnp.where(quarter % 2 == 0, -1.0, 1.0)
    for table_ref, out_ref, scale in ((cos_ref, cosd_ref, None), (sin_ref, sind_ref, sign)):
        dense = table_ref[0]
        rolled = [dense] + [pltpu.roll(dense, n_freq * kk, 1) for kk in range(1, per_row)]
        for j in range(per_row):
            spread = rolled[(-j) % per_row]
            for qq in range(1, per_row):
                spread = jnp.where(quarter == qq, rolled[(qq - j) % per_row], spread)
            out_ref[pl.ds(j, TQ // per_row, stride=per_row), :] = spread if scale is None else spread * scale
    cosd = cosd_ref[...]
    sind = sind_ref[...]

    def rope(t):
        rot = jnp.where(first_half, pltpu.roll(t, LANES - HEAD_DIM // 2, 1),
                        pltpu.roll(t, HEAD_DIM // 2, 1))
        return t * cosd + rot * sind

    def dup_halves(t):
        swapped = pltpu.roll(t, HEAD_DIM, 1)
        return jnp.where(low_head, t, swapped), jnp.where(low_head, swapped, t)

    def qk_norm(t, gain):
        sq = t * t
        lo = jnp.sum(jnp.where(low_head, sq, 0.0), axis=-1, keepdims=True)
        hi = jnp.sum(jnp.where(low_head, 0.0, sq), axis=-1, keepdims=True)
        inv = jnp.where(low_head, lax.rsqrt(lo * (1.0 / HEAD_DIM) + EPS), lax.rsqrt(hi * (1.0 / HEAD_DIM) + EPS))
        return t * inv * gain

    kd = dup_halves(rope(qk_norm(k, gk_ref[...])))
    vd = dup_halves(v)
    for kvh in range(ATT_KV_HEADS):
        kbands[kvh][WINDOW:WINDOW + TQ, :] = kd[kvh].astype(BF16)
        vbands[kvh][WINDOW:WINDOW + TQ, :] = vd[kvh].astype(BF16)

    q_heads = []
    for c in range(Q_WIDTH // LANES):
        cols = slice(c * LANES, (c + 1) * LANES)
        qc = rope(qk_norm(q[:, cols], gq_ref[:, cols]))
        q_heads.append(jnp.where(low_head, qc, 0.0).astype(BF16))
        q_heads.append(jnp.where(low_head, 0.0, qc).astype(BF16))

    qi = lax.broadcasted_iota(jnp.int32, (WINDOW, 2 * WINDOW), 0)
    kj = lax.broadcasted_iota(jnp.int32, (WINDOW, 2 * WINDOW), 1)
    band_mask = (kj > qi) & (kj <= qi + WINDOW)
    first_mask = band_mask & (kj >= jnp.where(i > 0, 0, WINDOW))
    low_head_w = lax.broadcasted_iota(jnp.int32, (WINDOW, LANES), 1) < HEAD_DIM

    def attention_scores(step):
        n, kvh = divmod(step, ATT_KV_HEADS)
        rows = slice(n * WINDOW, (n + 1) * WINDOW)
        mask = first_mask if n == 0 else band_mask
        heads = range(kvh * group, (kvh + 1) * group)
        q_st = jnp.concatenate([q_heads[hd][rows] for hd in heads], axis=0)
        kband = kbands[kvh][n * WINDOW:n * WINDOW + 2 * WINDOW, :]
        s_all = _dot_nt(q_st, kband)
        es, dens = [], []
        for j, hd in enumerate(heads):
            s = jnp.where(mask, s_all[j * WINDOW:(j + 1) * WINDOW], NEG_INF)
            sink = sinks_ref[hd]
            m = jnp.maximum(jnp.max(s, axis=-1, keepdims=True), sink)
            e = jnp.exp(s - m)
            dens.append(jnp.sum(e, axis=-1, keepdims=True) + jnp.exp(sink - m))
            es.append(e.astype(BF16))
        return jnp.concatenate(es, axis=0), dens

    def attention_output(step, probs):
        n, kvh = divmod(step, ATT_KV_HEADS)
        rows = slice(n * WINDOW, (n + 1) * WINDOW)
        e_all, dens = probs
        vband = vbands[kvh][n * WINDOW:n * WINDOW + 2 * WINDOW, :]
        o_all = _dot(e_all, vband)
        for cc in range(group // 2):
            oa = o_all[(2 * cc) * WINDOW:(2 * cc + 1) * WINDOW] / dens[2 * cc]
            ob = o_all[(2 * cc + 1) * WINDOW:(2 * cc + 2) * WINDOW] / dens[2 * cc + 1]
            col = kvh * (group // 2) + cc
            o_ref[rows, col * LANES:(col + 1) * LANES] = jnp.where(low_head_w, oa, ob).astype(BF16)

    def mem_scores(hd, xq):
        cols = slice(hd * MEM_HEAD_DIM, (hd + 1) * MEM_HEAD_DIM)
        s = _dot_nt(xq[:, cols], mk_ref[0, :, cols])
        e = jnp.exp(s - jnp.max(s, axis=-1, keepdims=True))
        return e.astype(BF16), jnp.sum(e, axis=-1, keepdims=True)

    def mem_output(hd, probs):
        cols = slice(hd * MEM_HEAD_DIM, (hd + 1) * MEM_HEAD_DIM)
        e, den = probs
        om_ref[:, cols] = (_dot(e, mv_ref[0, :, cols]) / den).astype(BF16)

    route = _router_select(route_logits)
    p0 = attention_scores(0)
    gates.append(gate(2))
    p1 = attention_scores(1)
    attention_output(0, p0)
    xq = _dot(h, wxq_ref[...])
    p2 = attention_scores(2)
    attention_output(1, p1)
    _router_emit(*route, tri_ref, x1_ref, rt_ref, cnt_ref)
    xq = jnp.concatenate(
        [_rms(xq[:, hd * MEM_HEAD_DIM:(hd + 1) * MEM_HEAD_DIM]) for hd in range(MEM_HEADS)], axis=-1)
    xq = (xq * gxq_ref[...]).astype(BF16)
    p3 = attention_scores(3)
    attention_output(2, p2)
    y_conv = conv_finish()
    m0 = mem_scores(0, xq)
    attention_output(3, p3)
    for r in kbands + vbands:
        r[0:WINDOW, :] = r[TQ:TQ + WINDOW, :]
    m1 = mem_scores(1, xq)
    y_attn = _dot(o_ref[...], woa_ref[...])
    mem_output(0, m0)
    m2 = mem_scores(2, xq)
    mem_output(1, m1)
    m3 = mem_scores(3, xq)
    mem_output(2, m2)
    merged = gates[0] * y_attn + gates[1] * y_conv
    mem_output(3, m3)
    merged = merged + gates[2] * _dot(om_ref[...], wom_ref[...])

    x1prev_ref[...] = x + _dot(merged.astype(BF16), wout_ref[...])


def _router_logits(x1, g2_ref, wrt_ref, brt_ref):
    h2 = _rms(x1) * g2_ref[...]
    h_hi, h_lo = _split_bf16(h2)
    w_hi, w_lo = _split_bf16(wrt_ref[...])
    return _dot_nt(w_hi, h_hi) + _dot_nt(w_hi, h_lo) + _dot_nt(w_lo, h_hi) + brt_ref[...]


def _router_select(lt):
    gl = [lt[g:g + 1] for g in range(N_GROUPS)]
    gmax = functools.reduce(jnp.maximum, gl)
    g_idx = jnp.full((1, TQ), N_GROUPS - 1, jnp.int32)
    for g in reversed(range(N_GROUPS - 1)):
        g_idx = jnp.where(gl[g] == gmax, g, g_idx)
    p_g = 1.0 / functools.reduce(jnp.add, [jnp.exp(r - gmax) for r in gl])
    el = []
    for kk in range(EXPERTS_PER_GROUP):
        row = lt[8 + kk:9 + kk]
        for g in range(1, N_GROUPS):
            off = 8 + g * EXPERTS_PER_GROUP + kk
            row = jnp.where(g_idx == g, lt[off:off + 1], row)
        el.append(row)

    def first_argmax(rows):
        top = functools.reduce(jnp.maximum, rows)
        idx = jnp.full((1, TQ), len(rows) - 1, jnp.int32)
        for kk in reversed(range(len(rows) - 1)):
            idx = jnp.where(rows[kk] == top, kk, idx)
        return top, idx

    v1, i1 = first_argmax(el)
    v2, i2 = first_argmax([jnp.where(i1 == kk, -jnp.inf, el[kk]) for kk in range(EXPERTS_PER_GROUP)])
    t = jnp.exp(v2 - v1)
    p1 = 1.0 / (1.0 + t)
    p2 = t * p1

    lo = jnp.minimum(i1, i2)
    hi = jnp.maximum(i1, i2)
    w_lo = jnp.where(i1 < i2, p_g * p1, p_g * p2)
    w_hi = jnp.where(i1 < i2, p_g * p2, p_g * p1)
    cls = g_idx * N_PAIRS + ((lo * (2 * EXPERTS_PER_GROUP - 1 - lo)) >> 1) + hi - lo - 1
    return cls, w_lo, w_hi


def _router_emit(cls, w_lo, w_hi, tri_ref, x1_ref, rt_ref, cnt_ref):
    cls_f = cls.astype(F32)
    onehot = jnp.broadcast_to(cls, (CLS_ROWS, TQ)) == lax.broadcasted_iota(jnp.int32, (CLS_ROWS, TQ), 0)
    onehot_bf = onehot.astype(F32).astype(BF16)
    prefix = _dot(onehot_bf, tri_ref[...])
    rank = jnp.sum(jnp.where(onehot, prefix, 0.0), axis=0, keepdims=True) - 1.0
    counts = _dot_nt(jnp.ones((8, TQ), BF16), onehot_bf)
    zero = jnp.zeros((1, TQ), F32)
    rt_ref[0] = jnp.concatenate([cls_f, rank] + [zero] * 6, axis=0)
    cnt_ref[0] = jnp.concatenate([counts, jnp.zeros((8, LANES - CLS_ROWS), F32)], axis=1)
    cols = jnp.concatenate([w_lo, w_hi, cls_f, jnp.zeros((LANES - 3, TQ), F32)], axis=0)
    x1_ref[0, :, D_MODEL:] = cols.T


FLAG_FIRST, FLAG_LAST, FLAG_VALID = 1, 2, 4
GATHER_SLOTS = 3


def _moe_kernel(dest_ref, tile_ref, elo_ref, ehi_ref, cls_ref, flag_ref,
                g2_ref, wg_lo, wu_lo, wd_lo, wg_hi, wu_hi, wd_hi, x1_hbm,
                out_hbm,
                perm_ref, xbuf, obuf, h2buf, wbuf_up, wbuf_down, gsem, ssem):
    k = pl.program_id(0)
    nk = pl.num_programs(0)
    t = tile_ref[k]
    xslot = t % GATHER_SLOTS
    oslot = t % 2
    flags = flag_ref[k]
    n_tok = x1_hbm.shape[0]
    n_tiles = n_tok // TM

    def gather_copy(r, tile, s):
        tok = perm_ref[tile * TM + r]
        return pltpu.make_async_copy(x1_hbm.at[pl.ds(tok, 1), :], xbuf.at[s, pl.ds(r, 1), :], gsem.at[s])

    def scatter_copy(r, tile, s):
        tok = perm_ref[tile * TM + r]
        return pltpu.make_async_copy(obuf.at[s, pl.ds(r, 1), :], out_hbm.at[pl.ds(tok, 1), :], ssem.at[s])

    k_prev = jnp.maximum(k - 1, 0)
    for side, (e_ref, wg, wu, wd) in enumerate(((elo_ref, wg_lo, wu_lo, wd_lo), (ehi_ref, wg_hi, wu_hi, wd_hi))):
        @pl.when((k == 0) | (e_ref[k] != e_ref[k_prev]))
        def _(side=side, wg=wg, wu=wu, wd=wd):
            wbuf_up[2 * side] = wg[0]
            wbuf_up[2 * side + 1] = wu[0]
            wbuf_down[side] = wd[0]

    def for_rows(fn):
        def body(r, carry):
            fn(r)
            return carry
        lax.fori_loop(0, TM, body, 0, unroll=8)

    @pl.when(k == 0)
    def _():
        def invert(tok, carry):
            perm_ref[dest_ref[tok]] = tok
            return carry
        lax.fori_loop(0, n_tok, invert, 0, unroll=8)
        for tile in range(GATHER_SLOTS - 1):
            for_rows(lambda r, tile=tile: gather_copy(r, tile, tile).start())

    is_first = (flags & FLAG_FIRST) != 0
    is_valid = (flags & FLAG_VALID) != 0
    ahead = GATHER_SLOTS - 1
    prefetch_tile = jnp.minimum(t + ahead, n_tiles - 1)
    prefetch_slot = (t + ahead) % GATHER_SLOTS

    @pl.when(is_first)
    def _():
        for_rows(lambda r: gather_copy(r, t, xslot).wait())

        @pl.when(t >= 2)
        def _():
            for_rows(lambda r: scatter_copy(r, t - 2, oslot).wait())

        xr = xbuf[xslot, :, 0:D_MODEL]
        h2buf[...] = (_rms(xr) * g2_ref[...]).astype(BF16)
        obuf[oslot] = xr

    def experts():
        w = xbuf[xslot, :, D_MODEL:XW]
        mine = w[:, 2:3] == cls_ref[k].astype(F32)
        h2 = h2buf[...]
        y = None
        for side in range(2):
            c = jnp.where(mine, w[:, side:side + 1], 0.0)
            hid = jax.nn.silu(_dot(h2, wbuf_up[2 * side])) * _dot(h2, wbuf_up[2 * side + 1]) * c
            part = _dot(hid.astype(BF16), wbuf_down[side])
            y = part if y is None else y + part
        obuf[oslot] += y

    @pl.when(is_first & (t > 0))
    def _():
        for r in range(TM):
            gather_copy(r, prefetch_tile, prefetch_slot).start()
            scatter_copy(r, t - 1, 1 - oslot).start()
        experts()

    @pl.when(is_first & (t == 0))
    def _():
        for r in range(TM):
            gather_copy(r, prefetch_tile, prefetch_slot).start()
        experts()

    @pl.when(is_valid & jnp.logical_not(is_first))
    def _():
        experts()

    @pl.when(k == nk - 1)
    def _():
        last = n_tiles - 1
        for_rows(lambda r: scatter_copy(r, last, last % 2).start())
        for tile in (last - 1, last):
            for_rows(lambda r, tile=tile: scatter_copy(r, tile, tile % 2).wait())
        for extra in range(1, GATHER_SLOTS):
            for_rows(lambda r, extra=extra: gather_copy(r, last, (last + extra) % GATHER_SLOTS).wait())


_CLASS_PAIRS = [(a, b) for a in range(EXPERTS_PER_GROUP) for b in range(a + 1, EXPERTS_PER_GROUP)]


def _routing_plan(cls, rank, tile_counts, n_tiles):
    counts = jnp.sum(tile_counts, axis=0)
    off = jnp.concatenate([jnp.zeros((1,), jnp.int32), jnp.cumsum(counts).astype(jnp.int32)])
    tile_base = jnp.cumsum(tile_counts, axis=0) - tile_counts + off[None, :-1]
    onehot = cls[:, :, None] == jnp.arange(N_CLASSES, dtype=jnp.int32)[None, None, :]
    dest = (jnp.sum(jnp.where(onehot, tile_base[:, None, :], 0), axis=-1) + rank).reshape(-1)
    first_tile = off[:-1] // TM
    last_tile = (off[1:] - 1) // TM
    n_items = jnp.where(counts > 0, last_tile - first_tile + 1, 0)
    istart = jnp.concatenate([jnp.zeros((1,), jnp.int32), jnp.cumsum(n_items).astype(jnp.int32)])
    total = istart[-1]
    ni = n_tiles + N_CLASSES - 1
    k = jnp.arange(ni, dtype=jnp.int32)
    valid = k < total
    kc = jnp.minimum(k, total - 1)
    c_of_k = jnp.minimum(jnp.sum(istart[None, 1:] <= kc[:, None], axis=1).astype(jnp.int32), N_CLASSES - 1)
    tile_k = first_tile[c_of_k] + kc - istart[c_of_k]
    prev_tile = jnp.concatenate([jnp.full((1,), -1, jnp.int32), tile_k[:-1]])
    next_tile = jnp.concatenate([tile_k[1:], jnp.full((1,), -1, jnp.int32)])
    first = valid & (tile_k != prev_tile)
    last = valid & ((tile_k != next_tile) | (k == total - 1))
    flags = (first * FLAG_FIRST + last * FLAG_LAST + valid * FLAG_VALID).astype(jnp.int32)
    pair_lo = jnp.array([p[0] for p in _CLASS_PAIRS], jnp.int32)
    pair_hi = jnp.array([p[1] for p in _CLASS_PAIRS], jnp.int32)
    base = (c_of_k // N_PAIRS) * EXPERTS_PER_GROUP
    e_lo = base + pair_lo[c_of_k % N_PAIRS]
    e_hi = base + pair_hi[c_of_k % N_PAIRS]
    cls_k = jnp.where(valid, c_of_k, -1)
    return dest.astype(jnp.int32), tile_k, e_lo, e_hi, cls_k, flags


def _const_spec(shape):
    nd = len(shape)
    return pl.BlockSpec(shape, lambda *_: (0,) * nd, pipeline_mode=pl.Buffered(1))


def _block_diag_ones(n, blk):
    r = jnp.arange(n) // blk
    return (r[:, None] == r[None, :]).astype(BF16)


def kernel(x, mem, positions, g_norm1, w_in, g_q, g_k, sinks, w_o_attn, w_conv_dw, b_conv_dw, g_conv_ln, b_conv_ln, w_conv_out, g_mem, w_kv_mem, g_xq, g_xk, w_o_mem, w_out, g_norm2, w_group, b_group, w_router, b_router, w_gate, w_up, w_down):
    B, S, D = x.shape
    M = mem.shape[1]
    assert D == D_MODEL and S % TQ == 0 and (B * S) % TM == 0 and w_in.shape[0] == 1
    assert (TQ // WINDOW) * ATT_KV_HEADS == MEM_HEADS and CONV_CH // LANES == N_BRANCHES + 1
    NT = S // TQ
    T = B * S
    l = 0

    bd128 = _block_diag_ones(XQ_WIDTH, MEM_HEAD_DIM)
    row = lambda v: v.reshape(1, -1).astype(F32)

    mk, mv = pl.pallas_call(
        _memkv_kernel,
        grid=(B,),
        in_specs=[pl.BlockSpec((1, M, D), lambda b: (b, 0, 0)),
                  _const_spec((1, D)), _const_spec((D, 2 * XQ_WIDTH)),
                  _const_spec((1, XQ_WIDTH)), _const_spec((XQ_WIDTH, XQ_WIDTH))],
        out_specs=[pl.BlockSpec((1, M, XQ_WIDTH), lambda b: (b, 0, 0))] * 2,
        out_shape=[jax.ShapeDtypeStruct((B, M, XQ_WIDTH), BF16)] * 2,
        compiler_params=pltpu.CompilerParams(dimension_semantics=("arbitrary",)),
        name="memkv",
    )(mem, row(g_mem[l]), w_kv_mem[l].astype(BF16), row(jnp.tile(g_xk[l], MEM_HEADS)), bd128)

    inv_freq = 1.0 / (ROPE_THETA ** (jnp.arange(0, HEAD_DIM, 2, dtype=F32) / HEAD_DIM))
    per_row = LANES // (HEAD_DIM // 2)
    ang = positions.astype(F32).reshape(B, S // per_row, per_row, 1) * inv_freq
    ang = ang.reshape(B, S // per_row, LANES)
    cos_tab, sin_tab = jnp.cos(ang), jnp.sin(ang)

    w = w_in[l]
    c0 = Q_WIDTH + 2 * KV_WIDTH
    c1 = c0 + GLU_WIDTH
    c2 = c1 + XQ_WIDTH
    w_qkv = w[:, :c0].astype(BF16)
    w_glu = w[:, c0:c1].astype(BF16)
    w_xq = w[:, c1:c2].astype(BF16)
    w_gt = w[:, c2:].astype(BF16)
    wdw = jnp.zeros((CONV_HALO, CONV_CH), F32).at[:CONV_WIDTH].set(w_conv_dw[l])
    w_rt = jnp.zeros((RT_ROWS, D), F32).at[0:N_GROUPS].set(w_group[l].T).at[8:8 + N_EXPERTS].set(w_router[l].T)
    b_rt = jnp.zeros((RT_ROWS, 1), F32).at[0:N_GROUPS, 0].set(b_group[l]).at[8:8 + N_EXPERTS, 0].set(b_router[l])

    n_mix = B * NT
    tile_in = lambda s: jnp.minimum(s, n_mix - 1)
    tile_out = lambda s: jnp.maximum(s - 1, 0)
    tile3 = lambda last: pl.BlockSpec((1, TQ, last), lambda s: (tile_in(s) // NT, tile_in(s) % NT, 0))
    per_batch = pl.BlockSpec((1, M, XQ_WIDTH), lambda s: (tile_in(s) // NT, 0, 0))
    rope_spec = pl.BlockSpec((1, TQ // per_row, LANES), lambda s: (tile_in(s) // NT, tile_in(s) % NT, 0))
    in_specs = [
        tile3(D), rope_spec, rope_spec, per_batch, per_batch,
        _const_spec((1, D)),
        _const_spec((D, c0)), _const_spec((D, GLU_WIDTH)), _const_spec((D, XQ_WIDTH)), _const_spec((D, GATE_WIDTH)),
        _const_spec((1, Q_WIDTH)), _const_spec((1, LANES)),
        pl.BlockSpec(memory_space=pltpu.SMEM),
        _const_spec((Q_WIDTH, D)),
        _const_spec((CONV_HALO, CONV_CH)), _const_spec((1, CONV_CH)), _const_spec((1, CONV_CH)), _const_spec((1, CONV_CH)),
        _const_spec((CONV_CH, D)),
        _const_spec((1, XQ_WIDTH)),
        _const_spec((XQ_WIDTH, D)),
        _const_spec((D, D)),
        _const_spec((1, D)),
        _const_spec((RT_ROWS, D)), _const_spec((RT_ROWS, 1)), _const_spec((TQ, TQ)),
    ]
    tri = (jnp.arange(TQ)[:, None] <= jnp.arange(TQ)[None, :]).astype(BF16)
    per_tile = lambda rows, last: pl.BlockSpec((1, rows, last), lambda s: (tile_out(s), 0, 0))
    x1_spec = pl.BlockSpec((1, TQ, XW), lambda s: (tile_out(s) // NT, tile_out(s) % NT, 0))
    assert (N_EXPERTS * D) % n_mix == 0 and (N_EXPERTS * EXPERT_FF) % n_mix == 0
    up_rows, down_rows = N_EXPERTS * D // n_mix, N_EXPERTS * EXPERT_FF // n_mix
    up_slice = pl.BlockSpec((up_rows, EXPERT_FF), lambda s: (tile_in(s), 0))
    down_slice = pl.BlockSpec((down_rows, D), lambda s: (tile_in(s), 0))
    in_specs += [up_slice, up_slice, down_slice]
    x1, rt, cnt, wg, wu, wd = pl.pallas_call(
        functools.partial(_mixer_kernel, tiles_per_seq=NT),
        grid=(n_mix + 1,),
        in_specs=in_specs,
        out_specs=[x1_spec, per_tile(8, TQ), per_tile(8, LANES), up_slice, up_slice, down_slice],
        out_shape=[jax.ShapeDtypeStruct((B, S, XW), F32), jax.ShapeDtypeStruct((B * NT, 8, TQ), F32),
                   jax.ShapeDtypeStruct((B * NT, 8, LANES), F32),
                   jax.ShapeDtypeStruct((N_EXPERTS * D, EXPERT_FF), BF16),
                   jax.ShapeDtypeStruct((N_EXPERTS * D, EXPERT_FF), BF16),
                   jax.ShapeDtypeStruct((N_EXPERTS * EXPERT_FF, D), BF16)],
        scratch_shapes=[pltpu.VMEM((WINDOW + TQ, LANES), BF16)] * 4
        + [pltpu.VMEM((CONV_CH // LANES, CONV_HALO + TQ, LANES), F32),
           pltpu.VMEM((CONV_CH // LANES, TQ, LANES), F32), pltpu.VMEM((TQ, Q_WIDTH), BF16),
           pltpu.VMEM((TQ, XQ_WIDTH), BF16), pltpu.VMEM((TQ, D), F32),
           pltpu.VMEM((TQ, LANES), F32), pltpu.VMEM((TQ, LANES), F32)],
        compiler_params=pltpu.CompilerParams(dimension_semantics=("arbitrary",),
                                             vmem_limit_bytes=VMEM_LIMIT),
        name="mixer",
    )(x, cos_tab, sin_tab, mk, mv, row(g_norm1[l]), w_qkv, w_glu, w_xq, w_gt,
      row(jnp.tile(g_q[l], ATT_HEADS) * (HEAD_DIM ** -0.5)), row(jnp.tile(g_k[l], ATT_KV_HEADS)),
      sinks[l].astype(F32), w_o_attn[l].astype(BF16),
      wdw, row(b_conv_dw[l]), row(g_conv_ln[l]), row(b_conv_ln[l]), w_conv_out[l].astype(BF16),
      row(jnp.tile(g_xq[l], MEM_HEADS) * (MEM_HEAD_DIM ** -0.5)), w_o_mem[l].astype(BF16),
      w_out[l].astype(BF16), row(g_norm2[l]), w_rt, b_rt, tri,
      w_gate[l].reshape(N_EXPERTS * D, EXPERT_FF), w_up[l].reshape(N_EXPERTS * D, EXPERT_FF),
      w_down[l].reshape(N_EXPERTS * EXPERT_FF, D))

    n_tiles = T // TM
    plan = _routing_plan(rt[:, 0, :].astype(jnp.int32), rt[:, 1, :].astype(jnp.int32),
                         cnt[:, 0, :N_CLASSES].astype(jnp.int32), n_tiles)
    dest, tile_k, e_lo, e_hi, cls_k, flags = plan
    ni = tile_k.shape[0]
    wg = wg.reshape(N_EXPERTS, D, EXPERT_FF)
    wu = wu.reshape(N_EXPERTS, D, EXPERT_FF)
    wd = wd.reshape(N_EXPERTS, EXPERT_FF, D)
    lo_map = lambda k, dest, tile, elo, ehi, cls, flg: (elo[k], 0, 0)
    hi_map = lambda k, dest, tile, elo, ehi, cls, flg: (ehi[k], 0, 0)
    up_block = (1, D, EXPERT_FF)
    down_block = (1, EXPERT_FF, D)
    out = pl.pallas_call(
        _moe_kernel,
        grid_spec=pltpu.PrefetchScalarGridSpec(
            num_scalar_prefetch=6,
            grid=(ni,),
            in_specs=[pl.BlockSpec((1, D), lambda k, *_: (0, 0)),
                      pl.BlockSpec(up_block, lo_map), pl.BlockSpec(up_block, lo_map),
                      pl.BlockSpec(down_block, lo_map),
                      pl.BlockSpec(up_block, hi_map), pl.BlockSpec(up_block, hi_map),
                      pl.BlockSpec(down_block, hi_map),
                      pl.BlockSpec(memory_space=pl.ANY)],
            out_specs=pl.BlockSpec(memory_space=pl.ANY),
            scratch_shapes=[pltpu.SMEM((T,), jnp.int32),
                            pltpu.VMEM((GATHER_SLOTS, TM, XW), F32), pltpu.VMEM((2, TM, D), F32),
                            pltpu.VMEM((TM, D), BF16),
                            pltpu.VMEM((4, D, EXPERT_FF), BF16), pltpu.VMEM((2, EXPERT_FF, D), BF16),
                            pltpu.SemaphoreType.DMA((GATHER_SLOTS,)), pltpu.SemaphoreType.DMA((2,))]),
        out_shape=jax.ShapeDtypeStruct((T, D), F32),
        compiler_params=pltpu.CompilerParams(dimension_semantics=("arbitrary",),
                                             vmem_limit_bytes=VMEM_LIMIT),
        name="moe",
    )(dest, tile_k, e_lo, e_hi, cls_k, flags, row(g_norm2[l]), wg, wu, wd, wg, wu, wd,
      x1.reshape(T, XW))
    return out.reshape(B, S, D)
```

```python
import functools

import jax
import jax.numpy as jnp
from jax import lax
from jax.experimental import pallas as pl
from jax.experimental.pallas import tpu as pltpu

F32 = jnp.float32
BF16 = jnp.bfloat16

D_MODEL = 1024
ATT_HEADS = 8
ATT_KV_HEADS = 2
HEAD_DIM = 64
WINDOW = 128
ROPE_THETA = 10000.0
CONV_CH = 512
CONV_WIDTH = 31
MEM_HEADS = 4
MEM_HEAD_DIM = 128
N_BRANCHES = 3
N_GROUPS = 4
EXPERTS_PER_GROUP = 4
N_EXPERTS = N_GROUPS * EXPERTS_PER_GROUP
EXPERT_FF = 512
EPS = 1e-6
LN_EPS = 1e-5
NEG_INF = -1e30

Q_WIDTH = ATT_HEADS * HEAD_DIM
KV_WIDTH = ATT_KV_HEADS * HEAD_DIM
GLU_WIDTH = 2 * CONV_CH
XQ_WIDTH = MEM_HEADS * MEM_HEAD_DIM
GATE_WIDTH = N_BRANCHES * D_MODEL

LANES = 128
TQ = 256
CONV_HALO = 32
CONV_ROWS = 128
TM = 256
RT_ROWS = 32
N_PAIRS = EXPERTS_PER_GROUP * (EXPERTS_PER_GROUP - 1) // 2
N_CLASSES = N_GROUPS * N_PAIRS
CLS_ROWS = 32
XW = D_MODEL + LANES
VMEM_LIMIT = 56 * 1024 * 1024


def _dot(a, b):
    return jnp.dot(a, b, preferred_element_type=F32)


def _dot_nt(a, b):
    return lax.dot_general(a, b, (((1,), (1,)), ((), ())), preferred_element_type=F32)


def _split_bf16(t):
    hi = t.astype(BF16)
    lo = (t - hi.astype(F32)).astype(BF16)
    return hi, lo


def _block_sum(t, bd):
    return _dot(t.astype(BF16), bd)


def _rms(t):
    return t * lax.rsqrt(jnp.mean(t * t, axis=-1, keepdims=True) + EPS)


def _memkv_kernel(mem_ref, gmem_ref, wkv_ref, gxk_ref, bd_ref, mk_ref, mv_ref):
    mn = (_rms(mem_ref[0]) * gmem_ref[...]).astype(BF16)
    kv = _dot(mn, wkv_ref[...])
    mk = kv[:, :XQ_WIDTH]
    ss = _block_sum(mk * mk, bd_ref[...])
    mk = mk * lax.rsqrt(ss * (1.0 / MEM_HEAD_DIM) + EPS) * gxk_ref[...]
    mk_ref[0] = mk.astype(BF16)
    mv_ref[0] = kv[:, XQ_WIDTH:].astype(BF16)


def _mixer_kernel(x_ref, cos_ref, sin_ref, mk_ref, mv_ref, g1_ref, wqkv_ref, wglu_ref, wxq_ref,
                  wgt_ref, gq_ref, gk_ref, sinks_ref, woa_ref, wdw_ref,
                  bdw_ref, gln_ref, bln_ref, wco_ref, gxq_ref, wom_ref, wout_ref, g2_ref,
                  wrt_ref, brt_ref, tri_ref, ewg_ref, ewu_ref, ewd_ref,
                  x1_ref, rt_ref, cnt_ref, ewg_out, ewu_out, ewd_out,
                  k0_ref, k1_ref, v0_ref, v1_ref, ubuf_ref, ybuf_ref, o_ref, om_ref, x1prev_ref,
                  cosd_ref, sind_ref, *, tiles_per_seq):
    s = pl.program_id(0)
    n_tiles = pl.num_programs(0) - 1
    i = jnp.minimum(s, n_tiles - 1) % tiles_per_seq
    kbands = (k0_ref, k1_ref)
    vbands = (v0_ref, v1_ref)

    @pl.when(s == 0)
    def _():
        x1prev_ref[...] = jnp.zeros((TQ, D_MODEL), F32)

    @pl.when(i == 0)
    def _():
        for r in kbands + vbands:
            r[0:WINDOW, :] = jnp.zeros((WINDOW, LANES), BF16)
        ubuf_ref[:, 0:CONV_HALO, :] = jnp.zeros((CONV_CH // LANES, CONV_HALO, LANES), F32)

    x1_prev = x1prev_ref[...]
    x1_ref[0, :, 0:D_MODEL] = x1_prev

    ewg_out[...] = ewg_ref[...].astype(BF16)
    ewu_out[...] = ewu_ref[...].astype(BF16)
    ewd_out[...] = ewd_ref[...].astype(BF16)

    x = x_ref[0]
    h = (_rms(x) * g1_ref[...]).astype(BF16)

    n_slabs = CONV_CH // LANES
    half = TQ // 2
    group = ATT_HEADS // ATT_KV_HEADS

    glu = _dot(h, wglu_ref[...])
    route_logits = _router_logits(x1_prev, g2_ref, wrt_ref, brt_ref)
    u = glu[:, :CONV_CH] * jax.nn.sigmoid(glu[:, CONV_CH:])
    for c in range(n_slabs):
        ubuf_ref[c, CONV_HALO:CONV_HALO + TQ, :] = u[:, c * LANES:(c + 1) * LANES]
    base = CONV_HALO - (CONV_WIDTH - 1)
    acc = [[None] * n_slabs for _ in range(2)]

    def conv_slab(c):
        cols = slice(c * LANES, (c + 1) * LANES)
        for par in range(2):
            parts = []
            for q0 in range(0, half, CONV_ROWS):
                a = jnp.broadcast_to(bdw_ref[:, cols], (CONV_ROWS, LANES))
                for j in range(CONV_WIDTH):
                    a = a + wdw_ref[j:j + 1, cols] * ubuf_ref[
                        c, pl.ds(base + j + par + 2 * q0, CONV_ROWS, stride=2), :]
                parts.append(a)
            acc[par][c] = jnp.concatenate(parts, axis=0)
        ubuf_ref[c, 0:CONV_HALO, :] = ubuf_ref[c, TQ:TQ + CONV_HALO, :]

    def gate(b):
        return jax.nn.sigmoid(_dot(h, wgt_ref[:, b * D_MODEL:(b + 1) * D_MODEL]))

    def conv_finish():
        for par in range(2):
            row_sum = functools.reduce(jnp.add, [jnp.sum(a, axis=-1, keepdims=True) for a in acc[par]])
            mu = row_sum * (1.0 / CONV_CH)
            cen = [a - mu for a in acc[par]]
            sq_sum = functools.reduce(jnp.add, [jnp.sum(t * t, axis=-1, keepdims=True) for t in cen])
            rstd = lax.rsqrt(sq_sum * (1.0 / CONV_CH) + LN_EPS)
            for c in range(n_slabs):
                cols = slice(c * LANES, (c + 1) * LANES)
                yln = cen[c] * rstd * gln_ref[:, cols] + bln_ref[:, cols]
                ybuf_ref[c, pl.ds(par, half, stride=2), :] = yln * jax.nn.sigmoid(yln)
        uo = jnp.concatenate([ybuf_ref[c] for c in range(n_slabs)], axis=-1).astype(BF16)
        return _dot(uo, wco_ref[...])

    for c in range(n_slabs):
        conv_slab(c)
    qkv = _dot(h, wqkv_ref[...])
    gates = [gate(0)]
    q = qkv[:, :Q_WIDTH]
    k = qkv[:, Q_WIDTH:Q_WIDTH + KV_WIDTH]
    v = qkv[:, Q_WIDTH + KV_WIDTH:]
    gates.append(gate(1))

    lane = lax.broadcasted_iota(jnp.int32, (TQ, LANES), 1)
    first_half = (lane % HEAD_DIM) < (HEAD_DIM // 2)
    low_head = lane < HEAD_DIM
    n_freq = HEAD_DIM // 2
    per_row = LANES // n_freq
    quarter = lax.broadcasted_iota(jnp.int32, (TQ // per_row, LANES), 1) // n_freq
    sign = jnp.where(quarter % 2 == 0, -1.0, 1.0)
    for table_ref, out_ref, scale in ((cos_ref, cosd_ref, None), (sin_ref, sind_ref, sign)):
        dense = table_ref[0]
        rolled = [dense] + [pltpu.roll(dense, n_freq * kk, 1) for kk in range(1, per_row)]
        for j in range(per_row):
            spread = rolled[(-j) % per_row]
            for qq in range(1, per_row):
                spread = jnp.where(quarter == qq, rolled[(qq - j) % per_row], spread)
            out_ref[pl.ds(j, TQ // per_row, stride=per_row), :] = spread if scale is None else spread * scale
    cosd = cosd_ref[...]
    sind = sind_ref[...]

    def rope(t):
        rot = jnp.where(first_half, pltpu.roll(t, LANES - HEAD_DIM // 2, 1),
                        pltpu.roll(t, HEAD_DIM // 2, 1))
        return t * cosd + rot * sind

    def dup_halves(t):
        swapped = pltpu.roll(t, HEAD_DIM, 1)
        return jnp.where(low_head, t, swapped), jnp.where(low_head, swapped, t)

    def qk_norm(t, gain):
        sq = t * t
        lo = jnp.sum(jnp.where(low_head, sq, 0.0), axis=-1, keepdims=True)
        hi = jnp.sum(jnp.where(low_head, 0.0, sq), axis=-1, keepdims=True)
        inv = jnp.where(low_head, lax.rsqrt(lo * (1.0 / HEAD_DIM) + EPS), lax.rsqrt(hi * (1.0 / HEAD_DIM) + EPS))
        return t * inv * gain

    kd = dup_halves(rope(qk_norm(k, gk_ref[...])))
    vd = dup_halves(v)
    for kvh in range(ATT_KV_HEADS):
        kbands[kvh][WINDOW:WINDOW + TQ, :] = kd[kvh].astype(BF16)
        vbands[kvh][WINDOW:WINDOW + TQ, :] = vd[kvh].astype(BF16)

    q_heads = []
    for c in range(Q_WIDTH // LANES):
        cols = slice(c * LANES, (c + 1) * LANES)
        qc = rope(qk_norm(q[:, cols], gq_ref[:, cols]))
        q_heads.append(jnp.where(low_head, qc, 0.0).astype(BF16))
        q_heads.append(jnp.where(low_head, 0.0, qc).astype(BF16))

    qi = lax.broadcasted_iota(jnp.int32, (WINDOW, 2 * WINDOW), 0)
    kj = lax.broadcasted_iota(jnp.int32, (WINDOW, 2 * WINDOW), 1)
    band_mask = (kj > qi) & (kj <= qi + WINDOW)
    first_mask = band_mask & (kj >= jnp.where(i > 0, 0, WINDOW))
    low_head_w = lax.broadcasted_iota(jnp.int32, (WINDOW, LANES), 1) < HEAD_DIM

    def attention_scores(step):
        n, kvh = divmod(step, ATT_KV_HEADS)
        rows = slice(n * WINDOW, (n + 1) * WINDOW)
        mask = first_mask if n == 0 else band_mask
        heads = range(kvh * group, (kvh + 1) * group)
        q_st = jnp.concatenate([q_heads[hd][rows] for hd in heads], axis=0)
        kband = kbands[kvh][n * WINDOW:n * WINDOW + 2 * WINDOW, :]
        s_all = _dot_nt(q_st, kband)
        es, dens = [], []
        for j, hd in enumerate(heads):
            s = jnp.where(mask, s_all[j * WINDOW:(j + 1) * WINDOW], NEG_INF)
            sink = sinks_ref[hd]
            m = jnp.maximum(jnp.max(s, axis=-1, keepdims=True), sink)
            e = jnp.exp(s - m)
            dens.append(jnp.sum(e, axis=-1, keepdims=True) + jnp.exp(sink - m))
            es.append(e.astype(BF16))
        return jnp.concatenate(es, axis=0), dens

    def attention_output(step, probs):
        n, kvh = divmod(step, ATT_KV_HEADS)
        rows = slice(n * WINDOW, (n + 1) * WINDOW)
        e_all, dens = probs
        vband = vbands[kvh][n * WINDOW:n * WINDOW + 2 * WINDOW, :]
        o_all = _dot(e_all, vband)
        for cc in range(group // 2):
            oa = o_all[(2 * cc) * WINDOW:(2 * cc + 1) * WINDOW] / dens[2 * cc]
            ob = o_all[(2 * cc + 1) * WINDOW:(2 * cc + 2) * WINDOW] / dens[2 * cc + 1]
            col = kvh * (group // 2) + cc
            o_ref[rows, col * LANES:(col + 1) * LANES] = jnp.where(low_head_w, oa, ob).astype(BF16)

    def mem_scores(hd, xq):
        cols = slice(hd * MEM_HEAD_DIM, (hd + 1) * MEM_HEAD_DIM)
        s = _dot_nt(xq[:, cols], mk_ref[0, :, cols])
        e = jnp.exp(s - jnp.max(s, axis=-1, keepdims=True))
        return e.astype(BF16), jnp.sum(e, axis=-1, keepdims=True)

    def mem_output(hd, probs):
        cols = slice(hd * MEM_HEAD_DIM, (hd + 1) * MEM_HEAD_DIM)
        e, den = probs
        om_ref[:, cols] = (_dot(e, mv_ref[0, :, cols]) / den).astype(BF16)

    route = _router_select(route_logits)
    p0 = attention_scores(0)
    gates.append(gate(2))
    p1 = attention_scores(1)
    attention_output(0, p0)
    xq = _dot(h, wxq_ref[...])
    p2 = attention_scores(2)
    attention_output(1, p1)
    _router_emit(*route, tri_ref, x1_ref, rt_ref, cnt_ref)
    xq = jnp.concatenate(
        [_rms(xq[:, hd * MEM_HEAD_DIM:(hd + 1) * MEM_HEAD_DIM]) for hd in range(MEM_HEADS)], axis=-1)
    xq = (xq * gxq_ref[...]).astype(BF16)
    p3 = attention_scores(3)
    attention_output(2, p2)
    y_conv = conv_finish()
    m0 = mem_scores(0, xq)
    attention_output(3, p3)
    for r in kbands + vbands:
        r[0:WINDOW, :] = r[TQ:TQ + WINDOW, :]
    m1 = mem_scores(1, xq)
    y_attn = _dot(o_ref[...], woa_ref[...])
    mem_output(0, m0)
    m2 = mem_scores(2, xq)
    mem_output(1, m1)
    m3 = mem_scores(3, xq)
    mem_output(2, m2)
    merged = gates[0] * y_attn + gates[1] * y_conv
    mem_output(3, m3)
    merged = merged + gates[2] * _dot(om_ref[...], wom_ref[...])

    x1prev_ref[...] = x + _dot(merged.astype(BF16), wout_ref[...])


def _router_logits(x1, g2_ref, wrt_ref, brt_ref):
    h2 = _rms(x1) * g2_ref[...]
    h_hi, h_lo = _split_bf16(h2)
    w_hi, w_lo = _split_bf16(wrt_ref[...])
    return _dot_nt(w_hi, h_hi) + _dot_nt(w_hi, h_lo) + _dot_nt(w_lo, h_hi) + brt_ref[...]


def _router_select(lt):
    gl = [lt[g:g + 1] for g in range(N_GROUPS)]
    gmax = functools.reduce(jnp.maximum, gl)
    g_idx = jnp.full((1, TQ), N_GROUPS - 1, jnp.int32)
    for g in reversed(range(N_GROUPS - 1)):
        g_idx = jnp.where(gl[g] == gmax, g, g_idx)
    p_g = 1.0 / functools.reduce(jnp.add, [jnp.exp(r - gmax) for r in gl])
    el = []
    for kk in range(EXPERTS_PER_GROUP):
        row = lt[8 + kk:9 + kk]
        for g in range(1, N_GROUPS):
            off = 8 + g * EXPERTS_PER_GROUP + kk
            row = jnp.where(g_idx == g, lt[off:off + 1], row)
        el.append(row)

    def first_argmax(rows):
        top = functools.reduce(jnp.maximum, rows)
        idx = jnp.full((1, TQ), len(rows) - 1, jnp.int32)
        for kk in reversed(range(len(rows) - 1)):
            idx = jnp.where(rows[kk] == top, kk, idx)
        return top, idx

    v1, i1 = first_argmax(el)
    v2, i2 = first_argmax([jnp.where(i1 == kk, -jnp.inf, el[kk]) for kk in range(EXPERTS_PER_GROUP)])
    t = jnp.exp(v2 - v1)
    p1 = 1.0 / (1.0 + t)
    p2 = t * p1

    lo = jnp.minimum(i1, i2)
    hi = jnp.maximum(i1, i2)
    w_lo = jnp.where(i1 < i2, p_g * p1, p_g * p2)
    w_hi = jnp.where(i1 < i2, p_g * p2, p_g * p1)
    cls = g_idx * N_PAIRS + ((lo * (2 * EXPERTS_PER_GROUP - 1 - lo)) >> 1) + hi - lo - 1
    return cls, w_lo, w_hi


def _router_emit(cls, w_lo, w_hi, tri_ref, x1_ref, rt_ref, cnt_ref):
    cls_f = cls.astype(F32)
    onehot = jnp.broadcast_to(cls, (CLS_ROWS, TQ)) == lax.broadcasted_iota(jnp.int32, (CLS_ROWS, TQ), 0)
    onehot_bf = onehot.astype(F32).astype(BF16)
    prefix = _dot(onehot_bf, tri_ref[...])
    rank = jnp.sum(jnp.where(onehot, prefix, 0.0), axis=0, keepdims=True) - 1.0
    counts = _dot_nt(jnp.ones((8, TQ), BF16), onehot_bf)
    zero = jnp.zeros((1, TQ), F32)
    rt_ref[0] = jnp.concatenate([cls_f, rank] + [zero] * 6, axis=0)
    cnt_ref[0] = jnp.concatenate([counts, jnp.zeros((8, LANES - CLS_ROWS), F32)], axis=1)
    cols = jnp.concatenate([w_lo, w_hi, cls_f, jnp.zeros((LANES - 3, TQ), F32)], axis=0)
    x1_ref[0, :, D_MODEL:] = cols.T


FLAG_FIRST, FLAG_LAST, FLAG_VALID = 1, 2, 4
GATHER_SLOTS = 3
ROW_GATHER_PRIORITY = 1


def _moe_kernel(dest_ref, tile_ref, elo_ref, ehi_ref, cls_ref, flag_ref,
                g2_ref, wg_lo, wu_lo, wd_lo, wg_hi, wu_hi, wd_hi, x1_hbm,
                out_hbm,
                perm_ref, xbuf, obuf, h2buf, wbuf_up, wbuf_down, gsem, ssem):
    k = pl.program_id(0)
    nk = pl.num_programs(0)
    t = tile_ref[k]
    xslot = t % GATHER_SLOTS
    oslot = t % 2
    flags = flag_ref[k]
    n_tok = x1_hbm.shape[0]
    n_tiles = n_tok // TM

    def gather_copy(r, tile, s):
        tok = perm_ref[tile * TM + r]
        return pltpu.make_async_copy(x1_hbm.at[pl.ds(tok, 1), :], xbuf.at[s, pl.ds(r, 1), :], gsem.at[s])

    def scatter_copy(r, tile, s):
        tok = perm_ref[tile * TM + r]
        return pltpu.make_async_copy(obuf.at[s, pl.ds(r, 1), :], out_hbm.at[pl.ds(tok, 1), :], ssem.at[s])

    k_prev = jnp.maximum(k - 1, 0)
    for side, (e_ref, wg, wu, wd) in enumerate(((elo_ref, wg_lo, wu_lo, wd_lo), (ehi_ref, wg_hi, wu_hi, wd_hi))):
        @pl.when((k == 0) | (e_ref[k] != e_ref[k_prev]))
        def _(side=side, wg=wg, wu=wu, wd=wd):
            wbuf_up[2 * side] = wg[0]
            wbuf_up[2 * side + 1] = wu[0]
            wbuf_down[side] = wd[0]

    def for_rows(fn):
        def body(r, carry):
            fn(r)
            return carry
        lax.fori_loop(0, TM, body, 0, unroll=8)

    @pl.when(k == 0)
    def _():
        def invert(tok, carry):
            perm_ref[dest_ref[tok]] = tok
            return carry
        lax.fori_loop(0, n_tok, invert, 0, unroll=8)
        for tile in range(GATHER_SLOTS - 1):
            for_rows(lambda r, tile=tile: gather_copy(r, tile, tile).start())

    is_first = (flags & FLAG_FIRST) != 0
    is_valid = (flags & FLAG_VALID) != 0
    ahead = GATHER_SLOTS - 1
    prefetch_tile = jnp.minimum(t + ahead, n_tiles - 1)
    prefetch_slot = (t + ahead) % GATHER_SLOTS

    @pl.when(is_first)
    def _():
        for_rows(lambda r: gather_copy(r, t, xslot).wait())

        @pl.when(t >= 2)
        def _():
            for_rows(lambda r: scatter_copy(r, t - 2, oslot).wait())

        xr = xbuf[xslot, :, 0:D_MODEL]
        h2buf[...] = (_rms(xr) * g2_ref[...]).astype(BF16)
        obuf[oslot] = xr

    def experts():
        w = xbuf[xslot, :, D_MODEL:XW]
        mine = w[:, 2:3] == cls_ref[k].astype(F32)
        h2 = h2buf[...]
        y = None
        for side in range(2):
            c = jnp.where(mine, w[:, side:side + 1], 0.0)
            hid = jax.nn.silu(_dot(h2, wbuf_up[2 * side])) * _dot(h2, wbuf_up[2 * side + 1]) * c
            part = _dot(hid.astype(BF16), wbuf_down[side])
            y = part if y is None else y + part
        obuf[oslot] += y

    @pl.when(is_first & (t > 0))
    def _():
        for r in range(TM):
            gather_copy(r, prefetch_tile, prefetch_slot).start(priority=ROW_GATHER_PRIORITY)
            scatter_copy(r, t - 1, 1 - oslot).start()
        experts()

    @pl.when(is_first & (t == 0))
    def _():
        for r in range(TM):
            gather_copy(r, prefetch_tile, prefetch_slot).start(priority=ROW_GATHER_PRIORITY)
        experts()

    @pl.when(is_valid & jnp.logical_not(is_first))
    def _():
        experts()

    @pl.when(k == nk - 1)
    def _():
        last = n_tiles - 1
        for_rows(lambda r: scatter_copy(r, last, last % 2).start())
        for tile in (last - 1, last):
            for_rows(lambda r, tile=tile: scatter_copy(r, tile, tile % 2).wait())
        for extra in range(1, GATHER_SLOTS):
            for_rows(lambda r, extra=extra: gather_copy(r, last, (last + extra) % GATHER_SLOTS).wait())


_CLASS_PAIRS = [(a, b) for a in range(EXPERTS_PER_GROUP) for b in range(a + 1, EXPERTS_PER_GROUP)]


def _routing_plan(cls, rank, tile_counts, n_tiles):
    counts = jnp.sum(tile_counts, axis=0)
    off = jnp.concatenate([jnp.zeros((1,), jnp.int32), jnp.cumsum(counts).astype(jnp.int32)])
    tile_base = jnp.cumsum(tile_counts, axis=0) - tile_counts + off[None, :-1]
    onehot = cls[:, :, None] == jnp.arange(N_CLASSES, dtype=jnp.int32)[None, None, :]
    dest = (jnp.sum(jnp.where(onehot, tile_base[:, None, :], 0), axis=-1) + rank).reshape(-1)
    first_tile = off[:-1] // TM
    last_tile = (off[1:] - 1) // TM
    n_items = jnp.where(counts > 0, last_tile - first_tile + 1, 0)
    istart = jnp.concatenate([jnp.zeros((1,), jnp.int32), jnp.cumsum(n_items).astype(jnp.int32)])
    total = istart[-1]
    ni = n_tiles + N_CLASSES - 1
    k = jnp.arange(ni, dtype=jnp.int32)
    valid = k < total
    kc = jnp.minimum(k, total - 1)
    c_of_k = jnp.minimum(jnp.sum(istart[None, 1:] <= kc[:, None], axis=1).astype(jnp.int32), N_CLASSES - 1)
    tile_k = first_tile[c_of_k] + kc - istart[c_of_k]
    prev_tile = jnp.concatenate([jnp.full((1,), -1, jnp.int32), tile_k[:-1]])
    next_tile = jnp.concatenate([tile_k[1:], jnp.full((1,), -1, jnp.int32)])
    first = valid & (tile_k != prev_tile)
    last = valid & ((tile_k != next_tile) | (k == total - 1))
    flags = (first * FLAG_FIRST + last * FLAG_LAST + valid * FLAG_VALID).astype(jnp.int32)
    pair_lo = jnp.array([p[0] for p in _CLASS_PAIRS], jnp.int32)
    pair_hi = jnp.array([p[1] for p in _CLASS_PAIRS], jnp.int32)
    base = (c_of_k // N_PAIRS) * EXPERTS_PER_GROUP
    e_lo = base + pair_lo[c_of_k % N_PAIRS]
    e_hi = base + pair_hi[c_of_k % N_PAIRS]
    cls_k = jnp.where(valid, c_of_k, -1)
    return dest.astype(jnp.int32), tile_k, e_lo, e_hi, cls_k, flags


def _const_spec(shape):
    nd = len(shape)
    return pl.BlockSpec(shape, lambda *_: (0,) * nd, pipeline_mode=pl.Buffered(1))


def _block_diag_ones(n, blk):
    r = jnp.arange(n) // blk
    return (r[:, None] == r[None, :]).astype(BF16)


def kernel(x, mem, positions, g_norm1, w_in, g_q, g_k, sinks, w_o_attn, w_conv_dw, b_conv_dw, g_conv_ln, b_conv_ln, w_conv_out, g_mem, w_kv_mem, g_xq, g_xk, w_o_mem, w_out, g_norm2, w_group, b_group, w_router, b_router, w_gate, w_up, w_down):
    B, S, D = x.shape
    M = mem.shape[1]
    assert D == D_MODEL and S % TQ == 0 and (B * S) % TM == 0 and w_in.shape[0] == 1
    assert (TQ // WINDOW) * ATT_KV_HEADS == MEM_HEADS and CONV_CH // LANES == N_BRANCHES + 1
    NT = S // TQ
    T = B * S
    l = 0

    bd128 = _block_diag_ones(XQ_WIDTH, MEM_HEAD_DIM)
    row = lambda v: v.reshape(1, -1).astype(F32)

    mk, mv = pl.pallas_call(
        _memkv_kernel,
        grid=(B,),
        in_specs=[pl.BlockSpec((1, M, D), lambda b: (b, 0, 0)),
                  _const_spec((1, D)), _const_spec((D, 2 * XQ_WIDTH)),
                  _const_spec((1, XQ_WIDTH)), _const_spec((XQ_WIDTH, XQ_WIDTH))],
        out_specs=[pl.BlockSpec((1, M, XQ_WIDTH), lambda b: (b, 0, 0))] * 2,
        out_shape=[jax.ShapeDtypeStruct((B, M, XQ_WIDTH), BF16)] * 2,
        compiler_params=pltpu.CompilerParams(dimension_semantics=("arbitrary",)),
        name="memkv",
    )(mem, row(g_mem[l]), w_kv_mem[l].astype(BF16), row(jnp.tile(g_xk[l], MEM_HEADS)), bd128)

    inv_freq = 1.0 / (ROPE_THETA ** (jnp.arange(0, HEAD_DIM, 2, dtype=F32) / HEAD_DIM))
    per_row = LANES // (HEAD_DIM // 2)
    ang = positions.astype(F32).reshape(B, S // per_row, per_row, 1) * inv_freq
    ang = ang.reshape(B, S // per_row, LANES)
    cos_tab, sin_tab = jnp.cos(ang), jnp.sin(ang)

    w = w_in[l]
    c0 = Q_WIDTH + 2 * KV_WIDTH
    c1 = c0 + GLU_WIDTH
    c2 = c1 + XQ_WIDTH
    w_qkv = w[:, :c0].astype(BF16)
    w_glu = w[:, c0:c1].astype(BF16)
    w_xq = w[:, c1:c2].astype(BF16)
    w_gt = w[:, c2:].astype(BF16)
    wdw = jnp.zeros((CONV_HALO, CONV_CH), F32).at[:CONV_WIDTH].set(w_conv_dw[l])
    w_rt = jnp.zeros((RT_ROWS, D), F32).at[0:N_GROUPS].set(w_group[l].T).at[8:8 + N_EXPERTS].set(w_router[l].T)
    b_rt = jnp.zeros((RT_ROWS, 1), F32).at[0:N_GROUPS, 0].set(b_group[l]).at[8:8 + N_EXPERTS, 0].set(b_router[l])

    n_mix = B * NT
    tile_in = lambda s: jnp.minimum(s, n_mix - 1)
    tile_out = lambda s: jnp.maximum(s - 1, 0)
    tile3 = lambda last: pl.BlockSpec((1, TQ, last), lambda s: (tile_in(s) // NT, tile_in(s) % NT, 0))
    per_batch = pl.BlockSpec((1, M, XQ_WIDTH), lambda s: (tile_in(s) // NT, 0, 0))
    rope_spec = pl.BlockSpec((1, TQ // per_row, LANES), lambda s: (tile_in(s) // NT, tile_in(s) % NT, 0))
    in_specs = [
        tile3(D), rope_spec, rope_spec, per_batch, per_batch,
        _const_spec((1, D)),
        _const_spec((D, c0)), _const_spec((D, GLU_WIDTH)), _const_spec((D, XQ_WIDTH)), _const_spec((D, GATE_WIDTH)),
        _const_spec((1, Q_WIDTH)), _const_spec((1, LANES)),
        pl.BlockSpec(memory_space=pltpu.SMEM),
        _const_spec((Q_WIDTH, D)),
        _const_spec((CONV_HALO, CONV_CH)), _const_spec((1, CONV_CH)), _const_spec((1, CONV_CH)), _const_spec((1, CONV_CH)),
        _const_spec((CONV_CH, D)),
        _const_spec((1, XQ_WIDTH)),
        _const_spec((XQ_WIDTH, D)),
        _const_spec((D, D)),
        _const_spec((1, D)),
        _const_spec((RT_ROWS, D)), _const_spec((RT_ROWS, 1)), _const_spec((TQ, TQ)),
    ]
    tri = (jnp.arange(TQ)[:, None] <= jnp.arange(TQ)[None, :]).astype(BF16)
    per_tile = lambda rows, last: pl.BlockSpec((1, rows, last), lambda s: (tile_out(s), 0, 0))
    x1_spec = pl.BlockSpec((1, TQ, XW), lambda s: (tile_out(s) // NT, tile_out(s) % NT, 0))
    assert (N_EXPERTS * D) % n_mix == 0 and (N_EXPERTS * EXPERT_FF) % n_mix == 0
    up_rows, down_rows = N_EXPERTS * D // n_mix, N_EXPERTS * EXPERT_FF // n_mix
    up_slice = pl.BlockSpec((up_rows, EXPERT_FF), lambda s: (tile_in(s), 0))
    down_slice = pl.BlockSpec((down_rows, D), lambda s: (tile_in(s), 0))
    in_specs += [up_slice, up_slice, down_slice]
    x1, rt, cnt, wg, wu, wd = pl.pallas_call(
        functools.partial(_mixer_kernel, tiles_per_seq=NT),
        grid=(n_mix + 1,),
        in_specs=in_specs,
        out_specs=[x1_spec, per_tile(8, TQ), per_tile(8, LANES), up_slice, up_slice, down_slice],
        out_shape=[jax.ShapeDtypeStruct((B, S, XW), F32), jax.ShapeDtypeStruct((B * NT, 8, TQ), F32),
                   jax.ShapeDtypeStruct((B * NT, 8, LANES), F32),
                   jax.ShapeDtypeStruct((N_EXPERTS * D, EXPERT_FF), BF16),
                   jax.ShapeDtypeStruct((N_EXPERTS * D, EXPERT_FF), BF16),
                   jax.ShapeDtypeStruct((N_EXPERTS * EXPERT_FF, D), BF16)],
        scratch_shapes=[pltpu.VMEM((WINDOW + TQ, LANES), BF16)] * 4
        + [pltpu.VMEM((CONV_CH // LANES, CONV_HALO + TQ, LANES), F32),
           pltpu.VMEM((CONV_CH // LANES, TQ, LANES), F32), pltpu.VMEM((TQ, Q_WIDTH), BF16),
           pltpu.VMEM((TQ, XQ_WIDTH), BF16), pltpu.VMEM((TQ, D), F32),
           pltpu.VMEM((TQ, LANES), F32), pltpu.VMEM((TQ, LANES), F32)],
        compiler_params=pltpu.CompilerParams(dimension_semantics=("arbitrary",),
                                             vmem_limit_bytes=VMEM_LIMIT),
        name="mixer",
    )(x, cos_tab, sin_tab, mk, mv, row(g_norm1[l]), w_qkv, w_glu, w_xq, w_gt,
      row(jnp.tile(g_q[l], ATT_HEADS) * (HEAD_DIM ** -0.5)), row(jnp.tile(g_k[l], ATT_KV_HEADS)),
      sinks[l].astype(F32), w_o_attn[l].astype(BF16),
      wdw, row(b_conv_dw[l]), row(g_conv_ln[l]), row(b_conv_ln[l]), w_conv_out[l].astype(BF16),
      row(jnp.tile(g_xq[l], MEM_HEADS) * (MEM_HEAD_DIM ** -0.5)), w_o_mem[l].astype(BF16),
      w_out[l].astype(BF16), row(g_norm2[l]), w_rt, b_rt, tri,
      w_gate[l].reshape(N_EXPERTS * D, EXPERT_FF), w_up[l].reshape(N_EXPERTS * D, EXPERT_FF),
      w_down[l].reshape(N_EXPERTS * EXPERT_FF, D))

    n_tiles = T // TM
    plan = _routing_plan(rt[:, 0, :].astype(jnp.int32), rt[:, 1, :].astype(jnp.int32),
                         cnt[:, 0, :N_CLASSES].astype(jnp.int32), n_tiles)
    dest, tile_k, e_lo, e_hi, cls_k, flags = plan
    ni = tile_k.shape[0]
    wg = wg.reshape(N_EXPERTS, D, EXPERT_FF)
    wu = wu.reshape(N_EXPERTS, D, EXPERT_FF)
    wd = wd.reshape(N_EXPERTS, EXPERT_FF, D)
    lo_map = lambda k, dest, tile, elo, ehi, cls, flg: (elo[k], 0, 0)
    hi_map = lambda k, dest, tile, elo, ehi, cls, flg: (ehi[k], 0, 0)
    up_block = (1, D, EXPERT_FF)
    down_block = (1, EXPERT_FF, D)
    out = pl.pallas_call(
        _moe_kernel,
        grid_spec=pltpu.PrefetchScalarGridSpec(
            num_scalar_prefetch=6,
            grid=(ni,),
            in_specs=[pl.BlockSpec((1, D), lambda k, *_: (0, 0)),
                      pl.BlockSpec(up_block, lo_map), pl.BlockSpec(up_block, lo_map),
                      pl.BlockSpec(down_block, lo_map),
                      pl.BlockSpec(up_block, hi_map), pl.BlockSpec(up_block, hi_map),
                      pl.BlockSpec(down_block, hi_map),
                      pl.BlockSpec(memory_space=pl.ANY)],
            out_specs=pl.BlockSpec(memory_space=pl.ANY),
            scratch_shapes=[pltpu.SMEM((T,), jnp.int32),
                            pltpu.VMEM((GATHER_SLOTS, TM, XW), F32), pltpu.VMEM((2, TM, D), F32),
                            pltpu.VMEM((TM, D), BF16),
                            pltpu.VMEM((4, D, EXPERT_FF), BF16), pltpu.VMEM((2, EXPERT_FF, D), BF16),
                            pltpu.SemaphoreType.DMA((GATHER_SLOTS,)), pltpu.SemaphoreType.DMA((2,))]),
        out_shape=jax.ShapeDtypeStruct((T, D), F32),
        compiler_params=pltpu.CompilerParams(dimension_semantics=("arbitrary",),
                                             vmem_limit_bytes=VMEM_LIMIT),
        name="moe",
    )(dest, tile_k, e_lo, e_hi, cls_k, flags, row(g_norm2[l]), wg, wu, wd, wg, wu, wd,
      x1.reshape(T, XW))
    return out.reshape(B, S, D)
```

```python
import functools

import jax
import jax.numpy as jnp
from jax import lax
from jax.experimental import pallas as pl
from jax.experimental.pallas import tpu as pltpu

F32 = jnp.float32
BF16 = jnp.bfloat16

D_MODEL = 1024
ATT_HEADS = 8
ATT_KV_HEADS = 2
HEAD_DIM = 64
WINDOW = 128
ROPE_THETA = 10000.0
CONV_CH = 512
CONV_WIDTH = 31
MEM_HEADS = 4
MEM_HEAD_DIM = 128
N_BRANCHES = 3
N_GROUPS = 4
EXPERTS_PER_GROUP = 4
N_EXPERTS = N_GROUPS * EXPERTS_PER_GROUP
EXPERT_FF = 512
EPS = 1e-6
LN_EPS = 1e-5
NEG_INF = -1e30

Q_WIDTH = ATT_HEADS * HEAD_DIM
KV_WIDTH = ATT_KV_HEADS * HEAD_DIM
GLU_WIDTH = 2 * CONV_CH
XQ_WIDTH = MEM_HEADS * MEM_HEAD_DIM
GATE_WIDTH = N_BRANCHES * D_MODEL
COL_GLU = Q_WIDTH + 2 * KV_WIDTH
COL_XQ = COL_GLU + GLU_WIDTH
COL_GATE = COL_XQ + XQ_WIDTH
IN_WIDTH = COL_GATE + GATE_WIDTH

LANES = 128
TQ = 256
CONV_HALO = 32
CONV_ROWS = 128
TM = 256
RT_ROWS = 32
N_PAIRS = EXPERTS_PER_GROUP * (EXPERTS_PER_GROUP - 1) // 2
N_CLASSES = N_GROUPS * N_PAIRS
CLS_ROWS = 32
XW = D_MODEL + LANES
VMEM_LIMIT = 56 * 1024 * 1024


def _dot(a, b):
    return jnp.dot(a, b, preferred_element_type=F32)


def _dot_nt(a, b):
    return lax.dot_general(a, b, (((1,), (1,)), ((), ())), preferred_element_type=F32)


def _split_bf16(t):
    hi = t.astype(BF16)
    lo = (t - hi.astype(F32)).astype(BF16)
    return hi, lo


def _block_sum(t, bd):
    return _dot(t.astype(BF16), bd)


def _rms(t):
    return t * lax.rsqrt(jnp.mean(t * t, axis=-1, keepdims=True) + EPS)


def _memkv_kernel(mem_ref, gmem_ref, wkv_ref, gxk_ref, bd_ref, mk_ref, mv_ref):
    mn = (_rms(mem_ref[0]) * gmem_ref[...]).astype(BF16)
    kv = _dot(mn, wkv_ref[...])
    mk = kv[:, :XQ_WIDTH]
    ss = _block_sum(mk * mk, bd_ref[...])
    mk = mk * lax.rsqrt(ss * (1.0 / MEM_HEAD_DIM) + EPS) * gxk_ref[...]
    mk_ref[0] = mk.astype(BF16)
    mv_ref[0] = kv[:, XQ_WIDTH:].astype(BF16)


def _mixer_kernel(x_ref, cos_ref, sin_ref, mk_ref, mv_ref, g1_ref, win_ref,
                  gq_ref, gk_ref, sinks_ref, woa_ref, wdw_ref,
                  bdw_ref, gln_ref, bln_ref, wco_ref, gxq_ref, wom_ref, wout_ref, g2_ref,
                  wrt_ref, brt_ref, tri_ref, ewg_ref, ewu_ref, ewd_ref,
                  x1_ref, rt_ref, cnt_ref, ewg_out, ewu_out, ewd_out,
                  k0_ref, k1_ref, v0_ref, v1_ref, ubuf_ref, ybuf_ref, o_ref, om_ref, x1prev_ref,
                  cosd_ref, sind_ref, *, tiles_per_seq):
    s = pl.program_id(0)
    n_tiles = pl.num_programs(0) - 1
    i = jnp.minimum(s, n_tiles - 1) % tiles_per_seq
    kbands = (k0_ref, k1_ref)
    vbands = (v0_ref, v1_ref)

    @pl.when(s == 0)
    def _():
        x1prev_ref[...] = jnp.zeros((TQ, D_MODEL), F32)

    @pl.when(i == 0)
    def _():
        for r in kbands + vbands:
            r[0:WINDOW, :] = jnp.zeros((WINDOW, LANES), BF16)
        ubuf_ref[:, 0:CONV_HALO, :] = jnp.zeros((CONV_CH // LANES, CONV_HALO, LANES), F32)

    x1_prev = x1prev_ref[...]
    x1_ref[0, :, 0:D_MODEL] = x1_prev

    ewg_out[...] = ewg_ref[...].astype(BF16)
    ewu_out[...] = ewu_ref[...].astype(BF16)
    ewd_out[...] = ewd_ref[...].astype(BF16)

    x = x_ref[0]
    h = (_rms(x) * g1_ref[...]).astype(BF16)

    n_slabs = CONV_CH // LANES
    half = TQ // 2
    group = ATT_HEADS // ATT_KV_HEADS

    glu = _dot(h, win_ref[:, COL_GLU:COL_XQ])
    route_logits = _router_logits(x1_prev, g2_ref, wrt_ref, brt_ref)
    u = glu[:, :CONV_CH] * jax.nn.sigmoid(glu[:, CONV_CH:])
    for c in range(n_slabs):
        ubuf_ref[c, CONV_HALO:CONV_HALO + TQ, :] = u[:, c * LANES:(c + 1) * LANES]
    base = CONV_HALO - (CONV_WIDTH - 1)
    acc = [[None] * n_slabs for _ in range(2)]

    def conv_slab(c):
        cols = slice(c * LANES, (c + 1) * LANES)
        for par in range(2):
            parts = []
            for q0 in range(0, half, CONV_ROWS):
                a = jnp.broadcast_to(bdw_ref[:, cols], (CONV_ROWS, LANES))
                for j in range(CONV_WIDTH):
                    a = a + wdw_ref[j:j + 1, cols] * ubuf_ref[
                        c, pl.ds(base + j + par + 2 * q0, CONV_ROWS, stride=2), :]
                parts.append(a)
            acc[par][c] = jnp.concatenate(parts, axis=0)
        ubuf_ref[c, 0:CONV_HALO, :] = ubuf_ref[c, TQ:TQ + CONV_HALO, :]

    def gate(b):
        return jax.nn.sigmoid(_dot(h, win_ref[:, COL_GATE + b * D_MODEL:COL_GATE + (b + 1) * D_MODEL]))

    def conv_finish():
        for par in range(2):
            row_sum = functools.reduce(jnp.add, [jnp.sum(a, axis=-1, keepdims=True) for a in acc[par]])
            mu = row_sum * (1.0 / CONV_CH)
            cen = [a - mu for a in acc[par]]
            sq_sum = functools.reduce(jnp.add, [jnp.sum(t * t, axis=-1, keepdims=True) for t in cen])
            rstd = lax.rsqrt(sq_sum * (1.0 / CONV_CH) + LN_EPS)
            for c in range(n_slabs):
                cols = slice(c * LANES, (c + 1) * LANES)
                yln = cen[c] * rstd * gln_ref[:, cols] + bln_ref[:, cols]
                ybuf_ref[c, pl.ds(par, half, stride=2), :] = yln * jax.nn.sigmoid(yln)
        uo = jnp.concatenate([ybuf_ref[c] for c in range(n_slabs)], axis=-1).astype(BF16)
        return _dot(uo, wco_ref[...])

    for c in range(n_slabs):
        conv_slab(c)
    qkv = _dot(h, win_ref[:, 0:COL_GLU])
    gates = [gate(0)]
    q = qkv[:, :Q_WIDTH]
    k = qkv[:, Q_WIDTH:Q_WIDTH + KV_WIDTH]
    v = qkv[:, Q_WIDTH + KV_WIDTH:]
    gates.append(gate(1))

    lane = lax.broadcasted_iota(jnp.int32, (TQ, LANES), 1)
    first_half = (lane % HEAD_DIM) < (HEAD_DIM // 2)
    low_head = lane < HEAD_DIM
    n_freq = HEAD_DIM // 2
    per_row = LANES // n_freq
    quarter = lax.broadcasted_iota(jnp.int32, (TQ // per_row, LANES), 1) // n_freq
    sign = jnp.where(quarter % 2 == 0, -1.0, 1.0)
    for table_ref, out_ref, scale in ((cos_ref, cosd_ref, None), (sin_ref, sind_ref, sign)):
        dense = table_ref[0]
        rolled = [dense] + [pltpu.roll(dense, n_freq * kk, 1) for kk in range(1, per_row)]
        for j in range(per_row):
            spread = rolled[(-j) % per_row]
            for qq in range(1, per_row):
                spread = jnp.where(quarter == qq, rolled[(qq - j) % per_row], spread)
            out_ref[pl.ds(j, TQ // per_row, stride=per_row), :] = spread if scale is None else spread * scale
    cosd = cosd_ref[...]
    sind = sind_ref[...]

    def rope(t):
        rot = jnp.where(first_half, pltpu.roll(t, LANES - HEAD_DIM // 2, 1),
                        pltpu.roll(t, HEAD_DIM // 2, 1))
        return t * cosd + rot * sind

    def dup_halves(t):
        swapped = pltpu.roll(t, HEAD_DIM, 1)
        return jnp.where(low_head, t, swapped), jnp.where(low_head, swapped, t)

    def qk_norm(t, gain):
        sq = t * t
        lo = jnp.sum(jnp.where(low_head, sq, 0.0), axis=-1, keepdims=True)
        hi = jnp.sum(jnp.where(low_head, 0.0, sq), axis=-1, keepdims=True)
        inv = jnp.where(low_head, lax.rsqrt(lo * (1.0 / HEAD_DIM) + EPS), lax.rsqrt(hi * (1.0 / HEAD_DIM) + EPS))
        return t * inv * gain

    kd = dup_halves(rope(qk_norm(k, gk_ref[...])))
    vd = dup_halves(v)
    for kvh in range(ATT_KV_HEADS):
        kbands[kvh][WINDOW:WINDOW + TQ, :] = kd[kvh].astype(BF16)
        vbands[kvh][WINDOW:WINDOW + TQ, :] = vd[kvh].astype(BF16)

    q_heads = []
    for c in range(Q_WIDTH // LANES):
        cols = slice(c * LANES, (c + 1) * LANES)
        qc = rope(qk_norm(q[:, cols], gq_ref[:, cols]))
        q_heads.append(jnp.where(low_head, qc, 0.0).astype(BF16))
        q_heads.append(jnp.where(low_head, 0.0, qc).astype(BF16))

    qi = lax.broadcasted_iota(jnp.int32, (WINDOW, 2 * WINDOW), 0)
    kj = lax.broadcasted_iota(jnp.int32, (WINDOW, 2 * WINDOW), 1)
    band_mask = (kj > qi) & (kj <= qi + WINDOW)
    first_mask = band_mask & (kj >= jnp.where(i > 0, 0, WINDOW))
    low_head_w = lax.broadcasted_iota(jnp.int32, (WINDOW, LANES), 1) < HEAD_DIM

    def attention_scores(step):
        n, kvh = divmod(step, ATT_KV_HEADS)
        rows = slice(n * WINDOW, (n + 1) * WINDOW)
        mask = first_mask if n == 0 else band_mask
        heads = range(kvh * group, (kvh + 1) * group)
        q_st = jnp.concatenate([q_heads[hd][rows] for hd in heads], axis=0)
        kband = kbands[kvh][n * WINDOW:n * WINDOW + 2 * WINDOW, :]
        s_all = _dot_nt(q_st, kband)
        es, dens = [], []
        for j, hd in enumerate(heads):
            s = jnp.where(mask, s_all[j * WINDOW:(j + 1) * WINDOW], NEG_INF)
            sink = sinks_ref[hd]
            m = jnp.maximum(jnp.max(s, axis=-1, keepdims=True), sink)
            e = jnp.exp(s - m)
            dens.append(jnp.sum(e, axis=-1, keepdims=True) + jnp.exp(sink - m))
            es.append(e.astype(BF16))
        return jnp.concatenate(es, axis=0), dens

    def attention_output(step, probs):
        n, kvh = divmod(step, ATT_KV_HEADS)
        rows = slice(n * WINDOW, (n + 1) * WINDOW)
        e_all, dens = probs
        vband = vbands[kvh][n * WINDOW:n * WINDOW + 2 * WINDOW, :]
        o_all = _dot(e_all, vband)
        for cc in range(group // 2):
            oa = o_all[(2 * cc) * WINDOW:(2 * cc + 1) * WINDOW] / dens[2 * cc]
            ob = o_all[(2 * cc + 1) * WINDOW:(2 * cc + 2) * WINDOW] / dens[2 * cc + 1]
            col = kvh * (group // 2) + cc
            o_ref[rows, col * LANES:(col + 1) * LANES] = jnp.where(low_head_w, oa, ob).astype(BF16)

    def mem_scores(hd, xq):
        cols = slice(hd * MEM_HEAD_DIM, (hd + 1) * MEM_HEAD_DIM)
        s = _dot_nt(xq[:, cols], mk_ref[0, :, cols])
        e = jnp.exp(s - jnp.max(s, axis=-1, keepdims=True))
        return e.astype(BF16), jnp.sum(e, axis=-1, keepdims=True)

    def mem_output(hd, probs):
        cols = slice(hd * MEM_HEAD_DIM, (hd + 1) * MEM_HEAD_DIM)
        e, den = probs
        om_ref[:, cols] = (_dot(e, mv_ref[0, :, cols]) / den).astype(BF16)

    route = _router_select(route_logits)
    p0 = attention_scores(0)
    gates.append(gate(2))
    p1 = attention_scores(1)
    attention_output(0, p0)
    xq = _dot(h, win_ref[:, COL_XQ:COL_GATE])
    p2 = attention_scores(2)
    attention_output(1, p1)
    _router_emit(*route, tri_ref, x1_ref, rt_ref, cnt_ref)
    xq = jnp.concatenate(
        [_rms(xq[:, hd * MEM_HEAD_DIM:(hd + 1) * MEM_HEAD_DIM]) for hd in range(MEM_HEADS)], axis=-1)
    xq = (xq * gxq_ref[...]).astype(BF16)
    p3 = attention_scores(3)
    attention_output(2, p2)
    y_conv = conv_finish()
    m0 = mem_scores(0, xq)
    attention_output(3, p3)
    for r in kbands + vbands:
        r[0:WINDOW, :] = r[TQ:TQ + WINDOW, :]
    m1 = mem_scores(1, xq)
    y_attn = _dot(o_ref[...], woa_ref[...])
    mem_output(0, m0)
    m2 = mem_scores(2, xq)
    mem_output(1, m1)
    m3 = mem_scores(3, xq)
    mem_output(2, m2)
    merged = gates[0] * y_attn + gates[1] * y_conv
    mem_output(3, m3)
    merged = merged + gates[2] * _dot(om_ref[...], wom_ref[...])

    x1prev_ref[...] = x + _dot(merged.astype(BF16), wout_ref[...])


def _router_logits(x1, g2_ref, wrt_ref, brt_ref):
    h2 = _rms(x1) * g2_ref[...]
    h_hi, h_lo = _split_bf16(h2)
    w_hi, w_lo = _split_bf16(wrt_ref[...])
    return _dot_nt(w_hi, h_hi) + _dot_nt(w_hi, h_lo) + _dot_nt(w_lo, h_hi) + brt_ref[...]


def _router_select(lt):
    gl = [lt[g:g + 1] for g in range(N_GROUPS)]
    gmax = functools.reduce(jnp.maximum, gl)
    g_idx = jnp.full((1, TQ), N_GROUPS - 1, jnp.int32)
    for g in reversed(range(N_GROUPS - 1)):
        g_idx = jnp.where(gl[g] == gmax, g, g_idx)
    p_g = 1.0 / functools.reduce(jnp.add, [jnp.exp(r - gmax) for r in gl])
    el = []
    for kk in range(EXPERTS_PER_GROUP):
        row = lt[8 + kk:9 + kk]
        for g in range(1, N_GROUPS):
            off = 8 + g * EXPERTS_PER_GROUP + kk
            row = jnp.where(g_idx == g, lt[off:off + 1], row)
        el.append(row)

    def first_argmax(rows):
        top = functools.reduce(jnp.maximum, rows)
        idx = jnp.full((1, TQ), len(rows) - 1, jnp.int32)
        for kk in reversed(range(len(rows) - 1)):
            idx = jnp.where(rows[kk] == top, kk, idx)
        return top, idx

    v1, i1 = first_argmax(el)
    v2, i2 = first_argmax([jnp.where(i1 == kk, -jnp.inf, el[kk]) for kk in range(EXPERTS_PER_GROUP)])
    t = jnp.exp(v2 - v1)
    p1 = 1.0 / (1.0 + t)
    p2 = t * p1

    lo = jnp.minimum(i1, i2)
    hi = jnp.maximum(i1, i2)
    w_lo = jnp.where(i1 < i2, p_g * p1, p_g * p2)
    w_hi = jnp.where(i1 < i2, p_g * p2, p_g * p1)
    cls = g_idx * N_PAIRS + ((lo * (2 * EXPERTS_PER_GROUP - 1 - lo)) >> 1) + hi - lo - 1
    return cls, w_lo, w_hi


def _router_emit(cls, w_lo, w_hi, tri_ref, x1_ref, rt_ref, cnt_ref):
    cls_f = cls.astype(F32)
    onehot = jnp.broadcast_to(cls, (CLS_ROWS, TQ)) == lax.broadcasted_iota(jnp.int32, (CLS_ROWS, TQ), 0)
    onehot_bf = onehot.astype(F32).astype(BF16)
    prefix = _dot(onehot_bf, tri_ref[...])
    rank = jnp.sum(jnp.where(onehot, prefix, 0.0), axis=0, keepdims=True) - 1.0
    counts = _dot_nt(jnp.ones((8, TQ), BF16), onehot_bf)
    zero = jnp.zeros((1, TQ), F32)
    rt_ref[0] = jnp.concatenate([cls_f, rank] + [zero] * 6, axis=0)
    cnt_ref[0] = jnp.concatenate([counts, jnp.zeros((8, LANES - CLS_ROWS), F32)], axis=1)
    cols = jnp.concatenate([w_lo, w_hi, cls_f, jnp.zeros((LANES - 3, TQ), F32)], axis=0)
    x1_ref[0, :, D_MODEL:] = cols.T


FLAG_FIRST, FLAG_LAST, FLAG_VALID = 1, 2, 4
GATHER_SLOTS = 3


def _moe_kernel(dest_ref, tile_ref, elo_ref, ehi_ref, cls_ref, flag_ref,
                g2_ref, wg_lo, wu_lo, wd_lo, wg_hi, wu_hi, wd_hi, x1_hbm,
                out_hbm,
                perm_ref, xbuf, obuf, h2buf, wbuf_up, wbuf_down, gsem, ssem):
    k = pl.program_id(0)
    nk = pl.num_programs(0)
    t = tile_ref[k]
    xslot = t % GATHER_SLOTS
    oslot = t % 2
    flags = flag_ref[k]
    n_tok = x1_hbm.shape[0]
    n_tiles = n_tok // TM

    def gather_copy(r, tile, s):
        tok = perm_ref[tile * TM + r]
        return pltpu.make_async_copy(x1_hbm.at[pl.ds(tok, 1), :], xbuf.at[s, pl.ds(r, 1), :], gsem.at[s])

    def scatter_copy(r, tile, s):
        tok = perm_ref[tile * TM + r]
        return pltpu.make_async_copy(obuf.at[s, pl.ds(r, 1), :], out_hbm.at[pl.ds(tok, 1), :], ssem.at[s])

    k_prev = jnp.maximum(k - 1, 0)
    for side, (e_ref, wg, wu, wd) in enumerate(((elo_ref, wg_lo, wu_lo, wd_lo), (ehi_ref, wg_hi, wu_hi, wd_hi))):
        @pl.when((k == 0) | (e_ref[k] != e_ref[k_prev]))
        def _(side=side, wg=wg, wu=wu, wd=wd):
            wbuf_up[2 * side] = wg[0]
            wbuf_up[2 * side + 1] = wu[0]
            wbuf_down[side] = wd[0]

    def for_rows(fn):
        def body(r, carry):
            fn(r)
            return carry
        lax.fori_loop(0, TM, body, 0, unroll=8)

    @pl.when(k == 0)
    def _():
        def invert(tok, carry):
            perm_ref[dest_ref[tok]] = tok
            return carry
        lax.fori_loop(0, n_tok, invert, 0, unroll=8)
        for tile in range(GATHER_SLOTS - 1):
            for_rows(lambda r, tile=tile: gather_copy(r, tile, tile).start())

    is_first = (flags & FLAG_FIRST) != 0
    is_valid = (flags & FLAG_VALID) != 0
    ahead = GATHER_SLOTS - 1
    prefetch_tile = jnp.minimum(t + ahead, n_tiles - 1)
    prefetch_slot = (t + ahead) % GATHER_SLOTS

    @pl.when(is_first)
    def _():
        for_rows(lambda r: gather_copy(r, t, xslot).wait())

        @pl.when(t >= 2)
        def _():
            for_rows(lambda r: scatter_copy(r, t - 2, oslot).wait())

        xr = xbuf[xslot, :, 0:D_MODEL]
        h2buf[...] = (_rms(xr) * g2_ref[...]).astype(BF16)
        obuf[oslot] = xr

    def experts():
        w = xbuf[xslot, :, D_MODEL:XW]
        mine = w[:, 2:3] == cls_ref[k].astype(F32)
        h2 = h2buf[...]
        y = None
        for side in range(2):
            c = jnp.where(mine, w[:, side:side + 1], 0.0)
            hid = jax.nn.silu(_dot(h2, wbuf_up[2 * side])) * _dot(h2, wbuf_up[2 * side + 1]) * c
            part = _dot(hid.astype(BF16), wbuf_down[side])
            y = part if y is None else y + part
        obuf[oslot] += y

    @pl.when(is_first & (t > 0))
    def _():
        for r in range(TM):
            gather_copy(r, prefetch_tile, prefetch_slot).start()
            scatter_copy(r, t - 1, 1 - oslot).start()
        experts()

    @pl.when(is_first & (t == 0))
    def _():
        for r in range(TM):
            gather_copy(r, prefetch_tile, prefetch_slot).start()
        experts()

    @pl.when(is_valid & jnp.logical_not(is_first))
    def _():
        experts()

    @pl.when(k == nk - 1)
    def _():
        last = n_tiles - 1
        for_rows(lambda r: scatter_copy(r, last, last % 2).start())
        for tile in (last - 1, last):
            for_rows(lambda r, tile=tile: scatter_copy(r, tile, tile % 2).wait())
        for extra in range(1, GATHER_SLOTS):
            for_rows(lambda r, extra=extra: gather_copy(r, last, (last + extra) % GATHER_SLOTS).wait())


_CLASS_PAIRS = [(a, b) for a in range(EXPERTS_PER_GROUP) for b in range(a + 1, EXPERTS_PER_GROUP)]


def _routing_plan(cls, rank, tile_counts, n_tiles):
    counts = jnp.sum(tile_counts, axis=0)
    off = jnp.concatenate([jnp.zeros((1,), jnp.int32), jnp.cumsum(counts).astype(jnp.int32)])
    tile_base = jnp.cumsum(tile_counts, axis=0) - tile_counts + off[None, :-1]
    onehot = cls[:, :, None] == jnp.arange(N_CLASSES, dtype=jnp.int32)[None, None, :]
    dest = (jnp.sum(jnp.where(onehot, tile_base[:, None, :], 0), axis=-1) + rank).reshape(-1)
    first_tile = off[:-1] // TM
    last_tile = (off[1:] - 1) // TM
    n_items = jnp.where(counts > 0, last_tile - first_tile + 1, 0)
    istart = jnp.concatenate([jnp.zeros((1,), jnp.int32), jnp.cumsum(n_items).astype(jnp.int32)])
    total = istart[-1]
    ni = n_tiles + N_CLASSES - 1
    k = jnp.arange(ni, dtype=jnp.int32)
    valid = k < total
    kc = jnp.minimum(k, total - 1)
    c_of_k = jnp.minimum(jnp.sum(istart[None, 1:] <= kc[:, None], axis=1).astype(jnp.int32), N_CLASSES - 1)
    tile_k = first_tile[c_of_k] + kc - istart[c_of_k]
    prev_tile = jnp.concatenate([jnp.full((1,), -1, jnp.int32), tile_k[:-1]])
    next_tile = jnp.concatenate([tile_k[1:], jnp.full((1,), -1, jnp.int32)])
    first = valid & (tile_k != prev_tile)
    last = valid & ((tile_k != next_tile) | (k == total - 1))
    flags = (first * FLAG_FIRST + last * FLAG_LAST + valid * FLAG_VALID).astype(jnp.int32)
    pair_lo = jnp.array([p[0] for p in _CLASS_PAIRS], jnp.int32)
    pair_hi = jnp.array([p[1] for p in _CLASS_PAIRS], jnp.int32)
    base = (c_of_k // N_PAIRS) * EXPERTS_PER_GROUP
    e_lo = base + pair_lo[c_of_k % N_PAIRS]
    e_hi = base + pair_hi[c_of_k % N_PAIRS]
    cls_k = jnp.where(valid, c_of_k, -1)
    return dest.astype(jnp.int32), tile_k, e_lo, e_hi, cls_k, flags


def _const_spec(shape):
    nd = len(shape)
    return pl.BlockSpec(shape, lambda *_: (0,) * nd, pipeline_mode=pl.Buffered(1))


def _block_diag_ones(n, blk):
    r = jnp.arange(n) // blk
    return (r[:, None] == r[None, :]).astype(BF16)


def kernel(x, mem, positions, g_norm1, w_in, g_q, g_k, sinks, w_o_attn, w_conv_dw, b_conv_dw, g_conv_ln, b_conv_ln, w_conv_out, g_mem, w_kv_mem, g_xq, g_xk, w_o_mem, w_out, g_norm2, w_group, b_group, w_router, b_router, w_gate, w_up, w_down):
    B, S, D = x.shape
    M = mem.shape[1]
    assert D == D_MODEL and S % TQ == 0 and (B * S) % TM == 0 and w_in.shape[0] == 1
    assert (TQ // WINDOW) * ATT_KV_HEADS == MEM_HEADS and CONV_CH // LANES == N_BRANCHES + 1
    NT = S // TQ
    T = B * S
    l = 0

    bd128 = _block_diag_ones(XQ_WIDTH, MEM_HEAD_DIM)
    row = lambda v: v.reshape(1, -1).astype(F32)

    mk, mv = pl.pallas_call(
        _memkv_kernel,
        grid=(B,),
        in_specs=[pl.BlockSpec((1, M, D), lambda b: (b, 0, 0)),
                  _const_spec((1, D)), _const_spec((D, 2 * XQ_WIDTH)),
                  _const_spec((1, XQ_WIDTH)), _const_spec((XQ_WIDTH, XQ_WIDTH))],
        out_specs=[pl.BlockSpec((1, M, XQ_WIDTH), lambda b: (b, 0, 0))] * 2,
        out_shape=[jax.ShapeDtypeStruct((B, M, XQ_WIDTH), BF16)] * 2,
        compiler_params=pltpu.CompilerParams(dimension_semantics=("arbitrary",)),
        name="memkv",
    )(mem, row(g_mem[l]), w_kv_mem[l].astype(BF16), row(jnp.tile(g_xk[l], MEM_HEADS)), bd128)

    inv_freq = 1.0 / (ROPE_THETA ** (jnp.arange(0, HEAD_DIM, 2, dtype=F32) / HEAD_DIM))
    per_row = LANES // (HEAD_DIM // 2)
    ang = positions.astype(F32).reshape(B, S // per_row, per_row, 1) * inv_freq
    ang = ang.reshape(B, S // per_row, LANES)
    cos_tab, sin_tab = jnp.cos(ang), jnp.sin(ang)

    assert w_in.shape[-1] == IN_WIDTH
    wdw = jnp.zeros((CONV_HALO, CONV_CH), F32).at[:CONV_WIDTH].set(w_conv_dw[l])
    w_rt = jnp.zeros((RT_ROWS, D), F32).at[0:N_GROUPS].set(w_group[l].T).at[8:8 + N_EXPERTS].set(w_router[l].T)
    b_rt = jnp.zeros((RT_ROWS, 1), F32).at[0:N_GROUPS, 0].set(b_group[l]).at[8:8 + N_EXPERTS, 0].set(b_router[l])

    n_mix = B * NT
    tile_in = lambda s: jnp.minimum(s, n_mix - 1)
    tile_out = lambda s: jnp.maximum(s - 1, 0)
    tile3 = lambda last: pl.BlockSpec((1, TQ, last), lambda s: (tile_in(s) // NT, tile_in(s) % NT, 0))
    per_batch = pl.BlockSpec((1, M, XQ_WIDTH), lambda s: (tile_in(s) // NT, 0, 0))
    rope_spec = pl.BlockSpec((1, TQ // per_row, LANES), lambda s: (tile_in(s) // NT, tile_in(s) % NT, 0))
    in_specs = [
        tile3(D), rope_spec, rope_spec, per_batch, per_batch,
        _const_spec((1, D)),
        _const_spec((D, IN_WIDTH)),
        _const_spec((1, Q_WIDTH)), _const_spec((1, LANES)),
        pl.BlockSpec(memory_space=pltpu.SMEM),
        _const_spec((Q_WIDTH, D)),
        _const_spec((CONV_HALO, CONV_CH)), _const_spec((1, CONV_CH)), _const_spec((1, CONV_CH)), _const_spec((1, CONV_CH)),
        _const_spec((CONV_CH, D)),
        _const_spec((1, XQ_WIDTH)),
        _const_spec((XQ_WIDTH, D)),
        _const_spec((D, D)),
        _const_spec((1, D)),
        _const_spec((RT_ROWS, D)), _const_spec((RT_ROWS, 1)), _const_spec((TQ, TQ)),
    ]
    tri = (jnp.arange(TQ)[:, None] <= jnp.arange(TQ)[None, :]).astype(BF16)
    per_tile = lambda rows, last: pl.BlockSpec((1, rows, last), lambda s: (tile_out(s), 0, 0))
    x1_spec = pl.BlockSpec((1, TQ, XW), lambda s: (tile_out(s) // NT, tile_out(s) % NT, 0))
    assert (N_EXPERTS * D) % n_mix == 0 and (N_EXPERTS * EXPERT_FF) % n_mix == 0
    up_rows, down_rows = N_EXPERTS * D // n_mix, N_EXPERTS * EXPERT_FF // n_mix
    up_slice = pl.BlockSpec((up_rows, EXPERT_FF), lambda s: (tile_in(s), 0))
    down_slice = pl.BlockSpec((down_rows, D), lambda s: (tile_in(s), 0))
    in_specs += [up_slice, up_slice, down_slice]
    x1, rt, cnt, wg, wu, wd = pl.pallas_call(
        functools.partial(_mixer_kernel, tiles_per_seq=NT),
        grid=(n_mix + 1,),
        in_specs=in_specs,
        out_specs=[x1_spec, per_tile(8, TQ), per_tile(8, LANES), up_slice, up_slice, down_slice],
        out_shape=[jax.ShapeDtypeStruct((B, S, XW), F32), jax.ShapeDtypeStruct((B * NT, 8, TQ), F32),
                   jax.ShapeDtypeStruct((B * NT, 8, LANES), F32),
                   jax.ShapeDtypeStruct((N_EXPERTS * D, EXPERT_FF), BF16),
                   jax.ShapeDtypeStruct((N_EXPERTS * D, EXPERT_FF), BF16),
                   jax.ShapeDtypeStruct((N_EXPERTS * EXPERT_FF, D), BF16)],
        scratch_shapes=[pltpu.VMEM((WINDOW + TQ, LANES), BF16)] * 4
        + [pltpu.VMEM((CONV_CH // LANES, CONV_HALO + TQ, LANES), F32),
           pltpu.VMEM((CONV_CH // LANES, TQ, LANES), F32), pltpu.VMEM((TQ, Q_WIDTH), BF16),
           pltpu.VMEM((TQ, XQ_WIDTH), BF16), pltpu.VMEM((TQ, D), F32),
           pltpu.VMEM((TQ, LANES), F32), pltpu.VMEM((TQ, LANES), F32)],
        compiler_params=pltpu.CompilerParams(dimension_semantics=("arbitrary",),
                                             vmem_limit_bytes=VMEM_LIMIT),
        name="mixer",
    )(x, cos_tab, sin_tab, mk, mv, row(g_norm1[l]), w_in[l].astype(BF16),
      row(jnp.tile(g_q[l], ATT_HEADS) * (HEAD_DIM ** -0.5)), row(jnp.tile(g_k[l], ATT_KV_HEADS)),
      sinks[l].astype(F32), w_o_attn[l].astype(BF16),
      wdw, row(b_conv_dw[l]), row(g_conv_ln[l]), row(b_conv_ln[l]), w_conv_out[l].astype(BF16),
      row(jnp.tile(g_xq[l], MEM_HEADS) * (MEM_HEAD_DIM ** -0.5)), w_o_mem[l].astype(BF16),
      w_out[l].astype(BF16), row(g_norm2[l]), w_rt, b_rt, tri,
      w_gate[l].reshape(N_EXPERTS * D, EXPERT_FF), w_up[l].reshape(N_EXPERTS * D, EXPERT_FF),
      w_down[l].reshape(N_EXPERTS * EXPERT_FF, D))

    n_tiles = T // TM
    plan = _routing_plan(rt[:, 0, :].astype(jnp.int32), rt[:, 1, :].astype(jnp.int32),
                         cnt[:, 0, :N_CLASSES].astype(jnp.int32), n_tiles)
    dest, tile_k, e_lo, e_hi, cls_k, flags = plan
    ni = tile_k.shape[0]
    wg = wg.reshape(N_EXPERTS, D, EXPERT_FF)
    wu = wu.reshape(N_EXPERTS, D, EXPERT_FF)
    wd = wd.reshape(N_EXPERTS, EXPERT_FF, D)
    lo_map = lambda k, dest, tile, elo, ehi, cls, flg: (elo[k], 0, 0)
    hi_map = lambda k, dest, tile, elo, ehi, cls, flg: (ehi[k], 0, 0)
    up_block = (1, D, EXPERT_FF)
    down_block = (1, EXPERT_FF, D)
    out = pl.pallas_call(
        _moe_kernel,
        grid_spec=pltpu.PrefetchScalarGridSpec(
            num_scalar_prefetch=6,
            grid=(ni,),
            in_specs=[pl.BlockSpec((1, D), lambda k, *_: (0, 0)),
                      pl.BlockSpec(up_block, lo_map), pl.BlockSpec(up_block, lo_map),
                      pl.BlockSpec(down_block, lo_map),
                      pl.BlockSpec(up_block, hi_map), pl.BlockSpec(up_block, hi_map),
                      pl.BlockSpec(down_block, hi_map),
                      pl.BlockSpec(memory_space=pl.ANY)],
            out_specs=pl.BlockSpec(memory_space=pl.ANY),
            scratch_shapes=[pltpu.SMEM((T,), jnp.int32),
                            pltpu.VMEM((GATHER_SLOTS, TM, XW), F32), pltpu.VMEM((2, TM, D), F32),
                            pltpu.VMEM((TM, D), BF16),
                            pltpu.VMEM((4, D, EXPERT_FF), BF16), pltpu.VMEM((2, EXPERT_FF, D), BF16),
                            pltpu.SemaphoreType.DMA((GATHER_SLOTS,)), pltpu.SemaphoreType.DMA((2,))]),
        out_shape=jax.ShapeDtypeStruct((T, D), F32),
        compiler_params=pltpu.CompilerParams(dimension_semantics=("arbitrary",),
                                             vmem_limit_bytes=VMEM_LIMIT),
        name="moe",
    )(dest, tile_k, e_lo, e_hi, cls_k, flags, row(g_norm2[l]), wg, wu, wd, wg, wu, wd,
      x1.reshape(T, XW))
    return out.reshape(B, S, D)
```

```python
import functools

import jax
import jax.numpy as jnp
from jax import lax
from jax.experimental import pallas as pl
from jax.experimental.pallas import tpu as pltpu

F32 = jnp.float32
BF16 = jnp.bfloat16

D_MODEL = 1024
ATT_HEADS = 8
ATT_KV_HEADS = 2
HEAD_DIM = 64
WINDOW = 128
ROPE_THETA = 10000.0
CONV_CH = 512
CONV_WIDTH = 31
MEM_HEADS = 4
MEM_HEAD_DIM = 128
N_BRANCHES = 3
N_GROUPS = 4
EXPERTS_PER_GROUP = 4
N_EXPERTS = N_GROUPS * EXPERTS_PER_GROUP
EXPERT_FF = 512
EPS = 1e-6
LN_EPS = 1e-5
NEG_INF = -1e30

Q_WIDTH = ATT_HEADS * HEAD_DIM
KV_WIDTH = ATT_KV_HEADS * HEAD_DIM
GLU_WIDTH = 2 * CONV_CH
XQ_WIDTH = MEM_HEADS * MEM_HEAD_DIM
GATE_WIDTH = N_BRANCHES * D_MODEL
COL_GLU = Q_WIDTH + 2 * KV_WIDTH
COL_XQ = COL_GLU + GLU_WIDTH
COL_GATE = COL_XQ + XQ_WIDTH
IN_WIDTH = COL_GATE + GATE_WIDTH

LANES = 128
TQ = 256
CONV_HALO = 32
CONV_ROWS = 128
TM = 256
RT_ROWS = 32
N_PAIRS = EXPERTS_PER_GROUP * (EXPERTS_PER_GROUP - 1) // 2
N_CLASSES = N_GROUPS * N_PAIRS
CLS_ROWS = 32
XW = D_MODEL + LANES
VMEM_LIMIT = 56 * 1024 * 1024


def _dot(a, b):
    return jnp.dot(a, b, preferred_element_type=F32)


def _dot_nt(a, b):
    return lax.dot_general(a, b, (((1,), (1,)), ((), ())), preferred_element_type=F32)


def _split_bf16(t):
    hi = t.astype(BF16)
    lo = (t - hi.astype(F32)).astype(BF16)
    return hi, lo


def _block_sum(t, bd):
    return _dot(t.astype(BF16), bd)


def _rms(t):
    return t * lax.rsqrt(jnp.mean(t * t, axis=-1, keepdims=True) + EPS)


def _memkv_kernel(mem_ref, gmem_ref, wkv_ref, gxk_ref, bd_ref, mk_ref, mv_ref):
    mn = (_rms(mem_ref[0]) * gmem_ref[...]).astype(BF16)
    kv = _dot(mn, wkv_ref[...])
    mk = kv[:, :XQ_WIDTH]
    ss = _block_sum(mk * mk, bd_ref[...])
    mk = mk * lax.rsqrt(ss * (1.0 / MEM_HEAD_DIM) + EPS) * gxk_ref[...]
    mk_ref[0] = mk.astype(BF16)
    mv_ref[0] = kv[:, XQ_WIDTH:].astype(BF16)


def _mixer_kernel(x_ref, cos_ref, sin_ref, mk_ref, mv_ref, g1_ref, win_ref,
                  gq_ref, gk_ref, sinks_ref, woa_ref, wdw_ref,
                  bdw_ref, gln_ref, bln_ref, wco_ref, gxq_ref, wom_ref, wout_ref, g2_ref,
                  wrt_ref, brt_ref, tri_ref, ewg_ref, ewu_ref, ewd_ref,
                  x1_ref, rt_ref, cnt_ref, ewg_out, ewu_out, ewd_out,
                  k0_ref, k1_ref, v0_ref, v1_ref, ubuf_ref, ybuf_ref, o_ref, om_ref, x1prev_ref,
                  cosd_ref, sind_ref, *, tiles_per_seq):
    s = pl.program_id(0)
    n_tiles = pl.num_programs(0) - 1
    i = jnp.minimum(s, n_tiles - 1) % tiles_per_seq
    kbands = (k0_ref, k1_ref)
    vbands = (v0_ref, v1_ref)

    @pl.when(s == 0)
    def _():
        x1prev_ref[...] = jnp.zeros((TQ, D_MODEL), F32)

    @pl.when(i == 0)
    def _():
        for r in kbands + vbands:
            r[0:WINDOW, :] = jnp.zeros((WINDOW, LANES), BF16)
        ubuf_ref[:, 0:CONV_HALO, :] = jnp.zeros((CONV_CH // LANES, CONV_HALO, LANES), F32)

    x1_prev = x1prev_ref[...]
    x1_ref[0, :, 0:D_MODEL] = x1_prev

    ewg_out[...] = ewg_ref[...].astype(BF16)
    ewu_out[...] = ewu_ref[...].astype(BF16)
    ewd_out[...] = ewd_ref[...].astype(BF16)

    x = x_ref[0]
    h = (_rms(x) * g1_ref[...]).astype(BF16)

    n_slabs = CONV_CH // LANES
    half = TQ // 2
    group = ATT_HEADS // ATT_KV_HEADS

    glu = _dot(h, win_ref[:, COL_GLU:COL_XQ])
    route_logits = _router_logits(x1_prev, g2_ref, wrt_ref, brt_ref)
    u = glu[:, :CONV_CH] * jax.nn.sigmoid(glu[:, CONV_CH:])
    for c in range(n_slabs):
        ubuf_ref[c, CONV_HALO:CONV_HALO + TQ, :] = u[:, c * LANES:(c + 1) * LANES]
    base = CONV_HALO - (CONV_WIDTH - 1)
    acc = [[None] * n_slabs for _ in range(2)]

    def conv_slab(c):
        cols = slice(c * LANES, (c + 1) * LANES)
        for par in range(2):
            parts = []
            for q0 in range(0, half, CONV_ROWS):
                a = jnp.broadcast_to(bdw_ref[:, cols], (CONV_ROWS, LANES))
                for j in range(CONV_WIDTH):
                    a = a + wdw_ref[j:j + 1, cols] * ubuf_ref[
                        c, pl.ds(base + j + par + 2 * q0, CONV_ROWS, stride=2), :]
                parts.append(a)
            acc[par][c] = jnp.concatenate(parts, axis=0)
        ubuf_ref[c, 0:CONV_HALO, :] = ubuf_ref[c, TQ:TQ + CONV_HALO, :]

    def gate(b):
        return jax.nn.sigmoid(_dot(h, win_ref[:, COL_GATE + b * D_MODEL:COL_GATE + (b + 1) * D_MODEL]))

    def conv_finish():
        for par in range(2):
            row_sum = functools.reduce(jnp.add, [jnp.sum(a, axis=-1, keepdims=True) for a in acc[par]])
            mu = row_sum * (1.0 / CONV_CH)
            cen = [a - mu for a in acc[par]]
            sq_sum = functools.reduce(jnp.add, [jnp.sum(t * t, axis=-1, keepdims=True) for t in cen])
            rstd = lax.rsqrt(sq_sum * (1.0 / CONV_CH) + LN_EPS)
            for c in range(n_slabs):
                cols = slice(c * LANES, (c + 1) * LANES)
                yln = cen[c] * rstd * gln_ref[:, cols] + bln_ref[:, cols]
                ybuf_ref[c, pl.ds(par, half, stride=2), :] = yln * jax.nn.sigmoid(yln)
        uo = jnp.concatenate([ybuf_ref[c] for c in range(n_slabs)], axis=-1).astype(BF16)
        return _dot(uo, wco_ref[...])

    for c in range(n_slabs):
        conv_slab(c)
    qkv = _dot(h, win_ref[:, 0:COL_GLU])
    gates = [gate(0)]
    q = qkv[:, :Q_WIDTH]
    k = qkv[:, Q_WIDTH:Q_WIDTH + KV_WIDTH]
    v = qkv[:, Q_WIDTH + KV_WIDTH:]
    gates.append(gate(1))

    lane = lax.broadcasted_iota(jnp.int32, (TQ, LANES), 1)
    first_half = (lane % HEAD_DIM) < (HEAD_DIM // 2)
    low_head = lane < HEAD_DIM
    n_freq = HEAD_DIM // 2
    per_row = LANES // n_freq
    quarter = lax.broadcasted_iota(jnp.int32, (TQ // per_row, LANES), 1) // n_freq
    sign = jnp.where(quarter % 2 == 0, -1.0, 1.0)
    for table_ref, out_ref, scale in ((cos_ref, cosd_ref, None), (sin_ref, sind_ref, sign)):
        dense = table_ref[0]
        rolled = [dense] + [pltpu.roll(dense, n_freq * kk, 1) for kk in range(1, per_row)]
        for j in range(per_row):
            spread = rolled[(-j) % per_row]
            for qq in range(1, per_row):
                spread = jnp.where(quarter == qq, rolled[(qq - j) % per_row], spread)
            out_ref[pl.ds(j, TQ // per_row, stride=per_row), :] = spread if scale is None else spread * scale
    cosd = cosd_ref[...]
    sind = sind_ref[...]

    def rope(t):
        rot = jnp.where(first_half, pltpu.roll(t, LANES - HEAD_DIM // 2, 1),
                        pltpu.roll(t, HEAD_DIM // 2, 1))
        return t * cosd + rot * sind

    def dup_halves(t):
        swapped = pltpu.roll(t, HEAD_DIM, 1)
        return jnp.where(low_head, t, swapped), jnp.where(low_head, swapped, t)

    def qk_norm(t, gain):
        sq = t * t
        lo = jnp.sum(jnp.where(low_head, sq, 0.0), axis=-1, keepdims=True)
        hi = jnp.sum(jnp.where(low_head, 0.0, sq), axis=-1, keepdims=True)
        inv = jnp.where(low_head, lax.rsqrt(lo * (1.0 / HEAD_DIM) + EPS), lax.rsqrt(hi * (1.0 / HEAD_DIM) + EPS))
        return t * inv * gain

    kd = dup_halves(rope(qk_norm(k, gk_ref[...])))
    vd = dup_halves(v)
    for kvh in range(ATT_KV_HEADS):
        kbands[kvh][WINDOW:WINDOW + TQ, :] = kd[kvh].astype(BF16)
        vbands[kvh][WINDOW:WINDOW + TQ, :] = vd[kvh].astype(BF16)

    q_heads = []
    for c in range(Q_WIDTH // LANES):
        cols = slice(c * LANES, (c + 1) * LANES)
        qc = rope(qk_norm(q[:, cols], gq_ref[:, cols]))
        q_heads.append(jnp.where(low_head, qc, 0.0).astype(BF16))
        q_heads.append(jnp.where(low_head, 0.0, qc).astype(BF16))

    qi = lax.broadcasted_iota(jnp.int32, (WINDOW, 2 * WINDOW), 0)
    kj = lax.broadcasted_iota(jnp.int32, (WINDOW, 2 * WINDOW), 1)
    band_mask = (kj > qi) & (kj <= qi + WINDOW)
    first_mask = band_mask & (kj >= jnp.where(i > 0, 0, WINDOW))
    low_head_w = lax.broadcasted_iota(jnp.int32, (WINDOW, LANES), 1) < HEAD_DIM

    def attention_scores(step):
        n, kvh = divmod(step, ATT_KV_HEADS)
        rows = slice(n * WINDOW, (n + 1) * WINDOW)
        mask = first_mask if n == 0 else band_mask
        heads = range(kvh * group, (kvh + 1) * group)
        q_st = jnp.concatenate([q_heads[hd][rows] for hd in heads], axis=0)
        kband = kbands[kvh][n * WINDOW:n * WINDOW + 2 * WINDOW, :]
        s_all = _dot_nt(q_st, kband)
        es, dens = [], []
        for j, hd in enumerate(heads):
            s = jnp.where(mask, s_all[j * WINDOW:(j + 1) * WINDOW], NEG_INF)
            sink = sinks_ref[hd]
            m = jnp.maximum(jnp.max(s, axis=-1, keepdims=True), sink)
            e = jnp.exp(s - m)
            dens.append(jnp.sum(e, axis=-1, keepdims=True) + jnp.exp(sink - m))
            es.append(e.astype(BF16))
        return jnp.concatenate(es, axis=0), dens

    def attention_output(step, probs):
        n, kvh = divmod(step, ATT_KV_HEADS)
        rows = slice(n * WINDOW, (n + 1) * WINDOW)
        e_all, dens = probs
        vband = vbands[kvh][n * WINDOW:n * WINDOW + 2 * WINDOW, :]
        o_all = _dot(e_all, vband)
        for cc in range(group // 2):
            oa = o_all[(2 * cc) * WINDOW:(2 * cc + 1) * WINDOW] / dens[2 * cc]
            ob = o_all[(2 * cc + 1) * WINDOW:(2 * cc + 2) * WINDOW] / dens[2 * cc + 1]
            col = kvh * (group // 2) + cc
            o_ref[rows, col * LANES:(col + 1) * LANES] = jnp.where(low_head_w, oa, ob).astype(BF16)

    def mem_scores(hd, xq):
        cols = slice(hd * MEM_HEAD_DIM, (hd + 1) * MEM_HEAD_DIM)
        s = _dot_nt(xq[:, cols], mk_ref[0, :, cols])
        e = jnp.exp(s - jnp.max(s, axis=-1, keepdims=True))
        return e.astype(BF16), jnp.sum(e, axis=-1, keepdims=True)

    def mem_output(hd, probs):
        cols = slice(hd * MEM_HEAD_DIM, (hd + 1) * MEM_HEAD_DIM)
        e, den = probs
        om_ref[:, cols] = (_dot(e, mv_ref[0, :, cols]) / den).astype(BF16)

    route = _router_select(route_logits)
    p0 = attention_scores(0)
    gates.append(gate(2))
    p1 = attention_scores(1)
    attention_output(0, p0)
    xq = _dot(h, win_ref[:, COL_XQ:COL_GATE])
    p2 = attention_scores(2)
    attention_output(1, p1)
    _router_emit(*route, tri_ref, x1_ref, rt_ref, cnt_ref)
    xq = jnp.concatenate(
        [_rms(xq[:, hd * MEM_HEAD_DIM:(hd + 1) * MEM_HEAD_DIM]) for hd in range(MEM_HEADS)], axis=-1)
    xq = (xq * gxq_ref[...]).astype(BF16)
    p3 = attention_scores(3)
    attention_output(2, p2)
    y_conv = conv_finish()
    m0 = mem_scores(0, xq)
    attention_output(3, p3)
    for r in kbands + vbands:
        r[0:WINDOW, :] = r[TQ:TQ + WINDOW, :]
    m1 = mem_scores(1, xq)
    y_attn = _dot(o_ref[...], woa_ref[...])
    mem_output(0, m0)
    m2 = mem_scores(2, xq)
    mem_output(1, m1)
    m3 = mem_scores(3, xq)
    mem_output(2, m2)
    merged = gates[0] * y_attn + gates[1] * y_conv
    mem_output(3, m3)
    merged = merged + gates[2] * _dot(om_ref[...], wom_ref[...])

    x1prev_ref[...] = x + _dot(merged.astype(BF16), wout_ref[...])


def _router_logits(x1, g2_ref, wrt_ref, brt_ref):
    h2 = _rms(x1) * g2_ref[...]
    h_hi, h_lo = _split_bf16(h2)
    w_hi, w_lo = _split_bf16(wrt_ref[...])
    return _dot_nt(w_hi, h_hi) + _dot_nt(w_hi, h_lo) + _dot_nt(w_lo, h_hi) + brt_ref[...]


def _router_select(lt):
    gl = [lt[g:g + 1] for g in range(N_GROUPS)]
    gmax = functools.reduce(jnp.maximum, gl)
    g_idx = jnp.full((1, TQ), N_GROUPS - 1, jnp.int32)
    for g in reversed(range(N_GROUPS - 1)):
        g_idx = jnp.where(gl[g] == gmax, g, g_idx)
    p_g = 1.0 / functools.reduce(jnp.add, [jnp.exp(r - gmax) for r in gl])
    el = []
    for kk in range(EXPERTS_PER_GROUP):
        row = lt[8 + kk:9 + kk]
        for g in range(1, N_GROUPS):
            off = 8 + g * EXPERTS_PER_GROUP + kk
            row = jnp.where(g_idx == g, lt[off:off + 1], row)
        el.append(row)

    def first_argmax(rows):
        top = functools.reduce(jnp.maximum, rows)
        idx = jnp.full((1, TQ), len(rows) - 1, jnp.int32)
        for kk in reversed(range(len(rows) - 1)):
            idx = jnp.where(rows[kk] == top, kk, idx)
        return top, idx

    v1, i1 = first_argmax(el)
    v2, i2 = first_argmax([jnp.where(i1 == kk, -jnp.inf, el[kk]) for kk in range(EXPERTS_PER_GROUP)])
    t = jnp.exp(v2 - v1)
    p1 = 1.0 / (1.0 + t)
    p2 = t * p1

    lo = jnp.minimum(i1, i2)
    hi = jnp.maximum(i1, i2)
    w_lo = jnp.where(i1 < i2, p_g * p1, p_g * p2)
    w_hi = jnp.where(i1 < i2, p_g * p2, p_g * p1)
    cls = g_idx * N_PAIRS + ((lo * (2 * EXPERTS_PER_GROUP - 1 - lo)) >> 1) + hi - lo - 1
    return cls, w_lo, w_hi


def _router_emit(cls, w_lo, w_hi, tri_ref, x1_ref, rt_ref, cnt_ref):
    cls_f = cls.astype(F32)
    onehot = jnp.broadcast_to(cls, (CLS_ROWS, TQ)) == lax.broadcasted_iota(jnp.int32, (CLS_ROWS, TQ), 0)
    onehot_bf = onehot.astype(F32).astype(BF16)
    prefix = _dot(onehot_bf, tri_ref[...])
    rank = jnp.sum(jnp.where(onehot, prefix, 0.0), axis=0, keepdims=True) - 1.0
    counts = _dot_nt(jnp.ones((8, TQ), BF16), onehot_bf)
    zero = jnp.zeros((1, TQ), F32)
    rt_ref[0] = jnp.concatenate([cls_f, rank] + [zero] * 6, axis=0)
    cnt_ref[0] = jnp.concatenate([counts, jnp.zeros((8, LANES - CLS_ROWS), F32)], axis=1)
    cols = jnp.concatenate([w_lo, w_hi, cls_f, jnp.zeros((LANES - 3, TQ), F32)], axis=0)
    x1_ref[0, :, D_MODEL:] = cols.T


FLAG_FIRST, FLAG_LAST, FLAG_VALID = 1, 2, 4
GATHER_SLOTS = 3


def _moe_kernel(dest_ref, tile_ref, elo_ref, ehi_ref, cls_ref, flag_ref,
                g2_ref, wg_lo, wu_lo, wd_lo, wg_hi, wu_hi, wd_hi, x1_hbm,
                out_hbm,
                perm_ref, xbuf, obuf, h2buf, wbuf_up, wbuf_down, gsem, ssem):
    k = pl.program_id(0)
    nk = pl.num_programs(0)
    t = tile_ref[k]
    xslot = t % GATHER_SLOTS
    oslot = t % 2
    flags = flag_ref[k]
    n_tok = x1_hbm.shape[0]
    n_tiles = n_tok // TM

    def gather_copy(r, tile, s):
        tok = perm_ref[tile * TM + r]
        return pltpu.make_async_copy(x1_hbm.at[pl.ds(tok, 1), :], xbuf.at[s, pl.ds(r, 1), :], gsem.at[s])

    def scatter_copy(r, tile, s):
        tok = perm_ref[tile * TM + r]
        return pltpu.make_async_copy(obuf.at[s, pl.ds(r, 1), :], out_hbm.at[pl.ds(tok, 1), :], ssem.at[s])

    k_prev = jnp.maximum(k - 1, 0)
    for side, (e_ref, wg, wu, wd) in enumerate(((elo_ref, wg_lo, wu_lo, wd_lo), (ehi_ref, wg_hi, wu_hi, wd_hi))):
        @pl.when((k == 0) | (e_ref[k] != e_ref[k_prev]))
        def _(side=side, wg=wg, wu=wu, wd=wd):
            wbuf_up[2 * side] = wg[0]
            wbuf_up[2 * side + 1] = wu[0]
            wbuf_down[side] = wd[0]

    def for_rows(fn):
        def body(r, carry):
            fn(r)
            return carry
        lax.fori_loop(0, TM, body, 0, unroll=8)

    @pl.when(k == 0)
    def _():
        def invert(tok, carry):
            perm_ref[dest_ref[tok]] = tok
            return carry
        lax.fori_loop(0, n_tok, invert, 0, unroll=16)
        for tile in range(GATHER_SLOTS - 1):
            for_rows(lambda r, tile=tile: gather_copy(r, tile, tile).start())

    is_first = (flags & FLAG_FIRST) != 0
    is_valid = (flags & FLAG_VALID) != 0
    ahead = GATHER_SLOTS - 1
    prefetch_tile = jnp.minimum(t + ahead, n_tiles - 1)
    prefetch_slot = (t + ahead) % GATHER_SLOTS

    def gather_tile_wait(s):
        pltpu.make_async_copy(x1_hbm.at[pl.ds(0, TM), :], xbuf.at[s], gsem.at[s]).wait()

    def scatter_tile_wait(s):
        pltpu.make_async_copy(obuf.at[s], out_hbm.at[pl.ds(0, TM), :], ssem.at[s]).wait()

    @pl.when(is_first)
    def _():
        gather_tile_wait(xslot)

        @pl.when(t >= 2)
        def _():
            scatter_tile_wait(oslot)

        xr = xbuf[xslot, :, 0:D_MODEL]
        h2buf[...] = (_rms(xr) * g2_ref[...]).astype(BF16)
        obuf[oslot] = xr

    def experts():
        w = xbuf[xslot, :, D_MODEL:XW]
        mine = w[:, 2:3] == cls_ref[k].astype(F32)
        h2 = h2buf[...]
        y = None
        for side in range(2):
            c = jnp.where(mine, w[:, side:side + 1], 0.0)
            hid = jax.nn.silu(_dot(h2, wbuf_up[2 * side])) * _dot(h2, wbuf_up[2 * side + 1]) * c
            part = _dot(hid.astype(BF16), wbuf_down[side])
            y = part if y is None else y + part
        obuf[oslot] += y

    @pl.when(is_first & (t > 0))
    def _():
        for r in range(TM):
            gather_copy(r, prefetch_tile, prefetch_slot).start()
            scatter_copy(r, t - 1, 1 - oslot).start()
        experts()

    @pl.when(is_first & (t == 0))
    def _():
        for r in range(TM):
            gather_copy(r, prefetch_tile, prefetch_slot).start()
        experts()

    @pl.when(is_valid & jnp.logical_not(is_first))
    def _():
        experts()

    @pl.when(k == nk - 1)
    def _():
        last = n_tiles - 1
        for_rows(lambda r: scatter_copy(r, last, last % 2).start())
        for tile in (last - 1, last):
            scatter_tile_wait(tile % 2)
        for extra in range(1, GATHER_SLOTS):
            gather_tile_wait((last + extra) % GATHER_SLOTS)


_CLASS_PAIRS = [(a, b) for a in range(EXPERTS_PER_GROUP) for b in range(a + 1, EXPERTS_PER_GROUP)]


def _routing_plan(cls, rank, tile_counts, n_tiles):
    counts = jnp.sum(tile_counts, axis=0)
    off = jnp.concatenate([jnp.zeros((1,), jnp.int32), jnp.cumsum(counts).astype(jnp.int32)])
    tile_base = jnp.cumsum(tile_counts, axis=0) - tile_counts + off[None, :-1]
    onehot = cls[:, :, None] == jnp.arange(N_CLASSES, dtype=jnp.int32)[None, None, :]
    dest = (jnp.sum(jnp.where(onehot, tile_base[:, None, :], 0), axis=-1) + rank).reshape(-1)
    first_tile = off[:-1] // TM
    last_tile = (off[1:] - 1) // TM
    n_items = jnp.where(counts > 0, last_tile - first_tile + 1, 0)
    istart = jnp.concatenate([jnp.zeros((1,), jnp.int32), jnp.cumsum(n_items).astype(jnp.int32)])
    total = istart[-1]
    ni = n_tiles + N_CLASSES - 1
    k = jnp.arange(ni, dtype=jnp.int32)
    valid = k < total
    kc = jnp.minimum(k, total - 1)
    c_of_k = jnp.minimum(jnp.sum(istart[None, 1:] <= kc[:, None], axis=1).astype(jnp.int32), N_CLASSES - 1)
    tile_k = first_tile[c_of_k] + kc - istart[c_of_k]
    prev_tile = jnp.concatenate([jnp.full((1,), -1, jnp.int32), tile_k[:-1]])
    next_tile = jnp.concatenate([tile_k[1:], jnp.full((1,), -1, jnp.int32)])
    first = valid & (tile_k != prev_tile)
    last = valid & ((tile_k != next_tile) | (k == total - 1))
    flags = (first * FLAG_FIRST + last * FLAG_LAST + valid * FLAG_VALID).astype(jnp.int32)
    pair_lo = jnp.array([p[0] for p in _CLASS_PAIRS], jnp.int32)
    pair_hi = jnp.array([p[1] for p in _CLASS_PAIRS], jnp.int32)
    base = (c_of_k // N_PAIRS) * EXPERTS_PER_GROUP
    e_lo = base + pair_lo[c_of_k % N_PAIRS]
    e_hi = base + pair_hi[c_of_k % N_PAIRS]
    cls_k = jnp.where(valid, c_of_k, -1)
    return dest.astype(jnp.int32), tile_k, e_lo, e_hi, cls_k, flags


def _const_spec(shape):
    nd = len(shape)
    return pl.BlockSpec(shape, lambda *_: (0,) * nd, pipeline_mode=pl.Buffered(1))


def _block_diag_ones(n, blk):
    r = jnp.arange(n) // blk
    return (r[:, None] == r[None, :]).astype(BF16)


def kernel(x, mem, positions, g_norm1, w_in, g_q, g_k, sinks, w_o_attn, w_conv_dw, b_conv_dw, g_conv_ln, b_conv_ln, w_conv_out, g_mem, w_kv_mem, g_xq, g_xk, w_o_mem, w_out, g_norm2, w_group, b_group, w_router, b_router, w_gate, w_up, w_down):
    B, S, D = x.shape
    M = mem.shape[1]
    assert D == D_MODEL and S % TQ == 0 and (B * S) % TM == 0 and w_in.shape[0] == 1
    assert (TQ // WINDOW) * ATT_KV_HEADS == MEM_HEADS and CONV_CH // LANES == N_BRANCHES + 1
    NT = S // TQ
    T = B * S
    l = 0

    bd128 = _block_diag_ones(XQ_WIDTH, MEM_HEAD_DIM)
    row = lambda v: v.reshape(1, -1).astype(F32)

    mk, mv = pl.pallas_call(
        _memkv_kernel,
        grid=(B,),
        in_specs=[pl.BlockSpec((1, M, D), lambda b: (b, 0, 0)),
                  _const_spec((1, D)), _const_spec((D, 2 * XQ_WIDTH)),
                  _const_spec((1, XQ_WIDTH)), _const_spec((XQ_WIDTH, XQ_WIDTH))],
        out_specs=[pl.BlockSpec((1, M, XQ_WIDTH), lambda b: (b, 0, 0))] * 2,
        out_shape=[jax.ShapeDtypeStruct((B, M, XQ_WIDTH), BF16)] * 2,
        compiler_params=pltpu.CompilerParams(dimension_semantics=("arbitrary",)),
        name="memkv",
    )(mem, row(g_mem[l]), w_kv_mem[l].astype(BF16), row(jnp.tile(g_xk[l], MEM_HEADS)), bd128)

    inv_freq = 1.0 / (ROPE_THETA ** (jnp.arange(0, HEAD_DIM, 2, dtype=F32) / HEAD_DIM))
    per_row = LANES // (HEAD_DIM // 2)
    ang = positions.astype(F32).reshape(B, S // per_row, per_row, 1) * inv_freq
    ang = ang.reshape(B, S // per_row, LANES)
    cos_tab, sin_tab = jnp.cos(ang), jnp.sin(ang)

    assert w_in.shape[-1] == IN_WIDTH
    wdw = jnp.zeros((CONV_HALO, CONV_CH), F32).at[:CONV_WIDTH].set(w_conv_dw[l])
    w_rt = jnp.zeros((RT_ROWS, D), F32).at[0:N_GROUPS].set(w_group[l].T).at[8:8 + N_EXPERTS].set(w_router[l].T)
    b_rt = jnp.zeros((RT_ROWS, 1), F32).at[0:N_GROUPS, 0].set(b_group[l]).at[8:8 + N_EXPERTS, 0].set(b_router[l])

    n_mix = B * NT
    tile_in = lambda s: jnp.minimum(s, n_mix - 1)
    tile_out = lambda s: jnp.maximum(s - 1, 0)
    tile3 = lambda last: pl.BlockSpec((1, TQ, last), lambda s: (tile_in(s) // NT, tile_in(s) % NT, 0))
    per_batch = pl.BlockSpec((1, M, XQ_WIDTH), lambda s: (tile_in(s) // NT, 0, 0))
    rope_spec = pl.BlockSpec((1, TQ // per_row, LANES), lambda s: (tile_in(s) // NT, tile_in(s) % NT, 0))
    in_specs = [
        tile3(D), rope_spec, rope_spec, per_batch, per_batch,
        _const_spec((1, D)),
        _const_spec((D, IN_WIDTH)),
        _const_spec((1, Q_WIDTH)), _const_spec((1, LANES)),
        pl.BlockSpec(memory_space=pltpu.SMEM),
        _const_spec((Q_WIDTH, D)),
        _const_spec((CONV_HALO, CONV_CH)), _const_spec((1, CONV_CH)), _const_spec((1, CONV_CH)), _const_spec((1, CONV_CH)),
        _const_spec((CONV_CH, D)),
        _const_spec((1, XQ_WIDTH)),
        _const_spec((XQ_WIDTH, D)),
        _const_spec((D, D)),
        _const_spec((1, D)),
        _const_spec((RT_ROWS, D)), _const_spec((RT_ROWS, 1)), _const_spec((TQ, TQ)),
    ]
    tri = (jnp.arange(TQ)[:, None] <= jnp.arange(TQ)[None, :]).astype(BF16)
    per_tile = lambda rows, last: pl.BlockSpec((1, rows, last), lambda s: (tile_out(s), 0, 0))
    x1_spec = pl.BlockSpec((1, TQ, XW), lambda s: (tile_out(s) // NT, tile_out(s) % NT, 0))
    assert (N_EXPERTS * D) % n_mix == 0 and (N_EXPERTS * EXPERT_FF) % n_mix == 0
    up_rows, down_rows = N_EXPERTS * D // n_mix, N_EXPERTS * EXPERT_FF // n_mix
    up_slice = pl.BlockSpec((up_rows, EXPERT_FF), lambda s: (tile_in(s), 0))
    down_slice = pl.BlockSpec((down_rows, D), lambda s: (tile_in(s), 0))
    in_specs += [up_slice, up_slice, down_slice]
    x1, rt, cnt, wg, wu, wd = pl.pallas_call(
        functools.partial(_mixer_kernel, tiles_per_seq=NT),
        grid=(n_mix + 1,),
        in_specs=in_specs,
        out_specs=[x1_spec, per_tile(8, TQ), per_tile(8, LANES), up_slice, up_slice, down_slice],
        out_shape=[jax.ShapeDtypeStruct((B, S, XW), F32), jax.ShapeDtypeStruct((B * NT, 8, TQ), F32),
                   jax.ShapeDtypeStruct((B * NT, 8, LANES), F32),
                   jax.ShapeDtypeStruct((N_EXPERTS * D, EXPERT_FF), BF16),
                   jax.ShapeDtypeStruct((N_EXPERTS * D, EXPERT_FF), BF16),
                   jax.ShapeDtypeStruct((N_EXPERTS * EXPERT_FF, D), BF16)],
        scratch_shapes=[pltpu.VMEM((WINDOW + TQ, LANES), BF16)] * 4
        + [pltpu.VMEM((CONV_CH // LANES, CONV_HALO + TQ, LANES), F32),
           pltpu.VMEM((CONV_CH // LANES, TQ, LANES), F32), pltpu.VMEM((TQ, Q_WIDTH), BF16),
           pltpu.VMEM((TQ, XQ_WIDTH), BF16), pltpu.VMEM((TQ, D), F32),
           pltpu.VMEM((TQ, LANES), F32), pltpu.VMEM((TQ, LANES), F32)],
        compiler_params=pltpu.CompilerParams(dimension_semantics=("arbitrary",),
                                             vmem_limit_bytes=VMEM_LIMIT),
        name="mixer",
    )(x, cos_tab, sin_tab, mk, mv, row(g_norm1[l]), w_in[l].astype(BF16),
      row(jnp.tile(g_q[l], ATT_HEADS) * (HEAD_DIM ** -0.5)), row(jnp.tile(g_k[l], ATT_KV_HEADS)),
      sinks[l].astype(F32), w_o_attn[l].astype(BF16),
      wdw, row(b_conv_dw[l]), row(g_conv_ln[l]), row(b_conv_ln[l]), w_conv_out[l].astype(BF16),
      row(jnp.tile(g_xq[l], MEM_HEADS) * (MEM_HEAD_DIM ** -0.5)), w_o_mem[l].astype(BF16),
      w_out[l].astype(BF16), row(g_norm2[l]), w_rt, b_rt, tri,
      w_gate[l].reshape(N_EXPERTS * D, EXPERT_FF), w_up[l].reshape(N_EXPERTS * D, EXPERT_FF),
      w_down[l].reshape(N_EXPERTS * EXPERT_FF, D))

    n_tiles = T // TM
    plan = _routing_plan(rt[:, 0, :].astype(jnp.int32), rt[:, 1, :].astype(jnp.int32),
                         cnt[:, 0, :N_CLASSES].astype(jnp.int32), n_tiles)
    dest, tile_k, e_lo, e_hi, cls_k, flags = plan
    ni = tile_k.shape[0]
    wg = wg.reshape(N_EXPERTS, D, EXPERT_FF)
    wu = wu.reshape(N_EXPERTS, D, EXPERT_FF)
    wd = wd.reshape(N_EXPERTS, EXPERT_FF, D)
    lo_map = lambda k, dest, tile, elo, ehi, cls, flg: (elo[k], 0, 0)
    hi_map = lambda k, dest, tile, elo, ehi, cls, flg: (ehi[k], 0, 0)
    up_block = (1, D, EXPERT_FF)
    down_block = (1, EXPERT_FF, D)
    out = pl.pallas_call(
        _moe_kernel,
        grid_spec=pltpu.PrefetchScalarGridSpec(
            num_scalar_prefetch=6,
            grid=(ni,),
            in_specs=[pl.BlockSpec((1, D), lambda k, *_: (0, 0)),
                      pl.BlockSpec(up_block, lo_map), pl.BlockSpec(up_block, lo_map),
                      pl.BlockSpec(down_block, lo_map),
                      pl.BlockSpec(up_block, hi_map), pl.BlockSpec(up_block, hi_map),
                      pl.BlockSpec(down_block, hi_map),
                      pl.BlockSpec(memory_space=pl.ANY)],
            out_specs=pl.BlockSpec(memory_space=pl.ANY),
            scratch_shapes=[pltpu.SMEM((T,), jnp.int32),
                            pltpu.VMEM((GATHER_SLOTS, TM, XW), F32), pltpu.VMEM((2, TM, D), F32),
                            pltpu.VMEM((TM, D), BF16),
                            pltpu.VMEM((4, D, EXPERT_FF), BF16), pltpu.VMEM((2, EXPERT_FF, D), BF16),
                            pltpu.SemaphoreType.DMA((GATHER_SLOTS,)), pltpu.SemaphoreType.DMA((2,))]),
        out_shape=jax.ShapeDtypeStruct((T, D), F32),
        compiler_params=pltpu.CompilerParams(dimension_semantics=("arbitrary",),
                                             vmem_limit_bytes=VMEM_LIMIT),
        name="moe",
    )(dest, tile_k, e_lo, e_hi, cls_k, flags, row(g_norm2[l]), wg, wu, wd, wg, wu, wd,
      x1.reshape(T, XW))
    return out.reshape(B, S, D)
```

```python
import functools

import jax
import jax.numpy as jnp
from jax import lax
from jax.experimental import pallas as pl
from jax.experimental.pallas import tpu as pltpu

F32 = jnp.float32
BF16 = jnp.bfloat16

D_MODEL = 1024
ATT_HEADS = 8
ATT_KV_HEADS = 2
HEAD_DIM = 64
WINDOW = 128
ROPE_THETA = 10000.0
CONV_CH = 512
CONV_WIDTH = 31
MEM_HEADS = 4
MEM_HEAD_DIM = 128
N_BRANCHES = 3
N_GROUPS = 4
EXPERTS_PER_GROUP = 4
N_EXPERTS = N_GROUPS * EXPERTS_PER_GROUP
EXPERT_FF = 512
EPS = 1e-6
LN_EPS = 1e-5
NEG_INF = -1e30

Q_WIDTH = ATT_HEADS * HEAD_DIM
KV_WIDTH = ATT_KV_HEADS * HEAD_DIM
GLU_WIDTH = 2 * CONV_CH
XQ_WIDTH = MEM_HEADS * MEM_HEAD_DIM
GATE_WIDTH = N_BRANCHES * D_MODEL
COL_GLU = Q_WIDTH + 2 * KV_WIDTH
COL_XQ = COL_GLU + GLU_WIDTH
COL_GATE = COL_XQ + XQ_WIDTH
IN_WIDTH = COL_GATE + GATE_WIDTH

LANES = 128
TQ = 256
CONV_HALO = 32
CONV_ROWS = 128
TM = 256
RT_ROWS = 32
N_PAIRS = EXPERTS_PER_GROUP * (EXPERTS_PER_GROUP - 1) // 2
N_CLASSES = N_GROUPS * N_PAIRS
CLS_ROWS = 32
XW = D_MODEL + LANES
VMEM_LIMIT = 56 * 1024 * 1024


def _dot(a, b):
    return jnp.dot(a, b, preferred_element_type=F32)


def _dot_nt(a, b):
    return lax.dot_general(a, b, (((1,), (1,)), ((), ())), preferred_element_type=F32)


def _split_bf16(t):
    hi = t.astype(BF16)
    lo = (t - hi.astype(F32)).astype(BF16)
    return hi, lo


def _block_sum(t, bd):
    return _dot(t.astype(BF16), bd)


def _rms(t):
    return t * lax.rsqrt(jnp.mean(t * t, axis=-1, keepdims=True) + EPS)


def _memkv_kernel(mem_ref, gmem_ref, wkv_ref, gxk_ref, bd_ref, mk_ref, mv_ref):
    mn = (_rms(mem_ref[0]) * gmem_ref[...]).astype(BF16)
    kv = _dot(mn, wkv_ref[...])
    mk = kv[:, :XQ_WIDTH]
    ss = _block_sum(mk * mk, bd_ref[...])
    mk = mk * lax.rsqrt(ss * (1.0 / MEM_HEAD_DIM) + EPS) * gxk_ref[...]
    mk_ref[0] = mk.astype(BF16)
    mv_ref[0] = kv[:, XQ_WIDTH:].astype(BF16)


def _mixer_kernel(x_ref, cos_ref, sin_ref, mk_ref, mv_ref, g1_ref, win_ref,
                  gq_ref, gk_ref, sinks_ref, woa_ref, wdw_ref,
                  bdw_ref, gln_ref, bln_ref, wco_ref, gxq_ref, wom_ref, wout_ref, g2_ref,
                  wrt_ref, brt_ref, tri_ref, ewg_ref, ewu_ref, ewd_ref,
                  x1_ref, rt_ref, cnt_ref, ewg_out, ewu_out, ewd_out,
                  k0_ref, k1_ref, v0_ref, v1_ref, ubuf_ref, ybuf_ref, o_ref, om_ref, x1prev_ref,
                  cosd_ref, sind_ref, *, tiles_per_seq):
    s = pl.program_id(0)
    n_tiles = pl.num_programs(0) - 1
    i = jnp.minimum(s, n_tiles - 1) % tiles_per_seq
    kbands = (k0_ref, k1_ref)
    vbands = (v0_ref, v1_ref)

    @pl.when(s == 0)
    def _():
        x1prev_ref[...] = jnp.zeros((TQ, D_MODEL), F32)

    @pl.when(i == 0)
    def _():
        for r in kbands + vbands:
            r[0:WINDOW, :] = jnp.zeros((WINDOW, LANES), BF16)
        ubuf_ref[:, 0:CONV_HALO, :] = jnp.zeros((CONV_CH // LANES, CONV_HALO, LANES), F32)

    x1_prev = x1prev_ref[...]
    x1_ref[0, :, 0:D_MODEL] = x1_prev

    ewg_out[...] = ewg_ref[...].astype(BF16)
    ewu_out[...] = ewu_ref[...].astype(BF16)
    ewd_out[...] = ewd_ref[...].astype(BF16)

    x = x_ref[0]
    h = (_rms(x) * g1_ref[...]).astype(BF16)

    n_slabs = CONV_CH // LANES
    half = TQ // 2
    group = ATT_HEADS // ATT_KV_HEADS

    glu = _dot(h, win_ref[:, COL_GLU:COL_XQ])
    route_logits = _router_logits(x1_prev, g2_ref, wrt_ref, brt_ref)
    u = glu[:, :CONV_CH] * jax.nn.sigmoid(glu[:, CONV_CH:])
    for c in range(n_slabs):
        ubuf_ref[c, CONV_HALO:CONV_HALO + TQ, :] = u[:, c * LANES:(c + 1) * LANES]
    base = CONV_HALO - (CONV_WIDTH - 1)
    acc = [[None] * n_slabs for _ in range(2)]

    def conv_slab(c):
        cols = slice(c * LANES, (c + 1) * LANES)
        for par in range(2):
            parts = []
            for q0 in range(0, half, CONV_ROWS):
                a = jnp.broadcast_to(bdw_ref[:, cols], (CONV_ROWS, LANES))
                for j in range(CONV_WIDTH):
                    a = a + wdw_ref[j:j + 1, cols] * ubuf_ref[
                        c, pl.ds(base + j + par + 2 * q0, CONV_ROWS, stride=2), :]
                parts.append(a)
            acc[par][c] = jnp.concatenate(parts, axis=0)
        ubuf_ref[c, 0:CONV_HALO, :] = ubuf_ref[c, TQ:TQ + CONV_HALO, :]

    def gate(b):
        return jax.nn.sigmoid(_dot(h, win_ref[:, COL_GATE + b * D_MODEL:COL_GATE + (b + 1) * D_MODEL]))

    def conv_finish():
        for par in range(2):
            row_sum = functools.reduce(jnp.add, [jnp.sum(a, axis=-1, keepdims=True) for a in acc[par]])
            mu = row_sum * (1.0 / CONV_CH)
            cen = [a - mu for a in acc[par]]
            sq_sum = functools.reduce(jnp.add, [jnp.sum(t * t, axis=-1, keepdims=True) for t in cen])
            rstd = lax.rsqrt(sq_sum * (1.0 / CONV_CH) + LN_EPS)
            for c in range(n_slabs):
                cols = slice(c * LANES, (c + 1) * LANES)
                yln = cen[c] * rstd * gln_ref[:, cols] + bln_ref[:, cols]
                ybuf_ref[c, pl.ds(par, half, stride=2), :] = yln * jax.nn.sigmoid(yln)
        uo = jnp.concatenate([ybuf_ref[c] for c in range(n_slabs)], axis=-1).astype(BF16)
        return _dot(uo, wco_ref[...])

    for c in range(n_slabs):
        conv_slab(c)
    qkv = _dot(h, win_ref[:, 0:COL_GLU])
    gates = [gate(0)]
    q = qkv[:, :Q_WIDTH]
    k = qkv[:, Q_WIDTH:Q_WIDTH + KV_WIDTH]
    v = qkv[:, Q_WIDTH + KV_WIDTH:]
    gates.append(gate(1))

    lane = lax.broadcasted_iota(jnp.int32, (TQ, LANES), 1)
    first_half = (lane % HEAD_DIM) < (HEAD_DIM // 2)
    low_head = lane < HEAD_DIM
    n_freq = HEAD_DIM // 2
    per_row = LANES // n_freq
    quarter = lax.broadcasted_iota(jnp.int32, (TQ // per_row, LANES), 1) // n_freq
    sign = jnp.where(quarter % 2 == 0, -1.0, 1.0)
    for table_ref, out_ref, scale in ((cos_ref, cosd_ref, None), (sin_ref, sind_ref, sign)):
        dense = table_ref[0]
        rolled = [dense] + [pltpu.roll(dense, n_freq * kk, 1) for kk in range(1, per_row)]
        for j in range(per_row):
            spread = rolled[(-j) % per_row]
            for qq in range(1, per_row):
                spread = jnp.where(quarter == qq, rolled[(qq - j) % per_row], spread)
            out_ref[pl.ds(j, TQ // per_row, stride=per_row), :] = spread if scale is None else spread * scale
    cosd = cosd_ref[...]
    sind = sind_ref[...]

    def rope(t):
        rot = jnp.where(first_half, pltpu.roll(t, LANES - HEAD_DIM // 2, 1),
                        pltpu.roll(t, HEAD_DIM // 2, 1))
        return t * cosd + rot * sind

    def dup_halves(t):
        swapped = pltpu.roll(t, HEAD_DIM, 1)
        return jnp.where(low_head, t, swapped), jnp.where(low_head, swapped, t)

    def qk_norm(t, gain):
        sq = t * t
        lo = jnp.sum(jnp.where(low_head, sq, 0.0), axis=-1, keepdims=True)
        hi = jnp.sum(jnp.where(low_head, 0.0, sq), axis=-1, keepdims=True)
        inv = jnp.where(low_head, lax.rsqrt(lo * (1.0 / HEAD_DIM) + EPS), lax.rsqrt(hi * (1.0 / HEAD_DIM) + EPS))
        return t * inv * gain

    kd = dup_halves(rope(qk_norm(k, gk_ref[...])))
    vd = dup_halves(v)
    for kvh in range(ATT_KV_HEADS):
        kbands[kvh][WINDOW:WINDOW + TQ, :] = kd[kvh].astype(BF16)
        vbands[kvh][WINDOW:WINDOW + TQ, :] = vd[kvh].astype(BF16)

    q_heads = []
    for c in range(Q_WIDTH // LANES):
        cols = slice(c * LANES, (c + 1) * LANES)
        qc = rope(qk_norm(q[:, cols], gq_ref[:, cols]))
        q_heads.append(jnp.where(low_head, qc, 0.0).astype(BF16))
        q_heads.append(jnp.where(low_head, 0.0, qc).astype(BF16))

    qi = lax.broadcasted_iota(jnp.int32, (WINDOW, 2 * WINDOW), 0)
    kj = lax.broadcasted_iota(jnp.int32, (WINDOW, 2 * WINDOW), 1)
    band_mask = (kj > qi) & (kj <= qi + WINDOW)
    first_mask = band_mask & (kj >= jnp.where(i > 0, 0, WINDOW))
    low_head_w = lax.broadcasted_iota(jnp.int32, (WINDOW, LANES), 1) < HEAD_DIM

    def attention_scores(step):
        n, kvh = divmod(step, ATT_KV_HEADS)
        rows = slice(n * WINDOW, (n + 1) * WINDOW)
        mask = first_mask if n == 0 else band_mask
        heads = range(kvh * group, (kvh + 1) * group)
        q_st = jnp.concatenate([q_heads[hd][rows] for hd in heads], axis=0)
        kband = kbands[kvh][n * WINDOW:n * WINDOW + 2 * WINDOW, :]
        s_all = _dot_nt(q_st, kband)
        es, dens = [], []
        for j, hd in enumerate(heads):
            s = jnp.where(mask, s_all[j * WINDOW:(j + 1) * WINDOW], NEG_INF)
            sink = sinks_ref[hd]
            m = jnp.maximum(jnp.max(s, axis=-1, keepdims=True), sink)
            e = jnp.exp(s - m)
            dens.append(jnp.sum(e, axis=-1, keepdims=True) + jnp.exp(sink - m))
            es.append(e.astype(BF16))
        return jnp.concatenate(es, axis=0), dens

    def attention_output(step, probs):
        n, kvh = divmod(step, ATT_KV_HEADS)
        rows = slice(n * WINDOW, (n + 1) * WINDOW)
        e_all, dens = probs
        vband = vbands[kvh][n * WINDOW:n * WINDOW + 2 * WINDOW, :]
        o_all = _dot(e_all, vband)
        for cc in range(group // 2):
            oa = o_all[(2 * cc) * WINDOW:(2 * cc + 1) * WINDOW] / dens[2 * cc]
            ob = o_all[(2 * cc + 1) * WINDOW:(2 * cc + 2) * WINDOW] / dens[2 * cc + 1]
            col = kvh * (group // 2) + cc
            o_ref[rows, col * LANES:(col + 1) * LANES] = jnp.where(low_head_w, oa, ob).astype(BF16)

    def mem_scores(hd, xq):
        cols = slice(hd * MEM_HEAD_DIM, (hd + 1) * MEM_HEAD_DIM)
        s = _dot_nt(xq[:, cols], mk_ref[0, :, cols])
        e = jnp.exp(s - jnp.max(s, axis=-1, keepdims=True))
        return e.astype(BF16), jnp.sum(e, axis=-1, keepdims=True)

    def mem_output(hd, probs):
        cols = slice(hd * MEM_HEAD_DIM, (hd + 1) * MEM_HEAD_DIM)
        e, den = probs
        om_ref[:, cols] = (_dot(e, mv_ref[0, :, cols]) / den).astype(BF16)

    route = _router_select(route_logits)
    p0 = attention_scores(0)
    gates.append(gate(2))
    p1 = attention_scores(1)
    attention_output(0, p0)
    xq = _dot(h, win_ref[:, COL_XQ:COL_GATE])
    p2 = attention_scores(2)
    attention_output(1, p1)
    _router_emit(*route, tri_ref, x1_ref, rt_ref, cnt_ref)
    xq = jnp.concatenate(
        [_rms(xq[:, hd * MEM_HEAD_DIM:(hd + 1) * MEM_HEAD_DIM]) for hd in range(MEM_HEADS)], axis=-1)
    xq = (xq * gxq_ref[...]).astype(BF16)
    p3 = attention_scores(3)
    attention_output(2, p2)
    y_conv = conv_finish()
    m0 = mem_scores(0, xq)
    attention_output(3, p3)
    for r in kbands + vbands:
        r[0:WINDOW, :] = r[TQ:TQ + WINDOW, :]
    m1 = mem_scores(1, xq)
    y_attn = _dot(o_ref[...], woa_ref[...])
    mem_output(0, m0)
    m2 = mem_scores(2, xq)
    mem_output(1, m1)
    m3 = mem_scores(3, xq)
    mem_output(2, m2)
    merged = gates[0] * y_attn + gates[1] * y_conv
    mem_output(3, m3)
    merged = merged + gates[2] * _dot(om_ref[...], wom_ref[...])

    x1prev_ref[...] = x + _dot(merged.astype(BF16), wout_ref[...])


def _router_logits(x1, g2_ref, wrt_ref, brt_ref):
    h2 = _rms(x1) * g2_ref[...]
    h_hi, h_lo = _split_bf16(h2)
    w_hi, w_lo = _split_bf16(wrt_ref[...])
    return _dot_nt(w_hi, h_hi) + _dot_nt(w_hi, h_lo) + _dot_nt(w_lo, h_hi) + brt_ref[...]


def _router_select(lt):
    gl = [lt[g:g + 1] for g in range(N_GROUPS)]
    gmax = functools.reduce(jnp.maximum, gl)
    g_idx = jnp.full((1, TQ), N_GROUPS - 1, jnp.int32)
    for g in reversed(range(N_GROUPS - 1)):
        g_idx = jnp.where(gl[g] == gmax, g, g_idx)
    p_g = 1.0 / functools.reduce(jnp.add, [jnp.exp(r - gmax) for r in gl])
    el = []
    for kk in range(EXPERTS_PER_GROUP):
        row = lt[8 + kk:9 + kk]
        for g in range(1, N_GROUPS):
            off = 8 + g * EXPERTS_PER_GROUP + kk
            row = jnp.where(g_idx == g, lt[off:off + 1], row)
        el.append(row)

    def first_argmax(rows):
        top = functools.reduce(jnp.maximum, rows)
        idx = jnp.full((1, TQ), len(rows) - 1, jnp.int32)
        for kk in reversed(range(len(rows) - 1)):
            idx = jnp.where(rows[kk] == top, kk, idx)
        return top, idx

    v1, i1 = first_argmax(el)
    v2, i2 = first_argmax([jnp.where(i1 == kk, -jnp.inf, el[kk]) for kk in range(EXPERTS_PER_GROUP)])
    t = jnp.exp(v2 - v1)
    p1 = 1.0 / (1.0 + t)
    p2 = t * p1

    lo = jnp.minimum(i1, i2)
    hi = jnp.maximum(i1, i2)
    w_lo = jnp.where(i1 < i2, p_g * p1, p_g * p2)
    w_hi = jnp.where(i1 < i2, p_g * p2, p_g * p1)
    cls = g_idx * N_PAIRS + ((lo * (2 * EXPERTS_PER_GROUP - 1 - lo)) >> 1) + hi - lo - 1
    return cls, w_lo, w_hi


def _router_emit(cls, w_lo, w_hi, tri_ref, x1_ref, rt_ref, cnt_ref):
    cls_f = cls.astype(F32)
    onehot = jnp.broadcast_to(cls, (CLS_ROWS, TQ)) == lax.broadcasted_iota(jnp.int32, (CLS_ROWS, TQ), 0)
    onehot_bf = onehot.astype(F32).astype(BF16)
    prefix = _dot(onehot_bf, tri_ref[...])
    rank = jnp.sum(jnp.where(onehot, prefix, 0.0), axis=0, keepdims=True) - 1.0
    counts = _dot_nt(jnp.ones((8, TQ), BF16), onehot_bf)
    zero = jnp.zeros((1, TQ), F32)
    rt_ref[0] = jnp.concatenate([cls_f, rank] + [zero] * 6, axis=0)
    cnt_ref[0] = jnp.concatenate([counts, jnp.zeros((8, LANES - CLS_ROWS), F32)], axis=1)
    cols = jnp.concatenate([w_lo, w_hi, cls_f, jnp.zeros((LANES - 3, TQ), F32)], axis=0)
    x1_ref[0, :, D_MODEL:] = cols.T


FLAG_FIRST, FLAG_LAST, FLAG_VALID = 1, 2, 4
GATHER_SLOTS = 4


def _moe_kernel(dest_ref, tile_ref, elo_ref, ehi_ref, cls_ref, flag_ref,
                g2_ref, wg_lo, wu_lo, wd_lo, wg_hi, wu_hi, wd_hi, x1_hbm,
                out_hbm,
                perm_ref, xbuf, obuf, h2buf, wbuf_up, wbuf_down, gsem, ssem):
    k = pl.program_id(0)
    nk = pl.num_programs(0)
    t = tile_ref[k]
    xslot = t % GATHER_SLOTS
    oslot = t % 2
    flags = flag_ref[k]
    n_tok = x1_hbm.shape[0]
    n_tiles = n_tok // TM

    def gather_copy(r, tile, s):
        tok = perm_ref[tile * TM + r]
        return pltpu.make_async_copy(x1_hbm.at[pl.ds(tok, 1), :], xbuf.at[s, pl.ds(r, 1), :], gsem.at[s])

    def scatter_copy(r, tile, s):
        tok = perm_ref[tile * TM + r]
        return pltpu.make_async_copy(obuf.at[s, pl.ds(r, 1), :], out_hbm.at[pl.ds(tok, 1), :], ssem.at[s])

    k_prev = jnp.maximum(k - 1, 0)
    for side, (e_ref, wg, wu, wd) in enumerate(((elo_ref, wg_lo, wu_lo, wd_lo), (ehi_ref, wg_hi, wu_hi, wd_hi))):
        @pl.when((k == 0) | (e_ref[k] != e_ref[k_prev]))
        def _(side=side, wg=wg, wu=wu, wd=wd):
            wbuf_up[2 * side] = wg[0]
            wbuf_up[2 * side + 1] = wu[0]
            wbuf_down[side] = wd[0]

    def for_rows(fn):
        def body(r, carry):
            fn(r)
            return carry
        lax.fori_loop(0, TM, body, 0, unroll=8)

    @pl.when(k == 0)
    def _():
        def invert(tok, carry):
            perm_ref[dest_ref[tok]] = tok
            return carry
        lax.fori_loop(0, n_tok, invert, 0, unroll=16)
        for tile in range(GATHER_SLOTS - 1):
            for_rows(lambda r, tile=tile: gather_copy(r, tile, tile).start())

    is_first = (flags & FLAG_FIRST) != 0
    is_valid = (flags & FLAG_VALID) != 0
    ahead = GATHER_SLOTS - 1
    prefetch_tile = jnp.minimum(t + ahead, n_tiles - 1)
    prefetch_slot = (t + ahead) % GATHER_SLOTS

    def gather_tile_wait(s):
        pltpu.make_async_copy(x1_hbm.at[pl.ds(0, TM), :], xbuf.at[s], gsem.at[s]).wait()

    def scatter_tile_wait(s):
        pltpu.make_async_copy(obuf.at[s], out_hbm.at[pl.ds(0, TM), :], ssem.at[s]).wait()

    @pl.when(is_first)
    def _():
        gather_tile_wait(xslot)

        @pl.when(t >= 2)
        def _():
            scatter_tile_wait(oslot)

        xr = xbuf[xslot, :, 0:D_MODEL]
        h2buf[...] = (_rms(xr) * g2_ref[...]).astype(BF16)
        obuf[oslot] = xr

    def experts():
        w = xbuf[xslot, :, D_MODEL:XW]
        mine = w[:, 2:3] == cls_ref[k].astype(F32)
        h2 = h2buf[...]
        y = None
        for side in range(2):
            c = jnp.where(mine, w[:, side:side + 1], 0.0)
            hid = jax.nn.silu(_dot(h2, wbuf_up[2 * side])) * _dot(h2, wbuf_up[2 * side + 1]) * c
            part = _dot(hid.astype(BF16), wbuf_down[side])
            y = part if y is None else y + part
        obuf[oslot] += y

    @pl.when(is_first & (t > 0))
    def _():
        for r in range(TM):
            gather_copy(r, prefetch_tile, prefetch_slot).start()
            scatter_copy(r, t - 1, 1 - oslot).start()
        experts()

    @pl.when(is_first & (t == 0))
    def _():
        for r in range(TM):
            gather_copy(r, prefetch_tile, prefetch_slot).start()
        experts()

    @pl.when(is_valid & jnp.logical_not(is_first))
    def _():
        experts()

    @pl.when(k == nk - 1)
    def _():
        last = n_tiles - 1
        for_rows(lambda r: scatter_copy(r, last, last % 2).start())
        for tile in (last - 1, last):
            scatter_tile_wait(tile % 2)
        for extra in range(1, GATHER_SLOTS):
            gather_tile_wait((last + extra) % GATHER_SLOTS)


_CLASS_PAIRS = [(a, b) for a in range(EXPERTS_PER_GROUP) for b in range(a + 1, EXPERTS_PER_GROUP)]


def _routing_plan(cls, rank, tile_counts, n_tiles):
    counts = jnp.sum(tile_counts, axis=0)
    off = jnp.concatenate([jnp.zeros((1,), jnp.int32), jnp.cumsum(counts).astype(jnp.int32)])
    tile_base = jnp.cumsum(tile_counts, axis=0) - tile_counts + off[None, :-1]
    onehot = cls[:, :, None] == jnp.arange(N_CLASSES, dtype=jnp.int32)[None, None, :]
    dest = (jnp.sum(jnp.where(onehot, tile_base[:, None, :], 0), axis=-1) + rank).reshape(-1)
    first_tile = off[:-1] // TM
    last_tile = (off[1:] - 1) // TM
    n_items = jnp.where(counts > 0, last_tile - first_tile + 1, 0)
    istart = jnp.concatenate([jnp.zeros((1,), jnp.int32), jnp.cumsum(n_items).astype(jnp.int32)])
    total = istart[-1]
    ni = n_tiles + N_CLASSES - 1
    k = jnp.arange(ni, dtype=jnp.int32)
    valid = k < total
    kc = jnp.minimum(k, total - 1)
    c_of_k = jnp.minimum(jnp.sum(istart[None, 1:] <= kc[:, None], axis=1).astype(jnp.int32), N_CLASSES - 1)
    tile_k = first_tile[c_of_k] + kc - istart[c_of_k]
    prev_tile = jnp.concatenate([jnp.full((1,), -1, jnp.int32), tile_k[:-1]])
    next_tile = jnp.concatenate([tile_k[1:], jnp.full((1,), -1, jnp.int32)])
    first = valid & (tile_k != prev_tile)
    last = valid & ((tile_k != next_tile) | (k == total - 1))
    flags = (first * FLAG_FIRST + last * FLAG_LAST + valid * FLAG_VALID).astype(jnp.int32)
    pair_lo = jnp.array([p[0] for p in _CLASS_PAIRS], jnp.int32)
    pair_hi = jnp.array([p[1] for p in _CLASS_PAIRS], jnp.int32)
    base = (c_of_k // N_PAIRS) * EXPERTS_PER_GROUP
    e_lo = base + pair_lo[c_of_k % N_PAIRS]
    e_hi = base + pair_hi[c_of_k % N_PAIRS]
    cls_k = jnp.where(valid, c_of_k, -1)
    return dest.astype(jnp.int32), tile_k, e_lo, e_hi, cls_k, flags


def _const_spec(shape):
    nd = len(shape)
    return pl.BlockSpec(shape, lambda *_: (0,) * nd, pipeline_mode=pl.Buffered(1))


def _block_diag_ones(n, blk):
    r = jnp.arange(n) // blk
    return (r[:, None] == r[None, :]).astype(BF16)


def kernel(x, mem, positions, g_norm1, w_in, g_q, g_k, sinks, w_o_attn, w_conv_dw, b_conv_dw, g_conv_ln, b_conv_ln, w_conv_out, g_mem, w_kv_mem, g_xq, g_xk, w_o_mem, w_out, g_norm2, w_group, b_group, w_router, b_router, w_gate, w_up, w_down):
    B, S, D = x.shape
    M = mem.shape[1]
    assert D == D_MODEL and S % TQ == 0 and (B * S) % TM == 0 and w_in.shape[0] == 1
    assert (TQ // WINDOW) * ATT_KV_HEADS == MEM_HEADS and CONV_CH // LANES == N_BRANCHES + 1
    NT = S // TQ
    T = B * S
    l = 0

    bd128 = _block_diag_ones(XQ_WIDTH, MEM_HEAD_DIM)
    row = lambda v: v.reshape(1, -1).astype(F32)

    mk, mv = pl.pallas_call(
        _memkv_kernel,
        grid=(B,),
        in_specs=[pl.BlockSpec((1, M, D), lambda b: (b, 0, 0)),
                  _const_spec((1, D)), _const_spec((D, 2 * XQ_WIDTH)),
                  _const_spec((1, XQ_WIDTH)), _const_spec((XQ_WIDTH, XQ_WIDTH))],
        out_specs=[pl.BlockSpec((1, M, XQ_WIDTH), lambda b: (b, 0, 0))] * 2,
        out_shape=[jax.ShapeDtypeStruct((B, M, XQ_WIDTH), BF16)] * 2,
        compiler_params=pltpu.CompilerParams(dimension_semantics=("arbitrary",)),
        name="memkv",
    )(mem, row(g_mem[l]), w_kv_mem[l].astype(BF16), row(jnp.tile(g_xk[l], MEM_HEADS)), bd128)

    inv_freq = 1.0 / (ROPE_THETA ** (jnp.arange(0, HEAD_DIM, 2, dtype=F32) / HEAD_DIM))
    per_row = LANES // (HEAD_DIM // 2)
    ang = positions.astype(F32).reshape(B, S // per_row, per_row, 1) * inv_freq
    ang = ang.reshape(B, S // per_row, LANES)
    cos_tab, sin_tab = jnp.cos(ang), jnp.sin(ang)

    assert w_in.shape[-1] == IN_WIDTH
    wdw = jnp.zeros((CONV_HALO, CONV_CH), F32).at[:CONV_WIDTH].set(w_conv_dw[l])
    w_rt = jnp.zeros((RT_ROWS, D), F32).at[0:N_GROUPS].set(w_group[l].T).at[8:8 + N_EXPERTS].set(w_router[l].T)
    b_rt = jnp.zeros((RT_ROWS, 1), F32).at[0:N_GROUPS, 0].set(b_group[l]).at[8:8 + N_EXPERTS, 0].set(b_router[l])

    n_mix = B * NT
    tile_in = lambda s: jnp.minimum(s, n_mix - 1)
    tile_out = lambda s: jnp.maximum(s - 1, 0)
    tile3 = lambda last: pl.BlockSpec((1, TQ, last), lambda s: (tile_in(s) // NT, tile_in(s) % NT, 0))
    per_batch = pl.BlockSpec((1, M, XQ_WIDTH), lambda s: (tile_in(s) // NT, 0, 0))
    rope_spec = pl.BlockSpec((1, TQ // per_row, LANES), lambda s: (tile_in(s) // NT, tile_in(s) % NT, 0))
    in_specs = [
        tile3(D), rope_spec, rope_spec, per_batch, per_batch,
        _const_spec((1, D)),
        _const_spec((D, IN_WIDTH)),
        _const_spec((1, Q_WIDTH)), _const_spec((1, LANES)),
        pl.BlockSpec(memory_space=pltpu.SMEM),
        _const_spec((Q_WIDTH, D)),
        _const_spec((CONV_HALO, CONV_CH)), _const_spec((1, CONV_CH)), _const_spec((1, CONV_CH)), _const_spec((1, CONV_CH)),
        _const_spec((CONV_CH, D)),
        _const_spec((1, XQ_WIDTH)),
        _const_spec((XQ_WIDTH, D)),
        _const_spec((D, D)),
        _const_spec((1, D)),
        _const_spec((RT_ROWS, D)), _const_spec((RT_ROWS, 1)), _const_spec((TQ, TQ)),
    ]
    tri = (jnp.arange(TQ)[:, None] <= jnp.arange(TQ)[None, :]).astype(BF16)
    per_tile = lambda rows, last: pl.BlockSpec((1, rows, last), lambda s: (tile_out(s), 0, 0))
    x1_spec = pl.BlockSpec((1, TQ, XW), lambda s: (tile_out(s) // NT, tile_out(s) % NT, 0))
    assert (N_EXPERTS * D) % n_mix == 0 and (N_EXPERTS * EXPERT_FF) % n_mix == 0
    up_rows, down_rows = N_EXPERTS * D // n_mix, N_EXPERTS * EXPERT_FF // n_mix
    up_slice = pl.BlockSpec((up_rows, EXPERT_FF), lambda s: (tile_in(s), 0))
    down_slice = pl.BlockSpec((down_rows, D), lambda s: (tile_in(s), 0))
    in_specs += [up_slice, up_slice, down_slice]
    x1, rt, cnt, wg, wu, wd = pl.pallas_call(
        functools.partial(_mixer_kernel, tiles_per_seq=NT),
        grid=(n_mix + 1,),
        in_specs=in_specs,
        out_specs=[x1_spec, per_tile(8, TQ), per_tile(8, LANES), up_slice, up_slice, down_slice],
        out_shape=[jax.ShapeDtypeStruct((B, S, XW), F32), jax.ShapeDtypeStruct((B * NT, 8, TQ), F32),
                   jax.ShapeDtypeStruct((B * NT, 8, LANES), F32),
                   jax.ShapeDtypeStruct((N_EXPERTS * D, EXPERT_FF), BF16),
                   jax.ShapeDtypeStruct((N_EXPERTS * D, EXPERT_FF), BF16),
                   jax.ShapeDtypeStruct((N_EXPERTS * EXPERT_FF, D), BF16)],
        scratch_shapes=[pltpu.VMEM((WINDOW + TQ, LANES), BF16)] * 4
        + [pltpu.VMEM((CONV_CH // LANES, CONV_HALO + TQ, LANES), F32),
           pltpu.VMEM((CONV_CH // LANES, TQ, LANES), F32), pltpu.VMEM((TQ, Q_WIDTH), BF16),
           pltpu.VMEM((TQ, XQ_WIDTH), BF16), pltpu.VMEM((TQ, D), F32),
           pltpu.VMEM((TQ, LANES), F32), pltpu.VMEM((TQ, LANES), F32)],
        compiler_params=pltpu.CompilerParams(dimension_semantics=("arbitrary",),
                                             vmem_limit_bytes=VMEM_LIMIT),
        name="mixer",
    )(x, cos_tab, sin_tab, mk, mv, row(g_norm1[l]), w_in[l].astype(BF16),
      row(jnp.tile(g_q[l], ATT_HEADS) * (HEAD_DIM ** -0.5)), row(jnp.tile(g_k[l], ATT_KV_HEADS)),
      sinks[l].astype(F32), w_o_attn[l].astype(BF16),
      wdw, row(b_conv_dw[l]), row(g_conv_ln[l]), row(b_conv_ln[l]), w_conv_out[l].astype(BF16),
      row(jnp.tile(g_xq[l], MEM_HEADS) * (MEM_HEAD_DIM ** -0.5)), w_o_mem[l].astype(BF16),
      w_out[l].astype(BF16), row(g_norm2[l]), w_rt, b_rt, tri,
      w_gate[l].reshape(N_EXPERTS * D, EXPERT_FF), w_up[l].reshape(N_EXPERTS * D, EXPERT_FF),
      w_down[l].reshape(N_EXPERTS * EXPERT_FF, D))

    n_tiles = T // TM
    plan = _routing_plan(rt[:, 0, :].astype(jnp.int32), rt[:, 1, :].astype(jnp.int32),
                         cnt[:, 0, :N_CLASSES].astype(jnp.int32), n_tiles)
    dest, tile_k, e_lo, e_hi, cls_k, flags = plan
    ni = tile_k.shape[0]
    wg = wg.reshape(N_EXPERTS, D, EXPERT_FF)
    wu = wu.reshape(N_EXPERTS, D, EXPERT_FF)
    wd = wd.reshape(N_EXPERTS, EXPERT_FF, D)
    lo_map = lambda k, dest, tile, elo, ehi, cls, flg: (elo[k], 0, 0)
    hi_map = lambda k, dest, tile, elo, ehi, cls, flg: (ehi[k], 0, 0)
    up_block = (1, D, EXPERT_FF)
    down_block = (1, EXPERT_FF, D)
    out = pl.pallas_call(
        _moe_kernel,
        grid_spec=pltpu.PrefetchScalarGridSpec(
            num_scalar_prefetch=6,
            grid=(ni,),
            in_specs=[pl.BlockSpec((1, D), lambda k, *_: (0, 0)),
                      pl.BlockSpec(up_block, lo_map), pl.BlockSpec(up_block, lo_map),
                      pl.BlockSpec(down_block, lo_map),
                      pl.BlockSpec(up_block, hi_map), pl.BlockSpec(up_block, hi_map),
                      pl.BlockSpec(down_block, hi_map),
                      pl.BlockSpec(memory_space=pl.ANY)],
            out_specs=pl.BlockSpec(memory_space=pl.ANY),
            scratch_shapes=[pltpu.SMEM((T,), jnp.int32),
                            pltpu.VMEM((GATHER_SLOTS, TM, XW), F32), pltpu.VMEM((2, TM, D), F32),
                            pltpu.VMEM((TM, D), BF16),
                            pltpu.VMEM((4, D, EXPERT_FF), BF16), pltpu.VMEM((2, EXPERT_FF, D), BF16),
                            pltpu.SemaphoreType.DMA((GATHER_SLOTS,)), pltpu.SemaphoreType.DMA((2,))]),
        out_shape=jax.ShapeDtypeStruct((T, D), F32),
        compiler_params=pltpu.CompilerParams(dimension_semantics=("arbitrary",),
                                             vmem_limit_bytes=VMEM_LIMIT),
        name="moe",
    )(dest, tile_k, e_lo, e_hi, cls_k, flags, row(g_norm2[l]), wg, wu, wd, wg, wu, wd,
      x1.reshape(T, XW))
    return out.reshape(B, S, D)
```

```python
import functools

import jax
import jax.numpy as jnp
from jax import lax
from jax.experimental import pallas as pl
from jax.experimental.pallas import tpu as pltpu

F32 = jnp.float32
BF16 = jnp.bfloat16

D_MODEL = 1024
ATT_HEADS = 8
ATT_KV_HEADS = 2
HEAD_DIM = 64
WINDOW = 128
ROPE_THETA = 10000.0
CONV_CH = 512
CONV_WIDTH = 31
MEM_HEADS = 4
MEM_HEAD_DIM = 128
N_BRANCHES = 3
N_GROUPS = 4
EXPERTS_PER_GROUP = 4
N_EXPERTS = N_GROUPS * EXPERTS_PER_GROUP
EXPERT_FF = 512
EPS = 1e-6
LN_EPS = 1e-5
NEG_INF = -1e30
LOG2E = 1.4426950408889634

Q_WIDTH = ATT_HEADS * HEAD_DIM
KV_WIDTH = ATT_KV_HEADS * HEAD_DIM
GLU_WIDTH = 2 * CONV_CH
XQ_WIDTH = MEM_HEADS * MEM_HEAD_DIM
GATE_WIDTH = N_BRANCHES * D_MODEL
COL_GLU = Q_WIDTH + 2 * KV_WIDTH
COL_XQ = COL_GLU + GLU_WIDTH
COL_GATE = COL_XQ + XQ_WIDTH
IN_WIDTH = COL_GATE + GATE_WIDTH

LANES = 128
TQ = 256
CONV_HALO = 32
CONV_ROWS = 128
TM = 256
RT_ROWS = 32
N_PAIRS = EXPERTS_PER_GROUP * (EXPERTS_PER_GROUP - 1) // 2
N_CLASSES = N_GROUPS * N_PAIRS
CLS_ROWS = 32
XW = D_MODEL + LANES
VMEM_LIMIT = 56 * 1024 * 1024


def _dot(a, b):
    return jnp.dot(a, b, preferred_element_type=F32)


def _dot_nt(a, b):
    return lax.dot_general(a, b, (((1,), (1,)), ((), ())), preferred_element_type=F32)


def _split_bf16(t):
    hi = t.astype(BF16)
    lo = (t - hi.astype(F32)).astype(BF16)
    return hi, lo


def _rms(t):
    return t * lax.rsqrt(jnp.mean(t * t, axis=-1, keepdims=True) + EPS)


def _rms_per_mem_head(t):
    return jnp.concatenate(
        [_rms(t[:, hd * MEM_HEAD_DIM:(hd + 1) * MEM_HEAD_DIM]) for hd in range(MEM_HEADS)], axis=-1)


def _memkv_kernel(mem_ref, gmem_ref, wkv_ref, gxk_ref, mk_ref, mv_ref):
    mn = (_rms(mem_ref[0]) * gmem_ref[...]).astype(BF16)
    kv = _dot(mn, wkv_ref[...])
    mk_ref[0] = (_rms_per_mem_head(kv[:, :XQ_WIDTH]) * gxk_ref[...]).astype(BF16)
    mv_ref[0] = kv[:, XQ_WIDTH:].astype(BF16)


def _mixer_kernel(x_ref, cos_ref, sin_ref, mk_ref, mv_ref, g1_ref, win_ref,
                  gq_ref, gk_ref, sinks_ref, woa_ref, wdw_ref,
                  bdw_ref, gln_ref, bln_ref, wco_ref, gxq_ref, wom_ref, wout_ref, g2_ref,
                  wrt_ref, brt_ref, tri_ref, ewg_ref, ewu_ref, ewd_ref,
                  x1_ref, rt_ref, cnt_ref, ewg_out, ewu_out, ewd_out,
                  k0_ref, k1_ref, v0_ref, v1_ref, ubuf_ref, ybuf_ref, o_ref, om_ref, x1prev_ref,
                  cosd_ref, sind_ref, *, tiles_per_seq):
    s = pl.program_id(0)
    n_tiles = pl.num_programs(0) - 1
    i = jnp.minimum(s, n_tiles - 1) % tiles_per_seq
    kbands = (k0_ref, k1_ref)
    vbands = (v0_ref, v1_ref)

    @pl.when(s == 0)
    def _():
        x1prev_ref[...] = jnp.zeros((TQ, D_MODEL), F32)

    @pl.when(i == 0)
    def _():
        for r in kbands + vbands:
            r[0:WINDOW, :] = jnp.zeros((WINDOW, LANES), BF16)
        ubuf_ref[:, 0:CONV_HALO, :] = jnp.zeros((CONV_CH // LANES, CONV_HALO, LANES), F32)

    x1_prev = x1prev_ref[...]
    x1_ref[0, :, 0:D_MODEL] = x1_prev

    ewg_out[...] = ewg_ref[...].astype(BF16)
    ewu_out[...] = ewu_ref[...].astype(BF16)
    ewd_out[...] = ewd_ref[...].astype(BF16)

    x = x_ref[0]
    h = (_rms(x) * g1_ref[...]).astype(BF16)

    n_slabs = CONV_CH // LANES
    half = TQ // 2
    group = ATT_HEADS // ATT_KV_HEADS

    glu = _dot(h, win_ref[:, COL_GLU:COL_XQ])
    route_logits = _router_logits(x1_prev, g2_ref, wrt_ref, brt_ref)
    u = glu[:, :CONV_CH] * jax.nn.sigmoid(glu[:, CONV_CH:])
    for c in range(n_slabs):
        ubuf_ref[c, CONV_HALO:CONV_HALO + TQ, :] = u[:, c * LANES:(c + 1) * LANES]
    base = CONV_HALO - (CONV_WIDTH - 1)
    acc = [[None] * n_slabs for _ in range(2)]

    def conv_slab(c):
        cols = slice(c * LANES, (c + 1) * LANES)
        for par in range(2):
            parts = []
            for q0 in range(0, half, CONV_ROWS):
                a = jnp.broadcast_to(bdw_ref[:, cols], (CONV_ROWS, LANES))
                for j in range(CONV_WIDTH):
                    a = a + wdw_ref[j:j + 1, cols] * ubuf_ref[
                        c, pl.ds(base + j + par + 2 * q0, CONV_ROWS, stride=2), :]
                parts.append(a)
            acc[par][c] = jnp.concatenate(parts, axis=0)
        ubuf_ref[c, 0:CONV_HALO, :] = ubuf_ref[c, TQ:TQ + CONV_HALO, :]

    def gate(b):
        return jax.nn.sigmoid(_dot(h, win_ref[:, COL_GATE + b * D_MODEL:COL_GATE + (b + 1) * D_MODEL]))

    def conv_finish():
        for par in range(2):
            row_sum = functools.reduce(jnp.add, [jnp.sum(a, axis=-1, keepdims=True) for a in acc[par]])
            mu = row_sum * (1.0 / CONV_CH)
            cen = [a - mu for a in acc[par]]
            sq_sum = functools.reduce(jnp.add, [jnp.sum(t * t, axis=-1, keepdims=True) for t in cen])
            rstd = lax.rsqrt(sq_sum * (1.0 / CONV_CH) + LN_EPS)
            for c in range(n_slabs):
                cols = slice(c * LANES, (c + 1) * LANES)
                yln = cen[c] * rstd * gln_ref[:, cols] + bln_ref[:, cols]
                ybuf_ref[c, pl.ds(par, half, stride=2), :] = yln * jax.nn.sigmoid(yln)
        uo = jnp.concatenate([ybuf_ref[c] for c in range(n_slabs)], axis=-1).astype(BF16)
        return _dot(uo, wco_ref[...])

    for c in range(n_slabs):
        conv_slab(c)
    qkv = _dot(h, win_ref[:, 0:COL_GLU])
    gates = [gate(0)]
    q = qkv[:, :Q_WIDTH]
    k = qkv[:, Q_WIDTH:Q_WIDTH + KV_WIDTH]
    v = qkv[:, Q_WIDTH + KV_WIDTH:]
    gates.append(gate(1))

    lane = lax.broadcasted_iota(jnp.int32, (TQ, LANES), 1)
    first_half = (lane % HEAD_DIM) < (HEAD_DIM // 2)
    low_head = lane < HEAD_DIM
    n_freq = HEAD_DIM // 2
    per_row = LANES // n_freq
    quarter = lax.broadcasted_iota(jnp.int32, (TQ // per_row, LANES), 1) // n_freq
    sign = jnp.where(quarter % 2 == 0, -1.0, 1.0)
    for table_ref, out_ref, scale in ((cos_ref, cosd_ref, None), (sin_ref, sind_ref, sign)):
        dense = table_ref[0]
        rolled = [dense] + [pltpu.roll(dense, n_freq * kk, 1) for kk in range(1, per_row)]
        for j in range(per_row):
            spread = rolled[(-j) % per_row]
            for qq in range(1, per_row):
                spread = jnp.where(quarter == qq, rolled[(qq - j) % per_row], spread)
            out_ref[pl.ds(j, TQ // per_row, stride=per_row), :] = spread if scale is None else spread * scale
    cosd = cosd_ref[...]
    sind = sind_ref[...]

    def rope(t):
        rot = jnp.where(first_half, pltpu.roll(t, LANES - HEAD_DIM // 2, 1),
                        pltpu.roll(t, HEAD_DIM // 2, 1))
        return t * cosd + rot * sind

    def dup_halves(t):
        swapped = pltpu.roll(t, HEAD_DIM, 1)
        return jnp.where(low_head, t, swapped), jnp.where(low_head, swapped, t)

    def qk_norm(t, gain):
        sq = t * t
        lo = jnp.sum(jnp.where(low_head, sq, 0.0), axis=-1, keepdims=True)
        hi = jnp.sum(jnp.where(low_head, 0.0, sq), axis=-1, keepdims=True)
        inv = jnp.where(low_head, lax.rsqrt(lo * (1.0 / HEAD_DIM) + EPS), lax.rsqrt(hi * (1.0 / HEAD_DIM) + EPS))
        return t * inv * gain

    kd = dup_halves(rope(qk_norm(k, gk_ref[...])))
    vd = dup_halves(v)
    for kvh in range(ATT_KV_HEADS):
        kbands[kvh][WINDOW:WINDOW + TQ, :] = kd[kvh].astype(BF16)
        vbands[kvh][WINDOW:WINDOW + TQ, :] = vd[kvh].astype(BF16)

    q_heads = []
    for c in range(Q_WIDTH // LANES):
        cols = slice(c * LANES, (c + 1) * LANES)
        qc = rope(qk_norm(q[:, cols], gq_ref[:, cols]))
        q_heads.append(jnp.where(low_head, qc, 0.0).astype(BF16))
        q_heads.append(jnp.where(low_head, 0.0, qc).astype(BF16))

    qi = lax.broadcasted_iota(jnp.int32, (WINDOW, 2 * WINDOW), 0)
    kj = lax.broadcasted_iota(jnp.int32, (WINDOW, 2 * WINDOW), 1)
    band_mask = (kj > qi) & (kj <= qi + WINDOW)
    first_mask = band_mask & (kj >= jnp.where(i > 0, 0, WINDOW))
    low_head_w = lax.broadcasted_iota(jnp.int32, (WINDOW, LANES), 1) < HEAD_DIM

    def attention_scores(step):
        n, kvh = divmod(step, ATT_KV_HEADS)
        rows = slice(n * WINDOW, (n + 1) * WINDOW)
        mask = first_mask if n == 0 else band_mask
        heads = range(kvh * group, (kvh + 1) * group)
        q_st = jnp.concatenate([q_heads[hd][rows] for hd in heads], axis=0)
        kband = kbands[kvh][n * WINDOW:n * WINDOW + 2 * WINDOW, :]
        s_all = _dot_nt(q_st, kband)
        es, dens = [], []
        for j, hd in enumerate(heads):
            s = jnp.where(mask, s_all[j * WINDOW:(j + 1) * WINDOW], NEG_INF)
            sink = sinks_ref[hd]
            m = jnp.maximum(jnp.max(s, axis=-1, keepdims=True), sink)
            e = jnp.exp2(s - m)
            dens.append(jnp.sum(e, axis=-1, keepdims=True) + jnp.exp2(sink - m))
            es.append(e.astype(BF16))
        return jnp.concatenate(es, axis=0), dens

    def attention_output(step, probs):
        n, kvh = divmod(step, ATT_KV_HEADS)
        rows = slice(n * WINDOW, (n + 1) * WINDOW)
        e_all, dens = probs
        vband = vbands[kvh][n * WINDOW:n * WINDOW + 2 * WINDOW, :]
        o_all = _dot(e_all, vband)
        for cc in range(group // 2):
            oa = o_all[(2 * cc) * WINDOW:(2 * cc + 1) * WINDOW] / dens[2 * cc]
            ob = o_all[(2 * cc + 1) * WINDOW:(2 * cc + 2) * WINDOW] / dens[2 * cc + 1]
            col = kvh * (group // 2) + cc
            o_ref[rows, col * LANES:(col + 1) * LANES] = jnp.where(low_head_w, oa, ob).astype(BF16)

    def mem_scores(hd, xq):
        cols = slice(hd * MEM_HEAD_DIM, (hd + 1) * MEM_HEAD_DIM)
        s = _dot_nt(xq[:, cols], mk_ref[0, :, cols])
        e = jnp.exp2(s - jnp.max(s, axis=-1, keepdims=True))
        return e.astype(BF16), jnp.sum(e, axis=-1, keepdims=True)

    def mem_output(hd, probs):
        cols = slice(hd * MEM_HEAD_DIM, (hd + 1) * MEM_HEAD_DIM)
        e, den = probs
        om_ref[:, cols] = (_dot(e, mv_ref[0, :, cols]) / den).astype(BF16)

    route = _router_select(route_logits)
    p0 = attention_scores(0)
    gates.append(gate(2))
    p1 = attention_scores(1)
    attention_output(0, p0)
    xq = _dot(h, win_ref[:, COL_XQ:COL_GATE])
    p2 = attention_scores(2)
    attention_output(1, p1)
    _router_emit(*route, tri_ref, x1_ref, rt_ref, cnt_ref)
    xq = (_rms_per_mem_head(xq) * gxq_ref[...]).astype(BF16)
    p3 = attention_scores(3)
    attention_output(2, p2)
    y_conv = conv_finish()
    m0 = mem_scores(0, xq)
    attention_output(3, p3)
    for r in kbands + vbands:
        r[0:WINDOW, :] = r[TQ:TQ + WINDOW, :]
    m1 = mem_scores(1, xq)
    y_attn = _dot(o_ref[...], woa_ref[...])
    mem_output(0, m0)
    m2 = mem_scores(2, xq)
    mem_output(1, m1)
    m3 = mem_scores(3, xq)
    mem_output(2, m2)
    merged = gates[0] * y_attn + gates[1] * y_conv
    mem_output(3, m3)
    merged = merged + gates[2] * _dot(om_ref[...], wom_ref[...])

    x1prev_ref[...] = x + _dot(merged.astype(BF16), wout_ref[...])


def _router_logits(x1, g2_ref, wrt_ref, brt_ref):
    h2 = _rms(x1) * g2_ref[...]
    h_hi, h_lo = _split_bf16(h2)
    w_hi, w_lo = _split_bf16(wrt_ref[...])
    return _dot_nt(w_hi, h_hi) + _dot_nt(w_hi, h_lo) + _dot_nt(w_lo, h_hi) + brt_ref[...]


def _router_select(lt):
    gl = [lt[g:g + 1] for g in range(N_GROUPS)]
    gmax = functools.reduce(jnp.maximum, gl)
    g_idx = jnp.full((1, TQ), N_GROUPS - 1, jnp.int32)
    for g in reversed(range(N_GROUPS - 1)):
        g_idx = jnp.where(gl[g] == gmax, g, g_idx)
    p_g = 1.0 / functools.reduce(jnp.add, [jnp.exp(r - gmax) for r in gl])
    el = []
    for kk in range(EXPERTS_PER_GROUP):
        row = lt[8 + kk:9 + kk]
        for g in range(1, N_GROUPS):
            off = 8 + g * EXPERTS_PER_GROUP + kk
            row = jnp.where(g_idx == g, lt[off:off + 1], row)
        el.append(row)

    def first_argmax(rows):
        top = functools.reduce(jnp.maximum, rows)
        idx = jnp.full((1, TQ), len(rows) - 1, jnp.int32)
        for kk in reversed(range(len(rows) - 1)):
            idx = jnp.where(rows[kk] == top, kk, idx)
        return top, idx

    v1, i1 = first_argmax(el)
    v2, i2 = first_argmax([jnp.where(i1 == kk, -jnp.inf, el[kk]) for kk in range(EXPERTS_PER_GROUP)])
    t = jnp.exp(v2 - v1)
    p1 = 1.0 / (1.0 + t)
    p2 = t * p1

    lo = jnp.minimum(i1, i2)
    hi = jnp.maximum(i1, i2)
    w_lo = jnp.where(i1 < i2, p_g * p1, p_g * p2)
    w_hi = jnp.where(i1 < i2, p_g * p2, p_g * p1)
    cls = g_idx * N_PAIRS + ((lo * (2 * EXPERTS_PER_GROUP - 1 - lo)) >> 1) + hi - lo - 1
    return cls, w_lo, w_hi


def _router_emit(cls, w_lo, w_hi, tri_ref, x1_ref, rt_ref, cnt_ref):
    cls_f = cls.astype(F32)
    onehot = jnp.broadcast_to(cls, (CLS_ROWS, TQ)) == lax.broadcasted_iota(jnp.int32, (CLS_ROWS, TQ), 0)
    onehot_bf = onehot.astype(F32).astype(BF16)
    prefix = _dot(onehot_bf, tri_ref[...])
    rank = jnp.sum(jnp.where(onehot, prefix, 0.0), axis=0, keepdims=True) - 1.0
    counts = _dot_nt(jnp.ones((8, TQ), BF16), onehot_bf)
    zero = jnp.zeros((1, TQ), F32)
    rt_ref[0] = jnp.concatenate([cls_f, rank] + [zero] * 6, axis=0)
    cnt_ref[0] = jnp.concatenate([counts, jnp.zeros((8, LANES - CLS_ROWS), F32)], axis=1)
    cols = jnp.concatenate([w_lo, w_hi, cls_f, jnp.zeros((LANES - 3, TQ), F32)], axis=0)
    x1_ref[0, :, D_MODEL:] = cols.T


FLAG_FIRST, FLAG_VALID = 1, 2
GATHER_SLOTS = 3


def _moe_kernel(dest_ref, tile_ref, elo_ref, ehi_ref, cls_ref, flag_ref,
                g2_ref, wg_lo, wu_lo, wd_lo, wg_hi, wu_hi, wd_hi, x1_hbm,
                out_hbm,
                perm_ref, xbuf, obuf, h2buf, wbuf_up, wbuf_down, gsem, ssem):
    k = pl.program_id(0)
    nk = pl.num_programs(0)
    t = tile_ref[k]
    xslot = t % GATHER_SLOTS
    oslot = t % 2
    flags = flag_ref[k]
    n_tok = x1_hbm.shape[0]
    n_tiles = n_tok // TM

    def gather_copy(r, tile, s):
        tok = perm_ref[tile * TM + r]
        return pltpu.make_async_copy(x1_hbm.at[pl.ds(tok, 1), :], xbuf.at[s, pl.ds(r, 1), :], gsem.at[s])

    def scatter_copy(r, tile, s):
        tok = perm_ref[tile * TM + r]
        return pltpu.make_async_copy(obuf.at[s, pl.ds(r, 1), :], out_hbm.at[pl.ds(tok, 1), :], ssem.at[s])

    k_prev = jnp.maximum(k - 1, 0)
    for side, (e_ref, wg, wu, wd) in enumerate(((elo_ref, wg_lo, wu_lo, wd_lo), (ehi_ref, wg_hi, wu_hi, wd_hi))):
        @pl.when((k == 0) | (e_ref[k] != e_ref[k_prev]))
        def _(side=side, wg=wg, wu=wu, wd=wd):
            wbuf_up[2 * side] = wg[0]
            wbuf_up[2 * side + 1] = wu[0]
            wbuf_down[side] = wd[0]

    def for_rows(fn):
        def body(r, carry):
            fn(r)
            return carry
        lax.fori_loop(0, TM, body, 0, unroll=8)

    @pl.when(k == 0)
    def _():
        def invert(tok, carry):
            perm_ref[dest_ref[tok]] = tok
            return carry
        lax.fori_loop(0, n_tok, invert, 0, unroll=16)
        for tile in range(GATHER_SLOTS - 1):
            for_rows(lambda r, tile=tile: gather_copy(r, tile, tile).start())

    is_first = (flags & FLAG_FIRST) != 0
    is_valid = (flags & FLAG_VALID) != 0
    ahead = GATHER_SLOTS - 1
    prefetch_tile = jnp.minimum(t + ahead, n_tiles - 1)
    prefetch_slot = (t + ahead) % GATHER_SLOTS

    def gather_tile_wait(s):
        pltpu.make_async_copy(x1_hbm.at[pl.ds(0, TM), :], xbuf.at[s], gsem.at[s]).wait()

    def scatter_tile_wait(s):
        pltpu.make_async_copy(obuf.at[s], out_hbm.at[pl.ds(0, TM), :], ssem.at[s]).wait()

    @pl.when(is_first)
    def _():
        gather_tile_wait(xslot)

        @pl.when(t >= 2)
        def _():
            scatter_tile_wait(oslot)

        xr = xbuf[xslot, :, 0:D_MODEL]
        h2buf[...] = (_rms(xr) * g2_ref[...]).astype(BF16)
        obuf[oslot] = xr

    def experts():
        w = xbuf[xslot, :, D_MODEL:XW]
        mine = w[:, 2:3] == cls_ref[k].astype(F32)
        h2 = h2buf[...]
        y = None
        for side in range(2):
            c = jnp.where(mine, w[:, side:side + 1], 0.0)
            hid = jax.nn.silu(_dot(h2, wbuf_up[2 * side])) * _dot(h2, wbuf_up[2 * side + 1]) * c
            part = _dot(hid.astype(BF16), wbuf_down[side])
            y = part if y is None else y + part
        obuf[oslot] += y

    @pl.when(is_first & (t > 0))
    def _():
        for r in range(TM):
            gather_copy(r, prefetch_tile, prefetch_slot).start()
            scatter_copy(r, t - 1, 1 - oslot).start()
        experts()

    @pl.when(is_first & (t == 0))
    def _():
        for r in range(TM):
            gather_copy(r, prefetch_tile, prefetch_slot).start()
        experts()

    @pl.when(is_valid & jnp.logical_not(is_first))
    def _():
        experts()

    @pl.when(k == nk - 1)
    def _():
        last = n_tiles - 1
        for_rows(lambda r: scatter_copy(r, last, last % 2).start())
        for tile in (last - 1, last):
            scatter_tile_wait(tile % 2)
        for extra in range(1, GATHER_SLOTS):
            gather_tile_wait((last + extra) % GATHER_SLOTS)


_CLASS_PAIRS = [(a, b) for a in range(EXPERTS_PER_GROUP) for b in range(a + 1, EXPERTS_PER_GROUP)]


def _routing_plan(cls, rank, tile_counts, n_tiles):
    counts = jnp.sum(tile_counts, axis=0)
    off = jnp.concatenate([jnp.zeros((1,), jnp.int32), jnp.cumsum(counts).astype(jnp.int32)])
    tile_base = jnp.cumsum(tile_counts, axis=0) - tile_counts + off[None, :-1]
    onehot = cls[:, :, None] == jnp.arange(N_CLASSES, dtype=jnp.int32)[None, None, :]
    dest = (jnp.sum(jnp.where(onehot, tile_base[:, None, :], 0), axis=-1) + rank).reshape(-1)
    first_tile = off[:-1] // TM
    last_tile = (off[1:] - 1) // TM
    n_items = jnp.where(counts > 0, last_tile - first_tile + 1, 0)
    istart = jnp.concatenate([jnp.zeros((1,), jnp.int32), jnp.cumsum(n_items).astype(jnp.int32)])
    total = istart[-1]
    ni = n_tiles + N_CLASSES - 1
    k = jnp.arange(ni, dtype=jnp.int32)
    valid = k < total
    kc = jnp.minimum(k, total - 1)
    c_of_k = jnp.minimum(jnp.sum(istart[None, 1:] <= kc[:, None], axis=1).astype(jnp.int32), N_CLASSES - 1)
    tile_k = first_tile[c_of_k] + kc - istart[c_of_k]
    prev_tile = jnp.concatenate([jnp.full((1,), -1, jnp.int32), tile_k[:-1]])
    first = valid & (tile_k != prev_tile)
    flags = (first * FLAG_FIRST + valid * FLAG_VALID).astype(jnp.int32)
    pair_lo = jnp.array([p[0] for p in _CLASS_PAIRS], jnp.int32)
    pair_hi = jnp.array([p[1] for p in _CLASS_PAIRS], jnp.int32)
    base = (c_of_k // N_PAIRS) * EXPERTS_PER_GROUP
    e_lo = base + pair_lo[c_of_k % N_PAIRS]
    e_hi = base + pair_hi[c_of_k % N_PAIRS]
    cls_k = jnp.where(valid, c_of_k, -1)
    return dest.astype(jnp.int32), tile_k, e_lo, e_hi, cls_k, flags


def _const_spec(shape):
    nd = len(shape)
    return pl.BlockSpec(shape, lambda *_: (0,) * nd, pipeline_mode=pl.Buffered(1))


def kernel(x, mem, positions, g_norm1, w_in, g_q, g_k, sinks, w_o_attn, w_conv_dw, b_conv_dw, g_conv_ln, b_conv_ln, w_conv_out, g_mem, w_kv_mem, g_xq, g_xk, w_o_mem, w_out, g_norm2, w_group, b_group, w_router, b_router, w_gate, w_up, w_down):
    B, S, D = x.shape
    M = mem.shape[1]
    assert D == D_MODEL and S % TQ == 0 and (B * S) % TM == 0 and w_in.shape[0] == 1
    assert (TQ // WINDOW) * ATT_KV_HEADS == MEM_HEADS and CONV_CH // LANES == N_BRANCHES + 1
    NT = S // TQ
    T = B * S
    l = 0

    row = lambda v: v.reshape(1, -1).astype(F32)

    mk, mv = pl.pallas_call(
        _memkv_kernel,
        grid=(B,),
        in_specs=[pl.BlockSpec((1, M, D), lambda b: (b, 0, 0)),
                  _const_spec((1, D)), _const_spec((D, 2 * XQ_WIDTH)),
                  _const_spec((1, XQ_WIDTH))],
        out_specs=[pl.BlockSpec((1, M, XQ_WIDTH), lambda b: (b, 0, 0))] * 2,
        out_shape=[jax.ShapeDtypeStruct((B, M, XQ_WIDTH), BF16)] * 2,
        compiler_params=pltpu.CompilerParams(dimension_semantics=("arbitrary",)),
        name="memkv",
    )(mem, row(g_mem[l]), w_kv_mem[l].astype(BF16), row(jnp.tile(g_xk[l], MEM_HEADS)))

    inv_freq = 1.0 / (ROPE_THETA ** (jnp.arange(0, HEAD_DIM, 2, dtype=F32) / HEAD_DIM))
    per_row = LANES // (HEAD_DIM // 2)
    ang = positions.astype(F32).reshape(B, S // per_row, per_row, 1) * inv_freq
    ang = ang.reshape(B, S // per_row, LANES)
    cos_tab, sin_tab = jnp.cos(ang), jnp.sin(ang)

    assert w_in.shape[-1] == IN_WIDTH
    wdw = jnp.zeros((CONV_HALO, CONV_CH), F32).at[:CONV_WIDTH].set(w_conv_dw[l])
    w_rt = jnp.zeros((RT_ROWS, D), F32).at[0:N_GROUPS].set(w_group[l].T).at[8:8 + N_EXPERTS].set(w_router[l].T)
    b_rt = jnp.zeros((RT_ROWS, 1), F32).at[0:N_GROUPS, 0].set(b_group[l]).at[8:8 + N_EXPERTS, 0].set(b_router[l])

    n_mix = B * NT
    tile_in = lambda s: jnp.minimum(s, n_mix - 1)
    tile_out = lambda s: jnp.maximum(s - 1, 0)
    tile3 = lambda last: pl.BlockSpec((1, TQ, last), lambda s: (tile_in(s) // NT, tile_in(s) % NT, 0))
    per_batch = pl.BlockSpec((1, M, XQ_WIDTH), lambda s: (tile_in(s) // NT, 0, 0))
    rope_spec = pl.BlockSpec((1, TQ // per_row, LANES), lambda s: (tile_in(s) // NT, tile_in(s) % NT, 0))
    in_specs = [
        tile3(D), rope_spec, rope_spec, per_batch, per_batch,
        _const_spec((1, D)),
        _const_spec((D, IN_WIDTH)),
        _const_spec((1, Q_WIDTH)), _const_spec((1, LANES)),
        pl.BlockSpec(memory_space=pltpu.SMEM),
        _const_spec((Q_WIDTH, D)),
        _const_spec((CONV_HALO, CONV_CH)), _const_spec((1, CONV_CH)), _const_spec((1, CONV_CH)), _const_spec((1, CONV_CH)),
        _const_spec((CONV_CH, D)),
        _const_spec((1, XQ_WIDTH)),
        _const_spec((XQ_WIDTH, D)),
        _const_spec((D, D)),
        _const_spec((1, D)),
        _const_spec((RT_ROWS, D)), _const_spec((RT_ROWS, 1)), _const_spec((TQ, TQ)),
    ]
    tri = (jnp.arange(TQ)[:, None] <= jnp.arange(TQ)[None, :]).astype(BF16)
    per_tile = lambda rows, last: pl.BlockSpec((1, rows, last), lambda s: (tile_out(s), 0, 0))
    x1_spec = pl.BlockSpec((1, TQ, XW), lambda s: (tile_out(s) // NT, tile_out(s) % NT, 0))
    assert (N_EXPERTS * D) % n_mix == 0 and (N_EXPERTS * EXPERT_FF) % n_mix == 0
    up_rows, down_rows = N_EXPERTS * D // n_mix, N_EXPERTS * EXPERT_FF // n_mix
    up_slice = pl.BlockSpec((up_rows, EXPERT_FF), lambda s: (tile_in(s), 0))
    down_slice = pl.BlockSpec((down_rows, D), lambda s: (tile_in(s), 0))
    in_specs += [up_slice, up_slice, down_slice]
    x1, rt, cnt, wg, wu, wd = pl.pallas_call(
        functools.partial(_mixer_kernel, tiles_per_seq=NT),
        grid=(n_mix + 1,),
        in_specs=in_specs,
        out_specs=[x1_spec, per_tile(8, TQ), per_tile(8, LANES), up_slice, up_slice, down_slice],
        out_shape=[jax.ShapeDtypeStruct((B, S, XW), F32), jax.ShapeDtypeStruct((B * NT, 8, TQ), F32),
                   jax.ShapeDtypeStruct((B * NT, 8, LANES), F32),
                   jax.ShapeDtypeStruct((N_EXPERTS * D, EXPERT_FF), BF16),
                   jax.ShapeDtypeStruct((N_EXPERTS * D, EXPERT_FF), BF16),
                   jax.ShapeDtypeStruct((N_EXPERTS * EXPERT_FF, D), BF16)],
        scratch_shapes=[pltpu.VMEM((WINDOW + TQ, LANES), BF16)] * 4
        + [pltpu.VMEM((CONV_CH // LANES, CONV_HALO + TQ, LANES), F32),
           pltpu.VMEM((CONV_CH // LANES, TQ, LANES), F32), pltpu.VMEM((TQ, Q_WIDTH), BF16),
           pltpu.VMEM((TQ, XQ_WIDTH), BF16), pltpu.VMEM((TQ, D), F32),
           pltpu.VMEM((TQ, LANES), F32), pltpu.VMEM((TQ, LANES), F32)],
        compiler_params=pltpu.CompilerParams(dimension_semantics=("arbitrary",),
                                             vmem_limit_bytes=VMEM_LIMIT),
        name="mixer",
    )(x, cos_tab, sin_tab, mk, mv, row(g_norm1[l]), w_in[l].astype(BF16),
      row(jnp.tile(g_q[l], ATT_HEADS) * (HEAD_DIM ** -0.5 * LOG2E)), row(jnp.tile(g_k[l], ATT_KV_HEADS)),
      sinks[l].astype(F32) * LOG2E, w_o_attn[l].astype(BF16),
      wdw, row(b_conv_dw[l]), row(g_conv_ln[l]), row(b_conv_ln[l]), w_conv_out[l].astype(BF16),
      row(jnp.tile(g_xq[l], MEM_HEADS) * (MEM_HEAD_DIM ** -0.5 * LOG2E)), w_o_mem[l].astype(BF16),
      w_out[l].astype(BF16), row(g_norm2[l]), w_rt, b_rt, tri,
      w_gate[l].reshape(N_EXPERTS * D, EXPERT_FF), w_up[l].reshape(N_EXPERTS * D, EXPERT_FF),
      w_down[l].reshape(N_EXPERTS * EXPERT_FF, D))

    n_tiles = T // TM
    plan = _routing_plan(rt[:, 0, :].astype(jnp.int32), rt[:, 1, :].astype(jnp.int32),
                         cnt[:, 0, :N_CLASSES].astype(jnp.int32), n_tiles)
    dest, tile_k, e_lo, e_hi, cls_k, flags = plan
    ni = tile_k.shape[0]
    wg = wg.reshape(N_EXPERTS, D, EXPERT_FF)
    wu = wu.reshape(N_EXPERTS, D, EXPERT_FF)
    wd = wd.reshape(N_EXPERTS, EXPERT_FF, D)
    lo_map = lambda k, dest, tile, elo, ehi, cls, flg: (elo[k], 0, 0)
    hi_map = lambda k, dest, tile, elo, ehi, cls, flg: (ehi[k], 0, 0)
    up_block = (1, D, EXPERT_FF)
    down_block = (1, EXPERT_FF, D)
    out = pl.pallas_call(
        _moe_kernel,
        grid_spec=pltpu.PrefetchScalarGridSpec(
            num_scalar_prefetch=6,
            grid=(ni,),
            in_specs=[pl.BlockSpec((1, D), lambda k, *_: (0, 0)),
                      pl.BlockSpec(up_block, lo_map), pl.BlockSpec(up_block, lo_map),
                      pl.BlockSpec(down_block, lo_map),
                      pl.BlockSpec(up_block, hi_map), pl.BlockSpec(up_block, hi_map),
                      pl.BlockSpec(down_block, hi_map),
                      pl.BlockSpec(memory_space=pl.ANY)],
            out_specs=pl.BlockSpec(memory_space=pl.ANY),
            scratch_shapes=[pltpu.SMEM((T,), jnp.int32),
                            pltpu.VMEM((GATHER_SLOTS, TM, XW), F32), pltpu.VMEM((2, TM, D), F32),
                            pltpu.VMEM((TM, D), BF16),
                            pltpu.VMEM((4, D, EXPERT_FF), BF16), pltpu.VMEM((2, EXPERT_FF, D), BF16),
                            pltpu.SemaphoreType.DMA((GATHER_SLOTS,)), pltpu.SemaphoreType.DMA((2,))]),
        out_shape=jax.ShapeDtypeStruct((T, D), F32),
        compiler_params=pltpu.CompilerParams(dimension_semantics=("arbitrary",),
                                             vmem_limit_bytes=VMEM_LIMIT),
        name="moe",
    )(dest, tile_k, e_lo, e_hi, cls_k, flags, row(g_norm2[l]), wg, wu, wd, wg, wu, wd,
      x1.reshape(T, XW))
    return out.reshape(B, S, D)
```

```python
import functools

import jax
import jax.numpy as jnp
from jax import lax
from jax.experimental import pallas as pl
from jax.experimental.pallas import tpu as pltpu

F32 = jnp.float32
BF16 = jnp.bfloat16

D_MODEL = 1024
ATT_HEADS = 8
ATT_KV_HEADS = 2
HEAD_DIM = 64
WINDOW = 128
ROPE_THETA = 10000.0
CONV_CH = 512
CONV_WIDTH = 31
MEM_HEADS = 4
MEM_HEAD_DIM = 128
N_BRANCHES = 3
N_GROUPS = 4
EXPERTS_PER_GROUP = 4
N_EXPERTS = N_GROUPS * EXPERTS_PER_GROUP
EXPERT_FF = 512
EPS = 1e-6
LN_EPS = 1e-5
NEG_INF = -1e30
LOG2E = 1.4426950408889634

Q_WIDTH = ATT_HEADS * HEAD_DIM
KV_WIDTH = ATT_KV_HEADS * HEAD_DIM
GLU_WIDTH = 2 * CONV_CH
XQ_WIDTH = MEM_HEADS * MEM_HEAD_DIM
GATE_WIDTH = N_BRANCHES * D_MODEL
COL_GLU = Q_WIDTH + 2 * KV_WIDTH
COL_XQ = COL_GLU + GLU_WIDTH
COL_GATE = COL_XQ + XQ_WIDTH
IN_WIDTH = COL_GATE + GATE_WIDTH

LANES = 128
TQ = 256
CONV_HALO = 32
CONV_ROWS = 128
TM = 256
RT_ROWS = 32
N_PAIRS = EXPERTS_PER_GROUP * (EXPERTS_PER_GROUP - 1) // 2
N_CLASSES = N_GROUPS * N_PAIRS
CLS_ROWS = 32
XW = D_MODEL + LANES
VMEM_LIMIT = 56 * 1024 * 1024


def _dot(a, b):
    return jnp.dot(a, b, preferred_element_type=F32)


def _dot_nt(a, b):
    return lax.dot_general(a, b, (((1,), (1,)), ((), ())), preferred_element_type=F32)


def _split_bf16(t):
    hi = t.astype(BF16)
    lo = (t - hi.astype(F32)).astype(BF16)
    return hi, lo


def _rms(t):
    return t * lax.rsqrt(jnp.mean(t * t, axis=-1, keepdims=True) + EPS)


def _rms_per_mem_head(t):
    return jnp.concatenate(
        [_rms(t[:, hd * MEM_HEAD_DIM:(hd + 1) * MEM_HEAD_DIM]) for hd in range(MEM_HEADS)], axis=-1)


def _memkv_kernel(mem_ref, gmem_ref, wkv_ref, gxk_ref, mk_ref, mv_ref):
    mn = (_rms(mem_ref[0]) * gmem_ref[...]).astype(BF16)
    kv = _dot(mn, wkv_ref[...])
    mk_ref[0] = (_rms_per_mem_head(kv[:, :XQ_WIDTH]) * gxk_ref[...]).astype(BF16)
    mv_ref[0] = kv[:, XQ_WIDTH:].astype(BF16)


def _mixer_kernel(x_ref, cos_ref, sin_ref, mk_ref, mv_ref, g1_ref, win_ref,
                  gq_ref, gk_ref, sinks_ref, woa_ref, wdw_ref,
                  bdw_ref, gln_ref, bln_ref, wco_ref, gxq_ref, wom_ref, wout_ref, g2_ref,
                  wrt_ref, brt_ref, tri_ref, ewg_ref, ewu_ref, ewd_ref,
                  x1_ref, rt_ref, cnt_ref, ewg_out, ewu_out, ewd_out,
                  k0_ref, k1_ref, v0_ref, v1_ref, ubuf_ref, ybuf_ref, o_ref, om_ref, x1prev_ref,
                  cosd_ref, sind_ref, *, tiles_per_seq):
    s = pl.program_id(0)
    n_tiles = pl.num_programs(0) - 1
    i = jnp.minimum(s, n_tiles - 1) % tiles_per_seq
    kbands = (k0_ref, k1_ref)
    vbands = (v0_ref, v1_ref)

    @pl.when(s == 0)
    def _():
        x1prev_ref[...] = jnp.zeros((TQ, D_MODEL), F32)

    @pl.when(i == 0)
    def _():
        for r in kbands + vbands:
            r[0:WINDOW, :] = jnp.zeros((WINDOW, LANES), BF16)
        ubuf_ref[:, 0:CONV_HALO, :] = jnp.zeros((CONV_CH // LANES, CONV_HALO, LANES), F32)

    x1_prev = x1prev_ref[...]
    x1_ref[0, :, 0:D_MODEL] = x1_prev

    ewg_out[...] = ewg_ref[...].astype(BF16)
    ewu_out[...] = ewu_ref[...].astype(BF16)
    ewd_out[...] = ewd_ref[...].astype(BF16)

    x = x_ref[0]
    h = (_rms(x) * g1_ref[...]).astype(BF16)

    n_slabs = CONV_CH // LANES
    half = TQ // 2
    group = ATT_HEADS // ATT_KV_HEADS

    glu = _dot(h, win_ref[:, COL_GLU:COL_XQ])
    route_logits = _router_logits(x1_prev, g2_ref, wrt_ref, brt_ref)
    u = glu[:, :CONV_CH] * jax.nn.sigmoid(glu[:, CONV_CH:])
    for c in range(n_slabs):
        ubuf_ref[c, CONV_HALO:CONV_HALO + TQ, :] = u[:, c * LANES:(c + 1) * LANES]
    base = CONV_HALO - (CONV_WIDTH - 1)
    acc = [[None] * n_slabs for _ in range(2)]

    def conv_slab(c):
        cols = slice(c * LANES, (c + 1) * LANES)
        for par in range(2):
            parts = []
            for q0 in range(0, half, CONV_ROWS):
                a = jnp.broadcast_to(bdw_ref[:, cols], (CONV_ROWS, LANES))
                for j in range(CONV_WIDTH):
                    a = a + wdw_ref[j:j + 1, cols] * ubuf_ref[
                        c, pl.ds(base + j + par + 2 * q0, CONV_ROWS, stride=2), :]
                parts.append(a)
            acc[par][c] = jnp.concatenate(parts, axis=0)
        ubuf_ref[c, 0:CONV_HALO, :] = ubuf_ref[c, TQ:TQ + CONV_HALO, :]

    def gate(b):
        return jax.nn.sigmoid(_dot(h, win_ref[:, COL_GATE + b * D_MODEL:COL_GATE + (b + 1) * D_MODEL]))

    def conv_finish():
        for par in range(2):
            row_sum = functools.reduce(jnp.add, [jnp.sum(a, axis=-1, keepdims=True) for a in acc[par]])
            mu = row_sum * (1.0 / CONV_CH)
            cen = [a - mu for a in acc[par]]
            sq_sum = functools.reduce(jnp.add, [jnp.sum(t * t, axis=-1, keepdims=True) for t in cen])
            rstd = lax.rsqrt(sq_sum * (1.0 / CONV_CH) + LN_EPS)
            for c in range(n_slabs):
                cols = slice(c * LANES, (c + 1) * LANES)
                yln = cen[c] * rstd * gln_ref[:, cols] + bln_ref[:, cols]
                ybuf_ref[c, pl.ds(par, half, stride=2), :] = yln * jax.nn.sigmoid(yln)
        uo = jnp.concatenate([ybuf_ref[c] for c in range(n_slabs)], axis=-1).astype(BF16)
        return _dot(uo, wco_ref[...])

    for c in range(n_slabs):
        conv_slab(c)
    qkv = _dot(h, win_ref[:, 0:COL_GLU])
    gates = [gate(0)]
    q = qkv[:, :Q_WIDTH]
    k = qkv[:, Q_WIDTH:Q_WIDTH + KV_WIDTH]
    v = qkv[:, Q_WIDTH + KV_WIDTH:]
    gates.append(gate(1))

    lane = lax.broadcasted_iota(jnp.int32, (TQ, LANES), 1)
    first_half = (lane % HEAD_DIM) < (HEAD_DIM // 2)
    low_head = lane < HEAD_DIM
    n_freq = HEAD_DIM // 2
    per_row = LANES // n_freq
    quarter = lax.broadcasted_iota(jnp.int32, (TQ // per_row, LANES), 1) // n_freq
    sign = jnp.where(quarter % 2 == 0, -1.0, 1.0)
    for table_ref, out_ref, scale in ((cos_ref, cosd_ref, None), (sin_ref, sind_ref, sign)):
        dense = table_ref[0]
        rolled = [dense] + [pltpu.roll(dense, n_freq * kk, 1) for kk in range(1, per_row)]
        for j in range(per_row):
            spread = rolled[(-j) % per_row]
            for qq in range(1, per_row):
                spread = jnp.where(quarter == qq, rolled[(qq - j) % per_row], spread)
            out_ref[pl.ds(j, TQ // per_row, stride=per_row), :] = spread if scale is None else spread * scale
    cosd = cosd_ref[...]
    sind = sind_ref[...]

    def rope(t):
        rot = jnp.where(first_half, pltpu.roll(t, LANES - HEAD_DIM // 2, 1),
                        pltpu.roll(t, HEAD_DIM // 2, 1))
        return t * cosd + rot * sind

    def dup_halves(t):
        swapped = pltpu.roll(t, HEAD_DIM, 1)
        return jnp.where(low_head, t, swapped), jnp.where(low_head, swapped, t)

    def qk_norm(t, gain):
        sq = t * t
        lo = jnp.sum(jnp.where(low_head, sq, 0.0), axis=-1, keepdims=True)
        hi = jnp.sum(jnp.where(low_head, 0.0, sq), axis=-1, keepdims=True)
        inv = jnp.where(low_head, lax.rsqrt(lo * (1.0 / HEAD_DIM) + EPS), lax.rsqrt(hi * (1.0 / HEAD_DIM) + EPS))
        return t * inv * gain

    kd = dup_halves(rope(qk_norm(k, gk_ref[...])))
    vd = dup_halves(v)
    for kvh in range(ATT_KV_HEADS):
        kbands[kvh][WINDOW:WINDOW + TQ, :] = kd[kvh].astype(BF16)
        vbands[kvh][WINDOW:WINDOW + TQ, :] = vd[kvh].astype(BF16)

    q_heads = []
    for c in range(Q_WIDTH // LANES):
        cols = slice(c * LANES, (c + 1) * LANES)
        qc = rope(qk_norm(q[:, cols], gq_ref[:, cols]))
        q_heads.append(jnp.where(low_head, qc, 0.0).astype(BF16))
        q_heads.append(jnp.where(low_head, 0.0, qc).astype(BF16))

    qi = lax.broadcasted_iota(jnp.int32, (WINDOW, 2 * WINDOW), 0)
    kj = lax.broadcasted_iota(jnp.int32, (WINDOW, 2 * WINDOW), 1)
    band_mask = (kj > qi) & (kj <= qi + WINDOW)
    first_mask = band_mask & (kj >= jnp.where(i > 0, 0, WINDOW))
    low_head_w = lax.broadcasted_iota(jnp.int32, (WINDOW, LANES), 1) < HEAD_DIM

    def attention_scores(step):
        n, kvh = divmod(step, ATT_KV_HEADS)
        rows = slice(n * WINDOW, (n + 1) * WINDOW)
        mask = first_mask if n == 0 else band_mask
        heads = range(kvh * group, (kvh + 1) * group)
        q_st = jnp.concatenate([q_heads[hd][rows] for hd in heads], axis=0)
        kband = kbands[kvh][n * WINDOW:n * WINDOW + 2 * WINDOW, :]
        s_all = _dot_nt(q_st, kband)
        es, dens = [], []
        for j, hd in enumerate(heads):
            s = jnp.where(mask, s_all[j * WINDOW:(j + 1) * WINDOW], NEG_INF)
            sink = sinks_ref[hd]
            m = jnp.maximum(jnp.max(s, axis=-1, keepdims=True), sink)
            e = jnp.exp2(s - m)
            dens.append(jnp.sum(e, axis=-1, keepdims=True) + jnp.exp2(sink - m))
            es.append(e.astype(BF16))
        return jnp.concatenate(es, axis=0), dens

    def attention_output(step, probs):
        n, kvh = divmod(step, ATT_KV_HEADS)
        rows = slice(n * WINDOW, (n + 1) * WINDOW)
        e_all, dens = probs
        vband = vbands[kvh][n * WINDOW:n * WINDOW + 2 * WINDOW, :]
        o_all = _dot(e_all, vband)
        for cc in range(group // 2):
            oa = o_all[(2 * cc) * WINDOW:(2 * cc + 1) * WINDOW] / dens[2 * cc]
            ob = o_all[(2 * cc + 1) * WINDOW:(2 * cc + 2) * WINDOW] / dens[2 * cc + 1]
            col = kvh * (group // 2) + cc
            o_ref[rows, col * LANES:(col + 1) * LANES] = jnp.where(low_head_w, oa, ob).astype(BF16)

    def mem_scores(hd, xq):
        cols = slice(hd * MEM_HEAD_DIM, (hd + 1) * MEM_HEAD_DIM)
        s = _dot_nt(xq[:, cols], mk_ref[0, :, cols])
        e = jnp.exp2(s - jnp.max(s, axis=-1, keepdims=True))
        return e.astype(BF16), jnp.sum(e, axis=-1, keepdims=True)

    def mem_output(hd, probs):
        cols = slice(hd * MEM_HEAD_DIM, (hd + 1) * MEM_HEAD_DIM)
        e, den = probs
        om_ref[:, cols] = (_dot(e, mv_ref[0, :, cols]) / den).astype(BF16)

    route = _router_select(route_logits)
    p0 = attention_scores(0)
    gates.append(gate(2))
    p1 = attention_scores(1)
    attention_output(0, p0)
    xq = _dot(h, win_ref[:, COL_XQ:COL_GATE])
    p2 = attention_scores(2)
    attention_output(1, p1)
    _router_emit(*route, tri_ref, x1_ref, rt_ref, cnt_ref)
    xq = (_rms_per_mem_head(xq) * gxq_ref[...]).astype(BF16)
    p3 = attention_scores(3)
    attention_output(2, p2)
    y_conv = conv_finish()
    m0 = mem_scores(0, xq)
    attention_output(3, p3)
    for r in kbands + vbands:
        r[0:WINDOW, :] = r[TQ:TQ + WINDOW, :]
    m1 = mem_scores(1, xq)
    y_attn = _dot(o_ref[...], woa_ref[...])
    mem_output(0, m0)
    m2 = mem_scores(2, xq)
    mem_output(1, m1)
    m3 = mem_scores(3, xq)
    mem_output(2, m2)
    merged = gates[0] * y_attn + gates[1] * y_conv
    mem_output(3, m3)
    merged = merged + gates[2] * _dot(om_ref[...], wom_ref[...])

    x1prev_ref[...] = x + _dot(merged.astype(BF16), wout_ref[...])


def _router_logits(x1, g2_ref, wrt_ref, brt_ref):
    h2 = _rms(x1) * g2_ref[...]
    h_hi, h_lo = _split_bf16(h2)
    w_hi, w_lo = _split_bf16(wrt_ref[...])
    return _dot_nt(w_hi, h_hi) + _dot_nt(w_hi, h_lo) + _dot_nt(w_lo, h_hi) + brt_ref[...]


def _router_select(lt):
    gl = [lt[g:g + 1] for g in range(N_GROUPS)]
    gmax = functools.reduce(jnp.maximum, gl)
    g_idx = jnp.full((1, TQ), N_GROUPS - 1, jnp.int32)
    for g in reversed(range(N_GROUPS - 1)):
        g_idx = jnp.where(gl[g] == gmax, g, g_idx)
    p_g = 1.0 / functools.reduce(jnp.add, [jnp.exp(r - gmax) for r in gl])
    el = []
    for kk in range(EXPERTS_PER_GROUP):
        row = lt[8 + kk:9 + kk]
        for g in range(1, N_GROUPS):
            off = 8 + g * EXPERTS_PER_GROUP + kk
            row = jnp.where(g_idx == g, lt[off:off + 1], row)
        el.append(row)

    def first_argmax(rows):
        top = functools.reduce(jnp.maximum, rows)
        idx = jnp.full((1, TQ), len(rows) - 1, jnp.int32)
        for kk in reversed(range(len(rows) - 1)):
            idx = jnp.where(rows[kk] == top, kk, idx)
        return top, idx

    v1, i1 = first_argmax(el)
    v2, i2 = first_argmax([jnp.where(i1 == kk, -jnp.inf, el[kk]) for kk in range(EXPERTS_PER_GROUP)])
    t = jnp.exp(v2 - v1)
    p1 = 1.0 / (1.0 + t)
    p2 = t * p1

    lo = jnp.minimum(i1, i2)
    hi = jnp.maximum(i1, i2)
    w_lo = jnp.where(i1 < i2, p_g * p1, p_g * p2)
    w_hi = jnp.where(i1 < i2, p_g * p2, p_g * p1)
    cls = g_idx * N_PAIRS + ((lo * (2 * EXPERTS_PER_GROUP - 1 - lo)) >> 1) + hi - lo - 1
    return cls, w_lo, w_hi


def _router_emit(cls, w_lo, w_hi, tri_ref, x1_ref, rt_ref, cnt_ref):
    cls_f = cls.astype(F32)
    onehot = jnp.broadcast_to(cls, (CLS_ROWS, TQ)) == lax.broadcasted_iota(jnp.int32, (CLS_ROWS, TQ), 0)
    onehot_bf = onehot.astype(F32).astype(BF16)
    prefix = _dot(onehot_bf, tri_ref[...])
    rank = jnp.sum(jnp.where(onehot, prefix, 0.0), axis=0, keepdims=True) - 1.0
    counts = _dot_nt(jnp.ones((8, TQ), BF16), onehot_bf)
    zero = jnp.zeros((1, TQ), F32)
    rt_ref[0] = jnp.concatenate([cls_f, rank] + [zero] * 6, axis=0)
    cnt_ref[0] = jnp.concatenate([counts, jnp.zeros((8, LANES - CLS_ROWS), F32)], axis=1)
    cols = jnp.concatenate([w_lo, w_hi, cls_f, jnp.zeros((LANES - 3, TQ), F32)], axis=0)
    x1_ref[0, :, D_MODEL:] = cols.T


FLAG_FIRST, FLAG_VALID = 1, 2
GATHER_SLOTS = 3
DMA_PRIORITIES = 2


def _moe_kernel(dest_ref, tile_ref, elo_ref, ehi_ref, cls_ref, flag_ref,
                g2_ref, wg_lo, wu_lo, wd_lo, wg_hi, wu_hi, wd_hi, x1_hbm,
                out_hbm,
                perm_ref, xbuf, obuf, h2buf, wbuf_up, wbuf_down, gsem, ssem):
    k = pl.program_id(0)
    nk = pl.num_programs(0)
    t = tile_ref[k]
    xslot = t % GATHER_SLOTS
    oslot = t % 2
    flags = flag_ref[k]
    n_tok = x1_hbm.shape[0]
    n_tiles = n_tok // TM

    def gather_copy(r, tile, s):
        tok = perm_ref[tile * TM + r]
        return pltpu.make_async_copy(x1_hbm.at[pl.ds(tok, 1), :], xbuf.at[s, pl.ds(r, 1), :], gsem.at[s])

    def scatter_copy(r, tile, s):
        tok = perm_ref[tile * TM + r]
        return pltpu.make_async_copy(obuf.at[s, pl.ds(r, 1), :], out_hbm.at[pl.ds(tok, 1), :], ssem.at[s])

    k_prev = jnp.maximum(k - 1, 0)
    for side, (e_ref, wg, wu, wd) in enumerate(((elo_ref, wg_lo, wu_lo, wd_lo), (ehi_ref, wg_hi, wu_hi, wd_hi))):
        @pl.when((k == 0) | (e_ref[k] != e_ref[k_prev]))
        def _(side=side, wg=wg, wu=wu, wd=wd):
            wbuf_up[2 * side] = wg[0]
            wbuf_up[2 * side + 1] = wu[0]
            wbuf_down[side] = wd[0]

    def for_rows(fn):
        def body(r, carry):
            fn(r)
            return carry
        lax.fori_loop(0, TM, body, 0, unroll=8)

    @pl.when(k == 0)
    def _():
        def invert(tok, carry):
            perm_ref[dest_ref[tok]] = tok
            return carry
        lax.fori_loop(0, n_tok, invert, 0, unroll=16)
        for tile in range(GATHER_SLOTS - 1):
            for_rows(lambda r, tile=tile: gather_copy(r, tile, tile).start())

    is_first = (flags & FLAG_FIRST) != 0
    is_valid = (flags & FLAG_VALID) != 0
    ahead = GATHER_SLOTS - 1
    prefetch_tile = jnp.minimum(t + ahead, n_tiles - 1)
    prefetch_slot = (t + ahead) % GATHER_SLOTS

    def gather_tile_wait(s):
        pltpu.make_async_copy(x1_hbm.at[pl.ds(0, TM), :], xbuf.at[s], gsem.at[s]).wait()

    def scatter_tile_wait(s):
        pltpu.make_async_copy(obuf.at[s], out_hbm.at[pl.ds(0, TM), :], ssem.at[s]).wait()

    @pl.when(is_first)
    def _():
        gather_tile_wait(xslot)

        @pl.when(t >= 2)
        def _():
            scatter_tile_wait(oslot)

        xr = xbuf[xslot, :, 0:D_MODEL]
        h2buf[...] = (_rms(xr) * g2_ref[...]).astype(BF16)
        obuf[oslot] = xr

    def experts():
        w = xbuf[xslot, :, D_MODEL:XW]
        mine = w[:, 2:3] == cls_ref[k].astype(F32)
        h2 = h2buf[...]
        y = None
        for side in range(2):
            c = jnp.where(mine, w[:, side:side + 1], 0.0)
            hid = jax.nn.silu(_dot(h2, wbuf_up[2 * side])) * _dot(h2, wbuf_up[2 * side + 1]) * c
            part = _dot(hid.astype(BF16), wbuf_down[side])
            y = part if y is None else y + part
        obuf[oslot] += y

    @pl.when(is_first & (t > 0))
    def _():
        for r in range(TM):
            gather_copy(r, prefetch_tile, prefetch_slot).start(priority=r % DMA_PRIORITIES)
            scatter_copy(r, t - 1, 1 - oslot).start(priority=r % DMA_PRIORITIES)
        experts()

    @pl.when(is_first & (t == 0))
    def _():
        for r in range(TM):
            gather_copy(r, prefetch_tile, prefetch_slot).start(priority=r % DMA_PRIORITIES)
        experts()

    @pl.when(is_valid & jnp.logical_not(is_first))
    def _():
        experts()

    @pl.when(k == nk - 1)
    def _():
        last = n_tiles - 1
        for_rows(lambda r: scatter_copy(r, last, last % 2).start())
        for tile in (last - 1, last):
            scatter_tile_wait(tile % 2)
        for extra in range(1, GATHER_SLOTS):
            gather_tile_wait((last + extra) % GATHER_SLOTS)


_CLASS_PAIRS = [(a, b) for a in range(EXPERTS_PER_GROUP) for b in range(a + 1, EXPERTS_PER_GROUP)]


def _routing_plan(cls, rank, tile_counts, n_tiles):
    counts = jnp.sum(tile_counts, axis=0)
    off = jnp.concatenate([jnp.zeros((1,), jnp.int32), jnp.cumsum(counts).astype(jnp.int32)])
    tile_base = jnp.cumsum(tile_counts, axis=0) - tile_counts + off[None, :-1]
    onehot = cls[:, :, None] == jnp.arange(N_CLASSES, dtype=jnp.int32)[None, None, :]
    dest = (jnp.sum(jnp.where(onehot, tile_base[:, None, :], 0), axis=-1) + rank).reshape(-1)
    first_tile = off[:-1] // TM
    last_tile = (off[1:] - 1) // TM
    n_items = jnp.where(counts > 0, last_tile - first_tile + 1, 0)
    istart = jnp.concatenate([jnp.zeros((1,), jnp.int32), jnp.cumsum(n_items).astype(jnp.int32)])
    total = istart[-1]
    ni = n_tiles + N_CLASSES - 1
    k = jnp.arange(ni, dtype=jnp.int32)
    valid = k < total
    kc = jnp.minimum(k, total - 1)
    c_of_k = jnp.minimum(jnp.sum(istart[None, 1:] <= kc[:, None], axis=1).astype(jnp.int32), N_CLASSES - 1)
    tile_k = first_tile[c_of_k] + kc - istart[c_of_k]
    prev_tile = jnp.concatenate([jnp.full((1,), -1, jnp.int32), tile_k[:-1]])
    first = valid & (tile_k != prev_tile)
    flags = (first * FLAG_FIRST + valid * FLAG_VALID).astype(jnp.int32)
    pair_lo = jnp.array([p[0] for p in _CLASS_PAIRS], jnp.int32)
    pair_hi = jnp.array([p[1] for p in _CLASS_PAIRS], jnp.int32)
    base = (c_of_k // N_PAIRS) * EXPERTS_PER_GROUP
    e_lo = base + pair_lo[c_of_k % N_PAIRS]
    e_hi = base + pair_hi[c_of_k % N_PAIRS]
    cls_k = jnp.where(valid, c_of_k, -1)
    return dest.astype(jnp.int32), tile_k, e_lo, e_hi, cls_k, flags


def _const_spec(shape):
    nd = len(shape)
    return pl.BlockSpec(shape, lambda *_: (0,) * nd, pipeline_mode=pl.Buffered(1))


def kernel(x, mem, positions, g_norm1, w_in, g_q, g_k, sinks, w_o_attn, w_conv_dw, b_conv_dw, g_conv_ln, b_conv_ln, w_conv_out, g_mem, w_kv_mem, g_xq, g_xk, w_o_mem, w_out, g_norm2, w_group, b_group, w_router, b_router, w_gate, w_up, w_down):
    B, S, D = x.shape
    M = mem.shape[1]
    assert D == D_MODEL and S % TQ == 0 and (B * S) % TM == 0 and w_in.shape[0] == 1
    assert (TQ // WINDOW) * ATT_KV_HEADS == MEM_HEADS and CONV_CH // LANES == N_BRANCHES + 1
    NT = S // TQ
    T = B * S
    l = 0

    row = lambda v: v.reshape(1, -1).astype(F32)

    mk, mv = pl.pallas_call(
        _memkv_kernel,
        grid=(B,),
        in_specs=[pl.BlockSpec((1, M, D), lambda b: (b, 0, 0)),
                  _const_spec((1, D)), _const_spec((D, 2 * XQ_WIDTH)),
                  _const_spec((1, XQ_WIDTH))],
        out_specs=[pl.BlockSpec((1, M, XQ_WIDTH), lambda b: (b, 0, 0))] * 2,
        out_shape=[jax.ShapeDtypeStruct((B, M, XQ_WIDTH), BF16)] * 2,
        compiler_params=pltpu.CompilerParams(dimension_semantics=("arbitrary",)),
        name="memkv",
    )(mem, row(g_mem[l]), w_kv_mem[l].astype(BF16), row(jnp.tile(g_xk[l], MEM_HEADS)))

    inv_freq = 1.0 / (ROPE_THETA ** (jnp.arange(0, HEAD_DIM, 2, dtype=F32) / HEAD_DIM))
    per_row = LANES // (HEAD_DIM // 2)
    ang = positions.astype(F32).reshape(B, S // per_row, per_row, 1) * inv_freq
    ang = ang.reshape(B, S // per_row, LANES)
    cos_tab, sin_tab = jnp.cos(ang), jnp.sin(ang)

    assert w_in.shape[-1] == IN_WIDTH
    wdw = jnp.zeros((CONV_HALO, CONV_CH), F32).at[:CONV_WIDTH].set(w_conv_dw[l])
    w_rt = jnp.zeros((RT_ROWS, D), F32).at[0:N_GROUPS].set(w_group[l].T).at[8:8 + N_EXPERTS].set(w_router[l].T)
    b_rt = jnp.zeros((RT_ROWS, 1), F32).at[0:N_GROUPS, 0].set(b_group[l]).at[8:8 + N_EXPERTS, 0].set(b_router[l])

    n_mix = B * NT
    tile_in = lambda s: jnp.minimum(s, n_mix - 1)
    tile_out = lambda s: jnp.maximum(s - 1, 0)
    tile3 = lambda last: pl.BlockSpec((1, TQ, last), lambda s: (tile_in(s) // NT, tile_in(s) % NT, 0))
    per_batch = pl.BlockSpec((1, M, XQ_WIDTH), lambda s: (tile_in(s) // NT, 0, 0))
    rope_spec = pl.BlockSpec((1, TQ // per_row, LANES), lambda s: (tile_in(s) // NT, tile_in(s) % NT, 0))
    in_specs = [
        tile3(D), rope_spec, rope_spec, per_batch, per_batch,
        _const_spec((1, D)),
        _const_spec((D, IN_WIDTH)),
        _const_spec((1, Q_WIDTH)), _const_spec((1, LANES)),
        pl.BlockSpec(memory_space=pltpu.SMEM),
        _const_spec((Q_WIDTH, D)),
        _const_spec((CONV_HALO, CONV_CH)), _const_spec((1, CONV_CH)), _const_spec((1, CONV_CH)), _const_spec((1, CONV_CH)),
        _const_spec((CONV_CH, D)),
        _const_spec((1, XQ_WIDTH)),
        _const_spec((XQ_WIDTH, D)),
        _const_spec((D, D)),
        _const_spec((1, D)),
        _const_spec((RT_ROWS, D)), _const_spec((RT_ROWS, 1)), _const_spec((TQ, TQ)),
    ]
    tri = (jnp.arange(TQ)[:, None] <= jnp.arange(TQ)[None, :]).astype(BF16)
    per_tile = lambda rows, last: pl.BlockSpec((1, rows, last), lambda s: (tile_out(s), 0, 0))
    x1_spec = pl.BlockSpec((1, TQ, XW), lambda s: (tile_out(s) // NT, tile_out(s) % NT, 0))
    assert (N_EXPERTS * D) % n_mix == 0 and (N_EXPERTS * EXPERT_FF) % n_mix == 0
    up_rows, down_rows = N_EXPERTS * D // n_mix, N_EXPERTS * EXPERT_FF // n_mix
    up_slice = pl.BlockSpec((up_rows, EXPERT_FF), lambda s: (tile_in(s), 0))
    down_slice = pl.BlockSpec((down_rows, D), lambda s: (tile_in(s), 0))
    in_specs += [up_slice, up_slice, down_slice]
    x1, rt, cnt, wg, wu, wd = pl.pallas_call(
        functools.partial(_mixer_kernel, tiles_per_seq=NT),
        grid=(n_mix + 1,),
        in_specs=in_specs,
        out_specs=[x1_spec, per_tile(8, TQ), per_tile(8, LANES), up_slice, up_slice, down_slice],
        out_shape=[jax.ShapeDtypeStruct((B, S, XW), F32), jax.ShapeDtypeStruct((B * NT, 8, TQ), F32),
                   jax.ShapeDtypeStruct((B * NT, 8, LANES), F32),
                   jax.ShapeDtypeStruct((N_EXPERTS * D, EXPERT_FF), BF16),
                   jax.ShapeDtypeStruct((N_EXPERTS * D, EXPERT_FF), BF16),
                   jax.ShapeDtypeStruct((N_EXPERTS * EXPERT_FF, D), BF16)],
        scratch_shapes=[pltpu.VMEM((WINDOW + TQ, LANES), BF16)] * 4
        + [pltpu.VMEM((CONV_CH // LANES, CONV_HALO + TQ, LANES), F32),
           pltpu.VMEM((CONV_CH // LANES, TQ, LANES), F32), pltpu.VMEM((TQ, Q_WIDTH), BF16),
           pltpu.VMEM((TQ, XQ_WIDTH), BF16), pltpu.VMEM((TQ, D), F32),
           pltpu.VMEM((TQ, LANES), F32), pltpu.VMEM((TQ, LANES), F32)],
        compiler_params=pltpu.CompilerParams(dimension_semantics=("arbitrary",),
                                             vmem_limit_bytes=VMEM_LIMIT),
        name="mixer",
    )(x, cos_tab, sin_tab, mk, mv, row(g_norm1[l]), w_in[l].astype(BF16),
      row(jnp.tile(g_q[l], ATT_HEADS) * (HEAD_DIM ** -0.5 * LOG2E)), row(jnp.tile(g_k[l], ATT_KV_HEADS)),
      sinks[l].astype(F32) * LOG2E, w_o_attn[l].astype(BF16),
      wdw, row(b_conv_dw[l]), row(g_conv_ln[l]), row(b_conv_ln[l]), w_conv_out[l].astype(BF16),
      row(jnp.tile(g_xq[l], MEM_HEADS) * (MEM_HEAD_DIM ** -0.5 * LOG2E)), w_o_mem[l].astype(BF16),
      w_out[l].astype(BF16), row(g_norm2[l]), w_rt, b_rt, tri,
      w_gate[l].reshape(N_EXPERTS * D, EXPERT_FF), w_up[l].reshape(N_EXPERTS * D, EXPERT_FF),
      w_down[l].reshape(N_EXPERTS * EXPERT_FF, D))

    n_tiles = T // TM
    plan = _routing_plan(rt[:, 0, :].astype(jnp.int32), rt[:, 1, :].astype(jnp.int32),
                         cnt[:, 0, :N_CLASSES].astype(jnp.int32), n_tiles)
    dest, tile_k, e_lo, e_hi, cls_k, flags = plan
    ni = tile_k.shape[0]
    wg = wg.reshape(N_EXPERTS, D, EXPERT_FF)
    wu = wu.reshape(N_EXPERTS, D, EXPERT_FF)
    wd = wd.reshape(N_EXPERTS, EXPERT_FF, D)
    lo_map = lambda k, dest, tile, elo, ehi, cls, flg: (elo[k], 0, 0)
    hi_map = lambda k, dest, tile, elo, ehi, cls, flg: (ehi[k], 0, 0)
    up_block = (1, D, EXPERT_FF)
    down_block = (1, EXPERT_FF, D)
    out = pl.pallas_call(
        _moe_kernel,
        grid_spec=pltpu.PrefetchScalarGridSpec(
            num_scalar_prefetch=6,
            grid=(ni,),
            in_specs=[pl.BlockSpec((1, D), lambda k, *_: (0, 0)),
                      pl.BlockSpec(up_block, lo_map), pl.BlockSpec(up_block, lo_map),
                      pl.BlockSpec(down_block, lo_map),
                      pl.BlockSpec(up_block, hi_map), pl.BlockSpec(up_block, hi_map),
                      pl.BlockSpec(down_block, hi_map),
                      pl.BlockSpec(memory_space=pl.ANY)],
            out_specs=pl.BlockSpec(memory_space=pl.ANY),
            scratch_shapes=[pltpu.SMEM((T,), jnp.int32),
                            pltpu.VMEM((GATHER_SLOTS, TM, XW), F32), pltpu.VMEM((2, TM, D), F32),
                            pltpu.VMEM((TM, D), BF16),
                            pltpu.VMEM((4, D, EXPERT_FF), BF16), pltpu.VMEM((2, EXPERT_FF, D), BF16),
                            pltpu.SemaphoreType.DMA((GATHER_SLOTS,)), pltpu.SemaphoreType.DMA((2,))]),
        out_shape=jax.ShapeDtypeStruct((T, D), F32),
        compiler_params=pltpu.CompilerParams(dimension_semantics=("arbitrary",),
                                             vmem_limit_bytes=VMEM_LIMIT),
        name="moe",
    )(dest, tile_k, e_lo, e_hi, cls_k, flags, row(g_norm2[l]), wg, wu, wd, wg, wu, wd,
      x1.reshape(T, XW))
    return out.reshape(B, S, D)
```

```python
import functools

import jax
import jax.numpy as jnp
from jax import lax
from jax.experimental import pallas as pl
from jax.experimental.pallas import tpu as pltpu

F32 = jnp.float32
BF16 = jnp.bfloat16

D_MODEL = 1024
ATT_HEADS = 8
ATT_KV_HEADS = 2
HEAD_DIM = 64
WINDOW = 128
ROPE_THETA = 10000.0
CONV_CH = 512
CONV_WIDTH = 31
MEM_HEADS = 4
MEM_HEAD_DIM = 128
N_BRANCHES = 3
N_GROUPS = 4
EXPERTS_PER_GROUP = 4
N_EXPERTS = N_GROUPS * EXPERTS_PER_GROUP
EXPERT_FF = 512
EPS = 1e-6
LN_EPS = 1e-5
NEG_INF = -1e30
LOG2E = 1.4426950408889634

Q_WIDTH = ATT_HEADS * HEAD_DIM
KV_WIDTH = ATT_KV_HEADS * HEAD_DIM
GLU_WIDTH = 2 * CONV_CH
XQ_WIDTH = MEM_HEADS * MEM_HEAD_DIM
GATE_WIDTH = N_BRANCHES * D_MODEL
COL_GLU = Q_WIDTH + 2 * KV_WIDTH
COL_XQ = COL_GLU + GLU_WIDTH
COL_GATE = COL_XQ + XQ_WIDTH
IN_WIDTH = COL_GATE + GATE_WIDTH

LANES = 128
TQ = 256
CONV_HALO = 32
CONV_ROWS = 128
TM = 256
RT_ROWS = 32
N_PAIRS = EXPERTS_PER_GROUP * (EXPERTS_PER_GROUP - 1) // 2
N_CLASSES = N_GROUPS * N_PAIRS
CLS_ROWS = 32
XW = D_MODEL + LANES
VMEM_LIMIT = 56 * 1024 * 1024


def _dot(a, b):
    return jnp.dot(a, b, preferred_element_type=F32)


def _dot_nt(a, b):
    return lax.dot_general(a, b, (((1,), (1,)), ((), ())), preferred_element_type=F32)


def _split_bf16(t):
    hi = t.astype(BF16)
    lo = (t - hi.astype(F32)).astype(BF16)
    return hi, lo


def _rms(t):
    return t * lax.rsqrt(jnp.mean(t * t, axis=-1, keepdims=True) + EPS)


def _rms_per_mem_head(t):
    return jnp.concatenate(
        [_rms(t[:, hd * MEM_HEAD_DIM:(hd + 1) * MEM_HEAD_DIM]) for hd in range(MEM_HEADS)], axis=-1)


def _memkv_kernel(mem_ref, gmem_ref, wkv_ref, gxk_ref, mk_ref, mv_ref):
    mn = (_rms(mem_ref[0]) * gmem_ref[...]).astype(BF16)
    kv = _dot(mn, wkv_ref[...])
    mk_ref[0] = (_rms_per_mem_head(kv[:, :XQ_WIDTH]) * gxk_ref[...]).astype(BF16)
    mv_ref[0] = kv[:, XQ_WIDTH:].astype(BF16)


def _mixer_kernel(x_ref, cos_ref, sin_ref, mk_ref, mv_ref, g1_ref, win_ref,
                  gq_ref, gk_ref, sinks_ref, woa_ref, wdw_ref,
                  bdw_ref, gln_ref, bln_ref, wco_ref, gxq_ref, wom_ref, wout_ref, g2_ref,
                  wrt_ref, brt_ref, tri_ref, ewg_ref, ewu_ref, ewd_ref,
                  x1_ref, rt_ref, cnt_ref, ewg_out, ewu_out, ewd_out,
                  k0_ref, k1_ref, v0_ref, v1_ref, ubuf_ref, ybuf_ref, o_ref, om_ref, x1prev_ref,
                  cosd_ref, sind_ref, *, tiles_per_seq):
    s = pl.program_id(0)
    n_tiles = pl.num_programs(0) - 1
    i = jnp.minimum(s, n_tiles - 1) % tiles_per_seq
    kbands = (k0_ref, k1_ref)
    vbands = (v0_ref, v1_ref)

    @pl.when(s == 0)
    def _():
        x1prev_ref[...] = jnp.zeros((TQ, D_MODEL), F32)

    @pl.when(i == 0)
    def _():
        for r in kbands + vbands:
            r[0:WINDOW, :] = jnp.zeros((WINDOW, LANES), BF16)
        ubuf_ref[:, 0:CONV_HALO, :] = jnp.zeros((CONV_CH // LANES, CONV_HALO, LANES), F32)

    x1_prev = x1prev_ref[...]
    x1_ref[0, :, 0:D_MODEL] = x1_prev

    ewg_out[...] = ewg_ref[...].astype(BF16)
    ewu_out[...] = ewu_ref[...].astype(BF16)
    ewd_out[...] = ewd_ref[...].astype(BF16)

    x = x_ref[0]
    h = (_rms(x) * g1_ref[...]).astype(BF16)

    n_slabs = CONV_CH // LANES
    half = TQ // 2
    group = ATT_HEADS // ATT_KV_HEADS

    glu = _dot(h, win_ref[:, COL_GLU:COL_XQ])
    route_logits = _router_logits(x1_prev, g2_ref, wrt_ref, brt_ref)
    u = glu[:, :CONV_CH] * jax.nn.sigmoid(glu[:, CONV_CH:])
    for c in range(n_slabs):
        ubuf_ref[c, CONV_HALO:CONV_HALO + TQ, :] = u[:, c * LANES:(c + 1) * LANES]
    base = CONV_HALO - (CONV_WIDTH - 1)
    acc = [[None] * n_slabs for _ in range(2)]

    def conv_slab(c):
        cols = slice(c * LANES, (c + 1) * LANES)
        for par in range(2):
            parts = []
            for q0 in range(0, half, CONV_ROWS):
                a = jnp.broadcast_to(bdw_ref[:, cols], (CONV_ROWS, LANES))
                for j in range(CONV_WIDTH):
                    a = a + wdw_ref[j:j + 1, cols] * ubuf_ref[
                        c, pl.ds(base + j + par + 2 * q0, CONV_ROWS, stride=2), :]
                parts.append(a)
            acc[par][c] = jnp.concatenate(parts, axis=0)
        ubuf_ref[c, 0:CONV_HALO, :] = ubuf_ref[c, TQ:TQ + CONV_HALO, :]

    def gate(b):
        return jax.nn.sigmoid(_dot(h, win_ref[:, COL_GATE + b * D_MODEL:COL_GATE + (b + 1) * D_MODEL]))

    def conv_finish():
        for par in range(2):
            row_sum = functools.reduce(jnp.add, [jnp.sum(a, axis=-1, keepdims=True) for a in acc[par]])
            mu = row_sum * (1.0 / CONV_CH)
            cen = [a - mu for a in acc[par]]
            sq_sum = functools.reduce(jnp.add, [jnp.sum(t * t, axis=-1, keepdims=True) for t in cen])
            rstd = lax.rsqrt(sq_sum * (1.0 / CONV_CH) + LN_EPS)
            for c in range(n_slabs):
                cols = slice(c * LANES, (c + 1) * LANES)
                yln = cen[c] * rstd * gln_ref[:, cols] + bln_ref[:, cols]
                ybuf_ref[c, pl.ds(par, half, stride=2), :] = yln * jax.nn.sigmoid(yln)
        uo = jnp.concatenate([ybuf_ref[c] for c in range(n_slabs)], axis=-1).astype(BF16)
        return _dot(uo, wco_ref[...])

    for c in range(n_slabs):
        conv_slab(c)
    qkv = _dot(h, win_ref[:, 0:COL_GLU])
    gates = [gate(0)]
    q = qkv[:, :Q_WIDTH]
    k = qkv[:, Q_WIDTH:Q_WIDTH + KV_WIDTH]
    v = qkv[:, Q_WIDTH + KV_WIDTH:]
    gates.append(gate(1))

    lane = lax.broadcasted_iota(jnp.int32, (TQ, LANES), 1)
    first_half = (lane % HEAD_DIM) < (HEAD_DIM // 2)
    low_head = lane < HEAD_DIM
    n_freq = HEAD_DIM // 2
    per_row = LANES // n_freq
    quarter = lax.broadcasted_iota(jnp.int32, (TQ // per_row, LANES), 1) // n_freq
    sign = jnp.where(quarter % 2 == 0, -1.0, 1.0)
    for table_ref, out_ref, scale in ((cos_ref, cosd_ref, None), (sin_ref, sind_ref, sign)):
        dense = table_ref[0]
        rolled = [dense] + [pltpu.roll(dense, n_freq * kk, 1) for kk in range(1, per_row)]
        for j in range(per_row):
            spread = rolled[(-j) % per_row]
            for qq in range(1, per_row):
                spread = jnp.where(quarter == qq, rolled[(qq - j) % per_row], spread)
            out_ref[pl.ds(j, TQ // per_row, stride=per_row), :] = spread if scale is None else spread * scale
    cosd = cosd_ref[...]
    sind = sind_ref[...]

    def rope(t):
        rot = jnp.where(first_half, pltpu.roll(t, LANES - HEAD_DIM // 2, 1),
                        pltpu.roll(t, HEAD_DIM // 2, 1))
        return t * cosd + rot * sind

    def dup_halves(t):
        swapped = pltpu.roll(t, HEAD_DIM, 1)
        return jnp.where(low_head, t, swapped), jnp.where(low_head, swapped, t)

    def qk_norm(t, gain):
        sq = t * t
        lo = jnp.sum(jnp.where(low_head, sq, 0.0), axis=-1, keepdims=True)
        hi = jnp.sum(jnp.where(low_head, 0.0, sq), axis=-1, keepdims=True)
        inv = jnp.where(low_head, lax.rsqrt(lo * (1.0 / HEAD_DIM) + EPS), lax.rsqrt(hi * (1.0 / HEAD_DIM) + EPS))
        return t * inv * gain

    kd = dup_halves(rope(qk_norm(k, gk_ref[...])))
    vd = dup_halves(v)
    for kvh in range(ATT_KV_HEADS):
        kbands[kvh][WINDOW:WINDOW + TQ, :] = kd[kvh].astype(BF16)
        vbands[kvh][WINDOW:WINDOW + TQ, :] = vd[kvh].astype(BF16)

    q_heads = []
    for c in range(Q_WIDTH // LANES):
        cols = slice(c * LANES, (c + 1) * LANES)
        qc = rope(qk_norm(q[:, cols], gq_ref[:, cols]))
        q_heads.append(jnp.where(low_head, qc, 0.0).astype(BF16))
        q_heads.append(jnp.where(low_head, 0.0, qc).astype(BF16))

    qi = lax.broadcasted_iota(jnp.int32, (WINDOW, 2 * WINDOW), 0)
    kj = lax.broadcasted_iota(jnp.int32, (WINDOW, 2 * WINDOW), 1)
    band_mask = (kj > qi) & (kj <= qi + WINDOW)
    first_mask = band_mask & (kj >= jnp.where(i > 0, 0, WINDOW))
    low_head_w = lax.broadcasted_iota(jnp.int32, (WINDOW, LANES), 1) < HEAD_DIM

    def attention_scores(step):
        n, kvh = divmod(step, ATT_KV_HEADS)
        rows = slice(n * WINDOW, (n + 1) * WINDOW)
        mask = first_mask if n == 0 else band_mask
        heads = range(kvh * group, (kvh + 1) * group)
        q_st = jnp.concatenate([q_heads[hd][rows] for hd in heads], axis=0)
        kband = kbands[kvh][n * WINDOW:n * WINDOW + 2 * WINDOW, :]
        s_all = _dot_nt(q_st, kband)
        es, dens = [], []
        for j, hd in enumerate(heads):
            s = jnp.where(mask, s_all[j * WINDOW:(j + 1) * WINDOW], NEG_INF)
            sink = sinks_ref[hd]
            m = jnp.maximum(jnp.max(s, axis=-1, keepdims=True), sink)
            e = jnp.exp2(s - m)
            dens.append(jnp.sum(e, axis=-1, keepdims=True) + jnp.exp2(sink - m))
            es.append(e.astype(BF16))
        return jnp.concatenate(es, axis=0), dens

    def attention_output(step, probs):
        n, kvh = divmod(step, ATT_KV_HEADS)
        rows = slice(n * WINDOW, (n + 1) * WINDOW)
        e_all, dens = probs
        vband = vbands[kvh][n * WINDOW:n * WINDOW + 2 * WINDOW, :]
        o_all = _dot(e_all, vband)
        for cc in range(group // 2):
            oa = o_all[(2 * cc) * WINDOW:(2 * cc + 1) * WINDOW] / dens[2 * cc]
            ob = o_all[(2 * cc + 1) * WINDOW:(2 * cc + 2) * WINDOW] / dens[2 * cc + 1]
            col = kvh * (group // 2) + cc
            o_ref[rows, col * LANES:(col + 1) * LANES] = jnp.where(low_head_w, oa, ob).astype(BF16)

    def mem_scores(hd, xq):
        cols = slice(hd * MEM_HEAD_DIM, (hd + 1) * MEM_HEAD_DIM)
        s = _dot_nt(xq[:, cols], mk_ref[0, :, cols])
        e = jnp.exp2(s - jnp.max(s, axis=-1, keepdims=True))
        return e.astype(BF16), jnp.sum(e, axis=-1, keepdims=True)

    def mem_output(hd, probs):
        cols = slice(hd * MEM_HEAD_DIM, (hd + 1) * MEM_HEAD_DIM)
        e, den = probs
        om_ref[:, cols] = (_dot(e, mv_ref[0, :, cols]) / den).astype(BF16)

    route = _router_select(route_logits)
    p0 = attention_scores(0)
    gates.append(gate(2))
    p1 = attention_scores(1)
    attention_output(0, p0)
    xq = _dot(h, win_ref[:, COL_XQ:COL_GATE])
    p2 = attention_scores(2)
    attention_output(1, p1)
    _router_emit(*route, tri_ref, x1_ref, rt_ref, cnt_ref)
    xq = (_rms_per_mem_head(xq) * gxq_ref[...]).astype(BF16)
    p3 = attention_scores(3)
    attention_output(2, p2)
    y_conv = conv_finish()
    m0 = mem_scores(0, xq)
    attention_output(3, p3)
    for r in kbands + vbands:
        r[0:WINDOW, :] = r[TQ:TQ + WINDOW, :]
    m1 = mem_scores(1, xq)
    y_attn = _dot(o_ref[...], woa_ref[...])
    mem_output(0, m0)
    m2 = mem_scores(2, xq)
    mem_output(1, m1)
    m3 = mem_scores(3, xq)
    mem_output(2, m2)
    merged = gates[0] * y_attn + gates[1] * y_conv
    mem_output(3, m3)
    merged = merged + gates[2] * _dot(om_ref[...], wom_ref[...])

    x1prev_ref[...] = x + _dot(merged.astype(BF16), wout_ref[...])


def _router_logits(x1, g2_ref, wrt_ref, brt_ref):
    h2 = _rms(x1) * g2_ref[...]
    h_hi, h_lo = _split_bf16(h2)
    w_hi, w_lo = _split_bf16(wrt_ref[...])
    return _dot_nt(w_hi, h_hi) + _dot_nt(w_hi, h_lo) + _dot_nt(w_lo, h_hi) + brt_ref[...]


def _router_select(lt):
    gl = [lt[g:g + 1] for g in range(N_GROUPS)]
    gmax = functools.reduce(jnp.maximum, gl)
    g_idx = jnp.full((1, TQ), N_GROUPS - 1, jnp.int32)
    for g in reversed(range(N_GROUPS - 1)):
        g_idx = jnp.where(gl[g] == gmax, g, g_idx)
    p_g = 1.0 / functools.reduce(jnp.add, [jnp.exp(r - gmax) for r in gl])
    el = []
    for kk in range(EXPERTS_PER_GROUP):
        row = lt[8 + kk:9 + kk]
        for g in range(1, N_GROUPS):
            off = 8 + g * EXPERTS_PER_GROUP + kk
            row = jnp.where(g_idx == g, lt[off:off + 1], row)
        el.append(row)

    def first_argmax(rows):
        top = functools.reduce(jnp.maximum, rows)
        idx = jnp.full((1, TQ), len(rows) - 1, jnp.int32)
        for kk in reversed(range(len(rows) - 1)):
            idx = jnp.where(rows[kk] == top, kk, idx)
        return top, idx

    v1, i1 = first_argmax(el)
    v2, i2 = first_argmax([jnp.where(i1 == kk, -jnp.inf, el[kk]) for kk in range(EXPERTS_PER_GROUP)])
    t = jnp.exp(v2 - v1)
    p1 = 1.0 / (1.0 + t)
    p2 = t * p1

    lo = jnp.minimum(i1, i2)
    hi = jnp.maximum(i1, i2)
    w_lo = jnp.where(i1 < i2, p_g * p1, p_g * p2)
    w_hi = jnp.where(i1 < i2, p_g * p2, p_g * p1)
    cls = g_idx * N_PAIRS + ((lo * (2 * EXPERTS_PER_GROUP - 1 - lo)) >> 1) + hi - lo - 1
    return cls, w_lo, w_hi


def _router_emit(cls, w_lo, w_hi, tri_ref, x1_ref, rt_ref, cnt_ref):
    cls_f = cls.astype(F32)
    onehot = jnp.broadcast_to(cls, (CLS_ROWS, TQ)) == lax.broadcasted_iota(jnp.int32, (CLS_ROWS, TQ), 0)
    onehot_bf = onehot.astype(F32).astype(BF16)
    prefix = _dot(onehot_bf, tri_ref[...])
    rank = jnp.sum(jnp.where(onehot, prefix, 0.0), axis=0, keepdims=True) - 1.0
    counts = _dot_nt(jnp.ones((8, TQ), BF16), onehot_bf)
    zero = jnp.zeros((1, TQ), F32)
    rt_ref[0] = jnp.concatenate([cls_f, rank] + [zero] * 6, axis=0)
    cnt_ref[0] = jnp.concatenate([counts, jnp.zeros((8, LANES - CLS_ROWS), F32)], axis=1)
    cols = jnp.concatenate([w_lo, w_hi, cls_f, jnp.zeros((LANES - 3, TQ), F32)], axis=0)
    x1_ref[0, :, D_MODEL:] = cols.T


FLAG_FIRST, FLAG_VALID = 1, 2
GATHER_SLOTS = 3


def _moe_kernel(dest_ref, tile_ref, elo_ref, ehi_ref, cls_ref, flag_ref,
                g2_ref, wg_lo, wu_lo, wd_lo, wg_hi, wu_hi, wd_hi, x1_hbm,
                out_hbm,
                perm_ref, xbuf, obuf, h2buf, wbuf_up, wbuf_down, gsem, ssem):
    k = pl.program_id(0)
    nk = pl.num_programs(0)
    t = tile_ref[k]
    xslot = t % GATHER_SLOTS
    oslot = t % 2
    flags = flag_ref[k]
    n_tok = x1_hbm.shape[0]
    n_tiles = n_tok // TM

    def gather_copy(r, tile, s):
        tok = perm_ref[tile * TM + r]
        return pltpu.make_async_copy(x1_hbm.at[pl.ds(tok, 1), :], xbuf.at[s, pl.ds(r, 1), :], gsem.at[s])

    def scatter_copy(r, tile, s):
        tok = perm_ref[tile * TM + r]
        return pltpu.make_async_copy(obuf.at[s, pl.ds(r, 1), :], out_hbm.at[pl.ds(tok, 1), :], ssem.at[s])

    k_prev = jnp.maximum(k - 1, 0)
    for side, (e_ref, wg, wu, wd) in enumerate(((elo_ref, wg_lo, wu_lo, wd_lo), (ehi_ref, wg_hi, wu_hi, wd_hi))):
        @pl.when((k == 0) | (e_ref[k] != e_ref[k_prev]))
        def _(side=side, wg=wg, wu=wu, wd=wd):
            wbuf_up[2 * side] = wg[0]
            wbuf_up[2 * side + 1] = wu[0]
            wbuf_down[side] = wd[0]

    def for_rows(fn):
        def body(r, carry):
            fn(r)
            return carry
        lax.fori_loop(0, TM, body, 0, unroll=8)

    @pl.when(k == 0)
    def _():
        def invert(tok, carry):
            perm_ref[dest_ref[tok]] = tok
            return carry
        lax.fori_loop(0, n_tok, invert, 0, unroll=16)
        for tile in range(GATHER_SLOTS - 1):
            for_rows(lambda r, tile=tile: gather_copy(r, tile, tile).start())

    is_first = (flags & FLAG_FIRST) != 0
    is_valid = (flags & FLAG_VALID) != 0
    ahead = GATHER_SLOTS - 1
    prefetch_tile = jnp.minimum(t + ahead, n_tiles - 1)
    prefetch_slot = (t + ahead) % GATHER_SLOTS

    def gather_tile_wait(s):
        pltpu.make_async_copy(x1_hbm.at[pl.ds(0, TM), :], xbuf.at[s], gsem.at[s]).wait()

    def scatter_tile_wait(s):
        pltpu.make_async_copy(obuf.at[s], out_hbm.at[pl.ds(0, TM), :], ssem.at[s]).wait()

    @pl.when(is_first)
    def _():
        gather_tile_wait(xslot)

        @pl.when(t >= 2)
        def _():
            scatter_tile_wait(oslot)

        xr = xbuf[xslot, :, 0:D_MODEL]
        h2buf[...] = (_rms(xr) * g2_ref[...]).astype(BF16)
        obuf[oslot] = xr

    def experts():
        w = xbuf[xslot, :, D_MODEL:XW]
        mine = w[:, 2:3] == cls_ref[k].astype(F32)
        h2 = h2buf[...]
        y = None
        for side in range(2):
            c = jnp.where(mine, w[:, side:side + 1], 0.0)
            hid = jax.nn.silu(_dot(h2, wbuf_up[2 * side])) * _dot(h2, wbuf_up[2 * side + 1]) * c
            part = _dot(hid.astype(BF16), wbuf_down[side])
            y = part if y is None else y + part
        obuf[oslot] += y

    @pl.when(is_first & (t > 0))
    def _():
        for r in range(TM):
            gather_copy(r, prefetch_tile, prefetch_slot).start()
            scatter_copy(r, t - 1, 1 - oslot).start()
        experts()

    @pl.when(is_first & (t == 0))
    def _():
        for r in range(TM):
            gather_copy(r, prefetch_tile, prefetch_slot).start()
        experts()

    @pl.when(is_valid & jnp.logical_not(is_first))
    def _():
        experts()

    @pl.when(k == nk - 1)
    def _():
        last = n_tiles - 1
        for_rows(lambda r: scatter_copy(r, last, last % 2).start())
        for tile in (last - 1, last):
            scatter_tile_wait(tile % 2)
        for extra in range(1, GATHER_SLOTS):
            gather_tile_wait((last + extra) % GATHER_SLOTS)


_CLASS_PAIRS = [(a, b) for a in range(EXPERTS_PER_GROUP) for b in range(a + 1, EXPERTS_PER_GROUP)]


def _routing_plan(cls, rank, tile_counts, n_tiles):
    counts = jnp.sum(tile_counts, axis=0)
    off = jnp.concatenate([jnp.zeros((1,), jnp.int32), jnp.cumsum(counts).astype(jnp.int32)])
    tile_base = jnp.cumsum(tile_counts, axis=0) - tile_counts + off[None, :-1]
    onehot = cls[:, :, None] == jnp.arange(N_CLASSES, dtype=jnp.int32)[None, None, :]
    dest = (jnp.sum(jnp.where(onehot, tile_base[:, None, :], 0), axis=-1) + rank).reshape(-1)
    first_tile = off[:-1] // TM
    last_tile = (off[1:] - 1) // TM
    n_items = jnp.where(counts > 0, last_tile - first_tile + 1, 0)
    istart = jnp.concatenate([jnp.zeros((1,), jnp.int32), jnp.cumsum(n_items).astype(jnp.int32)])
    total = istart[-1]
    ni = n_tiles + N_CLASSES - 1
    k = jnp.arange(ni, dtype=jnp.int32)
    valid = k < total
    kc = jnp.minimum(k, total - 1)
    c_of_k = jnp.minimum(jnp.sum(istart[None, 1:] <= kc[:, None], axis=1).astype(jnp.int32), N_CLASSES - 1)
    sel = c_of_k[:, None] == jnp.arange(N_CLASSES, dtype=jnp.int32)[None, :]
    pick = lambda table: jnp.sum(jnp.where(sel, table[None, :], 0), axis=1).astype(jnp.int32)
    tile_k = pick(first_tile) + kc - pick(istart[:-1])
    prev_tile = jnp.concatenate([jnp.full((1,), -1, jnp.int32), tile_k[:-1]])
    first = valid & (tile_k != prev_tile)
    flags = (first * FLAG_FIRST + valid * FLAG_VALID).astype(jnp.int32)
    experts_of_class = [(c // N_PAIRS * EXPERTS_PER_GROUP + lo, c // N_PAIRS * EXPERTS_PER_GROUP + hi)
                        for c in range(N_CLASSES) for lo, hi in [_CLASS_PAIRS[c % N_PAIRS]]]
    e_lo = pick(jnp.array([e[0] for e in experts_of_class], jnp.int32))
    e_hi = pick(jnp.array([e[1] for e in experts_of_class], jnp.int32))
    cls_k = jnp.where(valid, c_of_k, -1)
    return dest.astype(jnp.int32), tile_k, e_lo, e_hi, cls_k, flags


def _const_spec(shape):
    nd = len(shape)
    return pl.BlockSpec(shape, lambda *_: (0,) * nd, pipeline_mode=pl.Buffered(1))


def kernel(x, mem, positions, g_norm1, w_in, g_q, g_k, sinks, w_o_attn, w_conv_dw, b_conv_dw, g_conv_ln, b_conv_ln, w_conv_out, g_mem, w_kv_mem, g_xq, g_xk, w_o_mem, w_out, g_norm2, w_group, b_group, w_router, b_router, w_gate, w_up, w_down):
    B, S, D = x.shape
    M = mem.shape[1]
    assert D == D_MODEL and S % TQ == 0 and (B * S) % TM == 0 and w_in.shape[0] == 1
    assert (TQ // WINDOW) * ATT_KV_HEADS == MEM_HEADS and CONV_CH // LANES == N_BRANCHES + 1
    NT = S // TQ
    T = B * S
    l = 0

    row = lambda v: v.reshape(1, -1).astype(F32)

    mk, mv = pl.pallas_call(
        _memkv_kernel,
        grid=(B,),
        in_specs=[pl.BlockSpec((1, M, D), lambda b: (b, 0, 0)),
                  _const_spec((1, D)), _const_spec((D, 2 * XQ_WIDTH)),
                  _const_spec((1, XQ_WIDTH))],
        out_specs=[pl.BlockSpec((1, M, XQ_WIDTH), lambda b: (b, 0, 0))] * 2,
        out_shape=[jax.ShapeDtypeStruct((B, M, XQ_WIDTH), BF16)] * 2,
        compiler_params=pltpu.CompilerParams(dimension_semantics=("arbitrary",)),
        name="memkv",
    )(mem, row(g_mem[l]), w_kv_mem[l].astype(BF16), row(jnp.tile(g_xk[l], MEM_HEADS)))

    inv_freq = 1.0 / (ROPE_THETA ** (jnp.arange(0, HEAD_DIM, 2, dtype=F32) / HEAD_DIM))
    per_row = LANES // (HEAD_DIM // 2)
    ang = positions.astype(F32).reshape(B, S // per_row, per_row, 1) * inv_freq
    ang = ang.reshape(B, S // per_row, LANES)
    cos_tab, sin_tab = jnp.cos(ang), jnp.sin(ang)

    assert w_in.shape[-1] == IN_WIDTH
    wdw = jnp.zeros((CONV_HALO, CONV_CH), F32).at[:CONV_WIDTH].set(w_conv_dw[l])
    w_rt = jnp.zeros((RT_ROWS, D), F32).at[0:N_GROUPS].set(w_group[l].T).at[8:8 + N_EXPERTS].set(w_router[l].T)
    b_rt = jnp.zeros((RT_ROWS, 1), F32).at[0:N_GROUPS, 0].set(b_group[l]).at[8:8 + N_EXPERTS, 0].set(b_router[l])

    n_mix = B * NT
    tile_in = lambda s: jnp.minimum(s, n_mix - 1)
    tile_out = lambda s: jnp.maximum(s - 1, 0)
    tile3 = lambda last: pl.BlockSpec((1, TQ, last), lambda s: (tile_in(s) // NT, tile_in(s) % NT, 0))
    per_batch = pl.BlockSpec((1, M, XQ_WIDTH), lambda s: (tile_in(s) // NT, 0, 0))
    rope_spec = pl.BlockSpec((1, TQ // per_row, LANES), lambda s: (tile_in(s) // NT, tile_in(s) % NT, 0))
    in_specs = [
        tile3(D), rope_spec, rope_spec, per_batch, per_batch,
        _const_spec((1, D)),
        _const_spec((D, IN_WIDTH)),
        _const_spec((1, Q_WIDTH)), _const_spec((1, LANES)),
        pl.BlockSpec(memory_space=pltpu.SMEM),
        _const_spec((Q_WIDTH, D)),
        _const_spec((CONV_HALO, CONV_CH)), _const_spec((1, CONV_CH)), _const_spec((1, CONV_CH)), _const_spec((1, CONV_CH)),
        _const_spec((CONV_CH, D)),
        _const_spec((1, XQ_WIDTH)),
        _const_spec((XQ_WIDTH, D)),
        _const_spec((D, D)),
        _const_spec((1, D)),
        _const_spec((RT_ROWS, D)), _const_spec((RT_ROWS, 1)), _const_spec((TQ, TQ)),
    ]
    tri = (jnp.arange(TQ)[:, None] <= jnp.arange(TQ)[None, :]).astype(BF16)
    per_tile = lambda rows, last: pl.BlockSpec((1, rows, last), lambda s: (tile_out(s), 0, 0))
    x1_spec = pl.BlockSpec((1, TQ, XW), lambda s: (tile_out(s) // NT, tile_out(s) % NT, 0))
    assert (N_EXPERTS * D) % n_mix == 0 and (N_EXPERTS * EXPERT_FF) % n_mix == 0
    up_rows, down_rows = N_EXPERTS * D // n_mix, N_EXPERTS * EXPERT_FF // n_mix
    up_slice = pl.BlockSpec((up_rows, EXPERT_FF), lambda s: (tile_in(s), 0))
    down_slice = pl.BlockSpec((down_rows, D), lambda s: (tile_in(s), 0))
    in_specs += [up_slice, up_slice, down_slice]
    x1, rt, cnt, wg, wu, wd = pl.pallas_call(
        functools.partial(_mixer_kernel, tiles_per_seq=NT),
        grid=(n_mix + 1,),
        in_specs=in_specs,
        out_specs=[x1_spec, per_tile(8, TQ), per_tile(8, LANES), up_slice, up_slice, down_slice],
        out_shape=[jax.ShapeDtypeStruct((B, S, XW), F32), jax.ShapeDtypeStruct((B * NT, 8, TQ), F32),
                   jax.ShapeDtypeStruct((B * NT, 8, LANES), F32),
                   jax.ShapeDtypeStruct((N_EXPERTS * D, EXPERT_FF), BF16),
                   jax.ShapeDtypeStruct((N_EXPERTS * D, EXPERT_FF), BF16),
                   jax.ShapeDtypeStruct((N_EXPERTS * EXPERT_FF, D), BF16)],
        scratch_shapes=[pltpu.VMEM((WINDOW + TQ, LANES), BF16)] * 4
        + [pltpu.VMEM((CONV_CH // LANES, CONV_HALO + TQ, LANES), F32),
           pltpu.VMEM((CONV_CH // LANES, TQ, LANES), F32), pltpu.VMEM((TQ, Q_WIDTH), BF16),
           pltpu.VMEM((TQ, XQ_WIDTH), BF16), pltpu.VMEM((TQ, D), F32),
           pltpu.VMEM((TQ, LANES), F32), pltpu.VMEM((TQ, LANES), F32)],
        compiler_params=pltpu.CompilerParams(dimension_semantics=("arbitrary",),
                                             vmem_limit_bytes=VMEM_LIMIT),
        name="mixer",
    )(x, cos_tab, sin_tab, mk, mv, row(g_norm1[l]), w_in[l].astype(BF16),
      row(jnp.tile(g_q[l], ATT_HEADS) * (HEAD_DIM ** -0.5 * LOG2E)), row(jnp.tile(g_k[l], ATT_KV_HEADS)),
      sinks[l].astype(F32) * LOG2E, w_o_attn[l].astype(BF16),
      wdw, row(b_conv_dw[l]), row(g_conv_ln[l]), row(b_conv_ln[l]), w_conv_out[l].astype(BF16),
      row(jnp.tile(g_xq[l], MEM_HEADS) * (MEM_HEAD_DIM ** -0.5 * LOG2E)), w_o_mem[l].astype(BF16),
      w_out[l].astype(BF16), row(g_norm2[l]), w_rt, b_rt, tri,
      w_gate[l].reshape(N_EXPERTS * D, EXPERT_FF), w_up[l].reshape(N_EXPERTS * D, EXPERT_FF),
      w_down[l].reshape(N_EXPERTS * EXPERT_FF, D))

    n_tiles = T // TM
    plan = _routing_plan(rt[:, 0, :].astype(jnp.int32), rt[:, 1, :].astype(jnp.int32),
                         cnt[:, 0, :N_CLASSES].astype(jnp.int32), n_tiles)
    dest, tile_k, e_lo, e_hi, cls_k, flags = plan
    ni = tile_k.shape[0]
    wg = wg.reshape(N_EXPERTS, D, EXPERT_FF)
    wu = wu.reshape(N_EXPERTS, D, EXPERT_FF)
    wd = wd.reshape(N_EXPERTS, EXPERT_FF, D)
    lo_map = lambda k, dest, tile, elo, ehi, cls, flg: (elo[k], 0, 0)
    hi_map = lambda k, dest, tile, elo, ehi, cls, flg: (ehi[k], 0, 0)
    up_block = (1, D, EXPERT_FF)
    down_block = (1, EXPERT_FF, D)
    out = pl.pallas_call(
        _moe_kernel,
        grid_spec=pltpu.PrefetchScalarGridSpec(
            num_scalar_prefetch=6,
            grid=(ni,),
            in_specs=[pl.BlockSpec((1, D), lambda k, *_: (0, 0)),
                      pl.BlockSpec(up_block, lo_map), pl.BlockSpec(up_block, lo_map),
                      pl.BlockSpec(down_block, lo_map),
                      pl.BlockSpec(up_block, hi_map), pl.BlockSpec(up_block, hi_map),
                      pl.BlockSpec(down_block, hi_map),
                      pl.BlockSpec(memory_space=pl.ANY)],
            out_specs=pl.BlockSpec(memory_space=pl.ANY),
            scratch_shapes=[pltpu.SMEM((T,), jnp.int32),
                            pltpu.VMEM((GATHER_SLOTS, TM, XW), F32), pltpu.VMEM((2, TM, D), F32),
                            pltpu.VMEM((TM, D), BF16),
                            pltpu.VMEM((4, D, EXPERT_FF), BF16), pltpu.VMEM((2, EXPERT_FF, D), BF16),
                            pltpu.SemaphoreType.DMA((GATHER_SLOTS,)), pltpu.SemaphoreType.DMA((2,))]),
        out_shape=jax.ShapeDtypeStruct((T, D), F32),
        compiler_params=pltpu.CompilerParams(dimension_semantics=("arbitrary",),
                                             vmem_limit_bytes=VMEM_LIMIT),
        name="moe",
    )(dest, tile_k, e_lo, e_hi, cls_k, flags, row(g_norm2[l]), wg, wu, wd, wg, wu, wd,
      x1.reshape(T, XW))
    return out.reshape(B, S, D)
```

```python
import functools

import jax
import jax.numpy as jnp
from jax import lax
from jax.experimental import pallas as pl
from jax.experimental.pallas import tpu as pltpu

F32 = jnp.float32
BF16 = jnp.bfloat16

D_MODEL = 1024
ATT_HEADS = 8
ATT_KV_HEADS = 2
HEAD_DIM = 64
WINDOW = 128
ROPE_THETA = 10000.0
CONV_CH = 512
CONV_WIDTH = 31
MEM_HEADS = 4
MEM_HEAD_DIM = 128
N_BRANCHES = 3
N_GROUPS = 4
EXPERTS_PER_GROUP = 4
N_EXPERTS = N_GROUPS * EXPERTS_PER_GROUP
EXPERT_FF = 512
EPS = 1e-6
LN_EPS = 1e-5
NEG_INF = -1e30
LOG2E = 1.4426950408889634

Q_WIDTH = ATT_HEADS * HEAD_DIM
KV_WIDTH = ATT_KV_HEADS * HEAD_DIM
GLU_WIDTH = 2 * CONV_CH
XQ_WIDTH = MEM_HEADS * MEM_HEAD_DIM
GATE_WIDTH = N_BRANCHES * D_MODEL
COL_GLU = Q_WIDTH + 2 * KV_WIDTH
COL_XQ = COL_GLU + GLU_WIDTH
COL_GATE = COL_XQ + XQ_WIDTH
IN_WIDTH = COL_GATE + GATE_WIDTH
ROW_WCO = Q_WIDTH
ROW_WOM = ROW_WCO + CONV_CH
ROW_WOUT = ROW_WOM + XQ_WIDTH
PROJ_ROWS = ROW_WOUT + D_MODEL

LANES = 128
TQ = 256
CONV_HALO = 32
CONV_ROWS = 128
TM = 256
RT_ROWS = 32
N_PAIRS = EXPERTS_PER_GROUP * (EXPERTS_PER_GROUP - 1) // 2
N_CLASSES = N_GROUPS * N_PAIRS
CLS_ROWS = 32
XW = D_MODEL + LANES
VMEM_LIMIT = 56 * 1024 * 1024


def _dot(a, b):
    return jnp.dot(a, b, preferred_element_type=F32)


def _dot_nt(a, b):
    return lax.dot_general(a, b, (((1,), (1,)), ((), ())), preferred_element_type=F32)


def _split_bf16(t):
    hi = t.astype(BF16)
    lo = (t - hi.astype(F32)).astype(BF16)
    return hi, lo


def _rms(t):
    return t * lax.rsqrt(jnp.mean(t * t, axis=-1, keepdims=True) + EPS)


def _rms_per_mem_head(t):
    return jnp.concatenate(
        [_rms(t[:, hd * MEM_HEAD_DIM:(hd + 1) * MEM_HEAD_DIM]) for hd in range(MEM_HEADS)], axis=-1)


def _memkv_kernel(mem_ref, gmem_ref, wkv_ref, gxk_ref, mk_ref, mv_ref):
    mn = (_rms(mem_ref[0]) * gmem_ref[...]).astype(BF16)
    kv = _dot(mn, wkv_ref[...])
    mk_ref[0] = (_rms_per_mem_head(kv[:, :XQ_WIDTH]) * gxk_ref[...]).astype(BF16)
    mv_ref[0] = kv[:, XQ_WIDTH:].astype(BF16)


def _mixer_kernel(x_ref, cos_ref, sin_ref, mk_ref, mv_ref, g1_ref, win_ref,
                  gq_ref, gk_ref, sinks_ref, wproj_ref, wdw_ref,
                  bdw_ref, gln_ref, bln_ref, gxq_ref, g2_ref,
                  wrt_ref, brt_ref, tri_ref, ewg_ref, ewu_ref, ewd_ref,
                  x1_ref, rt_ref, cnt_ref, ewg_out, ewu_out, ewd_out,
                  k0_ref, k1_ref, v0_ref, v1_ref, ubuf_ref, ybuf_ref, o_ref, om_ref, x1prev_ref,
                  cosd_ref, sind_ref, *, tiles_per_seq):
    s = pl.program_id(0)
    n_tiles = pl.num_programs(0) - 1
    i = jnp.minimum(s, n_tiles - 1) % tiles_per_seq
    kbands = (k0_ref, k1_ref)
    vbands = (v0_ref, v1_ref)

    @pl.when(s == 0)
    def _():
        x1prev_ref[...] = jnp.zeros((TQ, D_MODEL), F32)

    @pl.when(i == 0)
    def _():
        for r in kbands + vbands:
            r[0:WINDOW, :] = jnp.zeros((WINDOW, LANES), BF16)
        ubuf_ref[:, 0:CONV_HALO, :] = jnp.zeros((CONV_CH // LANES, CONV_HALO, LANES), F32)

    x1_prev = x1prev_ref[...]
    x1_ref[0, :, 0:D_MODEL] = x1_prev

    ewg_out[...] = ewg_ref[...].astype(BF16)
    ewu_out[...] = ewu_ref[...].astype(BF16)
    ewd_out[...] = ewd_ref[...].astype(BF16)

    x = x_ref[0]
    h = (_rms(x) * g1_ref[...]).astype(BF16)

    n_slabs = CONV_CH // LANES
    half = TQ // 2
    group = ATT_HEADS // ATT_KV_HEADS

    glu = _dot(h, win_ref[:, COL_GLU:COL_XQ])
    route_logits = _router_logits(x1_prev, g2_ref, wrt_ref, brt_ref)
    u = glu[:, :CONV_CH] * jax.nn.sigmoid(glu[:, CONV_CH:])
    for c in range(n_slabs):
        ubuf_ref[c, CONV_HALO:CONV_HALO + TQ, :] = u[:, c * LANES:(c + 1) * LANES]
    base = CONV_HALO - (CONV_WIDTH - 1)
    acc = [[None] * n_slabs for _ in range(2)]

    def conv_slab(c):
        cols = slice(c * LANES, (c + 1) * LANES)
        for par in range(2):
            parts = []
            for q0 in range(0, half, CONV_ROWS):
                a = jnp.broadcast_to(bdw_ref[:, cols], (CONV_ROWS, LANES))
                for j in range(CONV_WIDTH):
                    a = a + wdw_ref[j:j + 1, cols] * ubuf_ref[
                        c, pl.ds(base + j + par + 2 * q0, CONV_ROWS, stride=2), :]
                parts.append(a)
            acc[par][c] = jnp.concatenate(parts, axis=0)
        ubuf_ref[c, 0:CONV_HALO, :] = ubuf_ref[c, TQ:TQ + CONV_HALO, :]

    def gate(b):
        return jax.nn.sigmoid(_dot(h, win_ref[:, COL_GATE + b * D_MODEL:COL_GATE + (b + 1) * D_MODEL]))

    def conv_finish():
        for par in range(2):
            row_sum = functools.reduce(jnp.add, [jnp.sum(a, axis=-1, keepdims=True) for a in acc[par]])
            mu = row_sum * (1.0 / CONV_CH)
            cen = [a - mu for a in acc[par]]
            sq_sum = functools.reduce(jnp.add, [jnp.sum(t * t, axis=-1, keepdims=True) for t in cen])
            rstd = lax.rsqrt(sq_sum * (1.0 / CONV_CH) + LN_EPS)
            for c in range(n_slabs):
                cols = slice(c * LANES, (c + 1) * LANES)
                yln = cen[c] * rstd * gln_ref[:, cols] + bln_ref[:, cols]
                ybuf_ref[c, pl.ds(par, half, stride=2), :] = yln * jax.nn.sigmoid(yln)
        uo = jnp.concatenate([ybuf_ref[c] for c in range(n_slabs)], axis=-1).astype(BF16)
        return _dot(uo, wproj_ref[ROW_WCO:ROW_WOM, :])

    for c in range(n_slabs):
        conv_slab(c)
    qkv = _dot(h, win_ref[:, 0:COL_GLU])
    gates = [gate(0)]
    q = qkv[:, :Q_WIDTH]
    k = qkv[:, Q_WIDTH:Q_WIDTH + KV_WIDTH]
    v = qkv[:, Q_WIDTH + KV_WIDTH:]
    gates.append(gate(1))

    lane = lax.broadcasted_iota(jnp.int32, (TQ, LANES), 1)
    first_half = (lane % HEAD_DIM) < (HEAD_DIM // 2)
    low_head = lane < HEAD_DIM
    n_freq = HEAD_DIM // 2
    per_row = LANES // n_freq
    quarter = lax.broadcasted_iota(jnp.int32, (TQ // per_row, LANES), 1) // n_freq
    sign = jnp.where(quarter % 2 == 0, -1.0, 1.0)
    for table_ref, out_ref, scale in ((cos_ref, cosd_ref, None), (sin_ref, sind_ref, sign)):
        dense = table_ref[0]
        rolled = [dense] + [pltpu.roll(dense, n_freq * kk, 1) for kk in range(1, per_row)]
        for j in range(per_row):
            spread = rolled[(-j) % per_row]
            for qq in range(1, per_row):
                spread = jnp.where(quarter == qq, rolled[(qq - j) % per_row], spread)
            out_ref[pl.ds(j, TQ // per_row, stride=per_row), :] = spread if scale is None else spread * scale
    cosd = cosd_ref[...]
    sind = sind_ref[...]

    def rope(t):
        rot = jnp.where(first_half, pltpu.roll(t, LANES - HEAD_DIM // 2, 1),
                        pltpu.roll(t, HEAD_DIM // 2, 1))
        return t * cosd + rot * sind

    def dup_halves(t):
        swapped = pltpu.roll(t, HEAD_DIM, 1)
        return jnp.where(low_head, t, swapped), jnp.where(low_head, swapped, t)

    def qk_norm(t, gain):
        sq = t * t
        lo = jnp.sum(jnp.where(low_head, sq, 0.0), axis=-1, keepdims=True)
        hi = jnp.sum(jnp.where(low_head, 0.0, sq), axis=-1, keepdims=True)
        inv = jnp.where(low_head, lax.rsqrt(lo * (1.0 / HEAD_DIM) + EPS), lax.rsqrt(hi * (1.0 / HEAD_DIM) + EPS))
        return t * inv * gain

    kd = dup_halves(rope(qk_norm(k, gk_ref[...])))
    vd = dup_halves(v)
    for kvh in range(ATT_KV_HEADS):
        kbands[kvh][WINDOW:WINDOW + TQ, :] = kd[kvh].astype(BF16)
        vbands[kvh][WINDOW:WINDOW + TQ, :] = vd[kvh].astype(BF16)

    q_heads = []
    for c in range(Q_WIDTH // LANES):
        cols = slice(c * LANES, (c + 1) * LANES)
        qc = rope(qk_norm(q[:, cols], gq_ref[:, cols]))
        q_heads.append(jnp.where(low_head, qc, 0.0).astype(BF16))
        q_heads.append(jnp.where(low_head, 0.0, qc).astype(BF16))

    qi = lax.broadcasted_iota(jnp.int32, (WINDOW, 2 * WINDOW), 0)
    kj = lax.broadcasted_iota(jnp.int32, (WINDOW, 2 * WINDOW), 1)
    band_mask = (kj > qi) & (kj <= qi + WINDOW)
    first_mask = band_mask & (kj >= jnp.where(i > 0, 0, WINDOW))
    low_head_w = lax.broadcasted_iota(jnp.int32, (WINDOW, LANES), 1) < HEAD_DIM

    def attention_scores(step):
        n, kvh = divmod(step, ATT_KV_HEADS)
        rows = slice(n * WINDOW, (n + 1) * WINDOW)
        mask = first_mask if n == 0 else band_mask
        heads = range(kvh * group, (kvh + 1) * group)
        q_st = jnp.concatenate([q_heads[hd][rows] for hd in heads], axis=0)
        kband = kbands[kvh][n * WINDOW:n * WINDOW + 2 * WINDOW, :]
        s_all = _dot_nt(q_st, kband)
        es, dens = [], []
        for j, hd in enumerate(heads):
            s = jnp.where(mask, s_all[j * WINDOW:(j + 1) * WINDOW], NEG_INF)
            sink = sinks_ref[hd]
            m = jnp.maximum(jnp.max(s, axis=-1, keepdims=True), sink)
            e = jnp.exp2(s - m)
            dens.append(jnp.sum(e, axis=-1, keepdims=True) + jnp.exp2(sink - m))
            es.append(e.astype(BF16))
        return jnp.concatenate(es, axis=0), dens

    def attention_output(step, probs):
        n, kvh = divmod(step, ATT_KV_HEADS)
        rows = slice(n * WINDOW, (n + 1) * WINDOW)
        e_all, dens = probs
        vband = vbands[kvh][n * WINDOW:n * WINDOW + 2 * WINDOW, :]
        o_all = _dot(e_all, vband)
        for cc in range(group // 2):
            oa = o_all[(2 * cc) * WINDOW:(2 * cc + 1) * WINDOW] / dens[2 * cc]
            ob = o_all[(2 * cc + 1) * WINDOW:(2 * cc + 2) * WINDOW] / dens[2 * cc + 1]
            col = kvh * (group // 2) + cc
            o_ref[rows, col * LANES:(col + 1) * LANES] = jnp.where(low_head_w, oa, ob).astype(BF16)

    def mem_scores(hd, xq):
        cols = slice(hd * MEM_HEAD_DIM, (hd + 1) * MEM_HEAD_DIM)
        s = _dot_nt(xq[:, cols], mk_ref[0, :, cols])
        e = jnp.exp2(s - jnp.max(s, axis=-1, keepdims=True))
        return e.astype(BF16), jnp.sum(e, axis=-1, keepdims=True)

    def mem_output(hd, probs):
        cols = slice(hd * MEM_HEAD_DIM, (hd + 1) * MEM_HEAD_DIM)
        e, den = probs
        om_ref[:, cols] = (_dot(e, mv_ref[0, :, cols]) / den).astype(BF16)

    route = _router_select(route_logits)
    p0 = attention_scores(0)
    gates.append(gate(2))
    p1 = attention_scores(1)
    attention_output(0, p0)
    xq = _dot(h, win_ref[:, COL_XQ:COL_GATE])
    p2 = attention_scores(2)
    attention_output(1, p1)
    _router_emit(*route, tri_ref, x1_ref, rt_ref, cnt_ref)
    xq = (_rms_per_mem_head(xq) * gxq_ref[...]).astype(BF16)
    p3 = attention_scores(3)
    attention_output(2, p2)
    y_conv = conv_finish()
    m0 = mem_scores(0, xq)
    attention_output(3, p3)
    for r in kbands + vbands:
        r[0:WINDOW, :] = r[TQ:TQ + WINDOW, :]
    m1 = mem_scores(1, xq)
    y_attn = _dot(o_ref[...], wproj_ref[0:ROW_WCO, :])
    mem_output(0, m0)
    m2 = mem_scores(2, xq)
    mem_output(1, m1)
    m3 = mem_scores(3, xq)
    mem_output(2, m2)
    merged = gates[0] * y_attn + gates[1] * y_conv
    mem_output(3, m3)
    merged = merged + gates[2] * _dot(om_ref[...], wproj_ref[ROW_WOM:ROW_WOUT, :])

    x1prev_ref[...] = x + _dot(merged.astype(BF16), wproj_ref[ROW_WOUT:ROW_WOUT + D_MODEL, :])


def _router_logits(x1, g2_ref, wrt_ref, brt_ref):
    h2 = _rms(x1) * g2_ref[...]
    h_hi, h_lo = _split_bf16(h2)
    w_hi, w_lo = _split_bf16(wrt_ref[...])
    return _dot_nt(w_hi, h_hi) + _dot_nt(w_hi, h_lo) + _dot_nt(w_lo, h_hi) + brt_ref[...]


def _router_select(lt):
    gl = [lt[g:g + 1] for g in range(N_GROUPS)]
    gmax = functools.reduce(jnp.maximum, gl)
    g_idx = jnp.full((1, TQ), N_GROUPS - 1, jnp.int32)
    for g in reversed(range(N_GROUPS - 1)):
        g_idx = jnp.where(gl[g] == gmax, g, g_idx)
    p_g = 1.0 / functools.reduce(jnp.add, [jnp.exp(r - gmax) for r in gl])
    el = []
    for kk in range(EXPERTS_PER_GROUP):
        row = lt[8 + kk:9 + kk]
        for g in range(1, N_GROUPS):
            off = 8 + g * EXPERTS_PER_GROUP + kk
            row = jnp.where(g_idx == g, lt[off:off + 1], row)
        el.append(row)

    def first_argmax(rows):
        top = functools.reduce(jnp.maximum, rows)
        idx = jnp.full((1, TQ), len(rows) - 1, jnp.int32)
        for kk in reversed(range(len(rows) - 1)):
            idx = jnp.where(rows[kk] == top, kk, idx)
        return top, idx

    v1, i1 = first_argmax(el)
    v2, i2 = first_argmax([jnp.where(i1 == kk, -jnp.inf, el[kk]) for kk in range(EXPERTS_PER_GROUP)])
    t = jnp.exp(v2 - v1)
    p1 = 1.0 / (1.0 + t)
    p2 = t * p1

    lo = jnp.minimum(i1, i2)
    hi = jnp.maximum(i1, i2)
    w_lo = jnp.where(i1 < i2, p_g * p1, p_g * p2)
    w_hi = jnp.where(i1 < i2, p_g * p2, p_g * p1)
    cls = g_idx * N_PAIRS + ((lo * (2 * EXPERTS_PER_GROUP - 1 - lo)) >> 1) + hi - lo - 1
    return cls, w_lo, w_hi


def _router_emit(cls, w_lo, w_hi, tri_ref, x1_ref, rt_ref, cnt_ref):
    cls_f = cls.astype(F32)
    onehot = jnp.broadcast_to(cls, (CLS_ROWS, TQ)) == lax.broadcasted_iota(jnp.int32, (CLS_ROWS, TQ), 0)
    onehot_bf = onehot.astype(F32).astype(BF16)
    prefix = _dot(onehot_bf, tri_ref[...])
    rank = jnp.sum(jnp.where(onehot, prefix, 0.0), axis=0, keepdims=True) - 1.0
    counts = _dot_nt(jnp.ones((8, TQ), BF16), onehot_bf)
    zero = jnp.zeros((1, TQ), F32)
    rt_ref[0] = jnp.concatenate([cls_f, rank] + [zero] * 6, axis=0)
    cnt_ref[0] = jnp.concatenate([counts, jnp.zeros((8, LANES - CLS_ROWS), F32)], axis=1)
    cols = jnp.concatenate([w_lo, w_hi, cls_f, jnp.zeros((LANES - 3, TQ), F32)], axis=0)
    x1_ref[0, :, D_MODEL:] = cols.T


FLAG_FIRST, FLAG_VALID = 1, 2
GATHER_SLOTS = 3


def _moe_kernel(dest_ref, tile_ref, elo_ref, ehi_ref, cls_ref, flag_ref,
                g2_ref, wg_lo, wu_lo, wd_lo, wg_hi, wu_hi, wd_hi, x1_hbm,
                out_hbm,
                perm_ref, xbuf, obuf, h2buf, wbuf_up, wbuf_down, gsem, ssem):
    k = pl.program_id(0)
    nk = pl.num_programs(0)
    t = tile_ref[k]
    xslot = t % GATHER_SLOTS
    oslot = t % 2
    flags = flag_ref[k]
    n_tok = x1_hbm.shape[0]
    n_tiles = n_tok // TM

    def gather_copy(r, tile, s):
        tok = perm_ref[tile * TM + r]
        return pltpu.make_async_copy(x1_hbm.at[pl.ds(tok, 1), :], xbuf.at[s, pl.ds(r, 1), :], gsem.at[s])

    def scatter_copy(r, tile, s):
        tok = perm_ref[tile * TM + r]
        return pltpu.make_async_copy(obuf.at[s, pl.ds(r, 1), :], out_hbm.at[pl.ds(tok, 1), :], ssem.at[s])

    k_prev = jnp.maximum(k - 1, 0)
    for side, (e_ref, wg, wu, wd) in enumerate(((elo_ref, wg_lo, wu_lo, wd_lo), (ehi_ref, wg_hi, wu_hi, wd_hi))):
        @pl.when((k == 0) | (e_ref[k] != e_ref[k_prev]))
        def _(side=side, wg=wg, wu=wu, wd=wd):
            wbuf_up[2 * side] = wg[0]
            wbuf_up[2 * side + 1] = wu[0]
            wbuf_down[side] = wd[0]

    def for_rows(fn):
        def body(r, carry):
            fn(r)
            return carry
        lax.fori_loop(0, TM, body, 0, unroll=8)

    @pl.when(k == 0)
    def _():
        def invert(tok, carry):
            perm_ref[dest_ref[tok]] = tok
            return carry
        lax.fori_loop(0, n_tok, invert, 0, unroll=16)
        for tile in range(GATHER_SLOTS - 1):
            for_rows(lambda r, tile=tile: gather_copy(r, tile, tile).start())

    is_first = (flags & FLAG_FIRST) != 0
    is_valid = (flags & FLAG_VALID) != 0
    ahead = GATHER_SLOTS - 1
    prefetch_tile = jnp.minimum(t + ahead, n_tiles - 1)
    prefetch_slot = (t + ahead) % GATHER_SLOTS

    def gather_tile_wait(s):
        pltpu.make_async_copy(x1_hbm.at[pl.ds(0, TM), :], xbuf.at[s], gsem.at[s]).wait()

    def scatter_tile_wait(s):
        pltpu.make_async_copy(obuf.at[s], out_hbm.at[pl.ds(0, TM), :], ssem.at[s]).wait()

    @pl.when(is_first)
    def _():
        gather_tile_wait(xslot)

        @pl.when(t >= 2)
        def _():
            scatter_tile_wait(oslot)

        xr = xbuf[xslot, :, 0:D_MODEL]
        h2buf[...] = (_rms(xr) * g2_ref[...]).astype(BF16)
        obuf[oslot] = xr

    def experts():
        w = xbuf[xslot, :, D_MODEL:XW]
        mine = w[:, 2:3] == cls_ref[k].astype(F32)
        h2 = h2buf[...]
        y = None
        for side in range(2):
            c = jnp.where(mine, w[:, side:side + 1], 0.0)
            hid = jax.nn.silu(_dot(h2, wbuf_up[2 * side])) * _dot(h2, wbuf_up[2 * side + 1]) * c
            part = _dot(hid.astype(BF16), wbuf_down[side])
            y = part if y is None else y + part
        obuf[oslot] += y

    @pl.when(is_first & (t > 0))
    def _():
        for r in range(TM):
            gather_copy(r, prefetch_tile, prefetch_slot).start()
            scatter_copy(r, t - 1, 1 - oslot).start()
        experts()

    @pl.when(is_first & (t == 0))
    def _():
        for r in range(TM):
            gather_copy(r, prefetch_tile, prefetch_slot).start()
        experts()

    @pl.when(is_valid & jnp.logical_not(is_first))
    def _():
        experts()

    @pl.when(k == nk - 1)
    def _():
        last = n_tiles - 1
        for_rows(lambda r: scatter_copy(r, last, last % 2).start())
        for tile in (last - 1, last):
            scatter_tile_wait(tile % 2)
        for extra in range(1, GATHER_SLOTS):
            gather_tile_wait((last + extra) % GATHER_SLOTS)


_CLASS_PAIRS = [(a, b) for a in range(EXPERTS_PER_GROUP) for b in range(a + 1, EXPERTS_PER_GROUP)]


def _routing_plan(cls, rank, tile_counts, n_tiles):
    counts = jnp.sum(tile_counts, axis=0)
    off = jnp.concatenate([jnp.zeros((1,), jnp.int32), jnp.cumsum(counts).astype(jnp.int32)])
    tile_base = jnp.cumsum(tile_counts, axis=0) - tile_counts + off[None, :-1]
    onehot = cls[:, :, None] == jnp.arange(N_CLASSES, dtype=jnp.int32)[None, None, :]
    dest = (jnp.sum(jnp.where(onehot, tile_base[:, None, :], 0), axis=-1) + rank).reshape(-1)
    first_tile = off[:-1] // TM
    last_tile = (off[1:] - 1) // TM
    n_items = jnp.where(counts > 0, last_tile - first_tile + 1, 0)
    istart = jnp.concatenate([jnp.zeros((1,), jnp.int32), jnp.cumsum(n_items).astype(jnp.int32)])
    total = istart[-1]
    ni = n_tiles + N_CLASSES - 1
    k = jnp.arange(ni, dtype=jnp.int32)
    valid = k < total
    kc = jnp.minimum(k, total - 1)
    c_of_k = jnp.minimum(jnp.sum(istart[None, 1:] <= kc[:, None], axis=1).astype(jnp.int32), N_CLASSES - 1)
    sel = c_of_k[:, None] == jnp.arange(N_CLASSES, dtype=jnp.int32)[None, :]
    pick = lambda table: jnp.sum(jnp.where(sel, table[None, :], 0), axis=1).astype(jnp.int32)
    tile_k = pick(first_tile) + kc - pick(istart[:-1])
    prev_tile = jnp.concatenate([jnp.full((1,), -1, jnp.int32), tile_k[:-1]])
    first = valid & (tile_k != prev_tile)
    flags = (first * FLAG_FIRST + valid * FLAG_VALID).astype(jnp.int32)
    experts_of_class = [(c // N_PAIRS * EXPERTS_PER_GROUP + lo, c // N_PAIRS * EXPERTS_PER_GROUP + hi)
                        for c in range(N_CLASSES) for lo, hi in [_CLASS_PAIRS[c % N_PAIRS]]]
    e_lo = pick(jnp.array([e[0] for e in experts_of_class], jnp.int32))
    e_hi = pick(jnp.array([e[1] for e in experts_of_class], jnp.int32))
    cls_k = jnp.where(valid, c_of_k, -1)
    return dest.astype(jnp.int32), tile_k, e_lo, e_hi, cls_k, flags


def _const_spec(shape):
    nd = len(shape)
    return pl.BlockSpec(shape, lambda *_: (0,) * nd, pipeline_mode=pl.Buffered(1))


def kernel(x, mem, positions, g_norm1, w_in, g_q, g_k, sinks, w_o_attn, w_conv_dw, b_conv_dw, g_conv_ln, b_conv_ln, w_conv_out, g_mem, w_kv_mem, g_xq, g_xk, w_o_mem, w_out, g_norm2, w_group, b_group, w_router, b_router, w_gate, w_up, w_down):
    B, S, D = x.shape
    M = mem.shape[1]
    assert D == D_MODEL and S % TQ == 0 and (B * S) % TM == 0 and w_in.shape[0] == 1
    assert (TQ // WINDOW) * ATT_KV_HEADS == MEM_HEADS and CONV_CH // LANES == N_BRANCHES + 1
    NT = S // TQ
    T = B * S
    l = 0

    row = lambda v: v.reshape(1, -1).astype(F32)

    mk, mv = pl.pallas_call(
        _memkv_kernel,
        grid=(B,),
        in_specs=[pl.BlockSpec((1, M, D), lambda b: (b, 0, 0)),
                  _const_spec((1, D)), _const_spec((D, 2 * XQ_WIDTH)),
                  _const_spec((1, XQ_WIDTH))],
        out_specs=[pl.BlockSpec((1, M, XQ_WIDTH), lambda b: (b, 0, 0))] * 2,
        out_shape=[jax.ShapeDtypeStruct((B, M, XQ_WIDTH), BF16)] * 2,
        compiler_params=pltpu.CompilerParams(dimension_semantics=("arbitrary",)),
        name="memkv",
    )(mem, row(g_mem[l]), w_kv_mem[l].astype(BF16), row(jnp.tile(g_xk[l], MEM_HEADS)))

    inv_freq = 1.0 / (ROPE_THETA ** (jnp.arange(0, HEAD_DIM, 2, dtype=F32) / HEAD_DIM))
    per_row = LANES // (HEAD_DIM // 2)
    ang = positions.astype(F32).reshape(B, S // per_row, per_row, 1) * inv_freq
    ang = ang.reshape(B, S // per_row, LANES)
    cos_tab, sin_tab = jnp.cos(ang), jnp.sin(ang)

    assert w_in.shape[-1] == IN_WIDTH
    wdw = jnp.zeros((CONV_HALO, CONV_CH), F32).at[:CONV_WIDTH].set(w_conv_dw[l])
    w_rt = jnp.zeros((RT_ROWS, D), F32).at[0:N_GROUPS].set(w_group[l].T).at[8:8 + N_EXPERTS].set(w_router[l].T)
    b_rt = jnp.zeros((RT_ROWS, 1), F32).at[0:N_GROUPS, 0].set(b_group[l]).at[8:8 + N_EXPERTS, 0].set(b_router[l])

    n_mix = B * NT
    tile_in = lambda s: jnp.minimum(s, n_mix - 1)
    tile_out = lambda s: jnp.maximum(s - 1, 0)
    tile3 = lambda last: pl.BlockSpec((1, TQ, last), lambda s: (tile_in(s) // NT, tile_in(s) % NT, 0))
    per_batch = pl.BlockSpec((1, M, XQ_WIDTH), lambda s: (tile_in(s) // NT, 0, 0))
    rope_spec = pl.BlockSpec((1, TQ // per_row, LANES), lambda s: (tile_in(s) // NT, tile_in(s) % NT, 0))
    in_specs = [
        tile3(D), rope_spec, rope_spec, per_batch, per_batch,
        _const_spec((1, D)),
        _const_spec((D, IN_WIDTH)),
        _const_spec((1, Q_WIDTH)), _const_spec((1, LANES)),
        pl.BlockSpec(memory_space=pltpu.SMEM),
        _const_spec((PROJ_ROWS, D)),
        _const_spec((CONV_HALO, CONV_CH)), _const_spec((1, CONV_CH)), _const_spec((1, CONV_CH)), _const_spec((1, CONV_CH)),
        _const_spec((1, XQ_WIDTH)),
        _const_spec((1, D)),
        _const_spec((RT_ROWS, D)), _const_spec((RT_ROWS, 1)), _const_spec((TQ, TQ)),
    ]
    tri = (jnp.arange(TQ)[:, None] <= jnp.arange(TQ)[None, :]).astype(BF16)
    per_tile = lambda rows, last: pl.BlockSpec((1, rows, last), lambda s: (tile_out(s), 0, 0))
    x1_spec = pl.BlockSpec((1, TQ, XW), lambda s: (tile_out(s) // NT, tile_out(s) % NT, 0))
    assert (N_EXPERTS * D) % n_mix == 0 and (N_EXPERTS * EXPERT_FF) % n_mix == 0
    up_rows, down_rows = N_EXPERTS * D // n_mix, N_EXPERTS * EXPERT_FF // n_mix
    up_slice = pl.BlockSpec((up_rows, EXPERT_FF), lambda s: (tile_in(s), 0))
    down_slice = pl.BlockSpec((down_rows, D), lambda s: (tile_in(s), 0))
    in_specs += [up_slice, up_slice, down_slice]
    x1, rt, cnt, wg, wu, wd = pl.pallas_call(
        functools.partial(_mixer_kernel, tiles_per_seq=NT),
        grid=(n_mix + 1,),
        in_specs=in_specs,
        out_specs=[x1_spec, per_tile(8, TQ), per_tile(8, LANES), up_slice, up_slice, down_slice],
        out_shape=[jax.ShapeDtypeStruct((B, S, XW), F32), jax.ShapeDtypeStruct((B * NT, 8, TQ), F32),
                   jax.ShapeDtypeStruct((B * NT, 8, LANES), F32),
                   jax.ShapeDtypeStruct((N_EXPERTS * D, EXPERT_FF), BF16),
                   jax.ShapeDtypeStruct((N_EXPERTS * D, EXPERT_FF), BF16),
                   jax.ShapeDtypeStruct((N_EXPERTS * EXPERT_FF, D), BF16)],
        scratch_shapes=[pltpu.VMEM((WINDOW + TQ, LANES), BF16)] * 4
        + [pltpu.VMEM((CONV_CH // LANES, CONV_HALO + TQ, LANES), F32),
           pltpu.VMEM((CONV_CH // LANES, TQ, LANES), F32), pltpu.VMEM((TQ, Q_WIDTH), BF16),
           pltpu.VMEM((TQ, XQ_WIDTH), BF16), pltpu.VMEM((TQ, D), F32),
           pltpu.VMEM((TQ, LANES), F32), pltpu.VMEM((TQ, LANES), F32)],
        compiler_params=pltpu.CompilerParams(dimension_semantics=("arbitrary",),
                                             vmem_limit_bytes=VMEM_LIMIT),
        name="mixer",
    )(x, cos_tab, sin_tab, mk, mv, row(g_norm1[l]), w_in[l].astype(BF16),
      row(jnp.tile(g_q[l], ATT_HEADS) * (HEAD_DIM ** -0.5 * LOG2E)), row(jnp.tile(g_k[l], ATT_KV_HEADS)),
      sinks[l].astype(F32) * LOG2E,
      jnp.concatenate([w_o_attn[l], w_conv_out[l], w_o_mem[l], w_out[l]], axis=0).astype(BF16),
      wdw, row(b_conv_dw[l]), row(g_conv_ln[l]), row(b_conv_ln[l]),
      row(jnp.tile(g_xq[l], MEM_HEADS) * (MEM_HEAD_DIM ** -0.5 * LOG2E)),
      row(g_norm2[l]), w_rt, b_rt, tri,
      w_gate[l].reshape(N_EXPERTS * D, EXPERT_FF), w_up[l].reshape(N_EXPERTS * D, EXPERT_FF),
      w_down[l].reshape(N_EXPERTS * EXPERT_FF, D))

    n_tiles = T // TM
    plan = _routing_plan(rt[:, 0, :].astype(jnp.int32), rt[:, 1, :].astype(jnp.int32),
                         cnt[:, 0, :N_CLASSES].astype(jnp.int32), n_tiles)
    dest, tile_k, e_lo, e_hi, cls_k, flags = plan
    ni = tile_k.shape[0]
    wg = wg.reshape(N_EXPERTS, D, EXPERT_FF)
    wu = wu.reshape(N_EXPERTS, D, EXPERT_FF)
    wd = wd.reshape(N_EXPERTS, EXPERT_FF, D)
    lo_map = lambda k, dest, tile, elo, ehi, cls, flg: (elo[k], 0, 0)
    hi_map = lambda k, dest, tile, elo, ehi, cls, flg: (ehi[k], 0, 0)
    up_block = (1, D, EXPERT_FF)
    down_block = (1, EXPERT_FF, D)
    out = pl.pallas_call(
        _moe_kernel,
        grid_spec=pltpu.PrefetchScalarGridSpec(
            num_scalar_prefetch=6,
            grid=(ni,),
            in_specs=[pl.BlockSpec((1, D), lambda k, *_: (0, 0)),
                      pl.BlockSpec(up_block, lo_map), pl.BlockSpec(up_block, lo_map),
                      pl.BlockSpec(down_block, lo_map),
                      pl.BlockSpec(up_block, hi_map), pl.BlockSpec(up_block, hi_map),
                      pl.BlockSpec(down_block, hi_map),
                      pl.BlockSpec(memory_space=pl.ANY)],
            out_specs=pl.BlockSpec(memory_space=pl.ANY),
            scratch_shapes=[pltpu.SMEM((T,), jnp.int32),
                            pltpu.VMEM((GATHER_SLOTS, TM, XW), F32), pltpu.VMEM((2, TM, D), F32),
                            pltpu.VMEM((TM, D), BF16),
                            pltpu.VMEM((4, D, EXPERT_FF), BF16), pltpu.VMEM((2, EXPERT_FF, D), BF16),
                            pltpu.SemaphoreType.DMA((GATHER_SLOTS,)), pltpu.SemaphoreType.DMA((2,))]),
        out_shape=jax.ShapeDtypeStruct((T, D), F32),
        compiler_params=pltpu.CompilerParams(dimension_semantics=("arbitrary",),
                                             vmem_limit_bytes=VMEM_LIMIT),
        name="moe",
    )(dest, tile_k, e_lo, e_hi, cls_k, flags, row(g_norm2[l]), wg, wu, wd, wg, wu, wd,
      x1.reshape(T, XW))
    return out.reshape(B, S, D)
```

```python
import functools

import jax
import jax.numpy as jnp
from jax import lax
from jax.experimental import pallas as pl
from jax.experimental.pallas import tpu as pltpu

F32 = jnp.float32
BF16 = jnp.bfloat16

D_MODEL = 1024
ATT_HEADS = 8
ATT_KV_HEADS = 2
HEAD_DIM = 64
WINDOW = 128
ROPE_THETA = 10000.0
CONV_CH = 512
CONV_WIDTH = 31
MEM_HEADS = 4
MEM_HEAD_DIM = 128
N_BRANCHES = 3
N_GROUPS = 4
EXPERTS_PER_GROUP = 4
N_EXPERTS = N_GROUPS * EXPERTS_PER_GROUP
EXPERT_FF = 512
EPS = 1e-6
LN_EPS = 1e-5
NEG_INF = -1e30
LOG2E = 1.4426950408889634

Q_WIDTH = ATT_HEADS * HEAD_DIM
KV_WIDTH = ATT_KV_HEADS * HEAD_DIM
GLU_WIDTH = 2 * CONV_CH
XQ_WIDTH = MEM_HEADS * MEM_HEAD_DIM
GATE_WIDTH = N_BRANCHES * D_MODEL
COL_GLU = Q_WIDTH + 2 * KV_WIDTH
COL_XQ = COL_GLU + GLU_WIDTH
COL_GATE = COL_XQ + XQ_WIDTH
IN_WIDTH = COL_GATE + GATE_WIDTH

LANES = 128
TQ = 256
CONV_HALO = 32
CONV_ROWS = 128
TM = 256
RT_ROWS = 32
N_PAIRS = EXPERTS_PER_GROUP * (EXPERTS_PER_GROUP - 1) // 2
N_CLASSES = N_GROUPS * N_PAIRS
CLS_ROWS = 32
XW = D_MODEL + LANES
VMEM_LIMIT = 56 * 1024 * 1024


def _dot(a, b):
    return jnp.dot(a, b, preferred_element_type=F32)


def _dot_nt(a, b):
    return lax.dot_general(a, b, (((1,), (1,)), ((), ())), preferred_element_type=F32)


def _split_bf16(t):
    hi = t.astype(BF16)
    lo = (t - hi.astype(F32)).astype(BF16)
    return hi, lo


def _rms(t):
    return t * lax.rsqrt(jnp.mean(t * t, axis=-1, keepdims=True) + EPS)


def _rms_per_mem_head(t):
    return jnp.concatenate(
        [_rms(t[:, hd * MEM_HEAD_DIM:(hd + 1) * MEM_HEAD_DIM]) for hd in range(MEM_HEADS)], axis=-1)


def _memkv_kernel(mem_ref, gmem_ref, wkv_ref, gxk_ref, mk_ref, mv_ref):
    mn = (_rms(mem_ref[0]) * gmem_ref[...]).astype(BF16)
    kv = _dot(mn, wkv_ref[...])
    mk_ref[0] = (_rms_per_mem_head(kv[:, :XQ_WIDTH]) * gxk_ref[...]).astype(BF16)
    mv_ref[0] = kv[:, XQ_WIDTH:].astype(BF16)


def _mixer_kernel(x_ref, cos_ref, sin_ref, mk_ref, mv_ref, g1_ref, win_ref,
                  gq_ref, gk_ref, sinks_ref, woa_ref, wdw_ref,
                  bdw_ref, gln_ref, bln_ref, wco_ref, gxq_ref, wom_ref, wout_ref, g2_ref,
                  wrt_ref, brt_ref, tri_ref, ewg_ref, ewu_ref, ewd_ref,
                  x1_ref, rt_ref, cnt_ref, ewg_out, ewu_out, ewd_out,
                  k0_ref, k1_ref, v0_ref, v1_ref, ubuf_ref, ybuf_ref, o_ref, om_ref, x1prev_ref,
                  cosd_ref, sind_ref, *, tiles_per_seq):
    s = pl.program_id(0)
    n_tiles = pl.num_programs(0) - 1
    i = jnp.minimum(s, n_tiles - 1) % tiles_per_seq
    kbands = (k0_ref, k1_ref)
    vbands = (v0_ref, v1_ref)

    @pl.when(s == 0)
    def _():
        x1prev_ref[...] = jnp.zeros((TQ, D_MODEL), F32)

    @pl.when(i == 0)
    def _():
        for r in kbands + vbands:
            r[0:WINDOW, :] = jnp.zeros((WINDOW, LANES), BF16)
        ubuf_ref[:, 0:CONV_HALO, :] = jnp.zeros((CONV_CH // LANES, CONV_HALO, LANES), F32)

    x1_prev = x1prev_ref[...]
    x1_ref[0, :, 0:D_MODEL] = x1_prev

    ewg_out[...] = ewg_ref[...].astype(BF16)
    ewu_out[...] = ewu_ref[...].astype(BF16)
    ewd_out[...] = ewd_ref[...].astype(BF16)

    x = x_ref[0]
    h = (_rms(x) * g1_ref[...]).astype(BF16)

    n_slabs = CONV_CH // LANES
    half = TQ // 2
    group = ATT_HEADS // ATT_KV_HEADS

    glu = _dot(h, win_ref[:, COL_GLU:COL_XQ])
    route_logits = _router_logits(x1_prev, g2_ref, wrt_ref, brt_ref)
    u = glu[:, :CONV_CH] * jax.nn.sigmoid(glu[:, CONV_CH:])
    for c in range(n_slabs):
        ubuf_ref[c, CONV_HALO:CONV_HALO + TQ, :] = u[:, c * LANES:(c + 1) * LANES]
    base = CONV_HALO - (CONV_WIDTH - 1)
    acc = [[None] * n_slabs for _ in range(2)]

    def conv_slab(c):
        cols = slice(c * LANES, (c + 1) * LANES)
        for par in range(2):
            parts = []
            for q0 in range(0, half, CONV_ROWS):
                a = jnp.broadcast_to(bdw_ref[:, cols], (CONV_ROWS, LANES))
                for j in range(CONV_WIDTH):
                    a = a + wdw_ref[j:j + 1, cols] * ubuf_ref[
                        c, pl.ds(base + j + par + 2 * q0, CONV_ROWS, stride=2), :]
                parts.append(a)
            acc[par][c] = jnp.concatenate(parts, axis=0)
        ubuf_ref[c, 0:CONV_HALO, :] = ubuf_ref[c, TQ:TQ + CONV_HALO, :]

    def gate(b):
        return jax.nn.sigmoid(_dot(h, win_ref[:, COL_GATE + b * D_MODEL:COL_GATE + (b + 1) * D_MODEL]))

    def conv_finish():
        for par in range(2):
            row_sum = functools.reduce(jnp.add, [jnp.sum(a, axis=-1, keepdims=True) for a in acc[par]])
            mu = row_sum * (1.0 / CONV_CH)
            cen = [a - mu for a in acc[par]]
            sq_sum = functools.reduce(jnp.add, [jnp.sum(t * t, axis=-1, keepdims=True) for t in cen])
            rstd = lax.rsqrt(sq_sum * (1.0 / CONV_CH) + LN_EPS)
            for c in range(n_slabs):
                cols = slice(c * LANES, (c + 1) * LANES)
                yln = cen[c] * rstd * gln_ref[:, cols] + bln_ref[:, cols]
                ybuf_ref[c, pl.ds(par, half, stride=2), :] = yln * jax.nn.sigmoid(yln)
        uo = jnp.concatenate([ybuf_ref[c] for c in range(n_slabs)], axis=-1).astype(BF16)
        return _dot(uo, wco_ref[...])

    for c in range(n_slabs):
        conv_slab(c)
    qkv = _dot(h, win_ref[:, 0:COL_GLU])
    gates = [gate(0)]
    q = qkv[:, :Q_WIDTH]
    k = qkv[:, Q_WIDTH:Q_WIDTH + KV_WIDTH]
    v = qkv[:, Q_WIDTH + KV_WIDTH:]
    gates.append(gate(1))

    lane = lax.broadcasted_iota(jnp.int32, (TQ, LANES), 1)
    first_half = (lane % HEAD_DIM) < (HEAD_DIM // 2)
    low_head = lane < HEAD_DIM
    n_freq = HEAD_DIM // 2
    per_row = LANES // n_freq
    quarter = lax.broadcasted_iota(jnp.int32, (TQ // per_row, LANES), 1) // n_freq
    sign = jnp.where(quarter % 2 == 0, -1.0, 1.0)
    for table_ref, out_ref, scale in ((cos_ref, cosd_ref, None), (sin_ref, sind_ref, sign)):
        dense = table_ref[0]
        rolled = [dense] + [pltpu.roll(dense, n_freq * kk, 1) for kk in range(1, per_row)]
        for j in range(per_row):
            spread = rolled[(-j) % per_row]
            for qq in range(1, per_row):
                spread = jnp.where(quarter == qq, rolled[(qq - j) % per_row], spread)
            out_ref[pl.ds(j, TQ // per_row, stride=per_row), :] = spread if scale is None else spread * scale
    cosd = cosd_ref[...]
    sind = sind_ref[...]

    def rope(t):
        rot = jnp.where(first_half, pltpu.roll(t, LANES - HEAD_DIM // 2, 1),
                        pltpu.roll(t, HEAD_DIM // 2, 1))
        return t * cosd + rot * sind

    def dup_halves(t):
        swapped = pltpu.roll(t, HEAD_DIM, 1)
        return jnp.where(low_head, t, swapped), jnp.where(low_head, swapped, t)

    def qk_norm(t, gain):
        sq = t * t
        lo = jnp.sum(jnp.where(low_head, sq, 0.0), axis=-1, keepdims=True)
        hi = jnp.sum(jnp.where(low_head, 0.0, sq), axis=-1, keepdims=True)
        inv = jnp.where(low_head, lax.rsqrt(lo * (1.0 / HEAD_DIM) + EPS), lax.rsqrt(hi * (1.0 / HEAD_DIM) + EPS))
        return t * inv * gain

    kd = dup_halves(rope(qk_norm(k, gk_ref[...])))
    vd = dup_halves(v)
    for kvh in range(ATT_KV_HEADS):
        kbands[kvh][WINDOW:WINDOW + TQ, :] = kd[kvh].astype(BF16)
        vbands[kvh][WINDOW:WINDOW + TQ, :] = vd[kvh].astype(BF16)

    q_heads = []
    for c in range(Q_WIDTH // LANES):
        cols = slice(c * LANES, (c + 1) * LANES)
        qc = rope(qk_norm(q[:, cols], gq_ref[:, cols]))
        q_heads.append(jnp.where(low_head, qc, 0.0).astype(BF16))
        q_heads.append(jnp.where(low_head, 0.0, qc).astype(BF16))

    qi = lax.broadcasted_iota(jnp.int32, (WINDOW, 2 * WINDOW), 0)
    kj = lax.broadcasted_iota(jnp.int32, (WINDOW, 2 * WINDOW), 1)
    band_mask = (kj > qi) & (kj <= qi + WINDOW)
    first_mask = band_mask & (kj >= jnp.where(i > 0, 0, WINDOW))
    low_head_w = lax.broadcasted_iota(jnp.int32, (WINDOW, LANES), 1) < HEAD_DIM

    def attention_scores(step):
        n, kvh = divmod(step, ATT_KV_HEADS)
        rows = slice(n * WINDOW, (n + 1) * WINDOW)
        mask = first_mask if n == 0 else band_mask
        heads = range(kvh * group, (kvh + 1) * group)
        q_st = jnp.concatenate([q_heads[hd][rows] for hd in heads], axis=0)
        kband = kbands[kvh][n * WINDOW:n * WINDOW + 2 * WINDOW, :]
        s_all = _dot_nt(q_st, kband)
        es, dens = [], []
        for j, hd in enumerate(heads):
            s = jnp.where(mask, s_all[j * WINDOW:(j + 1) * WINDOW], NEG_INF)
            sink = sinks_ref[hd]
            m = jnp.maximum(jnp.max(s, axis=-1, keepdims=True), sink)
            e = jnp.exp2(s - m)
            dens.append(jnp.sum(e, axis=-1, keepdims=True) + jnp.exp2(sink - m))
            es.append(e.astype(BF16))
        return jnp.concatenate(es, axis=0), dens

    def attention_output(step, probs):
        n, kvh = divmod(step, ATT_KV_HEADS)
        rows = slice(n * WINDOW, (n + 1) * WINDOW)
        e_all, dens = probs
        vband = vbands[kvh][n * WINDOW:n * WINDOW + 2 * WINDOW, :]
        o_all = _dot(e_all, vband)
        for cc in range(group // 2):
            oa = o_all[(2 * cc) * WINDOW:(2 * cc + 1) * WINDOW] / dens[2 * cc]
            ob = o_all[(2 * cc + 1) * WINDOW:(2 * cc + 2) * WINDOW] / dens[2 * cc + 1]
            col = kvh * (group // 2) + cc
            o_ref[rows, col * LANES:(col + 1) * LANES] = jnp.where(low_head_w, oa, ob).astype(BF16)

    def mem_scores(hd, xq):
        cols = slice(hd * MEM_HEAD_DIM, (hd + 1) * MEM_HEAD_DIM)
        s = _dot_nt(xq[:, cols], mk_ref[0, :, cols])
        e = jnp.exp2(s - jnp.max(s, axis=-1, keepdims=True))
        return e.astype(BF16), jnp.sum(e, axis=-1, keepdims=True)

    def mem_output(hd, probs):
        cols = slice(hd * MEM_HEAD_DIM, (hd + 1) * MEM_HEAD_DIM)
        e, den = probs
        om_ref[:, cols] = (_dot(e, mv_ref[0, :, cols]) / den).astype(BF16)

    route = _router_select(route_logits)
    p0 = attention_scores(0)
    gates.append(gate(2))
    p1 = attention_scores(1)
    attention_output(0, p0)
    xq = _dot(h, win_ref[:, COL_XQ:COL_GATE])
    p2 = attention_scores(2)
    attention_output(1, p1)
    _router_emit(*route, tri_ref, x1_ref, rt_ref, cnt_ref)
    xq = (_rms_per_mem_head(xq) * gxq_ref[...]).astype(BF16)
    p3 = attention_scores(3)
    attention_output(2, p2)
    y_conv = conv_finish()
    m0 = mem_scores(0, xq)
    attention_output(3, p3)
    for r in kbands + vbands:
        r[0:WINDOW, :] = r[TQ:TQ + WINDOW, :]
    m1 = mem_scores(1, xq)
    mem_output(0, m0)
    m2 = mem_scores(2, xq)
    mem_output(1, m1)
    m3 = mem_scores(3, xq)
    mem_output(2, m2)
    y_attn = _dot(o_ref[...], woa_ref[...])
    merged = gates[0] * y_attn + gates[1] * y_conv
    mem_output(3, m3)
    merged = merged + gates[2] * _dot(om_ref[...], wom_ref[...])

    x1prev_ref[...] = x + _dot(merged.astype(BF16), wout_ref[...])


def _router_logits(x1, g2_ref, wrt_ref, brt_ref):
    h2 = _rms(x1) * g2_ref[...]
    h_hi, h_lo = _split_bf16(h2)
    w_hi, w_lo = _split_bf16(wrt_ref[...])
    return _dot_nt(w_hi, h_hi) + _dot_nt(w_hi, h_lo) + _dot_nt(w_lo, h_hi) + brt_ref[...]


def _router_select(lt):
    gl = [lt[g:g + 1] for g in range(N_GROUPS)]
    gmax = functools.reduce(jnp.maximum, gl)
    g_idx = jnp.full((1, TQ), N_GROUPS - 1, jnp.int32)
    for g in reversed(range(N_GROUPS - 1)):
        g_idx = jnp.where(gl[g] == gmax, g, g_idx)
    p_g = 1.0 / functools.reduce(jnp.add, [jnp.exp(r - gmax) for r in gl])
    el = []
    for kk in range(EXPERTS_PER_GROUP):
        row = lt[8 + kk:9 + kk]
        for g in range(1, N_GROUPS):
            off = 8 + g * EXPERTS_PER_GROUP + kk
            row = jnp.where(g_idx == g, lt[off:off + 1], row)
        el.append(row)

    def first_argmax(rows):
        top = functools.reduce(jnp.maximum, rows)
        idx = jnp.full((1, TQ), len(rows) - 1, jnp.int32)
        for kk in reversed(range(len(rows) - 1)):
            idx = jnp.where(rows[kk] == top, kk, idx)
        return top, idx

    v1, i1 = first_argmax(el)
    v2, i2 = first_argmax([jnp.where(i1 == kk, -jnp.inf, el[kk]) for kk in range(EXPERTS_PER_GROUP)])
    t = jnp.exp(v2 - v1)
    p1 = 1.0 / (1.0 + t)
    p2 = t * p1

    lo = jnp.minimum(i1, i2)
    hi = jnp.maximum(i1, i2)
    w_lo = jnp.where(i1 < i2, p_g * p1, p_g * p2)
    w_hi = jnp.where(i1 < i2, p_g * p2, p_g * p1)
    cls = g_idx * N_PAIRS + ((lo * (2 * EXPERTS_PER_GROUP - 1 - lo)) >> 1) + hi - lo - 1
    return cls, w_lo, w_hi


def _router_emit(cls, w_lo, w_hi, tri_ref, x1_ref, rt_ref, cnt_ref):
    cls_f = cls.astype(F32)
    onehot = jnp.broadcast_to(cls, (CLS_ROWS, TQ)) == lax.broadcasted_iota(jnp.int32, (CLS_ROWS, TQ), 0)
    onehot_bf = onehot.astype(F32).astype(BF16)
    prefix = _dot(onehot_bf, tri_ref[...])
    rank = jnp.sum(jnp.where(onehot, prefix, 0.0), axis=0, keepdims=True) - 1.0
    counts = _dot_nt(jnp.ones((8, TQ), BF16), onehot_bf)
    zero = jnp.zeros((1, TQ), F32)
    rt_ref[0] = jnp.concatenate([cls_f, rank] + [zero] * 6, axis=0)
    cnt_ref[0] = jnp.concatenate([counts, jnp.zeros((8, LANES - CLS_ROWS), F32)], axis=1)
    cols = jnp.concatenate([w_lo, w_hi, cls_f, jnp.zeros((LANES - 3, TQ), F32)], axis=0)
    x1_ref[0, :, D_MODEL:] = cols.T


FLAG_FIRST, FLAG_VALID = 1, 2
GATHER_SLOTS = 3


def _moe_kernel(dest_ref, tile_ref, elo_ref, ehi_ref, cls_ref, flag_ref,
                g2_ref, wg_lo, wu_lo, wd_lo, wg_hi, wu_hi, wd_hi, x1_hbm,
                out_hbm,
                perm_ref, xbuf, obuf, h2buf, wbuf_up, wbuf_down, gsem, ssem):
    k = pl.program_id(0)
    nk = pl.num_programs(0)
    t = tile_ref[k]
    xslot = t % GATHER_SLOTS
    oslot = t % 2
    flags = flag_ref[k]
    n_tok = x1_hbm.shape[0]
    n_tiles = n_tok // TM

    def gather_copy(r, tile, s):
        tok = perm_ref[tile * TM + r]
        return pltpu.make_async_copy(x1_hbm.at[pl.ds(tok, 1), :], xbuf.at[s, pl.ds(r, 1), :], gsem.at[s])

    def scatter_copy(r, tile, s):
        tok = perm_ref[tile * TM + r]
        return pltpu.make_async_copy(obuf.at[s, pl.ds(r, 1), :], out_hbm.at[pl.ds(tok, 1), :], ssem.at[s])

    k_prev = jnp.maximum(k - 1, 0)
    for side, (e_ref, wg, wu, wd) in enumerate(((elo_ref, wg_lo, wu_lo, wd_lo), (ehi_ref, wg_hi, wu_hi, wd_hi))):
        @pl.when((k == 0) | (e_ref[k] != e_ref[k_prev]))
        def _(side=side, wg=wg, wu=wu, wd=wd):
            wbuf_up[2 * side] = wg[0]
            wbuf_up[2 * side + 1] = wu[0]
            wbuf_down[side] = wd[0]

    def for_rows(fn):
        def body(r, carry):
            fn(r)
            return carry
        lax.fori_loop(0, TM, body, 0, unroll=8)

    @pl.when(k == 0)
    def _():
        def invert(tok, carry):
            perm_ref[dest_ref[tok]] = tok
            return carry
        lax.fori_loop(0, n_tok, invert, 0, unroll=16)
        for tile in range(GATHER_SLOTS - 1):
            for_rows(lambda r, tile=tile: gather_copy(r, tile, tile).start())

    is_first = (flags & FLAG_FIRST) != 0
    is_valid = (flags & FLAG_VALID) != 0
    ahead = GATHER_SLOTS - 1
    prefetch_tile = jnp.minimum(t + ahead, n_tiles - 1)
    prefetch_slot = (t + ahead) % GATHER_SLOTS

    def gather_tile_wait(s):
        pltpu.make_async_copy(x1_hbm.at[pl.ds(0, TM), :], xbuf.at[s], gsem.at[s]).wait()

    def scatter_tile_wait(s):
        pltpu.make_async_copy(obuf.at[s], out_hbm.at[pl.ds(0, TM), :], ssem.at[s]).wait()

    @pl.when(is_first)
    def _():
        gather_tile_wait(xslot)

        @pl.when(t >= 2)
        def _():
            scatter_tile_wait(oslot)

        xr = xbuf[xslot, :, 0:D_MODEL]
        h2buf[...] = (_rms(xr) * g2_ref[...]).astype(BF16)
        obuf[oslot] = xr

    def experts():
        w = xbuf[xslot, :, D_MODEL:XW]
        mine = w[:, 2:3] == cls_ref[k].astype(F32)
        h2 = h2buf[...]
        y = None
        for side in range(2):
            c = jnp.where(mine, w[:, side:side + 1], 0.0)
            hid = jax.nn.silu(_dot(h2, wbuf_up[2 * side])) * _dot(h2, wbuf_up[2 * side + 1]) * c
            part = _dot(hid.astype(BF16), wbuf_down[side])
            y = part if y is None else y + part
        obuf[oslot] += y

    @pl.when(is_first & (t > 0))
    def _():
        for r in range(TM):
            gather_copy(r, prefetch_tile, prefetch_slot).start()
            scatter_copy(r, t - 1, 1 - oslot).start()
        experts()

    @pl.when(is_first & (t == 0))
    def _():
        for r in range(TM):
            gather_copy(r, prefetch_tile, prefetch_slot).start()
        experts()

    @pl.when(is_valid & jnp.logical_not(is_first))
    def _():
        experts()

    @pl.when(k == nk - 1)
    def _():
        last = n_tiles - 1
        for_rows(lambda r: scatter_copy(r, last, last % 2).start())
        for tile in (last - 1, last):
            scatter_tile_wait(tile % 2)
        for extra in range(1, GATHER_SLOTS):
            gather_tile_wait((last + extra) % GATHER_SLOTS)


_CLASS_PAIRS = [(a, b) for a in range(EXPERTS_PER_GROUP) for b in range(a + 1, EXPERTS_PER_GROUP)]


def _routing_plan(cls, rank, tile_counts, n_tiles):
    counts = jnp.sum(tile_counts, axis=0)
    off = jnp.concatenate([jnp.zeros((1,), jnp.int32), jnp.cumsum(counts).astype(jnp.int32)])
    tile_base = jnp.cumsum(tile_counts, axis=0) - tile_counts + off[None, :-1]
    onehot = cls[:, :, None] == jnp.arange(N_CLASSES, dtype=jnp.int32)[None, None, :]
    dest = (jnp.sum(jnp.where(onehot, tile_base[:, None, :], 0), axis=-1) + rank).reshape(-1)
    first_tile = off[:-1] // TM
    last_tile = (off[1:] - 1) // TM
    n_items = jnp.where(counts > 0, last_tile - first_tile + 1, 0)
    istart = jnp.concatenate([jnp.zeros((1,), jnp.int32), jnp.cumsum(n_items).astype(jnp.int32)])
    total = istart[-1]
    ni = n_tiles + N_CLASSES - 1
    k = jnp.arange(ni, dtype=jnp.int32)
    valid = k < total
    kc = jnp.minimum(k, total - 1)
    c_of_k = jnp.minimum(jnp.sum(istart[None, 1:] <= kc[:, None], axis=1).astype(jnp.int32), N_CLASSES - 1)
    sel = c_of_k[:, None] == jnp.arange(N_CLASSES, dtype=jnp.int32)[None, :]
    pick = lambda table: jnp.sum(jnp.where(sel, table[None, :], 0), axis=1).astype(jnp.int32)
    tile_k = pick(first_tile) + kc - pick(istart[:-1])
    prev_tile = jnp.concatenate([jnp.full((1,), -1, jnp.int32), tile_k[:-1]])
    first = valid & (tile_k != prev_tile)
    flags = (first * FLAG_FIRST + valid * FLAG_VALID).astype(jnp.int32)
    experts_of_class = [(c // N_PAIRS * EXPERTS_PER_GROUP + lo, c // N_PAIRS * EXPERTS_PER_GROUP + hi)
                        for c in range(N_CLASSES) for lo, hi in [_CLASS_PAIRS[c % N_PAIRS]]]
    e_lo = pick(jnp.array([e[0] for e in experts_of_class], jnp.int32))
    e_hi = pick(jnp.array([e[1] for e in experts_of_class], jnp.int32))
    cls_k = jnp.where(valid, c_of_k, -1)
    return dest.astype(jnp.int32), tile_k, e_lo, e_hi, cls_k, flags


def _const_spec(shape):
    nd = len(shape)
    return pl.BlockSpec(shape, lambda *_: (0,) * nd, pipeline_mode=pl.Buffered(1))


def kernel(x, mem, positions, g_norm1, w_in, g_q, g_k, sinks, w_o_attn, w_conv_dw, b_conv_dw, g_conv_ln, b_conv_ln, w_conv_out, g_mem, w_kv_mem, g_xq, g_xk, w_o_mem, w_out, g_norm2, w_group, b_group, w_router, b_router, w_gate, w_up, w_down):
    B, S, D = x.shape
    M = mem.shape[1]
    assert D == D_MODEL and S % TQ == 0 and (B * S) % TM == 0 and w_in.shape[0] == 1
    assert (TQ // WINDOW) * ATT_KV_HEADS == MEM_HEADS and CONV_CH // LANES == N_BRANCHES + 1
    NT = S // TQ
    T = B * S
    l = 0

    row = lambda v: v.reshape(1, -1).astype(F32)

    mk, mv = pl.pallas_call(
        _memkv_kernel,
        grid=(B,),
        in_specs=[pl.BlockSpec((1, M, D), lambda b: (b, 0, 0)),
                  _const_spec((1, D)), _const_spec((D, 2 * XQ_WIDTH)),
                  _const_spec((1, XQ_WIDTH))],
        out_specs=[pl.BlockSpec((1, M, XQ_WIDTH), lambda b: (b, 0, 0))] * 2,
        out_shape=[jax.ShapeDtypeStruct((B, M, XQ_WIDTH), BF16)] * 2,
        compiler_params=pltpu.CompilerParams(dimension_semantics=("arbitrary",)),
        name="memkv",
    )(mem, row(g_mem[l]), w_kv_mem[l].astype(BF16), row(jnp.tile(g_xk[l], MEM_HEADS)))

    inv_freq = 1.0 / (ROPE_THETA ** (jnp.arange(0, HEAD_DIM, 2, dtype=F32) / HEAD_DIM))
    per_row = LANES // (HEAD_DIM // 2)
    ang = positions.astype(F32).reshape(B, S // per_row, per_row, 1) * inv_freq
    ang = ang.reshape(B, S // per_row, LANES)
    cos_tab, sin_tab = jnp.cos(ang), jnp.sin(ang)

    assert w_in.shape[-1] == IN_WIDTH
    wdw = jnp.zeros((CONV_HALO, CONV_CH), F32).at[:CONV_WIDTH].set(w_conv_dw[l])
    w_rt = jnp.zeros((RT_ROWS, D), F32).at[0:N_GROUPS].set(w_group[l].T).at[8:8 + N_EXPERTS].set(w_router[l].T)
    b_rt = jnp.zeros((RT_ROWS, 1), F32).at[0:N_GROUPS, 0].set(b_group[l]).at[8:8 + N_EXPERTS, 0].set(b_router[l])

    n_mix = B * NT
    tile_in = lambda s: jnp.minimum(s, n_mix - 1)
    tile_out = lambda s: jnp.maximum(s - 1, 0)
    tile3 = lambda last: pl.BlockSpec((1, TQ, last), lambda s: (tile_in(s) // NT, tile_in(s) % NT, 0))
    per_batch = pl.BlockSpec((1, M, XQ_WIDTH), lambda s: (tile_in(s) // NT, 0, 0))
    rope_spec = pl.BlockSpec((1, TQ // per_row, LANES), lambda s: (tile_in(s) // NT, tile_in(s) % NT, 0))
    in_specs = [
        tile3(D), rope_spec, rope_spec, per_batch, per_batch,
        _const_spec((1, D)),
        _const_spec((D, IN_WIDTH)),
        _const_spec((1, Q_WIDTH)), _const_spec((1, LANES)),
        pl.BlockSpec(memory_space=pltpu.SMEM),
        _const_spec((Q_WIDTH, D)),
        _const_spec((CONV_HALO, CONV_CH)), _const_spec((1, CONV_CH)), _const_spec((1, CONV_CH)), _const_spec((1, CONV_CH)),
        _const_spec((CONV_CH, D)),
        _const_spec((1, XQ_WIDTH)),
        _const_spec((XQ_WIDTH, D)),
        _const_spec((D, D)),
        _const_spec((1, D)),
        _const_spec((RT_ROWS, D)), _const_spec((RT_ROWS, 1)), _const_spec((TQ, TQ)),
    ]
    tri = (jnp.arange(TQ)[:, None] <= jnp.arange(TQ)[None, :]).astype(BF16)
    per_tile = lambda rows, last: pl.BlockSpec((1, rows, last), lambda s: (tile_out(s), 0, 0))
    x1_spec = pl.BlockSpec((1, TQ, XW), lambda s: (tile_out(s) // NT, tile_out(s) % NT, 0))
    assert (N_EXPERTS * D) % n_mix == 0 and (N_EXPERTS * EXPERT_FF) % n_mix == 0
    up_rows, down_rows = N_EXPERTS * D // n_mix, N_EXPERTS * EXPERT_FF // n_mix
    up_slice = pl.BlockSpec((up_rows, EXPERT_FF), lambda s: (tile_in(s), 0))
    down_slice = pl.BlockSpec((down_rows, D), lambda s: (tile_in(s), 0))
    in_specs += [up_slice, up_slice, down_slice]
    x1, rt, cnt, wg, wu, wd = pl.pallas_call(
        functools.partial(_mixer_kernel, tiles_per_seq=NT),
        grid=(n_mix + 1,),
        in_specs=in_specs,
        out_specs=[x1_spec, per_tile(8, TQ), per_tile(8, LANES), up_slice, up_slice, down_slice],
        out_shape=[jax.ShapeDtypeStruct((B, S, XW), F32), jax.ShapeDtypeStruct((B * NT, 8, TQ), F32),
                   jax.ShapeDtypeStruct((B * NT, 8, LANES), F32),
                   jax.ShapeDtypeStruct((N_EXPERTS * D, EXPERT_FF), BF16),
                   jax.ShapeDtypeStruct((N_EXPERTS * D, EXPERT_FF), BF16),
                   jax.ShapeDtypeStruct((N_EXPERTS * EXPERT_FF, D), BF16)],
        scratch_shapes=[pltpu.VMEM((WINDOW + TQ, LANES), BF16)] * 4
        + [pltpu.VMEM((CONV_CH // LANES, CONV_HALO + TQ, LANES), F32),
           pltpu.VMEM((CONV_CH // LANES, TQ, LANES), F32), pltpu.VMEM((TQ, Q_WIDTH), BF16),
           pltpu.VMEM((TQ, XQ_WIDTH), BF16), pltpu.VMEM((TQ, D), F32),
           pltpu.VMEM((TQ, LANES), F32), pltpu.VMEM((TQ, LANES), F32)],
        compiler_params=pltpu.CompilerParams(dimension_semantics=("arbitrary",),
                                             vmem_limit_bytes=VMEM_LIMIT),
        name="mixer",
    )(x, cos_tab, sin_tab, mk, mv, row(g_norm1[l]), w_in[l].astype(BF16),
      row(jnp.tile(g_q[l], ATT_HEADS) * (HEAD_DIM ** -0.5 * LOG2E)), row(jnp.tile(g_k[l], ATT_KV_HEADS)),
      sinks[l].astype(F32) * LOG2E, w_o_attn[l].astype(BF16),
      wdw, row(b_conv_dw[l]), row(g_conv_ln[l]), row(b_conv_ln[l]), w_conv_out[l].astype(BF16),
      row(jnp.tile(g_xq[l], MEM_HEADS) * (MEM_HEAD_DIM ** -0.5 * LOG2E)), w_o_mem[l].astype(BF16),
      w_out[l].astype(BF16), row(g_norm2[l]), w_rt, b_rt, tri,
      w_gate[l].reshape(N_EXPERTS * D, EXPERT_FF), w_up[l].reshape(N_EXPERTS * D, EXPERT_FF),
      w_down[l].reshape(N_EXPERTS * EXPERT_FF, D))

    n_tiles = T // TM
    plan = _routing_plan(rt[:, 0, :].astype(jnp.int32), rt[:, 1, :].astype(jnp.int32),
                         cnt[:, 0, :N_CLASSES].astype(jnp.int32), n_tiles)
    dest, tile_k, e_lo, e_hi, cls_k, flags = plan
    ni = tile_k.shape[0]
    wg = wg.reshape(N_EXPERTS, D, EXPERT_FF)
    wu = wu.reshape(N_EXPERTS, D, EXPERT_FF)
    wd = wd.reshape(N_EXPERTS, EXPERT_FF, D)
    lo_map = lambda k, dest, tile, elo, ehi, cls, flg: (elo[k], 0, 0)
    hi_map = lambda k, dest, tile, elo, ehi, cls, flg: (ehi[k], 0, 0)
    up_block = (1, D, EXPERT_FF)
    down_block = (1, EXPERT_FF, D)
    out = pl.pallas_call(
        _moe_kernel,
        grid_spec=pltpu.PrefetchScalarGridSpec(
            num_scalar_prefetch=6,
            grid=(ni,),
            in_specs=[pl.BlockSpec((1, D), lambda k, *_: (0, 0)),
                      pl.BlockSpec(up_block, lo_map), pl.BlockSpec(up_block, lo_map),
                      pl.BlockSpec(down_block, lo_map),
                      pl.BlockSpec(up_block, hi_map), pl.BlockSpec(up_block, hi_map),
                      pl.BlockSpec(down_block, hi_map),
                      pl.BlockSpec(memory_space=pl.ANY)],
            out_specs=pl.BlockSpec(memory_space=pl.ANY),
            scratch_shapes=[pltpu.SMEM((T,), jnp.int32),
                            pltpu.VMEM((GATHER_SLOTS, TM, XW), F32), pltpu.VMEM((2, TM, D), F32),
                            pltpu.VMEM((TM, D), BF16),
                            pltpu.VMEM((4, D, EXPERT_FF), BF16), pltpu.VMEM((2, EXPERT_FF, D), BF16),
                            pltpu.SemaphoreType.DMA((GATHER_SLOTS,)), pltpu.SemaphoreType.DMA((2,))]),
        out_shape=jax.ShapeDtypeStruct((T, D), F32),
        compiler_params=pltpu.CompilerParams(dimension_semantics=("arbitrary",),
                                             vmem_limit_bytes=VMEM_LIMIT),
        name="moe",
    )(dest, tile_k, e_lo, e_hi, cls_k, flags, row(g_norm2[l]), wg, wu, wd, wg, wu, wd,
      x1.reshape(T, XW))
    return out.reshape(B, S, D)
```

```python
import functools

import jax
import jax.numpy as jnp
from jax import lax
from jax.experimental import pallas as pl
from jax.experimental.pallas import tpu as pltpu

F32 = jnp.float32
BF16 = jnp.bfloat16

D_MODEL = 1024
ATT_HEADS = 8
ATT_KV_HEADS = 2
HEAD_DIM = 64
WINDOW = 128
ROPE_THETA = 10000.0
CONV_CH = 512
CONV_WIDTH = 31
MEM_HEADS = 4
MEM_HEAD_DIM = 128
N_BRANCHES = 3
N_GROUPS = 4
EXPERTS_PER_GROUP = 4
N_EXPERTS = N_GROUPS * EXPERTS_PER_GROUP
EXPERT_FF = 512
EPS = 1e-6
LN_EPS = 1e-5
NEG_INF = -1e30
LOG2E = 1.4426950408889634

Q_WIDTH = ATT_HEADS * HEAD_DIM
KV_WIDTH = ATT_KV_HEADS * HEAD_DIM
GLU_WIDTH = 2 * CONV_CH
XQ_WIDTH = MEM_HEADS * MEM_HEAD_DIM
GATE_WIDTH = N_BRANCHES * D_MODEL
COL_GLU = Q_WIDTH + 2 * KV_WIDTH
COL_XQ = COL_GLU + GLU_WIDTH
COL_GATE = COL_XQ + XQ_WIDTH
IN_WIDTH = COL_GATE + GATE_WIDTH

LANES = 128
TQ = 256
CONV_HALO = 32
CONV_ROWS = 128
TM = 256
RT_ROWS = 32
N_PAIRS = EXPERTS_PER_GROUP * (EXPERTS_PER_GROUP - 1) // 2
N_CLASSES = N_GROUPS * N_PAIRS
CLS_ROWS = 32
XW = D_MODEL + LANES
VMEM_LIMIT = 56 * 1024 * 1024


def _dot(a, b):
    return jnp.dot(a, b, preferred_element_type=F32)


def _dot_nt(a, b):
    return lax.dot_general(a, b, (((1,), (1,)), ((), ())), preferred_element_type=F32)


def _split_bf16(t):
    hi = t.astype(BF16)
    lo = (t - hi.astype(F32)).astype(BF16)
    return hi, lo


def _rms(t):
    return t * lax.rsqrt(jnp.mean(t * t, axis=-1, keepdims=True) + EPS)


def _rms_per_mem_head(t):
    return jnp.concatenate(
        [_rms(t[:, hd * MEM_HEAD_DIM:(hd + 1) * MEM_HEAD_DIM]) for hd in range(MEM_HEADS)], axis=-1)


def _memkv_kernel(mem_ref, gmem_ref, wkv_ref, gxk_ref, mk_ref, mv_ref):
    mn = (_rms(mem_ref[0]) * gmem_ref[...]).astype(BF16)
    kv = _dot(mn, wkv_ref[...])
    mk_ref[0] = (_rms_per_mem_head(kv[:, :XQ_WIDTH]) * gxk_ref[...]).astype(BF16)
    mv_ref[0] = kv[:, XQ_WIDTH:].astype(BF16)


def _mixer_kernel(x_ref, cos_ref, sin_ref, mem_ref, gmem_ref, wkv_ref, gxk_ref, g1_ref, win_ref,
                  gq_ref, gk_ref, sinks_ref, woa_ref, wdw_ref,
                  bdw_ref, gln_ref, bln_ref, wco_ref, gxq_ref, wom_ref, wout_ref, g2_ref,
                  wrt_ref, brt_ref, tri_ref, ewg_ref, ewu_ref, ewd_ref,
                  x1_ref, rt_ref, cnt_ref, ewg_out, ewu_out, ewd_out,
                  k0_ref, k1_ref, v0_ref, v1_ref, ubuf_ref, ybuf_ref, o_ref, om_ref, x1prev_ref,
                  cosd_ref, sind_ref, mk_ref, mv_ref, *, tiles_per_seq):
    s = pl.program_id(0)
    n_tiles = pl.num_programs(0) - 1
    i = jnp.minimum(s, n_tiles - 1) % tiles_per_seq
    kbands = (k0_ref, k1_ref)
    vbands = (v0_ref, v1_ref)

    @pl.when(s == 0)
    def _():
        x1prev_ref[...] = jnp.zeros((TQ, D_MODEL), F32)

    @pl.when(i == 0)
    def _():
        _memkv_kernel(mem_ref, gmem_ref, wkv_ref, gxk_ref, mk_ref, mv_ref)
        for r in kbands + vbands:
            r[0:WINDOW, :] = jnp.zeros((WINDOW, LANES), BF16)
        ubuf_ref[:, 0:CONV_HALO, :] = jnp.zeros((CONV_CH // LANES, CONV_HALO, LANES), F32)

    x1_prev = x1prev_ref[...]
    x1_ref[0, :, 0:D_MODEL] = x1_prev

    ewg_out[...] = ewg_ref[...].astype(BF16)
    ewu_out[...] = ewu_ref[...].astype(BF16)
    ewd_out[...] = ewd_ref[...].astype(BF16)

    x = x_ref[0]
    h = (_rms(x) * g1_ref[...]).astype(BF16)

    n_slabs = CONV_CH // LANES
    half = TQ // 2
    group = ATT_HEADS // ATT_KV_HEADS

    glu = _dot(h, win_ref[:, COL_GLU:COL_XQ])
    route_logits = _router_logits(x1_prev, g2_ref, wrt_ref, brt_ref)
    u = glu[:, :CONV_CH] * jax.nn.sigmoid(glu[:, CONV_CH:])
    for c in range(n_slabs):
        ubuf_ref[c, CONV_HALO:CONV_HALO + TQ, :] = u[:, c * LANES:(c + 1) * LANES]
    base = CONV_HALO - (CONV_WIDTH - 1)
    acc = [[None] * n_slabs for _ in range(2)]

    def conv_slab(c):
        cols = slice(c * LANES, (c + 1) * LANES)
        for par in range(2):
            parts = []
            for q0 in range(0, half, CONV_ROWS):
                a = jnp.broadcast_to(bdw_ref[:, cols], (CONV_ROWS, LANES))
                for j in range(CONV_WIDTH):
                    a = a + wdw_ref[j:j + 1, cols] * ubuf_ref[
                        c, pl.ds(base + j + par + 2 * q0, CONV_ROWS, stride=2), :]
                parts.append(a)
            acc[par][c] = jnp.concatenate(parts, axis=0)
        ubuf_ref[c, 0:CONV_HALO, :] = ubuf_ref[c, TQ:TQ + CONV_HALO, :]

    def gate(b):
        return jax.nn.sigmoid(_dot(h, win_ref[:, COL_GATE + b * D_MODEL:COL_GATE + (b + 1) * D_MODEL]))

    def conv_finish():
        for par in range(2):
            row_sum = functools.reduce(jnp.add, [jnp.sum(a, axis=-1, keepdims=True) for a in acc[par]])
            mu = row_sum * (1.0 / CONV_CH)
            cen = [a - mu for a in acc[par]]
            sq_sum = functools.reduce(jnp.add, [jnp.sum(t * t, axis=-1, keepdims=True) for t in cen])
            rstd = lax.rsqrt(sq_sum * (1.0 / CONV_CH) + LN_EPS)
            for c in range(n_slabs):
                cols = slice(c * LANES, (c + 1) * LANES)
                yln = cen[c] * rstd * gln_ref[:, cols] + bln_ref[:, cols]
                ybuf_ref[c, pl.ds(par, half, stride=2), :] = yln * jax.nn.sigmoid(yln)
        uo = jnp.concatenate([ybuf_ref[c] for c in range(n_slabs)], axis=-1).astype(BF16)
        return _dot(uo, wco_ref[...])

    for c in range(n_slabs):
        conv_slab(c)
    qkv = _dot(h, win_ref[:, 0:COL_GLU])
    gates = [gate(0)]
    q = qkv[:, :Q_WIDTH]
    k = qkv[:, Q_WIDTH:Q_WIDTH + KV_WIDTH]
    v = qkv[:, Q_WIDTH + KV_WIDTH:]
    gates.append(gate(1))

    lane = lax.broadcasted_iota(jnp.int32, (TQ, LANES), 1)
    first_half = (lane % HEAD_DIM) < (HEAD_DIM // 2)
    low_head = lane < HEAD_DIM
    n_freq = HEAD_DIM // 2
    per_row = LANES // n_freq
    quarter = lax.broadcasted_iota(jnp.int32, (TQ // per_row, LANES), 1) // n_freq
    sign = jnp.where(quarter % 2 == 0, -1.0, 1.0)
    for table_ref, out_ref, scale in ((cos_ref, cosd_ref, None), (sin_ref, sind_ref, sign)):
        dense = table_ref[0]
        rolled = [dense] + [pltpu.roll(dense, n_freq * kk, 1) for kk in range(1, per_row)]
        for j in range(per_row):
            spread = rolled[(-j) % per_row]
            for qq in range(1, per_row):
                spread = jnp.where(quarter == qq, rolled[(qq - j) % per_row], spread)
            out_ref[pl.ds(j, TQ // per_row, stride=per_row), :] = spread if scale is None else spread * scale
    cosd = cosd_ref[...]
    sind = sind_ref[...]

    def rope(t):
        rot = jnp.where(first_half, pltpu.roll(t, LANES - HEAD_DIM // 2, 1),
                        pltpu.roll(t, HEAD_DIM // 2, 1))
        return t * cosd + rot * sind

    def dup_halves(t):
        swapped = pltpu.roll(t, HEAD_DIM, 1)
        return jnp.where(low_head, t, swapped), jnp.where(low_head, swapped, t)

    def qk_norm(t, gain):
        sq = t * t
        lo = jnp.sum(jnp.where(low_head, sq, 0.0), axis=-1, keepdims=True)
        hi = jnp.sum(jnp.where(low_head, 0.0, sq), axis=-1, keepdims=True)
        inv = jnp.where(low_head, lax.rsqrt(lo * (1.0 / HEAD_DIM) + EPS), lax.rsqrt(hi * (1.0 / HEAD_DIM) + EPS))
        return t * inv * gain

    kd = dup_halves(rope(qk_norm(k, gk_ref[...])))
    vd = dup_halves(v)
    for kvh in range(ATT_KV_HEADS):
        kbands[kvh][WINDOW:WINDOW + TQ, :] = kd[kvh].astype(BF16)
        vbands[kvh][WINDOW:WINDOW + TQ, :] = vd[kvh].astype(BF16)

    q_heads = []
    for c in range(Q_WIDTH // LANES):
        cols = slice(c * LANES, (c + 1) * LANES)
        qc = rope(qk_norm(q[:, cols], gq_ref[:, cols]))
        q_heads.append(jnp.where(low_head, qc, 0.0).astype(BF16))
        q_heads.append(jnp.where(low_head, 0.0, qc).astype(BF16))

    qi = lax.broadcasted_iota(jnp.int32, (WINDOW, 2 * WINDOW), 0)
    kj = lax.broadcasted_iota(jnp.int32, (WINDOW, 2 * WINDOW), 1)
    band_mask = (kj > qi) & (kj <= qi + WINDOW)
    first_mask = band_mask & (kj >= jnp.where(i > 0, 0, WINDOW))
    low_head_w = lax.broadcasted_iota(jnp.int32, (WINDOW, LANES), 1) < HEAD_DIM

    def attention_scores(step):
        n, kvh = divmod(step, ATT_KV_HEADS)
        rows = slice(n * WINDOW, (n + 1) * WINDOW)
        mask = first_mask if n == 0 else band_mask
        heads = range(kvh * group, (kvh + 1) * group)
        q_st = jnp.concatenate([q_heads[hd][rows] for hd in heads], axis=0)
        kband = kbands[kvh][n * WINDOW:n * WINDOW + 2 * WINDOW, :]
        s_all = _dot_nt(q_st, kband)
        es, dens = [], []
        for j, hd in enumerate(heads):
            s = jnp.where(mask, s_all[j * WINDOW:(j + 1) * WINDOW], NEG_INF)
            sink = sinks_ref[hd]
            m = jnp.maximum(jnp.max(s, axis=-1, keepdims=True), sink)
            e = jnp.exp2(s - m)
            dens.append(jnp.sum(e, axis=-1, keepdims=True) + jnp.exp2(sink - m))
            es.append(e.astype(BF16))
        return jnp.concatenate(es, axis=0), dens

    def attention_output(step, probs):
        n, kvh = divmod(step, ATT_KV_HEADS)
        rows = slice(n * WINDOW, (n + 1) * WINDOW)
        e_all, dens = probs
        vband = vbands[kvh][n * WINDOW:n * WINDOW + 2 * WINDOW, :]
        o_all = _dot(e_all, vband)
        for cc in range(group // 2):
            oa = o_all[(2 * cc) * WINDOW:(2 * cc + 1) * WINDOW] / dens[2 * cc]
            ob = o_all[(2 * cc + 1) * WINDOW:(2 * cc + 2) * WINDOW] / dens[2 * cc + 1]
            col = kvh * (group // 2) + cc
            o_ref[rows, col * LANES:(col + 1) * LANES] = jnp.where(low_head_w, oa, ob).astype(BF16)

    def mem_scores(hd, xq):
        cols = slice(hd * MEM_HEAD_DIM, (hd + 1) * MEM_HEAD_DIM)
        s = _dot_nt(xq[:, cols], mk_ref[0, :, cols])
        e = jnp.exp2(s - jnp.max(s, axis=-1, keepdims=True))
        return e.astype(BF16), jnp.sum(e, axis=-1, keepdims=True)

    def mem_output(hd, probs):
        cols = slice(hd * MEM_HEAD_DIM, (hd + 1) * MEM_HEAD_DIM)
        e, den = probs
        om_ref[:, cols] = (_dot(e, mv_ref[0, :, cols]) / den).astype(BF16)

    route = _router_select(route_logits)
    p0 = attention_scores(0)
    gates.append(gate(2))
    p1 = attention_scores(1)
    attention_output(0, p0)
    xq = _dot(h, win_ref[:, COL_XQ:COL_GATE])
    p2 = attention_scores(2)
    attention_output(1, p1)
    _router_emit(*route, tri_ref, x1_ref, rt_ref, cnt_ref)
    xq = (_rms_per_mem_head(xq) * gxq_ref[...]).astype(BF16)
    p3 = attention_scores(3)
    attention_output(2, p2)
    y_conv = conv_finish()
    m0 = mem_scores(0, xq)
    attention_output(3, p3)
    for r in kbands + vbands:
        r[0:WINDOW, :] = r[TQ:TQ + WINDOW, :]
    m1 = mem_scores(1, xq)
    mem_output(0, m0)
    m2 = mem_scores(2, xq)
    mem_output(1, m1)
    m3 = mem_scores(3, xq)
    mem_output(2, m2)
    y_attn = _dot(o_ref[...], woa_ref[...])
    merged = gates[0] * y_attn + gates[1] * y_conv
    mem_output(3, m3)
    merged = merged + gates[2] * _dot(om_ref[...], wom_ref[...])

    x1prev_ref[...] = x + _dot(merged.astype(BF16), wout_ref[...])


def _router_logits(x1, g2_ref, wrt_ref, brt_ref):
    h2 = _rms(x1) * g2_ref[...]
    h_hi, h_lo = _split_bf16(h2)
    w_hi, w_lo = _split_bf16(wrt_ref[...])
    return _dot_nt(w_hi, h_hi) + _dot_nt(w_hi, h_lo) + _dot_nt(w_lo, h_hi) + brt_ref[...]


def _router_select(lt):
    gl = [lt[g:g + 1] for g in range(N_GROUPS)]
    gmax = functools.reduce(jnp.maximum, gl)
    g_idx = jnp.full((1, TQ), N_GROUPS - 1, jnp.int32)
    for g in reversed(range(N_GROUPS - 1)):
        g_idx = jnp.where(gl[g] == gmax, g, g_idx)
    p_g = 1.0 / functools.reduce(jnp.add, [jnp.exp(r - gmax) for r in gl])
    el = []
    for kk in range(EXPERTS_PER_GROUP):
        row = lt[8 + kk:9 + kk]
        for g in range(1, N_GROUPS):
            off = 8 + g * EXPERTS_PER_GROUP + kk
            row = jnp.where(g_idx == g, lt[off:off + 1], row)
        el.append(row)

    def first_argmax(rows):
        top = functools.reduce(jnp.maximum, rows)
        idx = jnp.full((1, TQ), len(rows) - 1, jnp.int32)
        for kk in reversed(range(len(rows) - 1)):
            idx = jnp.where(rows[kk] == top, kk, idx)
        return top, idx

    v1, i1 = first_argmax(el)
    v2, i2 = first_argmax([jnp.where(i1 == kk, -jnp.inf, el[kk]) for kk in range(EXPERTS_PER_GROUP)])
    t = jnp.exp(v2 - v1)
    p1 = 1.0 / (1.0 + t)
    p2 = t * p1

    lo = jnp.minimum(i1, i2)
    hi = jnp.maximum(i1, i2)
    w_lo = jnp.where(i1 < i2, p_g * p1, p_g * p2)
    w_hi = jnp.where(i1 < i2, p_g * p2, p_g * p1)
    cls = g_idx * N_PAIRS + ((lo * (2 * EXPERTS_PER_GROUP - 1 - lo)) >> 1) + hi - lo - 1
    return cls, w_lo, w_hi


def _router_emit(cls, w_lo, w_hi, tri_ref, x1_ref, rt_ref, cnt_ref):
    cls_f = cls.astype(F32)
    onehot = jnp.broadcast_to(cls, (CLS_ROWS, TQ)) == lax.broadcasted_iota(jnp.int32, (CLS_ROWS, TQ), 0)
    onehot_bf = onehot.astype(F32).astype(BF16)
    prefix = _dot(onehot_bf, tri_ref[...])
    rank = jnp.sum(jnp.where(onehot, prefix, 0.0), axis=0, keepdims=True) - 1.0
    counts = _dot_nt(jnp.ones((8, TQ), BF16), onehot_bf)
    zero = jnp.zeros((1, TQ), F32)
    rt_ref[0] = jnp.concatenate([cls_f, rank] + [zero] * 6, axis=0)
    cnt_ref[0] = jnp.concatenate([counts, jnp.zeros((8, LANES - CLS_ROWS), F32)], axis=1)
    cols = jnp.concatenate([w_lo, w_hi, cls_f, jnp.zeros((LANES - 3, TQ), F32)], axis=0)
    x1_ref[0, :, D_MODEL:] = cols.T


FLAG_FIRST, FLAG_VALID = 1, 2
GATHER_SLOTS = 3


def _moe_kernel(dest_ref, tile_ref, elo_ref, ehi_ref, cls_ref, flag_ref,
                g2_ref, wg_lo, wu_lo, wd_lo, wg_hi, wu_hi, wd_hi, x1_hbm,
                out_hbm,
                perm_ref, xbuf, obuf, h2buf, wbuf_up, wbuf_down, gsem, ssem):
    k = pl.program_id(0)
    nk = pl.num_programs(0)
    t = tile_ref[k]
    xslot = t % GATHER_SLOTS
    oslot = t % 2
    flags = flag_ref[k]
    n_tok = x1_hbm.shape[0]
    n_tiles = n_tok // TM

    def gather_copy(r, tile, s):
        tok = perm_ref[tile * TM + r]
        return pltpu.make_async_copy(x1_hbm.at[pl.ds(tok, 1), :], xbuf.at[s, pl.ds(r, 1), :], gsem.at[s])

    def scatter_copy(r, tile, s):
        tok = perm_ref[tile * TM + r]
        return pltpu.make_async_copy(obuf.at[s, pl.ds(r, 1), :], out_hbm.at[pl.ds(tok, 1), :], ssem.at[s])

    k_prev = jnp.maximum(k - 1, 0)
    for side, (e_ref, wg, wu, wd) in enumerate(((elo_ref, wg_lo, wu_lo, wd_lo), (ehi_ref, wg_hi, wu_hi, wd_hi))):
        @pl.when((k == 0) | (e_ref[k] != e_ref[k_prev]))
        def _(side=side, wg=wg, wu=wu, wd=wd):
            wbuf_up[2 * side] = wg[0]
            wbuf_up[2 * side + 1] = wu[0]
            wbuf_down[side] = wd[0]

    def for_rows(fn):
        def body(r, carry):
            fn(r)
            return carry
        lax.fori_loop(0, TM, body, 0, unroll=8)

    @pl.when(k == 0)
    def _():
        def invert(tok, carry):
            perm_ref[dest_ref[tok]] = tok
            return carry
        lax.fori_loop(0, n_tok, invert, 0, unroll=16)
        for tile in range(GATHER_SLOTS - 1):
            for_rows(lambda r, tile=tile: gather_copy(r, tile, tile).start())

    is_first = (flags & FLAG_FIRST) != 0
    is_valid = (flags & FLAG_VALID) != 0
    ahead = GATHER_SLOTS - 1
    prefetch_tile = jnp.minimum(t + ahead, n_tiles - 1)
    prefetch_slot = (t + ahead) % GATHER_SLOTS

    def gather_tile_wait(s):
        pltpu.make_async_copy(x1_hbm.at[pl.ds(0, TM), :], xbuf.at[s], gsem.at[s]).wait()

    def scatter_tile_wait(s):
        pltpu.make_async_copy(obuf.at[s], out_hbm.at[pl.ds(0, TM), :], ssem.at[s]).wait()

    @pl.when(is_first)
    def _():
        gather_tile_wait(xslot)

        @pl.when(t >= 2)
        def _():
            scatter_tile_wait(oslot)

        xr = xbuf[xslot, :, 0:D_MODEL]
        h2buf[...] = (_rms(xr) * g2_ref[...]).astype(BF16)
        obuf[oslot] = xr

    def experts():
        w = xbuf[xslot, :, D_MODEL:XW]
        mine = w[:, 2:3] == cls_ref[k].astype(F32)
        h2 = h2buf[...]
        y = None
        for side in range(2):
            c = jnp.where(mine, w[:, side:side + 1], 0.0)
            hid = jax.nn.silu(_dot(h2, wbuf_up[2 * side])) * _dot(h2, wbuf_up[2 * side + 1]) * c
            part = _dot(hid.astype(BF16), wbuf_down[side])
            y = part if y is None else y + part
        obuf[oslot] += y

    @pl.when(is_first & (t > 0))
    def _():
        for r in range(TM):
            gather_copy(r, prefetch_tile, prefetch_slot).start()
            scatter_copy(r, t - 1, 1 - oslot).start()
        experts()

    @pl.when(is_first & (t == 0))
    def _():
        for r in range(TM):
            gather_copy(r, prefetch_tile, prefetch_slot).start()
        experts()

    @pl.when(is_valid & jnp.logical_not(is_first))
    def _():
        experts()

    @pl.when(k == nk - 1)
    def _():
        last = n_tiles - 1
        for_rows(lambda r: scatter_copy(r, last, last % 2).start())
        for tile in (last - 1, last):
            scatter_tile_wait(tile % 2)
        for extra in range(1, GATHER_SLOTS):
            gather_tile_wait((last + extra) % GATHER_SLOTS)


_CLASS_PAIRS = [(a, b) for a in range(EXPERTS_PER_GROUP) for b in range(a + 1, EXPERTS_PER_GROUP)]


def _routing_plan(cls, rank, tile_counts, n_tiles):
    counts = jnp.sum(tile_counts, axis=0)
    off = jnp.concatenate([jnp.zeros((1,), jnp.int32), jnp.cumsum(counts).astype(jnp.int32)])
    tile_base = jnp.cumsum(tile_counts, axis=0) - tile_counts + off[None, :-1]
    onehot = cls[:, :, None] == jnp.arange(N_CLASSES, dtype=jnp.int32)[None, None, :]
    dest = (jnp.sum(jnp.where(onehot, tile_base[:, None, :], 0), axis=-1) + rank).reshape(-1)
    first_tile = off[:-1] // TM
    last_tile = (off[1:] - 1) // TM
    n_items = jnp.where(counts > 0, last_tile - first_tile + 1, 0)
    istart = jnp.concatenate([jnp.zeros((1,), jnp.int32), jnp.cumsum(n_items).astype(jnp.int32)])
    total = istart[-1]
    ni = n_tiles + N_CLASSES - 1
    k = jnp.arange(ni, dtype=jnp.int32)
    valid = k < total
    kc = jnp.minimum(k, total - 1)
    c_of_k = jnp.minimum(jnp.sum(istart[None, 1:] <= kc[:, None], axis=1).astype(jnp.int32), N_CLASSES - 1)
    sel = c_of_k[:, None] == jnp.arange(N_CLASSES, dtype=jnp.int32)[None, :]
    pick = lambda table: jnp.sum(jnp.where(sel, table[None, :], 0), axis=1).astype(jnp.int32)
    tile_k = pick(first_tile) + kc - pick(istart[:-1])
    prev_tile = jnp.concatenate([jnp.full((1,), -1, jnp.int32), tile_k[:-1]])
    first = valid & (tile_k != prev_tile)
    flags = (first * FLAG_FIRST + valid * FLAG_VALID).astype(jnp.int32)
    experts_of_class = [(c // N_PAIRS * EXPERTS_PER_GROUP + lo, c // N_PAIRS * EXPERTS_PER_GROUP + hi)
                        for c in range(N_CLASSES) for lo, hi in [_CLASS_PAIRS[c % N_PAIRS]]]
    e_lo = pick(jnp.array([e[0] for e in experts_of_class], jnp.int32))
    e_hi = pick(jnp.array([e[1] for e in experts_of_class], jnp.int32))
    cls_k = jnp.where(valid, c_of_k, -1)
    return dest.astype(jnp.int32), tile_k, e_lo, e_hi, cls_k, flags


def _const_spec(shape):
    nd = len(shape)
    return pl.BlockSpec(shape, lambda *_: (0,) * nd, pipeline_mode=pl.Buffered(1))


def kernel(x, mem, positions, g_norm1, w_in, g_q, g_k, sinks, w_o_attn, w_conv_dw, b_conv_dw, g_conv_ln, b_conv_ln, w_conv_out, g_mem, w_kv_mem, g_xq, g_xk, w_o_mem, w_out, g_norm2, w_group, b_group, w_router, b_router, w_gate, w_up, w_down):
    B, S, D = x.shape
    M = mem.shape[1]
    assert D == D_MODEL and S % TQ == 0 and (B * S) % TM == 0 and w_in.shape[0] == 1
    assert (TQ // WINDOW) * ATT_KV_HEADS == MEM_HEADS and CONV_CH // LANES == N_BRANCHES + 1
    NT = S // TQ
    T = B * S
    l = 0

    row = lambda v: v.reshape(1, -1).astype(F32)

    inv_freq = 1.0 / (ROPE_THETA ** (jnp.arange(0, HEAD_DIM, 2, dtype=F32) / HEAD_DIM))
    per_row = LANES // (HEAD_DIM // 2)
    ang = positions.astype(F32).reshape(B, S // per_row, per_row, 1) * inv_freq
    ang = ang.reshape(B, S // per_row, LANES)
    cos_tab, sin_tab = jnp.cos(ang), jnp.sin(ang)

    assert w_in.shape[-1] == IN_WIDTH
    wdw = jnp.zeros((CONV_HALO, CONV_CH), F32).at[:CONV_WIDTH].set(w_conv_dw[l])
    w_rt = jnp.zeros((RT_ROWS, D), F32).at[0:N_GROUPS].set(w_group[l].T).at[8:8 + N_EXPERTS].set(w_router[l].T)
    b_rt = jnp.zeros((RT_ROWS, 1), F32).at[0:N_GROUPS, 0].set(b_group[l]).at[8:8 + N_EXPERTS, 0].set(b_router[l])

    n_mix = B * NT
    tile_in = lambda s: jnp.minimum(s, n_mix - 1)
    tile_out = lambda s: jnp.maximum(s - 1, 0)
    tile3 = lambda last: pl.BlockSpec((1, TQ, last), lambda s: (tile_in(s) // NT, tile_in(s) % NT, 0))
    per_batch = pl.BlockSpec((1, M, D), lambda s: (tile_in(s) // NT, 0, 0))
    rope_spec = pl.BlockSpec((1, TQ // per_row, LANES), lambda s: (tile_in(s) // NT, tile_in(s) % NT, 0))
    in_specs = [
        tile3(D), rope_spec, rope_spec,
        per_batch, _const_spec((1, D)), _const_spec((D, 2 * XQ_WIDTH)), _const_spec((1, XQ_WIDTH)),
        _const_spec((1, D)),
        _const_spec((D, IN_WIDTH)),
        _const_spec((1, Q_WIDTH)), _const_spec((1, LANES)),
        pl.BlockSpec(memory_space=pltpu.SMEM),
        _const_spec((Q_WIDTH, D)),
        _const_spec((CONV_HALO, CONV_CH)), _const_spec((1, CONV_CH)), _const_spec((1, CONV_CH)), _const_spec((1, CONV_CH)),
        _const_spec((CONV_CH, D)),
        _const_spec((1, XQ_WIDTH)),
        _const_spec((XQ_WIDTH, D)),
        _const_spec((D, D)),
        _const_spec((1, D)),
        _const_spec((RT_ROWS, D)), _const_spec((RT_ROWS, 1)), _const_spec((TQ, TQ)),
    ]
    tri = (jnp.arange(TQ)[:, None] <= jnp.arange(TQ)[None, :]).astype(BF16)
    per_tile = lambda rows, last: pl.BlockSpec((1, rows, last), lambda s: (tile_out(s), 0, 0))
    x1_spec = pl.BlockSpec((1, TQ, XW), lambda s: (tile_out(s) // NT, tile_out(s) % NT, 0))
    assert (N_EXPERTS * D) % n_mix == 0 and (N_EXPERTS * EXPERT_FF) % n_mix == 0
    up_rows, down_rows = N_EXPERTS * D // n_mix, N_EXPERTS * EXPERT_FF // n_mix
    up_slice = pl.BlockSpec((up_rows, EXPERT_FF), lambda s: (tile_in(s), 0))
    down_slice = pl.BlockSpec((down_rows, D), lambda s: (tile_in(s), 0))
    in_specs += [up_slice, up_slice, down_slice]
    x1, rt, cnt, wg, wu, wd = pl.pallas_call(
        functools.partial(_mixer_kernel, tiles_per_seq=NT),
        grid=(n_mix + 1,),
        in_specs=in_specs,
        out_specs=[x1_spec, per_tile(8, TQ), per_tile(8, LANES), up_slice, up_slice, down_slice],
        out_shape=[jax.ShapeDtypeStruct((B, S, XW), F32), jax.ShapeDtypeStruct((B * NT, 8, TQ), F32),
                   jax.ShapeDtypeStruct((B * NT, 8, LANES), F32),
                   jax.ShapeDtypeStruct((N_EXPERTS * D, EXPERT_FF), BF16),
                   jax.ShapeDtypeStruct((N_EXPERTS * D, EXPERT_FF), BF16),
                   jax.ShapeDtypeStruct((N_EXPERTS * EXPERT_FF, D), BF16)],
        scratch_shapes=[pltpu.VMEM((WINDOW + TQ, LANES), BF16)] * 4
        + [pltpu.VMEM((CONV_CH // LANES, CONV_HALO + TQ, LANES), F32),
           pltpu.VMEM((CONV_CH // LANES, TQ, LANES), F32), pltpu.VMEM((TQ, Q_WIDTH), BF16),
           pltpu.VMEM((TQ, XQ_WIDTH), BF16), pltpu.VMEM((TQ, D), F32),
           pltpu.VMEM((TQ, LANES), F32), pltpu.VMEM((TQ, LANES), F32),
           pltpu.VMEM((1, M, XQ_WIDTH), BF16), pltpu.VMEM((1, M, XQ_WIDTH), BF16)],
        compiler_params=pltpu.CompilerParams(dimension_semantics=("arbitrary",),
                                             vmem_limit_bytes=VMEM_LIMIT),
        name="mixer",
    )(x, cos_tab, sin_tab, mem, row(g_mem[l]), w_kv_mem[l].astype(BF16), row(jnp.tile(g_xk[l], MEM_HEADS)),
      row(g_norm1[l]), w_in[l].astype(BF16),
      row(jnp.tile(g_q[l], ATT_HEADS) * (HEAD_DIM ** -0.5 * LOG2E)), row(jnp.tile(g_k[l], ATT_KV_HEADS)),
      sinks[l].astype(F32) * LOG2E, w_o_attn[l].astype(BF16),
      wdw, row(b_conv_dw[l]), row(g_conv_ln[l]), row(b_conv_ln[l]), w_conv_out[l].astype(BF16),
      row(jnp.tile(g_xq[l], MEM_HEADS) * (MEM_HEAD_DIM ** -0.5 * LOG2E)), w_o_mem[l].astype(BF16),
      w_out[l].astype(BF16), row(g_norm2[l]), w_rt, b_rt, tri,
      w_gate[l].reshape(N_EXPERTS * D, EXPERT_FF), w_up[l].reshape(N_EXPERTS * D, EXPERT_FF),
      w_down[l].reshape(N_EXPERTS * EXPERT_FF, D))

    n_tiles = T // TM
    plan = _routing_plan(rt[:, 0, :].astype(jnp.int32), rt[:, 1, :].astype(jnp.int32),
                         cnt[:, 0, :N_CLASSES].astype(jnp.int32), n_tiles)
    dest, tile_k, e_lo, e_hi, cls_k, flags = plan
    ni = tile_k.shape[0]
    wg = wg.reshape(N_EXPERTS, D, EXPERT_FF)
    wu = wu.reshape(N_EXPERTS, D, EXPERT_FF)
    wd = wd.reshape(N_EXPERTS, EXPERT_FF, D)
    lo_map = lambda k, dest, tile, elo, ehi, cls, flg: (elo[k], 0, 0)
    hi_map = lambda k, dest, tile, elo, ehi, cls, flg: (ehi[k], 0, 0)
    up_block = (1, D, EXPERT_FF)
    down_block = (1, EXPERT_FF, D)
    out = pl.pallas_call(
        _moe_kernel,
        grid_spec=pltpu.PrefetchScalarGridSpec(
            num_scalar_prefetch=6,
            grid=(ni,),
            in_specs=[pl.BlockSpec((1, D), lambda k, *_: (0, 0)),
                      pl.BlockSpec(up_block, lo_map), pl.BlockSpec(up_block, lo_map),
                      pl.BlockSpec(down_block, lo_map),
                      pl.BlockSpec(up_block, hi_map), pl.BlockSpec(up_block, hi_map),
                      pl.BlockSpec(down_block, hi_map),
                      pl.BlockSpec(memory_space=pl.ANY)],
            out_specs=pl.BlockSpec(memory_space=pl.ANY),
            scratch_shapes=[pltpu.SMEM((T,), jnp.int32),
                            pltpu.VMEM((GATHER_SLOTS, TM, XW), F32), pltpu.VMEM((2, TM, D), F32),
                            pltpu.VMEM((TM, D), BF16),
                            pltpu.VMEM((4, D, EXPERT_FF), BF16), pltpu.VMEM((2, EXPERT_FF, D), BF16),
                            pltpu.SemaphoreType.DMA((GATHER_SLOTS,)), pltpu.SemaphoreType.DMA((2,))]),
        out_shape=jax.ShapeDtypeStruct((T, D), F32),
        compiler_params=pltpu.CompilerParams(dimension_semantics=("arbitrary",),
                                             vmem_limit_bytes=VMEM_LIMIT),
        name="moe",
    )(dest, tile_k, e_lo, e_hi, cls_k, flags, row(g_norm2[l]), wg, wu, wd, wg, wu, wd,
      x1.reshape(T, XW))
    return out.reshape(B, S, D)
```
